```python
import jax
import jax.numpy as jnp
from jax import lax
import numpy as np

D_MODEL = 2048
BATCH = 4
SEQ = 4096
DEPTH = 1

CHUNK = 64
Q_BLOCK = 128
NORM_EPS = 1e-6
NEG_INF = -1e30

FOX_HEADS = 8
FOX_HEAD_DIM = D_MODEL // (2 * FOX_HEADS)
FOX_WIDTH = FOX_HEADS * FOX_HEAD_DIM

MLSTM_HEADS = 8
MLSTM_V_DIM = D_MODEL // (2 * MLSTM_HEADS)
MLSTM_QK_DIM = MLSTM_V_DIM // 2
MLSTM_V_WIDTH = MLSTM_HEADS * MLSTM_V_DIM
MLSTM_QK_WIDTH = MLSTM_HEADS * MLSTM_QK_DIM
CONV_WIDTH = 4

MIX_WIDTH = FOX_WIDTH + MLSTM_V_WIDTH

IN_SIZES = (FOX_WIDTH, FOX_WIDTH, FOX_WIDTH, FOX_WIDTH, FOX_HEADS,
            MLSTM_QK_WIDTH, MLSTM_QK_WIDTH, MLSTM_V_WIDTH, MLSTM_V_WIDTH, MLSTM_V_WIDTH,
            MLSTM_HEADS, MLSTM_HEADS)
IN_COLS = sum(IN_SIZES)

kernel_name = 'fox_mlstm_parallel_heads_block'


def rms_norm(x, g):
    xf = x.astype(jnp.float32)
    y = xf * lax.rsqrt(jnp.mean(xf * xf, axis=-1, keepdims=True) + NORM_EPS)
    return (y * g.astype(jnp.float32)).astype(x.dtype)


def head_rms_norm(y, g):
    h, d = y.shape[1], y.shape[3]
    yf = y.astype(jnp.float32)
    yf = yf * lax.rsqrt(jnp.mean(yf * yf, axis=-1, keepdims=True) + NORM_EPS)
    return (yf * g.reshape(h, d)[None, :, None, :].astype(jnp.float32)).astype(y.dtype)


def to_heads(t, n_heads):
    b, s, w = t.shape
    return t.reshape(b, s, n_heads, w // n_heads).transpose(0, 2, 1, 3)


def from_heads(t):
    b, h, s, d = t.shape
    return t.transpose(0, 2, 1, 3).reshape(b, s, h * d)


def causal_dwconv(u, w, bias):
    s = u.shape[1]
    up = jnp.pad(u, ((0, 0), (CONV_WIDTH - 1, 0), (0, 0)))
    y = up[:, 0:s] * w[0]
    for j in range(1, CONV_WIDTH):
        y = y + up[:, j:j + s] * w[j]
    return y + bias


def fox_attention(q, k, v, f_pre):
    b, h, s, d = q.shape
    cum_logf = jnp.cumsum(jax.nn.log_sigmoid(f_pre), axis=-1)
    scale = d ** -0.5
    kpos = jnp.arange(s)

    def block(i):
        start = i * Q_BLOCK
        qb = lax.dynamic_slice_in_dim(q, start, Q_BLOCK, axis=2)
        fq = lax.dynamic_slice_in_dim(cum_logf, start, Q_BLOCK, axis=2)
        logits = jnp.einsum('bhqd,bhkd->bhqk', qb, k).astype(jnp.float32) * scale
        logits = logits + fq[..., :, None] - cum_logf[..., None, :]
        qpos = start + jnp.arange(Q_BLOCK)
        mask = kpos[None, :] <= qpos[:, None]
        p = jax.nn.softmax(jnp.where(mask, logits, NEG_INF), axis=-1)
        return jnp.einsum('bhqk,bhkd->bhqd', p.astype(v.dtype), v)

    out = lax.map(block, jnp.arange(s // Q_BLOCK))
    return out.transpose(1, 2, 0, 3, 4).reshape(b, h, s, d)


def mlstm_chunkwise(q, k, v, i_pre, log_f):
    b, h, s, dk = q.shape
    dv = v.shape[-1]
    nc = s // CHUNK

    def chunks(t):
        return jnp.moveaxis(t.reshape(b, h, nc, CHUNK, *t.shape[3:]), 2, 0)

    causal = jnp.tril(jnp.ones((CHUNK, CHUNK), dtype=bool))

    def body(carry, inp):
        c_prev, n_prev, m_prev = carry
        qc, kc, vc, ic, fc = inp
        bcum = jnp.cumsum(fc, axis=-1)
        dmat = bcum[..., :, None] - bcum[..., None, :] + ic[..., None, :]
        dmat = jnp.where(causal, dmat, NEG_INF)
        inter = bcum + m_prev[..., None]
        m_t = jnp.maximum(inter, jnp.max(dmat, axis=-1))
        w_intra = jnp.exp(dmat - m_t[..., None])
        w_inter = jnp.exp(inter - m_t)
        sc = jnp.einsum('bhtd,bhsd->bhts', qc, kc) * w_intra
        num = (jnp.einsum('bhts,bhsv->bhtv', sc, vc)
               + w_inter[..., None] * jnp.einsum('bhtd,bhvd->bhtv', qc, c_prev))
        den = jnp.sum(sc, axis=-1) + w_inter * jnp.einsum('bhtd,bhd->bht', qc, n_prev)
        h_t = num / jnp.maximum(jnp.abs(den), jnp.exp(-m_t))[..., None]
        b_last = bcum[..., -1]
        w_s = b_last[..., None] - bcum + ic
        m_new = jnp.maximum(b_last + m_prev, jnp.max(w_s, axis=-1))
        decay = jnp.exp(b_last + m_prev - m_new)
        w_s = jnp.exp(w_s - m_new[..., None])
        c_new = decay[..., None, None] * c_prev + jnp.einsum('bhs,bhsv,bhsd->bhvd', w_s, vc, kc)
        n_new = decay[..., None] * n_prev + jnp.einsum('bhs,bhsd->bhd', w_s, kc)
        return (c_new, n_new, m_new), h_t

    init = (jnp.zeros((b, h, dv, dk), jnp.float32),
            jnp.zeros((b, h, dk), jnp.float32),
            jnp.zeros((b, h), jnp.float32))
    _, hs = lax.scan(body, init, (chunks(q), chunks(k), chunks(v), chunks(i_pre), chunks(log_f)))
    return jnp.moveaxis(hs, 0, 2).reshape(b, h, s, dv)


def setup_inputs(seed: int = 0) -> dict:
    key = jax.random.key(seed)
    ks = jax.random.split(key, 13)
    f32 = jnp.float32
    nrm = jax.random.normal
    x = nrm(ks[0], (BATCH, SEQ, D_MODEL), f32)
    norm_w = 1.0 + 0.02 * nrm(ks[1], (DEPTH, D_MODEL), f32)
    w_in = nrm(ks[2], (DEPTH, D_MODEL, IN_COLS), f32) * D_MODEL ** -0.5
    fox_f_bias = jnp.linspace(2.0, 5.0, FOX_HEADS, dtype=f32)[None, :] + 0.1 * nrm(ks[3], (DEPTH, FOX_HEADS), f32)
    conv_w = nrm(ks[4], (DEPTH, CONV_WIDTH, 2 * MLSTM_QK_WIDTH), f32) * CONV_WIDTH ** -0.5
    conv_b = 0.01 * nrm(ks[5], (DEPTH, 2 * MLSTM_QK_WIDTH), f32)
    mlstm_i_bias = -3.0 + 0.1 * nrm(ks[6], (DEPTH, MLSTM_HEADS), f32)
    mlstm_f_bias = jnp.linspace(3.0, 6.0, MLSTM_HEADS, dtype=f32)[None, :] + 0.1 * nrm(ks[7], (DEPTH, MLSTM_HEADS), f32)
    fox_out_norm_w = 1.0 + 0.02 * nrm(ks[8], (DEPTH, FOX_WIDTH), f32)
    mlstm_out_norm_w = 1.0 + 0.02 * nrm(ks[9], (DEPTH, MLSTM_V_WIDTH), f32)
    w_out = nrm(ks[10], (DEPTH, MIX_WIDTH, D_MODEL), f32) * MIX_WIDTH ** -0.5
    final_norm_w = 1.0 + 0.02 * nrm(ks[11], (D_MODEL,), f32)
    return {'x': x, 'norm_w': norm_w, 'w_in': w_in, 'fox_f_bias': fox_f_bias,
            'conv_w': conv_w, 'conv_b': conv_b, 'mlstm_i_bias': mlstm_i_bias,
            'mlstm_f_bias': mlstm_f_bias, 'fox_out_norm_w': fox_out_norm_w,
            'mlstm_out_norm_w': mlstm_out_norm_w, 'w_out': w_out, 'final_norm_w': final_norm_w}


def reference(x, norm_w, w_in, fox_f_bias, conv_w, conv_b, mlstm_i_bias, mlstm_f_bias,
              fox_out_norm_w, mlstm_out_norm_w, w_out, final_norm_w):
    f32 = jnp.float32
    split_idx = [int(v) for v in np.cumsum(IN_SIZES)[:-1]]
    for l in range(DEPTH):
        h = rms_norm(x, norm_w[l])
        proj = jnp.einsum('bsd,de->bse', h, w_in[l])
        (fq, fk, fv, fz, ff, mq, mk, mv, mo, mz, mi, mf) = jnp.split(proj, split_idx, axis=-1)

        f_pre = (ff + fox_f_bias[l]).astype(f32).transpose(0, 2, 1)
        ya = fox_attention(to_heads(fq, FOX_HEADS), to_heads(fk, FOX_HEADS),
                           to_heads(fv, FOX_HEADS), f_pre)
        ya = from_heads(head_rms_norm(ya, fox_out_norm_w[l])) * jax.nn.silu(fz)

        qk = jax.nn.silu(causal_dwconv(jnp.concatenate([mq, mk], axis=-1), conv_w[l], conv_b[l]))
        q_m = to_heads(qk[..., :MLSTM_QK_WIDTH], MLSTM_HEADS).astype(f32)
        k_m = to_heads(qk[..., MLSTM_QK_WIDTH:], MLSTM_HEADS).astype(f32) * MLSTM_QK_DIM ** -0.5
        v_m = to_heads(mv, MLSTM_HEADS).astype(f32)
        i_pre = (mi + mlstm_i_bias[l]).astype(f32).transpose(0, 2, 1)
        log_f = jax.nn.log_sigmoid((mf + mlstm_f_bias[l]).astype(f32)).transpose(0, 2, 1)
        hb = mlstm_chunkwise(q_m, k_m, v_m, i_pre, log_f)
        hb = (hb * jax.nn.sigmoid(to_heads(mo, MLSTM_HEADS).astype(f32))).astype(x.dtype)
        yb = from_heads(head_rms_norm(hb, mlstm_out_norm_w[l])) * jax.nn.silu(mz)

        y = jnp.concatenate([ya, yb], axis=-1)
        x = x + jnp.einsum('bse,ed->bsd', y, w_out[l])
    return rms_norm(x, final_norm_w)
```

```python
import functools

import numpy as np
import jax
import jax.numpy as jnp
from jax import lax
from jax.experimental import pallas as pl
from jax.experimental.pallas import tpu as pltpu

D_MODEL = 2048
NORM_EPS = 1e-6
NEG_INF = -1e30

FOX_HEADS = 8
FOX_HEAD_DIM = 128
FOX_WIDTH = FOX_HEADS * FOX_HEAD_DIM
MLSTM_HEADS = 8
MLSTM_V_DIM = 128
MLSTM_QK_DIM = 64
MLSTM_V_WIDTH = MLSTM_HEADS * MLSTM_V_DIM
MLSTM_QK_WIDTH = MLSTM_HEADS * MLSTM_QK_DIM
CONV_WIDTH = 4
IN_SIZES = (FOX_WIDTH, FOX_WIDTH, FOX_WIDTH, FOX_WIDTH, FOX_HEADS,
            MLSTM_QK_WIDTH, MLSTM_QK_WIDTH, MLSTM_V_WIDTH, MLSTM_V_WIDTH, MLSTM_V_WIDTH,
            MLSTM_HEADS, MLSTM_HEADS)
IN_OFFS = tuple(int(v) for v in np.cumsum((0,) + IN_SIZES))

LANES = 128
SEG = 1024
VMEM_LIMIT = 56 * 1024 * 1024

PROJ_TM = 1024
GATE_R = 512
MLSTM_L = 256
ATT_T = 1024
OUT_TM = 512

LANE_G, LANE_U, LANE_A, LANE_NEGM = 8, 16, 24, 32

BF16 = jnp.bfloat16
F32 = jnp.float32


def _sigmoid(x):
    return 1.0 / (1.0 + jnp.exp(-x))


def _params(sem):
    return pltpu.CompilerParams(dimension_semantics=sem, vmem_limit_bytes=VMEM_LIMIT)


def _inproj_kernel(x_ref, nw_ref, w_ref, wg_ref, o16_ref, o32_ref, g_ref, hn_ref):
    j = pl.program_id(1)

    @pl.when(j == 0)
    def _():
        xf = x_ref[...]
        ms = jnp.mean(xf * xf, axis=-1, keepdims=True)
        hb = (xf * lax.rsqrt(ms + NORM_EPS) * nw_ref[...]).astype(BF16)
        hn_ref[...] = hb
        g_ref[...] = jnp.dot(hb, wg_ref[...], preferred_element_type=F32)

    acc = jnp.dot(hn_ref[...], w_ref[...], preferred_element_type=F32)

    @pl.when(j < 4)
    def _():
        o16_ref[...] = acc.astype(BF16)

    @pl.when(j >= 4)
    def _():
        o32_ref[...] = acc


def _inproj(x2, nw, w_big, w_gate):
    m = x2.shape[0]
    tm = PROJ_TM
    return pl.pallas_call(
        _inproj_kernel,
        grid=(m // tm, 8),
        in_specs=[
            pl.BlockSpec((tm, D_MODEL), lambda i, j: (i, 0)),
            pl.BlockSpec((1, D_MODEL), lambda i, j: (0, 0)),
            pl.BlockSpec((D_MODEL, SEG), lambda i, j: (0, j)),
            pl.BlockSpec((D_MODEL, 2 * LANES), lambda i, j: (0, 0)),
        ],
        out_specs=[
            pl.BlockSpec((None, tm, SEG), lambda i, j: (jnp.minimum(j, 3), i, 0)),
            pl.BlockSpec((None, tm, SEG), lambda i, j: (jnp.maximum(j - 4, 0), i, 0)),
            pl.BlockSpec((tm, 2 * LANES), lambda i, j: (i, 0)),
        ],
        out_shape=[
            jax.ShapeDtypeStruct((4, m, SEG), BF16),
            jax.ShapeDtypeStruct((4, m, SEG), F32),
            jax.ShapeDtypeStruct((m, 2 * LANES), F32),
        ],
        scratch_shapes=[pltpu.VMEM((tm, D_MODEL), BF16)],
        compiler_params=_params(("arbitrary", "arbitrary")),
        name="inproj",
    )(x2, nw, w_big, w_gate)


def _split3(v):
    hi = v.astype(BF16)
    r1 = v - hi.astype(F32)
    mid = r1.astype(BF16)
    lo = (r1 - mid.astype(F32)).astype(BF16)
    return hi, mid, lo


def _gates_kernel(g_ref, b_ref, ck_ref, mcol_ref, arow_ref, carry_ref, mprev_ref):
    rows, chunk = GATE_R, MLSTM_L

    @pl.when(pl.program_id(1) == 0)
    def _():
        carry_ref[...] = jnp.zeros_like(carry_ref)
        mprev_ref[...] = jnp.zeros_like(mprev_ref)

    v = g_ref[...] + b_ref[...]
    v1 = v[:, :LANES]
    v2 = v[:, LANES:]
    ls = jnp.minimum(v1, 0.0) - jnp.log1p(jnp.exp(-jnp.abs(v1)))

    r_i = lax.broadcasted_iota(jnp.int32, (rows, rows), 0)
    c_i = lax.broadcasted_iota(jnp.int32, (rows, rows), 1)
    tri = c_i <= r_i
    t_full = tri.astype(BF16)
    t_loc = (tri & ((r_i // chunk) == (c_i // chunk))).astype(BF16)
    hi, mid, lo = _split3(ls)

    def cum(t):
        return (jnp.dot(t, hi, preferred_element_type=F32)
                + jnp.dot(t, mid, preferred_element_type=F32)
                + jnp.dot(t, lo, preferred_element_type=F32))

    cumf = cum(t_full) + carry_ref[...]
    cuml = cum(t_loc)
    carry_ref[...] = cumf[rows - 1:rows, :]

    a = v2 - cuml
    t_in = lax.broadcasted_iota(jnp.int32, (rows, LANES), 0) % chunk
    cm = a
    s = 1
    while s < chunk:
        cm = jnp.where(t_in >= s, jnp.maximum(cm, pltpu.roll(cm, s, axis=0)), cm)
        s *= 2

    lane = lax.broadcasted_iota(jnp.int32, (chunk, LANES), 1)
    mprev = mprev_ref[...]
    for c in range(rows // chunk):
        sl = slice(c * chunk, (c + 1) * chunk)
        b = cuml[sl]
        mt = b + jnp.maximum(mprev, cm[sl])
        gt = b - mt
        ut = (b + mprev) - mt
        tile = jnp.where(lane < LANE_G, cumf[sl],
               jnp.where(lane < LANE_U, gt,
               jnp.where(lane < LANE_A, pltpu.roll(ut, LANE_U - LANE_G, axis=1),
               jnp.where(lane < LANE_NEGM, pltpu.roll(a[sl], LANE_A - LANE_G, axis=1),
                         pltpu.roll(-mt, LANE_NEGM - LANE_G, axis=1)))))
        mprev = mt[chunk - 1:chunk, :]
        mcol_ref[sl, :] = tile
        tt = tile.T
        for h in range(FOX_HEADS):
            ck_ref[h, :, sl] = tt[h:h + 1, :]
        arow_ref[:, sl] = tt[LANE_A:LANE_A + MLSTM_HEADS, :]
    mprev_ref[...] = mprev


def _gates(g3, bias_vec):
    b, s, _ = g3.shape
    r = GATE_R
    return pl.pallas_call(
        _gates_kernel,
        grid=(b, s // r),
        in_specs=[
            pl.BlockSpec((None, r, 2 * LANES), lambda bi, ri: (bi, ri, 0)),
            pl.BlockSpec((1, 2 * LANES), lambda bi, ri: (0, 0)),
        ],
        out_specs=[
            pl.BlockSpec((None, FOX_HEADS, 1, r), lambda bi, ri: (bi, 0, 0, ri)),
            pl.BlockSpec((None, r, LANES), lambda bi, ri: (bi, ri, 0)),
            pl.BlockSpec((None, MLSTM_HEADS, r), lambda bi, ri: (bi, 0, ri)),
        ],
        out_shape=[
            jax.ShapeDtypeStruct((b, FOX_HEADS, 1, s), F32),
            jax.ShapeDtypeStruct((b, s, LANES), F32),
            jax.ShapeDtypeStruct((b, MLSTM_HEADS, s), F32),
        ],
        scratch_shapes=[pltpu.VMEM((1, LANES), F32), pltpu.VMEM((1, LANES), F32)],
        compiler_params=_params(("arbitrary", "arbitrary")),
        name="gates",
    )(g3, bias_vec)


def _fox_kernel(q_ref, k_ref, v_ref, ck_ref, z_ref, nw_ref, o_ref, m_ref, l_ref, acc_ref):
    i = pl.program_id(2)
    j = pl.program_id(3)
    t = ATT_T
    scale = FOX_HEAD_DIM ** -0.5

    @pl.when(j == 0)
    def _():
        m_ref[...] = jnp.full_like(m_ref, NEG_INF)
        l_ref[...] = jnp.zeros_like(l_ref)
        acc_ref[...] = jnp.zeros_like(acc_ref)

    def step(masked):
        s = lax.dot_general(q_ref[...], k_ref[...], (((1,), (1,)), ((), ())),
                            preferred_element_type=F32)
        s = s * scale - ck_ref[...]
        if masked:
            r_i = lax.broadcasted_iota(jnp.int32, (t, t), 0)
            c_i = lax.broadcasted_iota(jnp.int32, (t, t), 1)
            s = jnp.where(c_i <= r_i, s, NEG_INF)
        m_prev = m_ref[...]
        m_new = jnp.maximum(m_prev, jnp.max(s, axis=-1, keepdims=True))
        alpha = jnp.exp(m_prev - m_new)
        p = jnp.exp(s - m_new)
        l_ref[...] = alpha * l_ref[...] + jnp.sum(p, axis=-1, keepdims=True)
        acc_ref[...] = alpha * acc_ref[...] + jnp.dot(p.astype(BF16), v_ref[...],
                                                      preferred_element_type=F32)
        m_ref[...] = m_new

    @pl.when(j < i)
    def _():
        step(False)

    @pl.when(j == i)
    def _():
        step(True)
        o = acc_ref[...] / l_ref[...]
        o = o * lax.rsqrt(jnp.mean(o * o, axis=-1, keepdims=True) + NORM_EPS) * nw_ref[...]
        z = z_ref[...]
        o_ref[...] = (o * (z * _sigmoid(z))).astype(BF16)


def _fox(p16, p32, ck, nw):
    _, b, s, _ = p16.shape
    t = ATT_T
    n = s // t
    hd = FOX_HEAD_DIM

    def kv_map(which):
        return lambda bi, h, i, j: (which, bi, jnp.minimum(j, i), h)

    return pl.pallas_call(
        _fox_kernel,
        grid=(b, FOX_HEADS, n, n),
        in_specs=[
            pl.BlockSpec((None, None, t, hd), lambda bi, h, i, j: (0, bi, i, h)),
            pl.BlockSpec((None, None, t, hd), kv_map(1)),
            pl.BlockSpec((None, None, t, hd), kv_map(2)),
            pl.BlockSpec((None, None, 1, t), lambda bi, h, i, j: (bi, h, 0, jnp.minimum(j, i))),
            pl.BlockSpec((None, None, t, hd), lambda bi, h, i, j: (0, bi, i, h)),
            pl.BlockSpec((1, hd), lambda bi, h, i, j: (0, h)),
        ],
        out_specs=pl.BlockSpec((None, t, hd), lambda bi, h, i, j: (bi, i, h)),
        out_shape=jax.ShapeDtypeStruct((b, s, FOX_WIDTH), BF16),
        scratch_shapes=[pltpu.VMEM((t, 1), F32), pltpu.VMEM((t, 1), F32), pltpu.VMEM((t, hd), F32)],
        compiler_params=_params(("arbitrary",) * 4),
        name="fox_attention",
    )(p16, p16, p16, ck, p32, nw)


def _mlstm_kernel(qk_ref, v_ref, og_ref, z_ref, mcol_ref, arow_ref, cw_ref, cb_ref, nw_ref,
                  out_ref, xs_ref, st_ref):
    n = MLSTM_L
    dv = MLSTM_V_DIM
    pad = 8

    @pl.when(pl.program_id(1) == 0)
    def _():
        xs_ref[0:pad, :] = jnp.zeros((pad, SEG), F32)
        st_ref[...] = jnp.zeros_like(st_ref)

    xs_ref[pad:pad + n, :] = qk_ref[...]
    base = pad - (CONV_WIDTH - 1)
    y = xs_ref[base:base + n, :] * cw_ref[0:1, :]
    for jj in range(1, CONV_WIDTH):
        y = y + xs_ref[base + jj:base + jj + n, :] * cw_ref[jj:jj + 1, :]
    y = y + cb_ref[...]
    xs_ref[0:pad, :] = xs_ref[n:n + pad, :]
    sy = y * _sigmoid(y)

    mc = mcol_ref[...]
    lane = lax.broadcasted_iota(jnp.int32, (n, LANES), 1)
    ones_col = jnp.where(lane == 0, 1.0, 0.0).astype(BF16)
    r_i = lax.broadcasted_iota(jnp.int32, (n, n), 0)
    c_i = lax.broadcasted_iota(jnp.int32, (n, n), 1)
    causal = c_i <= r_i

    for h in range(MLSTM_HEADS):
        pr = h // 2
        hs = slice(h * dv, (h + 1) * dv)
        qp = sy[:, pr * LANES:(pr + 1) * LANES].astype(BF16)
        kp = sy[:, MLSTM_QK_WIDTH + pr * LANES:MLSTM_QK_WIDTH + (pr + 1) * LANES] * (MLSTM_QK_DIM ** -0.5)
        kh = jnp.where((lane < MLSTM_QK_DIM) if h % 2 == 0 else (lane >= MLSTM_QK_DIM), kp, 0.0)

        g_col = mc[:, LANE_G + h:LANE_G + h + 1]
        u_col = mc[:, LANE_U + h:LANE_U + h + 1]
        a_col = mc[:, LANE_A + h:LANE_A + h + 1]
        negm_col = mc[:, LANE_NEGM + h:LANE_NEGM + h + 1]
        a_row = arow_ref[h:h + 1, :]

        s = lax.dot_general(qp, kh.astype(BF16), (((1,), (1,)), ((), ())), preferred_element_type=F32)
        sc = s * jnp.exp(jnp.where(causal, g_col + a_row, NEG_INF))
        vaug = jnp.concatenate([v_ref[:, hs], ones_col], axis=1)
        st = st_ref[h]
        res = (jnp.dot(sc.astype(BF16), vaug, preferred_element_type=F32)
               + jnp.exp(u_col) * jnp.dot(qp, st.astype(BF16), preferred_element_type=F32))
        num = res[:, :dv]
        den = res[:, dv:dv + 1]
        ht = num / jnp.maximum(jnp.abs(den), jnp.exp(negm_col))

        w_col = jnp.exp(a_col + g_col[n - 1:n, :])
        kw = (kh * w_col).astype(BF16)
        upd = lax.dot_general(kw, vaug, (((0,), (0,)), ((), ())), preferred_element_type=F32)
        st_ref[h] = jnp.exp(u_col[n - 1:n, :]) * st + upd

        hb = ht * _sigmoid(og_ref[:, hs])
        hn = hb * lax.rsqrt(jnp.mean(hb * hb, axis=-1, keepdims=True) + NORM_EPS) * nw_ref[:, hs]
        z = z_ref[:, hs]
        out_ref[:, hs] = (hn * (z * _sigmoid(z))).astype(BF16)


def _mlstm(p16, p32, mcol, arow, cw, cb, nw):
    _, b, s, _ = p16.shape
    n = MLSTM_L

    def seg_map(which):
        return lambda bi, c: (which, bi, c, 0)

    return pl.pallas_call(
        _mlstm_kernel,
        grid=(b, s // n),
        in_specs=[
            pl.BlockSpec((None, None, n, SEG), seg_map(1)),
            pl.BlockSpec((None, None, n, SEG), seg_map(3)),
            pl.BlockSpec((None, None, n, SEG), seg_map(2)),
            pl.BlockSpec((None, None, n, SEG), seg_map(3)),
            pl.BlockSpec((None, n, LANES), lambda bi, c: (bi, c, 0)),
            pl.BlockSpec((None, MLSTM_HEADS, n), lambda bi, c: (bi, 0, c)),
            pl.BlockSpec((CONV_WIDTH, SEG), lambda bi, c: (0, 0)),
            pl.BlockSpec((1, SEG), lambda bi, c: (0, 0)),
            pl.BlockSpec((1, SEG), lambda bi, c: (0, 0)),
        ],
        out_specs=pl.BlockSpec((None, n, SEG), lambda bi, c: (bi, c, 0)),
        out_shape=jax.ShapeDtypeStruct((b, s, MLSTM_V_WIDTH), BF16),
        scratch_shapes=[pltpu.VMEM((n + 8, SEG), F32),
                        pltpu.VMEM((MLSTM_HEADS, LANES, 2 * LANES), F32)],
        compiler_params=_params(("arbitrary", "arbitrary")),
        name="mlstm",
    )(p32, p16, p32, p32, mcol, arow, cw, cb, nw)


def _outproj_kernel(ya_ref, yb_ref, wa_ref, wb_ref, x_ref, fw_ref, o_ref, *, final):
    acc = (jnp.dot(ya_ref[...], wa_ref[...], preferred_element_type=F32)
           + jnp.dot(yb_ref[...], wb_ref[...], preferred_element_type=F32))
    r = x_ref[...] + acc
    if final:
        r = r * lax.rsqrt(jnp.mean(r * r, axis=-1, keepdims=True) + NORM_EPS) * fw_ref[...]
    o_ref[...] = r


def _outproj(ya, yb, wa, wb, x2, fw, final):
    m = x2.shape[0]
    tm = OUT_TM
    return pl.pallas_call(
        functools.partial(_outproj_kernel, final=final),
        grid=(m // tm,),
        in_specs=[
            pl.BlockSpec((tm, FOX_WIDTH), lambda i: (i, 0)),
            pl.BlockSpec((tm, MLSTM_V_WIDTH), lambda i: (i, 0)),
            pl.BlockSpec((FOX_WIDTH, D_MODEL), lambda i: (0, 0)),
            pl.BlockSpec((MLSTM_V_WIDTH, D_MODEL), lambda i: (0, 0)),
            pl.BlockSpec((tm, D_MODEL), lambda i: (i, 0)),
            pl.BlockSpec((1, D_MODEL), lambda i: (0, 0)),
        ],
        out_specs=pl.BlockSpec((tm, D_MODEL), lambda i: (i, 0)),
        out_shape=jax.ShapeDtypeStruct((m, D_MODEL), F32),
        compiler_params=_params(("arbitrary",)),
        name="outproj",
    )(ya, yb, wa, wb, x2, fw)


def _reorder_in_weight(w):
    o = IN_OFFS
    seg = lambda k: w[:, o[k]:o[k + 1]]
    w_big = jnp.concatenate([seg(0), seg(1), seg(2), seg(7), seg(3), seg(5), seg(6), seg(8), seg(9)],
                            axis=1).astype(BF16)
    zeros = lambda n: jnp.zeros((w.shape[0], n), w.dtype)
    w_gate = jnp.concatenate([seg(4), seg(11), zeros(LANES - 16), zeros(8), seg(10), zeros(LANES - 16)],
                             axis=1).astype(BF16)
    return w_big, w_gate


def kernel(x, norm_w, w_in, fox_f_bias, conv_w, conv_b, mlstm_i_bias, mlstm_f_bias,
           fox_out_norm_w, mlstm_out_norm_w, w_out, final_norm_w):
    b, s, d = x.shape
    depth = norm_w.shape[0]
    x2 = x.reshape(b * s, d)
    for l in range(depth):
        w_big, w_gate = _reorder_in_weight(w_in[l])
        bias_vec = jnp.concatenate([
            fox_f_bias[l], mlstm_f_bias[l], jnp.zeros((LANES - 16,), F32),
            jnp.zeros((8,), F32), mlstm_i_bias[l], jnp.zeros((LANES - 16,), F32)]).reshape(1, 2 * LANES)

        p16, p32, g = _inproj(x2, norm_w[l].reshape(1, d), w_big, w_gate)
        p16 = p16.reshape(4, b, s, SEG)
        p32 = p32.reshape(4, b, s, SEG)
        ck, mcol, arow = _gates(g.reshape(b, s, 2 * LANES), bias_vec)

        ya = _fox(p16, p32, ck, fox_out_norm_w[l].reshape(1, FOX_WIDTH))
        yb = _mlstm(p16, p32, mcol, arow, conv_w[l], conv_b[l].reshape(1, SEG),
                    mlstm_out_norm_w[l].reshape(1, MLSTM_V_WIDTH))

        wo = w_out[l].astype(BF16)
        x2 = _outproj(ya.reshape(b * s, FOX_WIDTH), yb.reshape(b * s, MLSTM_V_WIDTH),
                      wo[:FOX_WIDTH], wo[FOX_WIDTH:], x2, final_norm_w.reshape(1, d),
                      final=(l == depth - 1))
    return x2.reshape(b, s, d)
```

```python
import functools

import numpy as np
import jax
import jax.numpy as jnp
from jax import lax
from jax.experimental import pallas as pl
from jax.experimental.pallas import tpu as pltpu

D_MODEL = 2048
NORM_EPS = 1e-6
NEG_INF = -1e30

FOX_HEADS = 8
FOX_HEAD_DIM = 128
FOX_WIDTH = FOX_HEADS * FOX_HEAD_DIM
MLSTM_HEADS = 8
MLSTM_V_DIM = 128
MLSTM_QK_DIM = 64
MLSTM_V_WIDTH = MLSTM_HEADS * MLSTM_V_DIM
MLSTM_QK_WIDTH = MLSTM_HEADS * MLSTM_QK_DIM
CONV_WIDTH = 4
IN_SIZES = (FOX_WIDTH, FOX_WIDTH, FOX_WIDTH, FOX_WIDTH, FOX_HEADS,
            MLSTM_QK_WIDTH, MLSTM_QK_WIDTH, MLSTM_V_WIDTH, MLSTM_V_WIDTH, MLSTM_V_WIDTH,
            MLSTM_HEADS, MLSTM_HEADS)
IN_OFFS = tuple(int(v) for v in np.cumsum((0,) + IN_SIZES))

LANES = 128
SEG = 1024
VMEM_LIMIT = 56 * 1024 * 1024

PROJ_TM = 1024
GATE_R = 512
MLSTM_L = 256
ATT_T = 512
OUT_TM = 512

LOG2E = 1.4426950408889634
FOX_Q_SCALE = FOX_HEAD_DIM ** -0.5 * LOG2E
ONES_ROWS = 16

LANE_G, LANE_U, LANE_A, LANE_NEGM = 8, 16, 24, 32

BF16 = jnp.bfloat16
F32 = jnp.float32


def _sigmoid(x):
    return 1.0 / (1.0 + jnp.exp(-x))


def _params(sem):
    return pltpu.CompilerParams(dimension_semantics=sem, vmem_limit_bytes=VMEM_LIMIT)


def _inproj_kernel(x_ref, nw_ref, w_ref, wg_ref, o16_ref, vt_ref, o32_ref, g_ref, hn_ref):
    j = pl.program_id(1)

    @pl.when(j == 0)
    def _():
        xf = x_ref[...]
        ms = jnp.mean(xf * xf, axis=-1, keepdims=True)
        hb = (xf * lax.rsqrt(ms + NORM_EPS) * nw_ref[...]).astype(BF16)
        hn_ref[...] = hb
        g_ref[...] = jnp.dot(hb, wg_ref[...], preferred_element_type=F32)

    acc = jnp.dot(hn_ref[...], w_ref[...], preferred_element_type=F32)

    @pl.when(j == 0)
    def _():
        o16_ref[...] = (acc * FOX_Q_SCALE).astype(BF16)

    @pl.when((j == 1) | (j == 2))
    def _():
        o16_ref[...] = acc.astype(BF16)

    @pl.when(j == 3)
    def _():
        vt_ref[...] = acc.T.astype(BF16)

    @pl.when(j >= 4)
    def _():
        o32_ref[...] = acc


def _inproj(x2, nw, w_big, w_gate, seq):
    m = x2.shape[0]
    tm = PROJ_TM
    per_seq = seq // tm
    return pl.pallas_call(
        _inproj_kernel,
        grid=(m // tm, 8),
        in_specs=[
            pl.BlockSpec((tm, D_MODEL), lambda i, j: (i, 0)),
            pl.BlockSpec((1, D_MODEL), lambda i, j: (0, 0)),
            pl.BlockSpec((D_MODEL, SEG), lambda i, j: (0, j)),
            pl.BlockSpec((D_MODEL, 2 * LANES), lambda i, j: (0, 0)),
        ],
        out_specs=[
            pl.BlockSpec((None, tm, SEG), lambda i, j: (jnp.minimum(j, 2), i, 0)),
            pl.BlockSpec((None, SEG, tm), lambda i, j: (i // per_seq, 0, i % per_seq)),
            pl.BlockSpec((None, tm, SEG), lambda i, j: (jnp.maximum(j - 4, 0), i, 0)),
            pl.BlockSpec((tm, 2 * LANES), lambda i, j: (i, 0)),
        ],
        out_shape=[
            jax.ShapeDtypeStruct((3, m, SEG), BF16),
            jax.ShapeDtypeStruct((m // seq, SEG, seq), BF16),
            jax.ShapeDtypeStruct((4, m, SEG), F32),
            jax.ShapeDtypeStruct((m, 2 * LANES), F32),
        ],
        scratch_shapes=[pltpu.VMEM((tm, D_MODEL), BF16)],
        compiler_params=_params(("arbitrary", "arbitrary")),
        name="inproj",
    )(x2, nw, w_big, w_gate)


def _split3(v):
    hi = v.astype(BF16)
    r1 = v - hi.astype(F32)
    mid = r1.astype(BF16)
    lo = (r1 - mid.astype(F32)).astype(BF16)
    return hi, mid, lo


def _gates_kernel(g_ref, b_ref, mcol_ref, arow_ref, carry_ref, mprev_ref):
    rows, chunk = GATE_R, MLSTM_L

    @pl.when(pl.program_id(1) == 0)
    def _():
        carry_ref[...] = jnp.zeros_like(carry_ref)
        mprev_ref[...] = jnp.zeros_like(mprev_ref)

    v = g_ref[...] + b_ref[...]
    v1 = v[:, :LANES]
    v2 = v[:, LANES:]
    ls = jnp.minimum(v1, 0.0) - jnp.log1p(jnp.exp(-jnp.abs(v1)))

    r_i = lax.broadcasted_iota(jnp.int32, (rows, rows), 0)
    c_i = lax.broadcasted_iota(jnp.int32, (rows, rows), 1)
    tri = c_i <= r_i
    t_full = tri.astype(BF16)
    t_loc = (tri & ((r_i // chunk) == (c_i // chunk))).astype(BF16)
    hi, mid, lo = _split3(ls)

    def cum(t):
        return (jnp.dot(t, hi, preferred_element_type=F32)
                + jnp.dot(t, mid, preferred_element_type=F32)
                + jnp.dot(t, lo, preferred_element_type=F32))

    cumf = cum(t_full) + carry_ref[...]
    cuml = cum(t_loc)
    carry_ref[...] = cumf[rows - 1:rows, :]

    a = v2 - cuml
    t_in = lax.broadcasted_iota(jnp.int32, (rows, LANES), 0) % chunk
    cm = a
    s = 1
    while s < chunk:
        cm = jnp.where(t_in >= s, jnp.maximum(cm, pltpu.roll(cm, s, axis=0)), cm)
        s *= 2

    lane = lax.broadcasted_iota(jnp.int32, (chunk, LANES), 1)
    mprev = mprev_ref[...]
    for c in range(rows // chunk):
        sl = slice(c * chunk, (c + 1) * chunk)
        b = cuml[sl]
        mt = b + jnp.maximum(mprev, cm[sl])
        gt = b - mt
        ut = (b + mprev) - mt
        tile = jnp.where(lane < LANE_G, cumf[sl] * LOG2E,
               jnp.where(lane < LANE_U, gt,
               jnp.where(lane < LANE_A, pltpu.roll(ut, LANE_U - LANE_G, axis=1),
               jnp.where(lane < LANE_NEGM, pltpu.roll(a[sl], LANE_A - LANE_G, axis=1),
                         pltpu.roll(-mt, LANE_NEGM - LANE_G, axis=1)))))
        mprev = mt[chunk - 1:chunk, :]
        mcol_ref[sl, :] = tile
        arow_ref[:, sl] = tile.T[LANE_A:LANE_A + MLSTM_HEADS, :]
    mprev_ref[...] = mprev


def _gates(g3, bias_vec):
    b, s, _ = g3.shape
    r = GATE_R
    return pl.pallas_call(
        _gates_kernel,
        grid=(b, s // r),
        in_specs=[
            pl.BlockSpec((None, r, 2 * LANES), lambda bi, ri: (bi, ri, 0)),
            pl.BlockSpec((1, 2 * LANES), lambda bi, ri: (0, 0)),
        ],
        out_specs=[
            pl.BlockSpec((None, r, LANES), lambda bi, ri: (bi, ri, 0)),
            pl.BlockSpec((None, MLSTM_HEADS, r), lambda bi, ri: (bi, 0, ri)),
        ],
        out_shape=[
            jax.ShapeDtypeStruct((b, s, LANES), F32),
            jax.ShapeDtypeStruct((b, MLSTM_HEADS, s), F32),
        ],
        scratch_shapes=[pltpu.VMEM((1, LANES), F32), pltpu.VMEM((1, LANES), F32)],
        compiler_params=_params(("arbitrary", "arbitrary")),
        name="gates",
    )(g3, bias_vec)


def _fox_kernel(it_ref, jt_ref, q_ref, k_ref, vt_ref, gc_ref, z_ref, nw_ref, o_ref, m_ref, acc_ref):
    step_id = pl.program_id(1)
    i = it_ref[step_id]
    j = jt_ref[step_id]
    t = ATT_T
    hd = FOX_HEAD_DIM

    @pl.when(j == 0)
    def _():
        m_ref[...] = jnp.full_like(m_ref, NEG_INF)
        acc_ref[...] = jnp.zeros_like(acc_ref)

    def step(masked):
        gc = gc_ref[...]
        ones = jnp.ones((ONES_ROWS, t), BF16)
        if masked:
            valid = (lax.broadcasted_iota(jnp.int32, (t, t), 0)
                     <= lax.broadcasted_iota(jnp.int32, (t, t), 1))
        def scores(h):
            hs = slice(h * hd, (h + 1) * hd)
            st = lax.dot_general(k_ref[:, hs], q_ref[:, hs], (((1,), (1,)), ((), ())),
                                 preferred_element_type=F32) - gc[:, h:h + 1]
            return jnp.where(valid, st, NEG_INF) if masked else st

        st_next = scores(0)
        for h in range(FOX_HEADS):
            hs = slice(h * hd, (h + 1) * hd)
            st = st_next
            if h + 1 < FOX_HEADS:
                st_next = scores(h + 1)
            m_prev = m_ref[h:h + 1, :]
            m_new = jnp.maximum(m_prev, jnp.max(st, axis=0, keepdims=True))
            alpha = jnp.exp2(m_prev - m_new)
            pt = jnp.exp2(st - m_new).astype(BF16)
            vt_aug = jnp.concatenate([vt_ref[hs, :], ones], axis=0)
            acc_ref[h] = alpha * acc_ref[h] + jnp.dot(vt_aug, pt, preferred_element_type=F32)
            m_ref[h:h + 1, :] = m_new

    @pl.when(j < i)
    def _():
        step(False)

    @pl.when(j == i)
    def _():
        step(True)
        for h in range(FOX_HEADS):
            hs = slice(h * hd, (h + 1) * hd)
            acc = acc_ref[h]
            o = (acc[:hd] / acc[hd:hd + 1]).T
            o = o * lax.rsqrt(jnp.mean(o * o, axis=-1, keepdims=True) + NORM_EPS) * nw_ref[:, hs]
            z = z_ref[:, hs]
            o_ref[:, hs] = (o * (z * _sigmoid(z))).astype(BF16)


def _fox(p16, vt, p32, mcol, nw):
    _, b, s, _ = p16.shape
    t = ATT_T
    n = s // t
    pairs = [(i, j) for i in range(n) for j in range(i + 1)]
    it = jnp.asarray([p[0] for p in pairs], jnp.int32)
    jt = jnp.asarray([p[1] for p in pairs], jnp.int32)

    def q_map(which):
        return lambda bi, p, it, jt: (which, bi, it[p], 0)

    def k_map(which):
        return lambda bi, p, it, jt: (which, bi, jt[p], 0)

    grid_spec = pltpu.PrefetchScalarGridSpec(
        num_scalar_prefetch=2,
        grid=(b, len(pairs)),
        in_specs=[
            pl.BlockSpec((None, None, t, FOX_WIDTH), q_map(0)),
            pl.BlockSpec((None, None, t, FOX_WIDTH), k_map(1)),
            pl.BlockSpec((None, FOX_WIDTH, t), lambda bi, p, it, jt: (bi, 0, jt[p])),
            pl.BlockSpec((None, t, LANES), lambda bi, p, it, jt: (bi, jt[p], 0)),
            pl.BlockSpec((None, None, t, FOX_WIDTH), q_map(0)),
            pl.BlockSpec((1, FOX_WIDTH), lambda bi, p, it, jt: (0, 0)),
        ],
        out_specs=pl.BlockSpec((None, t, FOX_WIDTH), lambda bi, p, it, jt: (bi, it[p], 0)),
        scratch_shapes=[pltpu.VMEM((FOX_HEADS, t), F32),
                        pltpu.VMEM((FOX_HEADS, FOX_HEAD_DIM + ONES_ROWS, t), F32)],
    )
    return pl.pallas_call(
        _fox_kernel,
        grid_spec=grid_spec,
        out_shape=jax.ShapeDtypeStruct((b, s, FOX_WIDTH), BF16),
        compiler_params=_params(("arbitrary", "arbitrary")),
        name="fox_attention",
    )(it, jt, p16, p16, vt, mcol, p32, nw)


def _mlstm_kernel(qk_ref, v_ref, og_ref, z_ref, mcol_ref, arow_ref, cw_ref, cb_ref, nw_ref,
                  out_ref, xs_ref, st_ref):
    n = MLSTM_L
    dv = MLSTM_V_DIM
    pad = 8

    @pl.when(pl.program_id(1) == 0)
    def _():
        xs_ref[0:pad, :] = jnp.zeros((pad, SEG), F32)
        st_ref[...] = jnp.zeros_like(st_ref)

    xs_ref[pad:pad + n, :] = qk_ref[...]
    base = pad - (CONV_WIDTH - 1)
    y = xs_ref[base:base + n, :] * cw_ref[0:1, :]
    for jj in range(1, CONV_WIDTH):
        y = y + xs_ref[base + jj:base + jj + n, :] * cw_ref[jj:jj + 1, :]
    y = y + cb_ref[...]
    xs_ref[0:pad, :] = xs_ref[n:n + pad, :]
    sy = y * _sigmoid(y)

    mc = mcol_ref[...]
    lane = lax.broadcasted_iota(jnp.int32, (n, LANES), 1)
    ones_col = jnp.where(lane == 0, 1.0, 0.0).astype(BF16)
    r_i = lax.broadcasted_iota(jnp.int32, (n, n), 0)
    c_i = lax.broadcasted_iota(jnp.int32, (n, n), 1)
    causal = c_i <= r_i

    for h in range(MLSTM_HEADS):
        pr = h // 2
        hs = slice(h * dv, (h + 1) * dv)
        qp = sy[:, pr * LANES:(pr + 1) * LANES].astype(BF16)
        kp = sy[:, MLSTM_QK_WIDTH + pr * LANES:MLSTM_QK_WIDTH + (pr + 1) * LANES] * (MLSTM_QK_DIM ** -0.5)
        kh = jnp.where((lane < MLSTM_QK_DIM) if h % 2 == 0 else (lane >= MLSTM_QK_DIM), kp, 0.0)

        g_col = mc[:, LANE_G + h:LANE_G + h + 1]
        u_col = mc[:, LANE_U + h:LANE_U + h + 1]
        a_col = mc[:, LANE_A + h:LANE_A + h + 1]
        negm_col = mc[:, LANE_NEGM + h:LANE_NEGM + h + 1]
        a_row = arow_ref[h:h + 1, :]

        s = lax.dot_general(qp, kh.astype(BF16), (((1,), (1,)), ((), ())), preferred_element_type=F32)
        sc = s * jnp.exp(jnp.where(causal, g_col + a_row, NEG_INF))
        vaug = jnp.concatenate([v_ref[:, hs], ones_col], axis=1)
        st = st_ref[h]
        res = (jnp.dot(sc.astype(BF16), vaug, preferred_element_type=F32)
               + jnp.exp(u_col) * jnp.dot(qp, st.astype(BF16), preferred_element_type=F32))
        num = res[:, :dv]
        den = res[:, dv:dv + 1]
        ht = num / jnp.maximum(jnp.abs(den), jnp.exp(negm_col))

        w_col = jnp.exp(a_col + g_col[n - 1:n, :])
        kw = (kh * w_col).astype(BF16)
        upd = lax.dot_general(kw, vaug, (((0,), (0,)), ((), ())), preferred_element_type=F32)
        st_ref[h] = jnp.exp(u_col[n - 1:n, :]) * st + upd

        hb = ht * _sigmoid(og_ref[:, hs])
        hn = hb * lax.rsqrt(jnp.mean(hb * hb, axis=-1, keepdims=True) + NORM_EPS) * nw_ref[:, hs]
        z = z_ref[:, hs]
        out_ref[:, hs] = (hn * (z * _sigmoid(z))).astype(BF16)


def _mlstm(p16, p32, mcol, arow, cw, cb, nw):
    _, b, s, _ = p16.shape
    n = MLSTM_L

    def seg_map(which):
        return lambda bi, c: (which, bi, c, 0)

    return pl.pallas_call(
        _mlstm_kernel,
        grid=(b, s // n),
        in_specs=[
            pl.BlockSpec((None, None, n, SEG), seg_map(1)),
            pl.BlockSpec((None, None, n, SEG), seg_map(2)),
            pl.BlockSpec((None, None, n, SEG), seg_map(2)),
            pl.BlockSpec((None, None, n, SEG), seg_map(3)),
            pl.BlockSpec((None, n, LANES), lambda bi, c: (bi, c, 0)),
            pl.BlockSpec((None, MLSTM_HEADS, n), lambda bi, c: (bi, 0, c)),
            pl.BlockSpec((CONV_WIDTH, SEG), lambda bi, c: (0, 0)),
            pl.BlockSpec((1, SEG), lambda bi, c: (0, 0)),
            pl.BlockSpec((1, SEG), lambda bi, c: (0, 0)),
        ],
        out_specs=pl.BlockSpec((None, n, SEG), lambda bi, c: (bi, c, 0)),
        out_shape=jax.ShapeDtypeStruct((b, s, MLSTM_V_WIDTH), BF16),
        scratch_shapes=[pltpu.VMEM((n + 8, SEG), F32),
                        pltpu.VMEM((MLSTM_HEADS, LANES, 2 * LANES), F32)],
        compiler_params=_params(("arbitrary", "arbitrary")),
        name="mlstm",
    )(p32, p16, p32, p32, mcol, arow, cw, cb, nw)


def _outproj_kernel(ya_ref, yb_ref, wa_ref, wb_ref, x_ref, fw_ref, o_ref, *, final):
    acc = (jnp.dot(ya_ref[...], wa_ref[...], preferred_element_type=F32)
           + jnp.dot(yb_ref[...], wb_ref[...], preferred_element_type=F32))
    r = x_ref[...] + acc
    if final:
        r = r * lax.rsqrt(jnp.mean(r * r, axis=-1, keepdims=True) + NORM_EPS) * fw_ref[...]
    o_ref[...] = r


def _outproj(ya, yb, wa, wb, x2, fw, final):
    m = x2.shape[0]
    tm = OUT_TM
    return pl.pallas_call(
        functools.partial(_outproj_kernel, final=final),
        grid=(m // tm,),
        in_specs=[
            pl.BlockSpec((tm, FOX_WIDTH), lambda i: (i, 0)),
            pl.BlockSpec((tm, MLSTM_V_WIDTH), lambda i: (i, 0)),
            pl.BlockSpec((FOX_WIDTH, D_MODEL), lambda i: (0, 0)),
            pl.BlockSpec((MLSTM_V_WIDTH, D_MODEL), lambda i: (0, 0)),
            pl.BlockSpec((tm, D_MODEL), lambda i: (i, 0)),
            pl.BlockSpec((1, D_MODEL), lambda i: (0, 0)),
        ],
        out_specs=pl.BlockSpec((tm, D_MODEL), lambda i: (i, 0)),
        out_shape=jax.ShapeDtypeStruct((m, D_MODEL), F32),
        compiler_params=_params(("arbitrary",)),
        name="outproj",
    )(ya, yb, wa, wb, x2, fw)


def _reorder_in_weight(w):
    o = IN_OFFS
    seg = lambda k: w[:, o[k]:o[k + 1]]
    w_big = jnp.concatenate([seg(0), seg(1), seg(7), seg(2), seg(3), seg(5), seg(6), seg(8), seg(9)],
                            axis=1).astype(BF16)
    zeros = lambda n: jnp.zeros((w.shape[0], n), w.dtype)
    w_gate = jnp.concatenate([seg(4), seg(11), zeros(LANES - 16), zeros(8), seg(10), zeros(LANES - 16)],
                             axis=1).astype(BF16)
    return w_big, w_gate


def kernel(x, norm_w, w_in, fox_f_bias, conv_w, conv_b, mlstm_i_bias, mlstm_f_bias,
           fox_out_norm_w, mlstm_out_norm_w, w_out, final_norm_w):
    b, s, d = x.shape
    depth = norm_w.shape[0]
    x2 = x.reshape(b * s, d)
    for l in range(depth):
        w_big, w_gate = _reorder_in_weight(w_in[l])
        bias_vec = jnp.concatenate([
            fox_f_bias[l], mlstm_f_bias[l], jnp.zeros((LANES - 16,), F32),
            jnp.zeros((8,), F32), mlstm_i_bias[l], jnp.zeros((LANES - 16,), F32)]).reshape(1, 2 * LANES)

        p16, vt, p32, g = _inproj(x2, norm_w[l].reshape(1, d), w_big, w_gate, s)
        p16 = p16.reshape(3, b, s, SEG)
        p32 = p32.reshape(4, b, s, SEG)
        mcol, arow = _gates(g.reshape(b, s, 2 * LANES), bias_vec)

        ya = _fox(p16, vt, p32, mcol, fox_out_norm_w[l].reshape(1, FOX_WIDTH))
        yb = _mlstm(p16, p32, mcol, arow, conv_w[l], conv_b[l].reshape(1, SEG),
                    mlstm_out_norm_w[l].reshape(1, MLSTM_V_WIDTH))

        wo = w_out[l].astype(BF16)
        x2 = _outproj(ya.reshape(b * s, FOX_WIDTH), yb.reshape(b * s, MLSTM_V_WIDTH),
                      wo[:FOX_WIDTH], wo[FOX_WIDTH:], x2, final_norm_w.reshape(1, d),
                      final=(l == depth - 1))
    return x2.reshape(b, s, d)
```

```python
import functools

import numpy as np
import jax
import jax.numpy as jnp
from jax import lax
from jax.experimental import pallas as pl
from jax.experimental.pallas import tpu as pltpu

D_MODEL = 2048
NORM_EPS = 1e-6
NEG_INF = -1e30

FOX_HEADS = 8
FOX_HEAD_DIM = 128
FOX_WIDTH = FOX_HEADS * FOX_HEAD_DIM
MLSTM_HEADS = 8
MLSTM_V_DIM = 128
MLSTM_QK_DIM = 64
MLSTM_V_WIDTH = MLSTM_HEADS * MLSTM_V_DIM
MLSTM_QK_WIDTH = MLSTM_HEADS * MLSTM_QK_DIM
CONV_WIDTH = 4
IN_SIZES = (FOX_WIDTH, FOX_WIDTH, FOX_WIDTH, FOX_WIDTH, FOX_HEADS,
            MLSTM_QK_WIDTH, MLSTM_QK_WIDTH, MLSTM_V_WIDTH, MLSTM_V_WIDTH, MLSTM_V_WIDTH,
            MLSTM_HEADS, MLSTM_HEADS)
IN_OFFS = tuple(int(v) for v in np.cumsum((0,) + IN_SIZES))

LANES = 128
SEG = 1024
VMEM_LIMIT = 56 * 1024 * 1024
INPROJ_VMEM_LIMIT = 62 * 1024 * 1024

PROJ_TM = 1024
GATE_R = 512
MLSTM_L = 256
ATT_T = 512
OUT_TM = 512

LOG2E = 1.4426950408889634
FOX_Q_SCALE = FOX_HEAD_DIM ** -0.5 * LOG2E
ONES_ROWS = 16

LANE_G, LANE_U, LANE_A, LANE_NEGM = 8, 16, 24, 32
ROW_G, ROW_U, ROW_NEGM = 0, 8, 24
GATE_ROWS = 32

BF16 = jnp.bfloat16
F32 = jnp.float32
NT_DIMS = (((1,), (1,)), ((), ()))
TN_DIMS = (((0,), (0,)), ((), ()))


def _sigmoid(x):
    return 1.0 / (1.0 + jnp.exp(-x))


def _params(sem):
    return pltpu.CompilerParams(dimension_semantics=sem, vmem_limit_bytes=VMEM_LIMIT)


def _inproj_kernel(x_ref, nw_ref, wt_ref, wg_ref, r16_ref, r32_ref, t16_ref, t32_ref, g_ref, hn_ref):
    j = pl.program_id(1)

    @pl.when(j == 0)
    def _():
        xf = x_ref[...]
        ms = jnp.mean(xf * xf, axis=-1, keepdims=True)
        hb = (xf * lax.rsqrt(ms + NORM_EPS) * nw_ref[...]).astype(BF16)
        hn_ref[...] = hb
        g_ref[...] = jnp.dot(hb, wg_ref[...], preferred_element_type=F32)

    @pl.when(j <= 2)
    def _():
        acc = lax.dot_general(hn_ref[...], wt_ref[...], NT_DIMS, preferred_element_type=F32)

        @pl.when(j == 0)
        def _():
            r16_ref[...] = (acc * FOX_Q_SCALE).astype(BF16)

        @pl.when(j == 1)
        def _():
            r16_ref[...] = acc.astype(BF16)

        @pl.when(j == 2)
        def _():
            r32_ref[...] = acc

    @pl.when(j >= 3)
    def _():
        acc_t = lax.dot_general(wt_ref[...], hn_ref[...], NT_DIMS, preferred_element_type=F32)

        @pl.when(j <= 4)
        def _():
            t16_ref[...] = acc_t.astype(BF16)

        @pl.when(j >= 5)
        def _():
            t32_ref[...] = acc_t


def _inproj(x2, nw, w_t, w_gate, seq):
    m = x2.shape[0]
    tm = PROJ_TM
    per_seq = seq // tm
    t_map = lambda lo, hi: (lambda i, j: (jnp.clip(j - lo, 0, hi), i // per_seq, 0, i % per_seq))
    return pl.pallas_call(
        _inproj_kernel,
        grid=(m // tm, 8),
        in_specs=[
            pl.BlockSpec((tm, D_MODEL), lambda i, j: (i, 0)),
            pl.BlockSpec((1, D_MODEL), lambda i, j: (0, 0)),
            pl.BlockSpec((SEG, D_MODEL), lambda i, j: (j, 0)),
            pl.BlockSpec((D_MODEL, 2 * LANES), lambda i, j: (0, 0)),
        ],
        out_specs=[
            pl.BlockSpec((None, tm, SEG), lambda i, j: (jnp.minimum(j, 1), i, 0)),
            pl.BlockSpec((tm, SEG), lambda i, j: (i, 0)),
            pl.BlockSpec((None, None, SEG, tm), t_map(3, 1)),
            pl.BlockSpec((None, None, SEG, tm), t_map(5, 2)),
            pl.BlockSpec((tm, 2 * LANES), lambda i, j: (i, 0)),
        ],
        out_shape=[
            jax.ShapeDtypeStruct((2, m, SEG), BF16),
            jax.ShapeDtypeStruct((m, SEG), F32),
            jax.ShapeDtypeStruct((2, m // seq, SEG, seq), BF16),
            jax.ShapeDtypeStruct((3, m // seq, SEG, seq), F32),
            jax.ShapeDtypeStruct((m, 2 * LANES), F32),
        ],
        scratch_shapes=[pltpu.VMEM((tm, D_MODEL), BF16)],
        compiler_params=pltpu.CompilerParams(dimension_semantics=("arbitrary", "arbitrary"),
                                             vmem_limit_bytes=INPROJ_VMEM_LIMIT),
        name="inproj",
    )(x2, nw, w_t, w_gate)


def _split3(v):
    hi = v.astype(BF16)
    r1 = v - hi.astype(F32)
    mid = r1.astype(BF16)
    lo = (r1 - mid.astype(F32)).astype(BF16)
    return hi, mid, lo


def _gates_kernel(g_ref, b_ref, mcol_ref, mrow_ref, carry_ref, mprev_ref):
    rows, chunk = GATE_R, MLSTM_L

    @pl.when(pl.program_id(1) == 0)
    def _():
        carry_ref[...] = jnp.zeros_like(carry_ref)
        mprev_ref[...] = jnp.zeros_like(mprev_ref)

    v = g_ref[...] + b_ref[...]
    v1 = v[:, :LANES]
    v2 = v[:, LANES:]
    ls = jnp.minimum(v1, 0.0) - jnp.log1p(jnp.exp(-jnp.abs(v1)))

    r_i = lax.broadcasted_iota(jnp.int32, (rows, rows), 0)
    c_i = lax.broadcasted_iota(jnp.int32, (rows, rows), 1)
    tri = c_i <= r_i
    t_full = tri.astype(BF16)
    t_loc = (tri & ((r_i // chunk) == (c_i // chunk))).astype(BF16)
    hi, mid, lo = _split3(ls)

    def cum(t):
        return (jnp.dot(t, hi, preferred_element_type=F32)
                + jnp.dot(t, mid, preferred_element_type=F32)
                + jnp.dot(t, lo, preferred_element_type=F32))

    cumf = cum(t_full) + carry_ref[...]
    cuml = cum(t_loc)
    carry_ref[...] = cumf[rows - 1:rows, :]

    a = v2 - cuml
    t_in = lax.broadcasted_iota(jnp.int32, (rows, LANES), 0) % chunk
    cm = a
    s = 1
    while s < chunk:
        cm = jnp.where(t_in >= s, jnp.maximum(cm, pltpu.roll(cm, s, axis=0)), cm)
        s *= 2

    lane = lax.broadcasted_iota(jnp.int32, (chunk, LANES), 1)
    mprev = mprev_ref[...]
    for c in range(rows // chunk):
        sl = slice(c * chunk, (c + 1) * chunk)
        b = cuml[sl]
        mt = b + jnp.maximum(mprev, cm[sl])
        gt = b - mt
        ut = (b + mprev) - mt
        tile = jnp.where(lane < LANE_G, cumf[sl] * LOG2E,
               jnp.where(lane < LANE_U, gt,
               jnp.where(lane < LANE_A, pltpu.roll(ut, LANE_U - LANE_G, axis=1),
               jnp.where(lane < LANE_NEGM, pltpu.roll(a[sl], LANE_A - LANE_G, axis=1),
                         pltpu.roll(-mt, LANE_NEGM - LANE_G, axis=1)))))
        mprev = mt[chunk - 1:chunk, :]
        mcol_ref[sl, :] = tile
        mrow_ref[:, sl] = tile.T[LANE_G:LANE_G + GATE_ROWS, :]
    mprev_ref[...] = mprev


def _gates(g3, bias_vec):
    b, s, _ = g3.shape
    r = GATE_R
    return pl.pallas_call(
        _gates_kernel,
        grid=(b, s // r),
        in_specs=[
            pl.BlockSpec((None, r, 2 * LANES), lambda bi, ri: (bi, ri, 0)),
            pl.BlockSpec((1, 2 * LANES), lambda bi, ri: (0, 0)),
        ],
        out_specs=[
            pl.BlockSpec((None, r, LANES), lambda bi, ri: (bi, ri, 0)),
            pl.BlockSpec((None, GATE_ROWS, r), lambda bi, ri: (bi, 0, ri)),
        ],
        out_shape=[
            jax.ShapeDtypeStruct((b, s, LANES), F32),
            jax.ShapeDtypeStruct((b, GATE_ROWS, s), F32),
        ],
        scratch_shapes=[pltpu.VMEM((1, LANES), F32), pltpu.VMEM((1, LANES), F32)],
        compiler_params=_params(("arbitrary", "arbitrary")),
        name="gates",
    )(g3, bias_vec)


def _fox_kernel(it_ref, jt_ref, q_ref, k_ref, vt_ref, gc_ref, zt_ref, nwb_ref, o_ref, m_ref, acc_ref):
    step_id = pl.program_id(1)
    i = it_ref[step_id]
    j = jt_ref[step_id]
    t = ATT_T
    hd = FOX_HEAD_DIM

    @pl.when(j == 0)
    def _():
        m_ref[...] = jnp.full_like(m_ref, NEG_INF)
        acc_ref[...] = jnp.zeros_like(acc_ref)

    def step(masked):
        gc = gc_ref[...]
        ones = jnp.ones((ONES_ROWS, t), BF16)
        if masked:
            valid = (lax.broadcasted_iota(jnp.int32, (t, t), 0)
                     <= lax.broadcasted_iota(jnp.int32, (t, t), 1))

        def scores(h):
            hs = slice(h * hd, (h + 1) * hd)
            st = lax.dot_general(k_ref[:, hs], q_ref[:, hs], NT_DIMS,
                                 preferred_element_type=F32) - gc[:, h:h + 1]
            return jnp.where(valid, st, NEG_INF) if masked else st

        st_next = scores(0)
        for h in range(FOX_HEADS):
            hs = slice(h * hd, (h + 1) * hd)
            st = st_next
            if h + 1 < FOX_HEADS:
                st_next = scores(h + 1)
            m_prev = m_ref[h:h + 1, :]
            m_new = jnp.maximum(m_prev, jnp.max(st, axis=0, keepdims=True))
            alpha = jnp.exp2(m_prev - m_new)
            pt = jnp.exp2(st - m_new).astype(BF16)
            vt_aug = jnp.concatenate([vt_ref[hs, :], ones], axis=0)
            acc_ref[h] = alpha * acc_ref[h] + jnp.dot(vt_aug, pt, preferred_element_type=F32)
            m_ref[h:h + 1, :] = m_new

    @pl.when(j < i)
    def _():
        step(False)

    @pl.when(j == i)
    def _():
        step(True)
        for h in range(FOX_HEADS):
            hs = slice(h * hd, (h + 1) * hd)
            acc = acc_ref[h]
            o = acc[:hd] * (1.0 / acc[hd:hd + 1])
            o = o * lax.rsqrt(jnp.mean(o * o, axis=0, keepdims=True) + NORM_EPS) * nwb_ref[hs, :]
            z = zt_ref[hs, :]
            o_ref[hs, :] = (o * (z * _sigmoid(z))).astype(BF16)


def _fox(r16, t16, t32, mcol, nwb):
    _, b, s, _ = r16.shape
    t = ATT_T
    n = s // t
    pairs = [(i, j) for i in range(n) for j in range(i + 1)]
    it = jnp.asarray([p[0] for p in pairs], jnp.int32)
    jt = jnp.asarray([p[1] for p in pairs], jnp.int32)

    grid_spec = pltpu.PrefetchScalarGridSpec(
        num_scalar_prefetch=2,
        grid=(b, len(pairs)),
        in_specs=[
            pl.BlockSpec((None, None, t, FOX_WIDTH), lambda bi, p, it, jt: (0, bi, it[p], 0)),
            pl.BlockSpec((None, None, t, FOX_WIDTH), lambda bi, p, it, jt: (1, bi, jt[p], 0)),
            pl.BlockSpec((None, None, FOX_WIDTH, t), lambda bi, p, it, jt: (0, bi, 0, jt[p])),
            pl.BlockSpec((None, t, LANES), lambda bi, p, it, jt: (bi, jt[p], 0)),
            pl.BlockSpec((None, None, FOX_WIDTH, t), lambda bi, p, it, jt: (0, bi, 0, it[p])),
            pl.BlockSpec((FOX_WIDTH, t), lambda bi, p, it, jt: (0, 0)),
        ],
        out_specs=pl.BlockSpec((None, FOX_WIDTH, t), lambda bi, p, it, jt: (bi, 0, it[p])),
        scratch_shapes=[pltpu.VMEM((FOX_HEADS, t), F32),
                        pltpu.VMEM((FOX_HEADS, FOX_HEAD_DIM + ONES_ROWS, t), F32)],
    )
    return pl.pallas_call(
        _fox_kernel,
        grid_spec=grid_spec,
        out_shape=jax.ShapeDtypeStruct((b, FOX_WIDTH, s), BF16),
        compiler_params=_params(("arbitrary", "arbitrary")),
        name="fox_attention",
    )(it, jt, r16, r16, t16, mcol, t32, nwb)


def _mlstm_kernel(qk_ref, vt_ref, ogt_ref, zt_ref, mcol_ref, mrow_ref, cw_ref, cb_ref, nwb_ref,
                  out_ref, xs_ref, st_ref):
    n = MLSTM_L
    dv = MLSTM_V_DIM
    pad = 8

    @pl.when(pl.program_id(1) == 0)
    def _():
        xs_ref[0:pad, :] = jnp.zeros((pad, SEG), F32)
        st_ref[...] = jnp.zeros_like(st_ref)

    xs_ref[pad:pad + n, :] = qk_ref[...]
    base = pad - (CONV_WIDTH - 1)
    y = xs_ref[base:base + n, :] * cw_ref[0:1, :]
    for jj in range(1, CONV_WIDTH):
        y = y + xs_ref[base + jj:base + jj + n, :] * cw_ref[jj:jj + 1, :]
    y = y + cb_ref[...]
    xs_ref[0:pad, :] = xs_ref[n:n + pad, :]
    sy = y * _sigmoid(y)

    mc = mcol_ref[...]
    mr = mrow_ref[...]
    lane = lax.broadcasted_iota(jnp.int32, (n, LANES), 1)
    ones_rows = jnp.where(lax.broadcasted_iota(jnp.int32, (ONES_ROWS, n), 0) == 0, 1.0, 0.0).astype(BF16)
    valid = lax.broadcasted_iota(jnp.int32, (n, n), 0) <= lax.broadcasted_iota(jnp.int32, (n, n), 1)
    heads = range(MLSTM_HEADS)
    row = lambda base_row, h: mr[base_row + h:base_row + h + 1, :]

    qp, kh, st_t, inter_t, vt_aug = [], [], [], [], []
    for h in heads:
        pr = h // 2
        qp.append(sy[:, pr * LANES:(pr + 1) * LANES].astype(BF16))
        kp = sy[:, MLSTM_QK_WIDTH + pr * LANES:MLSTM_QK_WIDTH + (pr + 1) * LANES] * (MLSTM_QK_DIM ** -0.5)
        kh.append(jnp.where((lane < MLSTM_QK_DIM) if h % 2 == 0 else (lane >= MLSTM_QK_DIM), kp, 0.0))
        st_t.append(lax.dot_general(kh[h].astype(BF16), qp[h], NT_DIMS, preferred_element_type=F32))
        inter_t.append(lax.dot_general(st_ref[h].astype(BF16), qp[h], NT_DIMS, preferred_element_type=F32))
        vt_aug.append(jnp.concatenate([vt_ref[h * dv:(h + 1) * dv, :], ones_rows], axis=0))

    for h in heads:
        hs = slice(h * dv, (h + 1) * dv)
        a_col = mc[:, LANE_A + h:LANE_A + h + 1]
        sc_t = st_t[h] * jnp.exp(jnp.where(valid, a_col + row(ROW_G, h), NEG_INF))
        res = (jnp.dot(vt_aug[h], sc_t.astype(BF16), preferred_element_type=F32)
               + jnp.exp(row(ROW_U, h)) * inter_t[h])
        den = res[dv:dv + 1, :]
        ht = res[:dv] * (1.0 / jnp.maximum(jnp.abs(den), jnp.exp(row(ROW_NEGM, h))))
        hb = ht * _sigmoid(ogt_ref[hs, :])
        hn = hb * lax.rsqrt(jnp.mean(hb * hb, axis=0, keepdims=True) + NORM_EPS) * nwb_ref[hs, :]
        z = zt_ref[hs, :]
        out_ref[hs, :] = (hn * (z * _sigmoid(z))).astype(BF16)

    for h in heads:
        a_col = mc[:, LANE_A + h:LANE_A + h + 1]
        g_last = mc[n - 1:n, LANE_G + h:LANE_G + h + 1]
        u_last = mc[n - 1:n, LANE_U + h:LANE_U + h + 1]
        kw = (kh[h] * jnp.exp(a_col + g_last)).astype(BF16)
        upd = jnp.dot(vt_aug[h], kw, preferred_element_type=F32)
        st_ref[h] = jnp.exp(u_last) * st_ref[h] + upd


def _mlstm(r32, t16, t32, mcol, mrow, cw, cb, nwb):
    b, s, _ = r32.shape
    n = MLSTM_L

    def t_map(which):
        return lambda bi, c: (which, bi, 0, c)

    return pl.pallas_call(
        _mlstm_kernel,
        grid=(b, s // n),
        in_specs=[
            pl.BlockSpec((None, n, SEG), lambda bi, c: (bi, c, 0)),
            pl.BlockSpec((None, None, SEG, n), t_map(1)),
            pl.BlockSpec((None, None, SEG, n), t_map(1)),
            pl.BlockSpec((None, None, SEG, n), t_map(2)),
            pl.BlockSpec((None, n, LANES), lambda bi, c: (bi, c, 0)),
            pl.BlockSpec((None, GATE_ROWS, n), lambda bi, c: (bi, 0, c)),
            pl.BlockSpec((CONV_WIDTH, SEG), lambda bi, c: (0, 0)),
            pl.BlockSpec((1, SEG), lambda bi, c: (0, 0)),
            pl.BlockSpec((SEG, n), lambda bi, c: (0, 0)),
        ],
        out_specs=pl.BlockSpec((None, SEG, n), lambda bi, c: (bi, 0, c)),
        out_shape=jax.ShapeDtypeStruct((b, MLSTM_V_WIDTH, s), BF16),
        scratch_shapes=[pltpu.VMEM((n + 8, SEG), F32),
                        pltpu.VMEM((MLSTM_HEADS, MLSTM_V_DIM + ONES_ROWS, LANES), F32)],
        compiler_params=_params(("arbitrary", "arbitrary")),
        name="mlstm",
    )(r32, t16, t32, t32, mcol, mrow, cw, cb, nwb)


def _outproj_kernel(yat_ref, ybt_ref, wa_ref, wb_ref, x_ref, fw_ref, o_ref, *, final):
    acc = (lax.dot_general(yat_ref[...], wa_ref[...], TN_DIMS, preferred_element_type=F32)
           + lax.dot_general(ybt_ref[...], wb_ref[...], TN_DIMS, preferred_element_type=F32))
    r = x_ref[...] + acc
    if final:
        r = r * lax.rsqrt(jnp.mean(r * r, axis=-1, keepdims=True) + NORM_EPS) * fw_ref[...]
    o_ref[...] = r


def _outproj(yat, ybt, wa, wb, x2, fw, final):
    m = x2.shape[0]
    _, _, seq = yat.shape
    tm = OUT_TM
    per_seq = seq // tm
    y_map = lambda i: (i // per_seq, 0, i % per_seq)
    return pl.pallas_call(
        functools.partial(_outproj_kernel, final=final),
        grid=(m // tm,),
        in_specs=[
            pl.BlockSpec((None, FOX_WIDTH, tm), y_map),
            pl.BlockSpec((None, MLSTM_V_WIDTH, tm), y_map),
            pl.BlockSpec((FOX_WIDTH, D_MODEL), lambda i: (0, 0)),
            pl.BlockSpec((MLSTM_V_WIDTH, D_MODEL), lambda i: (0, 0)),
            pl.BlockSpec((tm, D_MODEL), lambda i: (i, 0)),
            pl.BlockSpec((1, D_MODEL), lambda i: (0, 0)),
        ],
        out_specs=pl.BlockSpec((tm, D_MODEL), lambda i: (i, 0)),
        out_shape=jax.ShapeDtypeStruct((m, D_MODEL), F32),
        compiler_params=_params(("arbitrary",)),
        name="outproj",
    )(yat, ybt, wa, wb, x2, fw)


def _reorder_in_weight(w):
    o = IN_OFFS
    seg = lambda k: w[:, o[k]:o[k + 1]]
    w_t = jnp.concatenate([seg(0), seg(1), seg(5), seg(6), seg(2), seg(7), seg(3), seg(8), seg(9)],
                          axis=1).astype(BF16).T
    zeros = lambda n: jnp.zeros((w.shape[0], n), w.dtype)
    w_gate = jnp.concatenate([seg(4), seg(11), zeros(LANES - 16), zeros(8), seg(10), zeros(LANES - 16)],
                             axis=1).astype(BF16)
    return w_t, w_gate


def _lane_broadcast(v, width):
    return jnp.broadcast_to(v[:, None], (v.shape[0], width))


def kernel(x, norm_w, w_in, fox_f_bias, conv_w, conv_b, mlstm_i_bias, mlstm_f_bias,
           fox_out_norm_w, mlstm_out_norm_w, w_out, final_norm_w):
    b, s, d = x.shape
    depth = norm_w.shape[0]
    x2 = x.reshape(b * s, d)
    for l in range(depth):
        w_t, w_gate = _reorder_in_weight(w_in[l])
        bias_vec = jnp.concatenate([
            fox_f_bias[l], mlstm_f_bias[l], jnp.zeros((LANES - 16,), F32),
            jnp.zeros((8,), F32), mlstm_i_bias[l], jnp.zeros((LANES - 16,), F32)]).reshape(1, 2 * LANES)

        r16, r32, t16, t32, g = _inproj(x2, norm_w[l].reshape(1, d), w_t, w_gate, s)
        r16 = r16.reshape(2, b, s, SEG)
        mcol, mrow = _gates(g.reshape(b, s, 2 * LANES), bias_vec)

        yat = _fox(r16, t16, t32, mcol, _lane_broadcast(fox_out_norm_w[l], ATT_T))
        ybt = _mlstm(r32.reshape(b, s, SEG), t16, t32, mcol, mrow, conv_w[l], conv_b[l].reshape(1, SEG),
                     _lane_broadcast(mlstm_out_norm_w[l], MLSTM_L))

        wo = w_out[l].astype(BF16)
        x2 = _outproj(yat, ybt, wo[:FOX_WIDTH], wo[FOX_WIDTH:], x2, final_norm_w.reshape(1, d),
                      final=(l == depth - 1))
    return x2.reshape(b, s, d)
```

```python
import functools

import numpy as np
import jax
import jax.numpy as jnp
from jax import lax
from jax.experimental import pallas as pl
from jax.experimental.pallas import tpu as pltpu

D_MODEL = 2048
NORM_EPS = 1e-6
NEG_INF = -1e30

FOX_HEADS = 8
FOX_HEAD_DIM = 128
FOX_WIDTH = FOX_HEADS * FOX_HEAD_DIM
MLSTM_HEADS = 8
MLSTM_V_DIM = 128
MLSTM_QK_DIM = 64
MLSTM_V_WIDTH = MLSTM_HEADS * MLSTM_V_DIM
MLSTM_QK_WIDTH = MLSTM_HEADS * MLSTM_QK_DIM
CONV_WIDTH = 4
IN_SIZES = (FOX_WIDTH, FOX_WIDTH, FOX_WIDTH, FOX_WIDTH, FOX_HEADS,
            MLSTM_QK_WIDTH, MLSTM_QK_WIDTH, MLSTM_V_WIDTH, MLSTM_V_WIDTH, MLSTM_V_WIDTH,
            MLSTM_HEADS, MLSTM_HEADS)
IN_OFFS = tuple(int(v) for v in np.cumsum((0,) + IN_SIZES))

LANES = 128
SEG = 1024
VMEM_LIMIT = 56 * 1024 * 1024
INPROJ_VMEM_LIMIT = 62 * 1024 * 1024

PROJ_TM = 1024
GATE_R = 512
MLSTM_L = 256
ATT_T = 512
OUT_TM = 512

LOG2E = 1.4426950408889634
FOX_Q_SCALE = FOX_HEAD_DIM ** -0.5 * LOG2E
ONES_ROWS = 16
ATT_LOOKAHEAD = 1

LANE_G, LANE_U, LANE_A, LANE_NEGM = 8, 16, 24, 32
ROW_G, ROW_U, ROW_NEGM = 0, 8, 24
GATE_ROWS = 32

BF16 = jnp.bfloat16
F32 = jnp.float32
NT_DIMS = (((1,), (1,)), ((), ()))
TN_DIMS = (((0,), (0,)), ((), ()))


def _sigmoid(x):
    return 1.0 / (1.0 + jnp.exp(-x))


def _params(sem):
    return pltpu.CompilerParams(dimension_semantics=sem, vmem_limit_bytes=VMEM_LIMIT)


def _inproj_kernel(x_ref, nw_ref, wt_ref, wg_ref, r16_ref, r32_ref, t16_ref, t32_ref, g_ref, hn_ref):
    j = pl.program_id(1)

    @pl.when(j == 0)
    def _():
        xf = x_ref[...]
        ms = jnp.mean(xf * xf, axis=-1, keepdims=True)
        hb = (xf * lax.rsqrt(ms + NORM_EPS) * nw_ref[...]).astype(BF16)
        hn_ref[...] = hb
        g_ref[...] = jnp.dot(hb, wg_ref[...], preferred_element_type=F32)

    @pl.when(j <= 2)
    def _():
        acc = lax.dot_general(hn_ref[...], wt_ref[...], NT_DIMS, preferred_element_type=F32)

        @pl.when(j == 0)
        def _():
            r16_ref[...] = (acc * FOX_Q_SCALE).astype(BF16)

        @pl.when(j == 1)
        def _():
            r16_ref[...] = acc.astype(BF16)

        @pl.when(j == 2)
        def _():
            r32_ref[...] = acc

    @pl.when(j >= 3)
    def _():
        acc_t = lax.dot_general(wt_ref[...], hn_ref[...], NT_DIMS, preferred_element_type=F32)

        @pl.when(j <= 4)
        def _():
            t16_ref[...] = acc_t.astype(BF16)

        @pl.when(j >= 5)
        def _():
            t32_ref[...] = acc_t


def _inproj(x2, nw, w_t, w_gate, seq):
    m = x2.shape[0]
    tm = PROJ_TM
    per_seq = seq // tm
    t_map = lambda lo, hi: (lambda i, j: (jnp.clip(j - lo, 0, hi), i // per_seq, 0, i % per_seq))
    return pl.pallas_call(
        _inproj_kernel,
        grid=(m // tm, 8),
        in_specs=[
            pl.BlockSpec((tm, D_MODEL), lambda i, j: (i, 0)),
            pl.BlockSpec((1, D_MODEL), lambda i, j: (0, 0)),
            pl.BlockSpec((SEG, D_MODEL), lambda i, j: (j, 0)),
            pl.BlockSpec((D_MODEL, 2 * LANES), lambda i, j: (0, 0)),
        ],
        out_specs=[
            pl.BlockSpec((None, tm, SEG), lambda i, j: (jnp.minimum(j, 1), i, 0)),
            pl.BlockSpec((tm, SEG), lambda i, j: (i, 0)),
            pl.BlockSpec((None, None, SEG, tm), t_map(3, 1)),
            pl.BlockSpec((None, None, SEG, tm), t_map(5, 2)),
            pl.BlockSpec((tm, 2 * LANES), lambda i, j: (i, 0)),
        ],
        out_shape=[
            jax.ShapeDtypeStruct((2, m, SEG), BF16),
            jax.ShapeDtypeStruct((m, SEG), F32),
            jax.ShapeDtypeStruct((2, m // seq, SEG, seq), BF16),
            jax.ShapeDtypeStruct((3, m // seq, SEG, seq), F32),
            jax.ShapeDtypeStruct((m, 2 * LANES), F32),
        ],
        scratch_shapes=[pltpu.VMEM((tm, D_MODEL), BF16)],
        compiler_params=pltpu.CompilerParams(dimension_semantics=("arbitrary", "arbitrary"),
                                             vmem_limit_bytes=INPROJ_VMEM_LIMIT),
        name="inproj",
    )(x2, nw, w_t, w_gate)


def _split3(v):
    hi = v.astype(BF16)
    r1 = v - hi.astype(F32)
    mid = r1.astype(BF16)
    lo = (r1 - mid.astype(F32)).astype(BF16)
    return hi, mid, lo


def _gates_kernel(g_ref, b_ref, mcol_ref, mrow_ref, carry_ref, mprev_ref):
    rows, chunk = GATE_R, MLSTM_L

    @pl.when(pl.program_id(1) == 0)
    def _():
        carry_ref[...] = jnp.zeros_like(carry_ref)
        mprev_ref[...] = jnp.zeros_like(mprev_ref)

    v = g_ref[...] + b_ref[...]
    v1 = v[:, :LANES]
    v2 = v[:, LANES:]
    ls = jnp.minimum(v1, 0.0) - jnp.log1p(jnp.exp(-jnp.abs(v1)))

    r_i = lax.broadcasted_iota(jnp.int32, (rows, rows), 0)
    c_i = lax.broadcasted_iota(jnp.int32, (rows, rows), 1)
    tri = c_i <= r_i
    t_full = tri.astype(BF16)
    t_loc = (tri & ((r_i // chunk) == (c_i // chunk))).astype(BF16)
    hi, mid, lo = _split3(ls)

    def cum(t):
        return (jnp.dot(t, hi, preferred_element_type=F32)
                + jnp.dot(t, mid, preferred_element_type=F32)
                + jnp.dot(t, lo, preferred_element_type=F32))

    cumf = cum(t_full) + carry_ref[...]
    cuml = cum(t_loc)
    carry_ref[...] = cumf[rows - 1:rows, :]

    a = v2 - cuml
    t_in = lax.broadcasted_iota(jnp.int32, (rows, LANES), 0) % chunk
    cm = a
    s = 1
    while s < chunk:
        cm = jnp.where(t_in >= s, jnp.maximum(cm, pltpu.roll(cm, s, axis=0)), cm)
        s *= 2

    lane = lax.broadcasted_iota(jnp.int32, (chunk, LANES), 1)
    mprev = mprev_ref[...]
    for c in range(rows // chunk):
        sl = slice(c * chunk, (c + 1) * chunk)
        b = cuml[sl]
        mt = b + jnp.maximum(mprev, cm[sl])
        gt = b - mt
        ut = (b + mprev) - mt
        tile = jnp.where(lane < LANE_G, cumf[sl] * LOG2E,
               jnp.where(lane < LANE_U, gt,
               jnp.where(lane < LANE_A, pltpu.roll(ut, LANE_U - LANE_G, axis=1),
               jnp.where(lane < LANE_NEGM, pltpu.roll(a[sl], LANE_A - LANE_G, axis=1),
                         pltpu.roll(-mt, LANE_NEGM - LANE_G, axis=1)))))
        mprev = mt[chunk - 1:chunk, :]
        mcol_ref[sl, :] = tile
        mrow_ref[:, sl] = tile.T[LANE_G:LANE_G + GATE_ROWS, :]
    mprev_ref[...] = mprev


def _gates(g3, bias_vec):
    b, s, _ = g3.shape
    r = GATE_R
    return pl.pallas_call(
        _gates_kernel,
        grid=(b, s // r),
        in_specs=[
            pl.BlockSpec((None, r, 2 * LANES), lambda bi, ri: (bi, ri, 0)),
            pl.BlockSpec((1, 2 * LANES), lambda bi, ri: (0, 0)),
        ],
        out_specs=[
            pl.BlockSpec((None, r, LANES), lambda bi, ri: (bi, ri, 0)),
            pl.BlockSpec((None, GATE_ROWS, r), lambda bi, ri: (bi, 0, ri)),
        ],
        out_shape=[
            jax.ShapeDtypeStruct((b, s, LANES), F32),
            jax.ShapeDtypeStruct((b, GATE_ROWS, s), F32),
        ],
        scratch_shapes=[pltpu.VMEM((1, LANES), F32), pltpu.VMEM((1, LANES), F32)],
        compiler_params=_params(("arbitrary", "arbitrary")),
        name="gates",
    )(g3, bias_vec)


def _fox_kernel(it_ref, jt_ref, q_ref, k_ref, vt_ref, gc_ref, zt_ref, nwb_ref, o_ref, m_ref, acc_ref):
    step_id = pl.program_id(1)
    i = it_ref[step_id]
    j = jt_ref[step_id]
    t = ATT_T
    hd = FOX_HEAD_DIM

    @pl.when(j == 0)
    def _():
        m_ref[...] = jnp.full_like(m_ref, NEG_INF)
        acc_ref[...] = jnp.zeros_like(acc_ref)

    def step(masked):
        gc = gc_ref[...]
        ones = jnp.ones((ONES_ROWS, t), BF16)
        if masked:
            valid = (lax.broadcasted_iota(jnp.int32, (t, t), 0)
                     <= lax.broadcasted_iota(jnp.int32, (t, t), 1))

        def scores(h):
            hs = slice(h * hd, (h + 1) * hd)
            st = lax.dot_general(k_ref[:, hs], q_ref[:, hs], NT_DIMS,
                                 preferred_element_type=F32) - gc[:, h:h + 1]
            return jnp.where(valid, st, NEG_INF) if masked else st

        def probs(h, st):
            m_prev = m_ref[h:h + 1, :]
            m_new = jnp.maximum(m_prev, jnp.max(st, axis=0, keepdims=True))
            m_ref[h:h + 1, :] = m_new
            return jnp.exp2(m_prev - m_new), jnp.exp2(st - m_new).astype(BF16)

        def accumulate(h, alpha, pt):
            vt_aug = jnp.concatenate([vt_ref[h * hd:(h + 1) * hd, :], ones], axis=0)
            acc_ref[h] = alpha * acc_ref[h] + jnp.dot(vt_aug, pt, preferred_element_type=F32)

        st, ap = {}, {}
        for h in range(-ATT_LOOKAHEAD, FOX_HEADS + 1):
            if 0 <= h + ATT_LOOKAHEAD < FOX_HEADS:
                st[h + ATT_LOOKAHEAD] = scores(h + ATT_LOOKAHEAD)
            if 0 <= h < FOX_HEADS:
                ap[h] = probs(h, st.pop(h))
            if 0 <= h - 1 < FOX_HEADS:
                accumulate(h - 1, *ap.pop(h - 1))

    @pl.when(j < i)
    def _():
        step(False)

    @pl.when(j == i)
    def _():
        step(True)
        for h in range(FOX_HEADS):
            hs = slice(h * hd, (h + 1) * hd)
            acc = acc_ref[h]
            o = acc[:hd] * (1.0 / acc[hd:hd + 1])
            o = o * lax.rsqrt(jnp.mean(o * o, axis=0, keepdims=True) + NORM_EPS) * nwb_ref[hs, :]
            z = zt_ref[hs, :]
            o_ref[hs, :] = (o * (z * _sigmoid(z))).astype(BF16)


def _fox(r16, t16, t32, mcol, nwb):
    _, b, s, _ = r16.shape
    t = ATT_T
    n = s // t
    pairs = [(i, j) for i in range(n) for j in range(i + 1)]
    it = jnp.asarray([p[0] for p in pairs], jnp.int32)
    jt = jnp.asarray([p[1] for p in pairs], jnp.int32)

    grid_spec = pltpu.PrefetchScalarGridSpec(
        num_scalar_prefetch=2,
        grid=(b, len(pairs)),
        in_specs=[
            pl.BlockSpec((None, None, t, FOX_WIDTH), lambda bi, p, it, jt: (0, bi, it[p], 0)),
            pl.BlockSpec((None, None, t, FOX_WIDTH), lambda bi, p, it, jt: (1, bi, jt[p], 0)),
            pl.BlockSpec((None, None, FOX_WIDTH, t), lambda bi, p, it, jt: (0, bi, 0, jt[p])),
            pl.BlockSpec((None, t, LANES), lambda bi, p, it, jt: (bi, jt[p], 0)),
            pl.BlockSpec((None, None, FOX_WIDTH, t), lambda bi, p, it, jt: (0, bi, 0, it[p])),
            pl.BlockSpec((FOX_WIDTH, t), lambda bi, p, it, jt: (0, 0)),
        ],
        out_specs=pl.BlockSpec((None, FOX_WIDTH, t), lambda bi, p, it, jt: (bi, 0, it[p])),
        scratch_shapes=[pltpu.VMEM((FOX_HEADS, t), F32),
                        pltpu.VMEM((FOX_HEADS, FOX_HEAD_DIM + ONES_ROWS, t), F32)],
    )
    return pl.pallas_call(
        _fox_kernel,
        grid_spec=grid_spec,
        out_shape=jax.ShapeDtypeStruct((b, FOX_WIDTH, s), BF16),
        compiler_params=_params(("arbitrary", "arbitrary")),
        name="fox_attention",
    )(it, jt, r16, r16, t16, mcol, t32, nwb)


def _mlstm_kernel(qk_ref, vt_ref, ogt_ref, zt_ref, mcol_ref, mrow_ref, cw_ref, cb_ref, nwb_ref,
                  out_ref, xs_ref, st_ref):
    n = MLSTM_L
    dv = MLSTM_V_DIM
    pad = 8

    @pl.when(pl.program_id(1) == 0)
    def _():
        xs_ref[0:pad, :] = jnp.zeros((pad, SEG), F32)
        st_ref[...] = jnp.zeros_like(st_ref)

    xs_ref[pad:pad + n, :] = qk_ref[...]
    base = pad - (CONV_WIDTH - 1)
    y = xs_ref[base:base + n, :] * cw_ref[0:1, :]
    for jj in range(1, CONV_WIDTH):
        y = y + xs_ref[base + jj:base + jj + n, :] * cw_ref[jj:jj + 1, :]
    y = y + cb_ref[...]
    xs_ref[0:pad, :] = xs_ref[n:n + pad, :]
    sy = y * _sigmoid(y)

    mc = mcol_ref[...]
    mr = mrow_ref[...]
    lane = lax.broadcasted_iota(jnp.int32, (n, LANES), 1)
    ones_rows = jnp.where(lax.broadcasted_iota(jnp.int32, (ONES_ROWS, n), 0) == 0, 1.0, 0.0).astype(BF16)
    valid = lax.broadcasted_iota(jnp.int32, (n, n), 0) <= lax.broadcasted_iota(jnp.int32, (n, n), 1)
    heads = range(MLSTM_HEADS)
    row = lambda base_row, h: mr[base_row + h:base_row + h + 1, :]

    qp, kh, st_t, inter_t, vt_aug = [], [], [], [], []
    for h in heads:
        pr = h // 2
        qp.append(sy[:, pr * LANES:(pr + 1) * LANES].astype(BF16))
        kp = sy[:, MLSTM_QK_WIDTH + pr * LANES:MLSTM_QK_WIDTH + (pr + 1) * LANES] * (MLSTM_QK_DIM ** -0.5)
        kh.append(jnp.where((lane < MLSTM_QK_DIM) if h % 2 == 0 else (lane >= MLSTM_QK_DIM), kp, 0.0))
        st_t.append(lax.dot_general(kh[h].astype(BF16), qp[h], NT_DIMS, preferred_element_type=F32))
        inter_t.append(lax.dot_general(st_ref[h].astype(BF16), qp[h], NT_DIMS, preferred_element_type=F32))
        vt_aug.append(jnp.concatenate([vt_ref[h * dv:(h + 1) * dv, :], ones_rows], axis=0))

    for h in heads:
        hs = slice(h * dv, (h + 1) * dv)
        a_col = mc[:, LANE_A + h:LANE_A + h + 1]
        sc_t = st_t[h] * jnp.exp(jnp.where(valid, a_col + row(ROW_G, h), NEG_INF))
        res = (jnp.dot(vt_aug[h], sc_t.astype(BF16), preferred_element_type=F32)
               + jnp.exp(row(ROW_U, h)) * inter_t[h])
        den = res[dv:dv + 1, :]
        ht = res[:dv] * (1.0 / jnp.maximum(jnp.abs(den), jnp.exp(row(ROW_NEGM, h))))
        hb = ht * _sigmoid(ogt_ref[hs, :])
        hn = hb * lax.rsqrt(jnp.mean(hb * hb, axis=0, keepdims=True) + NORM_EPS) * nwb_ref[hs, :]
        z = zt_ref[hs, :]
        out_ref[hs, :] = (hn * (z * _sigmoid(z))).astype(BF16)

    for h in heads:
        a_col = mc[:, LANE_A + h:LANE_A + h + 1]
        g_last = mc[n - 1:n, LANE_G + h:LANE_G + h + 1]
        u_last = mc[n - 1:n, LANE_U + h:LANE_U + h + 1]
        kw = (kh[h] * jnp.exp(a_col + g_last)).astype(BF16)
        upd = jnp.dot(vt_aug[h], kw, preferred_element_type=F32)
        st_ref[h] = jnp.exp(u_last) * st_ref[h] + upd


def _mlstm(r32, t16, t32, mcol, mrow, cw, cb, nwb):
    b, s, _ = r32.shape
    n = MLSTM_L

    def t_map(which):
        return lambda bi, c: (which, bi, 0, c)

    return pl.pallas_call(
        _mlstm_kernel,
        grid=(b, s // n),
        in_specs=[
            pl.BlockSpec((None, n, SEG), lambda bi, c: (bi, c, 0)),
            pl.BlockSpec((None, None, SEG, n), t_map(1)),
            pl.BlockSpec((None, None, SEG, n), t_map(1)),
            pl.BlockSpec((None, None, SEG, n), t_map(2)),
            pl.BlockSpec((None, n, LANES), lambda bi, c: (bi, c, 0)),
            pl.BlockSpec((None, GATE_ROWS, n), lambda bi, c: (bi, 0, c)),
            pl.BlockSpec((CONV_WIDTH, SEG), lambda bi, c: (0, 0)),
            pl.BlockSpec((1, SEG), lambda bi, c: (0, 0)),
            pl.BlockSpec((SEG, n), lambda bi, c: (0, 0)),
        ],
        out_specs=pl.BlockSpec((None, SEG, n), lambda bi, c: (bi, 0, c)),
        out_shape=jax.ShapeDtypeStruct((b, MLSTM_V_WIDTH, s), BF16),
        scratch_shapes=[pltpu.VMEM((n + 8, SEG), F32),
                        pltpu.VMEM((MLSTM_HEADS, MLSTM_V_DIM + ONES_ROWS, LANES), F32)],
        compiler_params=_params(("arbitrary", "arbitrary")),
        name="mlstm",
    )(r32, t16, t32, t32, mcol, mrow, cw, cb, nwb)


def _outproj_kernel(yat_ref, ybt_ref, wa_ref, wb_ref, x_ref, fw_ref, o_ref, *, final):
    acc = (lax.dot_general(yat_ref[...], wa_ref[...], TN_DIMS, preferred_element_type=F32)
           + lax.dot_general(ybt_ref[...], wb_ref[...], TN_DIMS, preferred_element_type=F32))
    r = x_ref[...] + acc
    if final:
        r = r * lax.rsqrt(jnp.mean(r * r, axis=-1, keepdims=True) + NORM_EPS) * fw_ref[...]
    o_ref[...] = r


def _outproj(yat, ybt, wa, wb, x2, fw, final):
    m = x2.shape[0]
    _, _, seq = yat.shape
    tm = OUT_TM
    per_seq = seq // tm
    y_map = lambda i: (i // per_seq, 0, i % per_seq)
    return pl.pallas_call(
        functools.partial(_outproj_kernel, final=final),
        grid=(m // tm,),
        in_specs=[
            pl.BlockSpec((None, FOX_WIDTH, tm), y_map),
            pl.BlockSpec((None, MLSTM_V_WIDTH, tm), y_map),
            pl.BlockSpec((FOX_WIDTH, D_MODEL), lambda i: (0, 0)),
            pl.BlockSpec((MLSTM_V_WIDTH, D_MODEL), lambda i: (0, 0)),
            pl.BlockSpec((tm, D_MODEL), lambda i: (i, 0)),
            pl.BlockSpec((1, D_MODEL), lambda i: (0, 0)),
        ],
        out_specs=pl.BlockSpec((tm, D_MODEL), lambda i: (i, 0)),
        out_shape=jax.ShapeDtypeStruct((m, D_MODEL), F32),
        compiler_params=_params(("arbitrary",)),
        name="outproj",
    )(yat, ybt, wa, wb, x2, fw)


SEG_STARTS = (IN_OFFS[0], IN_OFFS[1], IN_OFFS[5], IN_OFFS[2], IN_OFFS[7], IN_OFFS[3], IN_OFFS[8], IN_OFFS[9])
SEG_SHIFT = FOX_HEADS
assert all(st % SEG in (0, SEG_SHIFT) for st in SEG_STARTS) and IN_OFFS[6] == IN_OFFS[5] + MLSTM_QK_WIDTH
WPREP_K = 1024


def _wprep_kernel(main_blk_ref, extra_blk_ref, main_ref, extra_ref, o_ref):
    del main_blk_ref, extra_blk_ref
    j = pl.program_id(0)
    shifted = functools.reduce(jnp.logical_or, [j == k for k, st in enumerate(SEG_STARTS) if st % SEG])

    @pl.when(jnp.logical_not(shifted))
    def _():
        o_ref[...] = main_ref[...].T.astype(BF16)

    @pl.when(shifted)
    def _():
        full = jnp.concatenate([main_ref[...], extra_ref[...]], axis=1).T
        o_ref[...] = full[SEG_SHIFT:SEG_SHIFT + SEG, :].astype(BF16)


def _wprep(w):
    d = w.shape[0]
    main_blk = jnp.asarray([st // SEG for st in SEG_STARTS], jnp.int32)
    extra_blk = jnp.asarray([(st // SEG + 1) * (SEG // LANES) for st in SEG_STARTS], jnp.int32)
    grid_spec = pltpu.PrefetchScalarGridSpec(
        num_scalar_prefetch=2,
        grid=(len(SEG_STARTS), d // WPREP_K),
        in_specs=[
            pl.BlockSpec((WPREP_K, SEG), lambda j, k, mb, eb: (k, mb[j])),
            pl.BlockSpec((WPREP_K, LANES), lambda j, k, mb, eb: (k, eb[j])),
        ],
        out_specs=pl.BlockSpec((SEG, WPREP_K), lambda j, k, mb, eb: (j, k)),
    )
    return pl.pallas_call(
        _wprep_kernel,
        grid_spec=grid_spec,
        out_shape=jax.ShapeDtypeStruct((len(SEG_STARTS) * SEG, d), BF16),
        compiler_params=_params(("arbitrary", "arbitrary")),
        name="weight_layout",
    )(main_blk, extra_blk, w, w)


def _gate_weight(w):
    o = IN_OFFS
    seg = lambda k: w[:, o[k]:o[k + 1]]
    zeros = lambda n: jnp.zeros((w.shape[0], n), w.dtype)
    return jnp.concatenate([seg(4), seg(11), zeros(LANES - 16), zeros(8), seg(10), zeros(LANES - 16)],
                           axis=1).astype(BF16)


def _lane_broadcast(v, width):
    return jnp.broadcast_to(v[:, None], (v.shape[0], width))


def kernel(x, norm_w, w_in, fox_f_bias, conv_w, conv_b, mlstm_i_bias, mlstm_f_bias,
           fox_out_norm_w, mlstm_out_norm_w, w_out, final_norm_w):
    b, s, d = x.shape
    depth = norm_w.shape[0]
    x2 = x.reshape(b * s, d)
    for l in range(depth):
        w_t, w_gate = _wprep(w_in[l]), _gate_weight(w_in[l])
        bias_vec = jnp.concatenate([
            fox_f_bias[l], mlstm_f_bias[l], jnp.zeros((LANES - 16,), F32),
            jnp.zeros((8,), F32), mlstm_i_bias[l], jnp.zeros((LANES - 16,), F32)]).reshape(1, 2 * LANES)

        r16, r32, t16, t32, g = _inproj(x2, norm_w[l].reshape(1, d), w_t, w_gate, s)
        r16 = r16.reshape(2, b, s, SEG)
        mcol, mrow = _gates(g.reshape(b, s, 2 * LANES), bias_vec)

        yat = _fox(r16, t16, t32, mcol, _lane_broadcast(fox_out_norm_w[l], ATT_T))
        ybt = _mlstm(r32.reshape(b, s, SEG), t16, t32, mcol, mrow, conv_w[l], conv_b[l].reshape(1, SEG),
                     _lane_broadcast(mlstm_out_norm_w[l], MLSTM_L))

        wo = w_out[l].astype(BF16)
        x2 = _outproj(yat, ybt, wo[:FOX_WIDTH], wo[FOX_WIDTH:], x2, final_norm_w.reshape(1, d),
                      final=(l == depth - 1))
    return x2.reshape(b, s, d)
```

```python
import functools

import numpy as np
import jax
import jax.numpy as jnp
from jax import lax
from jax.experimental import pallas as pl
from jax.experimental.pallas import tpu as pltpu

D_MODEL = 2048
NORM_EPS = 1e-6
NEG_INF = -1e30

FOX_HEADS = 8
FOX_HEAD_DIM = 128
FOX_WIDTH = FOX_HEADS * FOX_HEAD_DIM
MLSTM_HEADS = 8
MLSTM_V_DIM = 128
MLSTM_QK_DIM = 64
MLSTM_V_WIDTH = MLSTM_HEADS * MLSTM_V_DIM
MLSTM_QK_WIDTH = MLSTM_HEADS * MLSTM_QK_DIM
CONV_WIDTH = 4
IN_SIZES = (FOX_WIDTH, FOX_WIDTH, FOX_WIDTH, FOX_WIDTH, FOX_HEADS,
            MLSTM_QK_WIDTH, MLSTM_QK_WIDTH, MLSTM_V_WIDTH, MLSTM_V_WIDTH, MLSTM_V_WIDTH,
            MLSTM_HEADS, MLSTM_HEADS)
IN_OFFS = tuple(int(v) for v in np.cumsum((0,) + IN_SIZES))

LANES = 128
SEG = 1024
VMEM_LIMIT = 56 * 1024 * 1024
INPROJ_VMEM_LIMIT = 62 * 1024 * 1024

PROJ_TM = 1024
GATE_R = 1024
MLSTM_L = 256
ATT_T = 512
OUT_TM = 512

LOG2E = 1.4426950408889634
FOX_Q_SCALE = FOX_HEAD_DIM ** -0.5 * LOG2E
ONES_ROWS = 16
ATT_LOOKAHEAD = 1

LANE_G, LANE_U, LANE_A, LANE_NEGM = 8, 16, 24, 32
ROW_G, ROW_U, ROW_NEGM = 0, 8, 24
GATE_ROWS = 32

BF16 = jnp.bfloat16
F32 = jnp.float32
NT_DIMS = (((1,), (1,)), ((), ()))
TN_DIMS = (((0,), (0,)), ((), ()))


def _sigmoid(x):
    return 1.0 / (1.0 + jnp.exp(-x))


def _params(sem):
    return pltpu.CompilerParams(dimension_semantics=sem, vmem_limit_bytes=VMEM_LIMIT)


def _inproj_kernel(x_ref, nw_ref, wt_ref, wg_ref, r16_ref, r32_ref, t16_ref, t32_ref, g_ref, hn_ref):
    j = pl.program_id(1)

    @pl.when(j == 0)
    def _():
        xf = x_ref[...]
        ms = jnp.mean(xf * xf, axis=-1, keepdims=True)
        hb = (xf * lax.rsqrt(ms + NORM_EPS) * nw_ref[...]).astype(BF16)
        hn_ref[...] = hb
        g_ref[...] = lax.dot_general(hb, wg_ref[...], NT_DIMS, preferred_element_type=F32)

    @pl.when(j <= 2)
    def _():
        acc = lax.dot_general(hn_ref[...], wt_ref[...], NT_DIMS, preferred_element_type=F32)

        @pl.when(j == 0)
        def _():
            r16_ref[...] = (acc * FOX_Q_SCALE).astype(BF16)

        @pl.when(j == 1)
        def _():
            r16_ref[...] = acc.astype(BF16)

        @pl.when(j == 2)
        def _():
            r32_ref[...] = acc

    @pl.when(j >= 3)
    def _():
        acc_t = lax.dot_general(wt_ref[...], hn_ref[...], NT_DIMS, preferred_element_type=F32)

        @pl.when(j <= 4)
        def _():
            t16_ref[...] = acc_t.astype(BF16)

        @pl.when(j >= 5)
        def _():
            t32_ref[...] = acc_t


def _inproj(x2, nw, w_t, w_gate, seq):
    m = x2.shape[0]
    tm = PROJ_TM
    per_seq = seq // tm
    t_map = lambda lo, hi: (lambda i, j: (jnp.clip(j - lo, 0, hi), i // per_seq, 0, i % per_seq))
    return pl.pallas_call(
        _inproj_kernel,
        grid=(m // tm, 8),
        in_specs=[
            pl.BlockSpec((tm, D_MODEL), lambda i, j: (i, 0)),
            pl.BlockSpec((1, D_MODEL), lambda i, j: (0, 0)),
            pl.BlockSpec((SEG, D_MODEL), lambda i, j: (j, 0)),
            pl.BlockSpec((2 * LANES, D_MODEL), lambda i, j: (0, 0)),
        ],
        out_specs=[
            pl.BlockSpec((None, tm, SEG), lambda i, j: (jnp.minimum(j, 1), i, 0)),
            pl.BlockSpec((tm, SEG), lambda i, j: (i, 0)),
            pl.BlockSpec((None, None, SEG, tm), t_map(3, 1)),
            pl.BlockSpec((None, None, SEG, tm), t_map(5, 2)),
            pl.BlockSpec((tm, 2 * LANES), lambda i, j: (i, 0)),
        ],
        out_shape=[
            jax.ShapeDtypeStruct((2, m, SEG), BF16),
            jax.ShapeDtypeStruct((m, SEG), F32),
            jax.ShapeDtypeStruct((2, m // seq, SEG, seq), BF16),
            jax.ShapeDtypeStruct((3, m // seq, SEG, seq), F32),
            jax.ShapeDtypeStruct((m, 2 * LANES), F32),
        ],
        scratch_shapes=[pltpu.VMEM((tm, D_MODEL), BF16)],
        compiler_params=pltpu.CompilerParams(dimension_semantics=("arbitrary", "arbitrary"),
                                             vmem_limit_bytes=INPROJ_VMEM_LIMIT),
        name="inproj",
    )(x2, nw, w_t, w_gate)


def _split3(v):
    hi = v.astype(BF16)
    r1 = v - hi.astype(F32)
    mid = r1.astype(BF16)
    lo = (r1 - mid.astype(F32)).astype(BF16)
    return hi, mid, lo


def _gates_kernel(g_ref, b_ref, mcol_ref, mrow_ref, carry_ref, mprev_ref):
    rows, chunk = GATE_R, MLSTM_L

    @pl.when(pl.program_id(1) == 0)
    def _():
        carry_ref[...] = jnp.zeros_like(carry_ref)
        mprev_ref[...] = jnp.zeros_like(mprev_ref)

    v = g_ref[...] + b_ref[...]
    v1 = v[:, :LANES]
    v2 = v[:, LANES:]
    ls = jnp.minimum(v1, 0.0) - jnp.log1p(jnp.exp(-jnp.abs(v1)))

    tri = (lax.broadcasted_iota(jnp.int32, (chunk, chunk), 1)
           <= lax.broadcasted_iota(jnp.int32, (chunk, chunk), 0)).astype(BF16)
    hi, mid, lo = _split3(ls)
    cuml = jnp.concatenate([
        jnp.dot(tri, hi[c * chunk:(c + 1) * chunk], preferred_element_type=F32)
        + jnp.dot(tri, mid[c * chunk:(c + 1) * chunk], preferred_element_type=F32)
        + jnp.dot(tri, lo[c * chunk:(c + 1) * chunk], preferred_element_type=F32)
        for c in range(rows // chunk)], axis=0)
    carry = carry_ref[...]
    cumf = []
    for c in range(rows // chunk):
        cumf.append(cuml[c * chunk:(c + 1) * chunk] + carry)
        carry = cumf[c][chunk - 1:chunk, :]
    carry_ref[...] = carry
    cumf = jnp.concatenate(cumf, axis=0)

    a = v2 - cuml
    t_in = lax.broadcasted_iota(jnp.int32, (rows, LANES), 0) % chunk
    cm = a
    s = 1
    while s < chunk:
        cm = jnp.where(t_in >= s, jnp.maximum(cm, pltpu.roll(cm, s, axis=0)), cm)
        s *= 2

    lane = lax.broadcasted_iota(jnp.int32, (chunk, LANES), 1)
    mprev = mprev_ref[...]
    for c in range(rows // chunk):
        sl = slice(c * chunk, (c + 1) * chunk)
        b = cuml[sl]
        mt = b + jnp.maximum(mprev, cm[sl])
        gt = b - mt
        ut = (b + mprev) - mt
        tile = jnp.where(lane < LANE_G, cumf[sl] * LOG2E,
               jnp.where(lane < LANE_U, gt,
               jnp.where(lane < LANE_A, pltpu.roll(ut, LANE_U - LANE_G, axis=1),
               jnp.where(lane < LANE_NEGM, pltpu.roll(a[sl], LANE_A - LANE_G, axis=1),
                         pltpu.roll(-mt, LANE_NEGM - LANE_G, axis=1)))))
        mprev = mt[chunk - 1:chunk, :]
        mcol_ref[sl, :] = tile
        mrow_ref[:, sl] = tile.T[LANE_G:LANE_G + GATE_ROWS, :]
    mprev_ref[...] = mprev


def _gates(g3, bias_vec):
    b, s, _ = g3.shape
    r = GATE_R
    return pl.pallas_call(
        _gates_kernel,
        grid=(b, s // r),
        in_specs=[
            pl.BlockSpec((None, r, 2 * LANES), lambda bi, ri: (bi, ri, 0)),
            pl.BlockSpec((1, 2 * LANES), lambda bi, ri: (0, 0)),
        ],
        out_specs=[
            pl.BlockSpec((None, r, LANES), lambda bi, ri: (bi, ri, 0)),
            pl.BlockSpec((None, GATE_ROWS, r), lambda bi, ri: (bi, 0, ri)),
        ],
        out_shape=[
            jax.ShapeDtypeStruct((b, s, LANES), F32),
            jax.ShapeDtypeStruct((b, GATE_ROWS, s), F32),
        ],
        scratch_shapes=[pltpu.VMEM((1, LANES), F32), pltpu.VMEM((1, LANES), F32)],
        compiler_params=_params(("arbitrary", "arbitrary")),
        name="gates",
    )(g3, bias_vec)


def _fox_kernel(it_ref, jt_ref, q_ref, k_ref, vt_ref, gc_ref, zt_ref, nwb_ref, o_ref, m_ref, acc_ref):
    step_id = pl.program_id(1)
    i = it_ref[step_id]
    j = jt_ref[step_id]
    t = ATT_T
    hd = FOX_HEAD_DIM

    @pl.when(j == 0)
    def _():
        m_ref[...] = jnp.full_like(m_ref, NEG_INF)
        acc_ref[...] = jnp.zeros_like(acc_ref)

    def step(masked):
        gc = gc_ref[...]
        ones = jnp.ones((ONES_ROWS, t), BF16)
        if masked:
            valid = (lax.broadcasted_iota(jnp.int32, (t, t), 0)
                     <= lax.broadcasted_iota(jnp.int32, (t, t), 1))

        def scores(h):
            hs = slice(h * hd, (h + 1) * hd)
            st = lax.dot_general(k_ref[:, hs], q_ref[:, hs], NT_DIMS,
                                 preferred_element_type=F32) - gc[:, h:h + 1]
            return jnp.where(valid, st, NEG_INF) if masked else st

        def probs(h, st):
            m_prev = m_ref[h:h + 1, :]
            m_new = jnp.maximum(m_prev, jnp.max(st, axis=0, keepdims=True))
            m_ref[h:h + 1, :] = m_new
            return jnp.exp2(m_prev - m_new), jnp.exp2(st - m_new).astype(BF16)

        def accumulate(h, alpha, pt):
            vt_aug = jnp.concatenate([vt_ref[h * hd:(h + 1) * hd, :], ones], axis=0)
            acc_ref[h] = alpha * acc_ref[h] + jnp.dot(vt_aug, pt, preferred_element_type=F32)

        st, ap = {}, {}
        for h in range(-ATT_LOOKAHEAD, FOX_HEADS + 1):
            if 0 <= h + ATT_LOOKAHEAD < FOX_HEADS:
                st[h + ATT_LOOKAHEAD] = scores(h + ATT_LOOKAHEAD)
            if 0 <= h < FOX_HEADS:
                ap[h] = probs(h, st.pop(h))
            if 0 <= h - 1 < FOX_HEADS:
                accumulate(h - 1, *ap.pop(h - 1))

    @pl.when(j < i)
    def _():
        step(False)

    @pl.when(j == i)
    def _():
        step(True)
        for h in range(FOX_HEADS):
            hs = slice(h * hd, (h + 1) * hd)
            acc = acc_ref[h]
            o = acc[:hd] * (1.0 / acc[hd:hd + 1])
            o = o * lax.rsqrt(jnp.mean(o * o, axis=0, keepdims=True) + NORM_EPS) * nwb_ref[hs, :]
            z = zt_ref[hs, :]
            o_ref[hs, :] = (o * (z * _sigmoid(z))).astype(BF16)


def _fox(r16, t16, t32, mcol, nwb):
    _, b, s, _ = r16.shape
    t = ATT_T
    n = s // t
    pairs = [(i, j) for i in range(n) for j in range(i + 1)]
    it = jnp.asarray([p[0] for p in pairs], jnp.int32)
    jt = jnp.asarray([p[1] for p in pairs], jnp.int32)

    grid_spec = pltpu.PrefetchScalarGridSpec(
        num_scalar_prefetch=2,
        grid=(b, len(pairs)),
        in_specs=[
            pl.BlockSpec((None, None, t, FOX_WIDTH), lambda bi, p, it, jt: (0, bi, it[p], 0)),
            pl.BlockSpec((None, None, t, FOX_WIDTH), lambda bi, p, it, jt: (1, bi, jt[p], 0)),
            pl.BlockSpec((None, None, FOX_WIDTH, t), lambda bi, p, it, jt: (0, bi, 0, jt[p])),
            pl.BlockSpec((None, t, LANES), lambda bi, p, it, jt: (bi, jt[p], 0)),
            pl.BlockSpec((None, None, FOX_WIDTH, t), lambda bi, p, it, jt: (0, bi, 0, it[p])),
            pl.BlockSpec((FOX_WIDTH, t), lambda bi, p, it, jt: (0, 0)),
        ],
        out_specs=pl.BlockSpec((None, FOX_WIDTH, t), lambda bi, p, it, jt: (bi, 0, it[p])),
        scratch_shapes=[pltpu.VMEM((FOX_HEADS, t), F32),
                        pltpu.VMEM((FOX_HEADS, FOX_HEAD_DIM + ONES_ROWS, t), F32)],
    )
    return pl.pallas_call(
        _fox_kernel,
        grid_spec=grid_spec,
        out_shape=jax.ShapeDtypeStruct((b, FOX_WIDTH, s), BF16),
        compiler_params=_params(("arbitrary", "arbitrary")),
        name="fox_attention",
    )(it, jt, r16, r16, t16, mcol, t32, nwb)


def _mlstm_kernel(qk_ref, vt_ref, ogt_ref, zt_ref, mcol_ref, mrow_ref, cw_ref, cb_ref, nwb_ref,
                  out_ref, xs_ref, st_ref):
    n = MLSTM_L
    dv = MLSTM_V_DIM
    pad = 8

    @pl.when(pl.program_id(1) == 0)
    def _():
        xs_ref[0:pad, :] = jnp.zeros((pad, SEG), F32)
        st_ref[...] = jnp.zeros_like(st_ref)

    xs_ref[pad:pad + n, :] = qk_ref[...]
    base = pad - (CONV_WIDTH - 1)
    y = xs_ref[base:base + n, :] * cw_ref[0:1, :]
    for jj in range(1, CONV_WIDTH):
        y = y + xs_ref[base + jj:base + jj + n, :] * cw_ref[jj:jj + 1, :]
    y = y + cb_ref[...]
    xs_ref[0:pad, :] = xs_ref[n:n + pad, :]
    sy = y * _sigmoid(y)

    mc = mcol_ref[...]
    mr = mrow_ref[...]
    lane = lax.broadcasted_iota(jnp.int32, (n, LANES), 1)
    ones_rows = jnp.where(lax.broadcasted_iota(jnp.int32, (ONES_ROWS, n), 0) == 0, 1.0, 0.0).astype(BF16)
    valid = lax.broadcasted_iota(jnp.int32, (n, n), 0) <= lax.broadcasted_iota(jnp.int32, (n, n), 1)
    heads = range(MLSTM_HEADS)
    row = lambda base_row, h: mr[base_row + h:base_row + h + 1, :]

    qp, kh, st_t, inter_t, vt_aug = [], [], [], [], []
    for h in heads:
        pr = h // 2
        qp.append(sy[:, pr * LANES:(pr + 1) * LANES].astype(BF16))
        kp = sy[:, MLSTM_QK_WIDTH + pr * LANES:MLSTM_QK_WIDTH + (pr + 1) * LANES] * (MLSTM_QK_DIM ** -0.5)
        kh.append(jnp.where((lane < MLSTM_QK_DIM) if h % 2 == 0 else (lane >= MLSTM_QK_DIM), kp, 0.0))
        st_t.append(lax.dot_general(kh[h].astype(BF16), qp[h], NT_DIMS, preferred_element_type=F32))
        inter_t.append(lax.dot_general(st_ref[h].astype(BF16), qp[h], NT_DIMS, preferred_element_type=F32))
        vt_aug.append(jnp.concatenate([vt_ref[h * dv:(h + 1) * dv, :], ones_rows], axis=0))

    for h in heads:
        hs = slice(h * dv, (h + 1) * dv)
        a_col = mc[:, LANE_A + h:LANE_A + h + 1]
        sc_t = st_t[h] * jnp.exp(jnp.where(valid, a_col + row(ROW_G, h), NEG_INF))
        res = (jnp.dot(vt_aug[h], sc_t.astype(BF16), preferred_element_type=F32)
               + jnp.exp(row(ROW_U, h)) * inter_t[h])
        den = res[dv:dv + 1, :]
        ht = res[:dv] * (1.0 / jnp.maximum(jnp.abs(den), jnp.exp(row(ROW_NEGM, h))))
        hb = ht * _sigmoid(ogt_ref[hs, :])
        hn = hb * lax.rsqrt(jnp.mean(hb * hb, axis=0, keepdims=True) + NORM_EPS) * nwb_ref[hs, :]
        z = zt_ref[hs, :]
        out_ref[hs, :] = (hn * (z * _sigmoid(z))).astype(BF16)

    for h in heads:
        a_col = mc[:, LANE_A + h:LANE_A + h + 1]
        g_last = mc[n - 1:n, LANE_G + h:LANE_G + h + 1]
        u_last = mc[n - 1:n, LANE_U + h:LANE_U + h + 1]
        kw = (kh[h] * jnp.exp(a_col + g_last)).astype(BF16)
        upd = jnp.dot(vt_aug[h], kw, preferred_element_type=F32)
        st_ref[h] = jnp.exp(u_last) * st_ref[h] + upd


def _mlstm(r32, t16, t32, mcol, mrow, cw, cb, nwb):
    b, s, _ = r32.shape
    n = MLSTM_L

    def t_map(which):
        return lambda bi, c: (which, bi, 0, c)

    return pl.pallas_call(
        _mlstm_kernel,
        grid=(b, s // n),
        in_specs=[
            pl.BlockSpec((None, n, SEG), lambda bi, c: (bi, c, 0)),
            pl.BlockSpec((None, None, SEG, n), t_map(1)),
            pl.BlockSpec((None, None, SEG, n), t_map(1)),
            pl.BlockSpec((None, None, SEG, n), t_map(2)),
            pl.BlockSpec((None, n, LANES), lambda bi, c: (bi, c, 0)),
            pl.BlockSpec((None, GATE_ROWS, n), lambda bi, c: (bi, 0, c)),
            pl.BlockSpec((CONV_WIDTH, SEG), lambda bi, c: (0, 0)),
            pl.BlockSpec((1, SEG), lambda bi, c: (0, 0)),
            pl.BlockSpec((SEG, n), lambda bi, c: (0, 0)),
        ],
        out_specs=pl.BlockSpec((None, SEG, n), lambda bi, c: (bi, 0, c)),
        out_shape=jax.ShapeDtypeStruct((b, MLSTM_V_WIDTH, s), BF16),
        scratch_shapes=[pltpu.VMEM((n + 8, SEG), F32),
                        pltpu.VMEM((MLSTM_HEADS, MLSTM_V_DIM + ONES_ROWS, LANES), F32)],
        compiler_params=_params(("arbitrary", "arbitrary")),
        name="mlstm",
    )(r32, t16, t32, t32, mcol, mrow, cw, cb, nwb)


def _outproj_kernel(yat_ref, ybt_ref, wa_ref, wb_ref, x_ref, fw_ref, o_ref, *, final):
    acc = (lax.dot_general(yat_ref[...], wa_ref[...], TN_DIMS, preferred_element_type=F32)
           + lax.dot_general(ybt_ref[...], wb_ref[...], TN_DIMS, preferred_element_type=F32))
    r = x_ref[...] + acc
    if final:
        r = r * lax.rsqrt(jnp.mean(r * r, axis=-1, keepdims=True) + NORM_EPS) * fw_ref[...]
    o_ref[...] = r


def _outproj(yat, ybt, wa, wb, x2, fw, final):
    m = x2.shape[0]
    _, _, seq = yat.shape
    tm = OUT_TM
    per_seq = seq // tm
    y_map = lambda i: (i // per_seq, 0, i % per_seq)
    return pl.pallas_call(
        functools.partial(_outproj_kernel, final=final),
        grid=(m // tm,),
        in_specs=[
            pl.BlockSpec((None, FOX_WIDTH, tm), y_map),
            pl.BlockSpec((None, MLSTM_V_WIDTH, tm), y_map),
            pl.BlockSpec((FOX_WIDTH, D_MODEL), lambda i: (0, 0)),
            pl.BlockSpec((MLSTM_V_WIDTH, D_MODEL), lambda i: (0, 0)),
            pl.BlockSpec((tm, D_MODEL), lambda i: (i, 0)),
            pl.BlockSpec((1, D_MODEL), lambda i: (0, 0)),
        ],
        out_specs=pl.BlockSpec((tm, D_MODEL), lambda i: (i, 0)),
        out_shape=jax.ShapeDtypeStruct((m, D_MODEL), F32),
        compiler_params=_params(("arbitrary",)),
        name="outproj",
    )(yat, ybt, wa, wb, x2, fw)


SEG_STARTS = (IN_OFFS[0], IN_OFFS[1], IN_OFFS[5], IN_OFFS[2], IN_OFFS[7], IN_OFFS[3], IN_OFFS[8], IN_OFFS[9])
SEG_SHIFT = FOX_HEADS
assert all(st % SEG in (0, SEG_SHIFT) for st in SEG_STARTS) and IN_OFFS[6] == IN_OFFS[5] + MLSTM_QK_WIDTH
F32_SUBLANES = 8
assert SEG_SHIFT == F32_SUBLANES and MLSTM_HEADS == F32_SUBLANES

J_FOX_GATE = 2
assert SEG_STARTS[J_FOX_GATE] - SEG_SHIFT == IN_OFFS[4]
GW_FOX_F, GW_MLSTM_F, GW_MLSTM_I = 0, 8, LANES + 8


def _wprep_kernel(main_blk_ref, extra_blk_ref, main_ref, extra_ref, mi_ref, mf_ref, o_ref, gw_ref, gwf_ref):
    del main_blk_ref, extra_blk_ref
    j = pl.program_id(0)
    shifted = functools.reduce(jnp.logical_or, [j == k for k, st in enumerate(SEG_STARTS) if st % SEG])

    @pl.when(j == 0)
    def _():
        gwf_ref[...] = jnp.zeros_like(gwf_ref)
        gwf_ref[GW_MLSTM_I:GW_MLSTM_I + MLSTM_HEADS, :] = mi_ref[...]
        gwf_ref[GW_MLSTM_F:GW_MLSTM_F + MLSTM_HEADS, :] = mf_ref[...]

    @pl.when(jnp.logical_not(shifted))
    def _():
        o_ref[...] = main_ref[...].astype(BF16)

    @pl.when(shifted)
    def _():
        full = jnp.concatenate([main_ref[...], extra_ref[...]], axis=0)
        o_ref[...] = full[SEG_SHIFT:SEG_SHIFT + SEG, :].astype(BF16)

    @pl.when(j == J_FOX_GATE)
    def _():
        gwf_ref[GW_FOX_F:GW_FOX_F + FOX_HEADS, :] = main_ref[0:SEG_SHIFT, :]

    @pl.when(j == len(SEG_STARTS) - 1)
    def _():
        gw_ref[...] = gwf_ref[...].astype(BF16)


def _wprep(w_t):
    d = w_t.shape[1]
    main_blk = jnp.asarray([st // SEG for st in SEG_STARTS], jnp.int32)
    extra_blk = jnp.asarray([(st // SEG + 1) * (SEG // F32_SUBLANES) for st in SEG_STARTS], jnp.int32)
    grid_spec = pltpu.PrefetchScalarGridSpec(
        num_scalar_prefetch=2,
        grid=(len(SEG_STARTS),),
        in_specs=[
            pl.BlockSpec((SEG, d), lambda j, mb, eb: (mb[j], 0)),
            pl.BlockSpec((F32_SUBLANES, d), lambda j, mb, eb: (eb[j], 0)),
            pl.BlockSpec((MLSTM_HEADS, d), lambda j, mb, eb: (IN_OFFS[10] // F32_SUBLANES, 0)),
            pl.BlockSpec((MLSTM_HEADS, d), lambda j, mb, eb: (IN_OFFS[11] // F32_SUBLANES, 0)),
        ],
        out_specs=[
            pl.BlockSpec((SEG, d), lambda j, mb, eb: (j, 0)),
            pl.BlockSpec((2 * LANES, d), lambda j, mb, eb: (0, 0)),
        ],
        scratch_shapes=[pltpu.VMEM((2 * LANES, d), F32)],
    )
    return pl.pallas_call(
        _wprep_kernel,
        grid_spec=grid_spec,
        out_shape=[jax.ShapeDtypeStruct((len(SEG_STARTS) * SEG, d), BF16),
                   jax.ShapeDtypeStruct((2 * LANES, d), BF16)],
        compiler_params=_params(("arbitrary",)),
        name="weight_layout",
    )(main_blk, extra_blk, w_t, w_t, w_t, w_t)


def _lane_broadcast(v, width):
    return jnp.broadcast_to(v[:, None], (v.shape[0], width))


def kernel(x, norm_w, w_in, fox_f_bias, conv_w, conv_b, mlstm_i_bias, mlstm_f_bias,
           fox_out_norm_w, mlstm_out_norm_w, w_out, final_norm_w):
    b, s, d = x.shape
    depth = norm_w.shape[0]
    x2 = x.reshape(b * s, d)
    for l in range(depth):
        w_t, w_gate = _wprep(w_in[l].T)
        bias_vec = jnp.concatenate([
            fox_f_bias[l], mlstm_f_bias[l], jnp.zeros((LANES - 16,), F32),
            jnp.zeros((8,), F32), mlstm_i_bias[l], jnp.zeros((LANES - 16,), F32)]).reshape(1, 2 * LANES)

        r16, r32, t16, t32, g = _inproj(x2, norm_w[l].reshape(1, d), w_t, w_gate, s)
        r16 = r16.reshape(2, b, s, SEG)
        mcol, mrow = _gates(g.reshape(b, s, 2 * LANES), bias_vec)

        yat = _fox(r16, t16, t32, mcol, _lane_broadcast(fox_out_norm_w[l], ATT_T))
        ybt = _mlstm(r32.reshape(b, s, SEG), t16, t32, mcol, mrow, conv_w[l], conv_b[l].reshape(1, SEG),
                     _lane_broadcast(mlstm_out_norm_w[l], MLSTM_L))

        wo = w_out[l].astype(BF16)
        x2 = _outproj(yat, ybt, wo[:FOX_WIDTH], wo[FOX_WIDTH:], x2, final_norm_w.reshape(1, d),
                      final=(l == depth - 1))
    return x2.reshape(b, s, d)
```

```python
import functools

import numpy as np
import jax
import jax.numpy as jnp
from jax import lax
from jax.experimental import pallas as pl
from jax.experimental.pallas import tpu as pltpu

D_MODEL = 2048
NORM_EPS = 1e-6
NEG_INF = -1e30

FOX_HEADS = 8
FOX_HEAD_DIM = 128
FOX_WIDTH = FOX_HEADS * FOX_HEAD_DIM
MLSTM_HEADS = 8
MLSTM_V_DIM = 128
MLSTM_QK_DIM = 64
MLSTM_V_WIDTH = MLSTM_HEADS * MLSTM_V_DIM
MLSTM_QK_WIDTH = MLSTM_HEADS * MLSTM_QK_DIM
CONV_WIDTH = 4
IN_SIZES = (FOX_WIDTH, FOX_WIDTH, FOX_WIDTH, FOX_WIDTH, FOX_HEADS,
            MLSTM_QK_WIDTH, MLSTM_QK_WIDTH, MLSTM_V_WIDTH, MLSTM_V_WIDTH, MLSTM_V_WIDTH,
            MLSTM_HEADS, MLSTM_HEADS)
IN_OFFS = tuple(int(v) for v in np.cumsum((0,) + IN_SIZES))

LANES = 128
SEG = 1024
VMEM_LIMIT = 56 * 1024 * 1024
BIG_TILE_VMEM_LIMIT = 62 * 1024 * 1024

PROJ_TM = 256
GATE_R = 1024
MLSTM_L = 256
ATT_T = 512
OUT_TM = 1024

LOG2E = 1.4426950408889634
FOX_Q_SCALE = FOX_HEAD_DIM ** -0.5 * LOG2E
ONES_ROWS = 16
ATT_QSPLIT = 1
ATT_LOOKAHEAD = 1

LANE_G, LANE_U, LANE_A, LANE_NEGM = 8, 16, 24, 32
ROW_G, ROW_U, ROW_NEGM = 0, 8, 24
GATE_ROWS = 32

BF16 = jnp.bfloat16
F32 = jnp.float32
NT_DIMS = (((1,), (1,)), ((), ()))
TN_DIMS = (((0,), (0,)), ((), ()))


def _sigmoid(x):
    return 1.0 / (1.0 + jnp.exp(-x))


def _params(sem, vmem_limit=VMEM_LIMIT):
    return pltpu.CompilerParams(dimension_semantics=sem, vmem_limit_bytes=vmem_limit)


def _inproj_kernel(x_ref, nw_ref, wt_ref, wg_ref, kr_ref, qkr_ref, t16_ref, t32_ref, g_ref):
    xf = x_ref[...]
    ms = jnp.mean(xf * xf, axis=-1, keepdims=True)
    hb = (xf * lax.rsqrt(ms + NORM_EPS) * nw_ref[...]).astype(BF16)

    def seg_t(j):
        return lax.dot_general(wt_ref[j * SEG:(j + 1) * SEG, :], hb, NT_DIMS, preferred_element_type=F32)

    t16_ref[0:SEG, :] = (seg_t(0) * FOX_Q_SCALE).astype(BF16)
    kr_ref[...] = seg_t(1).T.astype(BF16)
    qkr_ref[...] = seg_t(2).T
    t16_ref[SEG:2 * SEG, :] = seg_t(3).astype(BF16)
    t16_ref[2 * SEG:3 * SEG, :] = seg_t(4).astype(BF16)
    for r in range(3):
        t32_ref[r * SEG:(r + 1) * SEG, :] = seg_t(5 + r)
    g_ref[...] = lax.dot_general(hb, wg_ref[...], NT_DIMS, preferred_element_type=F32)


def _inproj(x2, nw, w_t, w_gate, seq):
    m = x2.shape[0]
    tm = PROJ_TM
    per_seq = seq // tm
    t_map = lambda i: (i // per_seq, 0, i % per_seq)
    return pl.pallas_call(
        _inproj_kernel,
        grid=(m // tm,),
        in_specs=[
            pl.BlockSpec((tm, D_MODEL), lambda i: (i, 0)),
            pl.BlockSpec((1, D_MODEL), lambda i: (0, 0)),
            pl.BlockSpec(w_t.shape, lambda i: (0, 0)),
            pl.BlockSpec((2 * LANES, D_MODEL), lambda i: (0, 0)),
        ],
        out_specs=[
            pl.BlockSpec((tm, SEG), lambda i: (i, 0)),
            pl.BlockSpec((tm, SEG), lambda i: (i, 0)),
            pl.BlockSpec((None, 3 * SEG, tm), t_map),
            pl.BlockSpec((None, 3 * SEG, tm), t_map),
            pl.BlockSpec((tm, 2 * LANES), lambda i: (i, 0)),
        ],
        out_shape=[
            jax.ShapeDtypeStruct((m, SEG), BF16),
            jax.ShapeDtypeStruct((m, SEG), F32),
            jax.ShapeDtypeStruct((m // seq, 3 * SEG, seq), BF16),
            jax.ShapeDtypeStruct((m // seq, 3 * SEG, seq), F32),
            jax.ShapeDtypeStruct((m, 2 * LANES), F32),
        ],
        compiler_params=_params(("arbitrary",), BIG_TILE_VMEM_LIMIT),
        name="inproj",
    )(x2, nw, w_t, w_gate)


def _split3(v):
    hi = v.astype(BF16)
    r1 = v - hi.astype(F32)
    mid = r1.astype(BF16)
    lo = (r1 - mid.astype(F32)).astype(BF16)
    return hi, mid, lo


def _gates_kernel(g_ref, b_ref, mcol_ref, mrow_ref, carry_ref, mprev_ref):
    rows, chunk = GATE_R, MLSTM_L

    @pl.when(pl.program_id(1) == 0)
    def _():
        carry_ref[...] = jnp.zeros_like(carry_ref)
        mprev_ref[...] = jnp.zeros_like(mprev_ref)

    v = g_ref[...] + b_ref[...]
    v1 = v[:, :LANES]
    v2 = v[:, LANES:]
    ls = jnp.minimum(v1, 0.0) - jnp.log1p(jnp.exp(-jnp.abs(v1)))

    tri = (lax.broadcasted_iota(jnp.int32, (chunk, chunk), 1)
           <= lax.broadcasted_iota(jnp.int32, (chunk, chunk), 0)).astype(BF16)
    hi, mid, lo = _split3(ls)
    cuml = jnp.concatenate([
        jnp.dot(tri, hi[c * chunk:(c + 1) * chunk], preferred_element_type=F32)
        + jnp.dot(tri, mid[c * chunk:(c + 1) * chunk], preferred_element_type=F32)
        + jnp.dot(tri, lo[c * chunk:(c + 1) * chunk], preferred_element_type=F32)
        for c in range(rows // chunk)], axis=0)
    carry = carry_ref[...]
    cumf = []
    for c in range(rows // chunk):
        cumf.append(cuml[c * chunk:(c + 1) * chunk] + carry)
        carry = cumf[c][chunk - 1:chunk, :]
    carry_ref[...] = carry
    cumf = jnp.concatenate(cumf, axis=0)

    a = v2 - cuml
    t_in = lax.broadcasted_iota(jnp.int32, (rows, LANES), 0) % chunk
    cm = a
    s = 1
    while s < chunk:
        cm = jnp.where(t_in >= s, jnp.maximum(cm, pltpu.roll(cm, s, axis=0)), cm)
        s *= 2

    lane = lax.broadcasted_iota(jnp.int32, (chunk, LANES), 1)
    mprev = mprev_ref[...]
    for c in range(rows // chunk):
        sl = slice(c * chunk, (c + 1) * chunk)
        b = cuml[sl]
        mt = b + jnp.maximum(mprev, cm[sl])
        gt = b - mt
        ut = (b + mprev) - mt
        tile = jnp.where(lane < LANE_G, cumf[sl] * LOG2E,
               jnp.where(lane < LANE_U, gt,
               jnp.where(lane < LANE_A, pltpu.roll(ut, LANE_U - LANE_G, axis=1),
               jnp.where(lane < LANE_NEGM, pltpu.roll(a[sl], LANE_A - LANE_G, axis=1),
                         pltpu.roll(-mt, LANE_NEGM - LANE_G, axis=1)))))
        mprev = mt[chunk - 1:chunk, :]
        mcol_ref[sl, :] = tile
        mrow_ref[:, sl] = tile.T[LANE_G:LANE_G + GATE_ROWS, :]
    mprev_ref[...] = mprev


def _gates(g3, bias_vec):
    b, s, _ = g3.shape
    r = GATE_R
    return pl.pallas_call(
        _gates_kernel,
        grid=(b, s // r),
        in_specs=[
            pl.BlockSpec((None, r, 2 * LANES), lambda bi, ri: (bi, ri, 0)),
            pl.BlockSpec((1, 2 * LANES), lambda bi, ri: (0, 0)),
        ],
        out_specs=[
            pl.BlockSpec((None, r, LANES), lambda bi, ri: (bi, ri, 0)),
            pl.BlockSpec((None, GATE_ROWS, r), lambda bi, ri: (bi, 0, ri)),
        ],
        out_shape=[
            jax.ShapeDtypeStruct((b, s, LANES), F32),
            jax.ShapeDtypeStruct((b, GATE_ROWS, s), F32),
        ],
        scratch_shapes=[pltpu.VMEM((1, LANES), F32), pltpu.VMEM((1, LANES), F32)],
        compiler_params=_params(("arbitrary", "arbitrary")),
        name="gates",
    )(g3, bias_vec)


def _fox_kernel(it_ref, jt_ref, qt_ref, k_ref, vt_ref, gc_ref, zt_ref, nwb_ref, o_ref, m_ref, acc_ref):
    step_id = pl.program_id(1)
    i = it_ref[step_id]
    j = jt_ref[step_id]
    t = ATT_T
    hd = FOX_HEAD_DIM

    @pl.when(j == 0)
    def _():
        m_ref[...] = jnp.full_like(m_ref, NEG_INF)
        acc_ref[...] = jnp.zeros_like(acc_ref)

    def step(masked):
        gc = gc_ref[...]
        tqs = t // ATT_QSPLIT
        units = [(h, c) for h in range(FOX_HEADS) for c in range(ATT_QSPLIT)]
        nkeys = lambda c: (c + 1) * tqs if masked else t

        def scores(h, c):
            hs = slice(h * hd, (h + 1) * hd)
            nk = nkeys(c)
            st = jnp.dot(k_ref[0:nk, hs], qt_ref[hs, c * tqs:(c + 1) * tqs],
                         preferred_element_type=F32) - gc[0:nk, h:h + 1]
            if masked:
                valid = (lax.broadcasted_iota(jnp.int32, (nk, tqs), 0)
                         <= lax.broadcasted_iota(jnp.int32, (nk, tqs), 1) + c * tqs)
                st = jnp.where(valid, st, NEG_INF)
            return st

        def probs(h, c, st):
            qs = slice(c * tqs, (c + 1) * tqs)
            m_prev = m_ref[h:h + 1, qs]
            m_new = jnp.maximum(m_prev, jnp.max(st, axis=0, keepdims=True))
            m_ref[h:h + 1, qs] = m_new
            return jnp.exp2(m_prev - m_new), jnp.exp2(st - m_new).astype(BF16)

        def accumulate(h, c, alpha, pt):
            qs = slice(c * tqs, (c + 1) * tqs)
            nk = nkeys(c)
            vt_aug = jnp.concatenate([vt_ref[h * hd:(h + 1) * hd, 0:nk], jnp.ones((ONES_ROWS, nk), BF16)], axis=0)
            acc_ref[h, :, qs] = alpha * acc_ref[h, :, qs] + jnp.dot(vt_aug, pt, preferred_element_type=F32)

        st, ap = {}, {}
        for u in range(-ATT_LOOKAHEAD, len(units) + 1):
            if 0 <= u + ATT_LOOKAHEAD < len(units):
                st[u + ATT_LOOKAHEAD] = scores(*units[u + ATT_LOOKAHEAD])
            if 0 <= u < len(units):
                ap[u] = probs(*units[u], st.pop(u))
            if 0 <= u - 1 < len(units):
                accumulate(*units[u - 1], *ap.pop(u - 1))

    @pl.when(j < i)
    def _():
        step(False)

    @pl.when(j == i)
    def _():
        step(True)
        for h in range(FOX_HEADS):
            hs = slice(h * hd, (h + 1) * hd)
            acc = acc_ref[h]
            o = acc[:hd] * (1.0 / acc[hd:hd + 1])
            o = o * lax.rsqrt(jnp.mean(o * o, axis=0, keepdims=True) + NORM_EPS) * nwb_ref[hs, :]
            z = zt_ref[hs, :]
            o_ref[hs, :] = (o * (z * _sigmoid(z))).astype(BF16)


def _fox(kr, t16, t32, mcol, nwb):
    b, s, _ = kr.shape
    t = ATT_T
    n = s // t
    pairs = [(i, j) for i in range(n) for j in range(i + 1)]
    it = jnp.asarray([p[0] for p in pairs], jnp.int32)
    jt = jnp.asarray([p[1] for p in pairs], jnp.int32)

    grid_spec = pltpu.PrefetchScalarGridSpec(
        num_scalar_prefetch=2,
        grid=(b, len(pairs)),
        in_specs=[
            pl.BlockSpec((None, FOX_WIDTH, t), lambda bi, p, it, jt: (bi, 0, it[p])),
            pl.BlockSpec((None, t, FOX_WIDTH), lambda bi, p, it, jt: (bi, jt[p], 0)),
            pl.BlockSpec((None, FOX_WIDTH, t), lambda bi, p, it, jt: (bi, 1, jt[p])),
            pl.BlockSpec((None, t, LANES), lambda bi, p, it, jt: (bi, jt[p], 0)),
            pl.BlockSpec((None, FOX_WIDTH, t), lambda bi, p, it, jt: (bi, 0, it[p])),
            pl.BlockSpec((FOX_WIDTH, t), lambda bi, p, it, jt: (0, 0)),
        ],
        out_specs=pl.BlockSpec((None, FOX_WIDTH, t), lambda bi, p, it, jt: (bi, 0, it[p])),
        scratch_shapes=[pltpu.VMEM((FOX_HEADS, t), F32),
                        pltpu.VMEM((FOX_HEADS, FOX_HEAD_DIM + ONES_ROWS, t), F32)],
    )
    return pl.pallas_call(
        _fox_kernel,
        grid_spec=grid_spec,
        out_shape=jax.ShapeDtypeStruct((b, FOX_WIDTH, s), BF16),
        compiler_params=_params(("arbitrary", "arbitrary")),
        name="fox_attention",
    )(it, jt, t16, kr, t16, mcol, t32, nwb)


def _mlstm_kernel(qk_ref, vt_ref, ogt_ref, zt_ref, mcol_ref, mrow_ref, cw_ref, cb_ref, nwb_ref,
                  out_ref, xs_ref, st_ref):
    n = MLSTM_L
    dv = MLSTM_V_DIM
    pad = 8

    @pl.when(pl.program_id(1) == 0)
    def _():
        xs_ref[0:pad, :] = jnp.zeros((pad, SEG), F32)
        st_ref[...] = jnp.zeros_like(st_ref)

    xs_ref[pad:pad + n, :] = qk_ref[...]
    base = pad - (CONV_WIDTH - 1)
    y = xs_ref[base:base + n, :] * cw_ref[0:1, :]
    for jj in range(1, CONV_WIDTH):
        y = y + xs_ref[base + jj:base + jj + n, :] * cw_ref[jj:jj + 1, :]
    y = y + cb_ref[...]
    xs_ref[0:pad, :] = xs_ref[n:n + pad, :]
    sy = y * _sigmoid(y)

    mc = mcol_ref[...]
    mr = mrow_ref[...]
    lane = lax.broadcasted_iota(jnp.int32, (n, LANES), 1)
    ones_rows = jnp.where(lax.broadcasted_iota(jnp.int32, (ONES_ROWS, n), 0) == 0, 1.0, 0.0).astype(BF16)
    valid = lax.broadcasted_iota(jnp.int32, (n, n), 0) <= lax.broadcasted_iota(jnp.int32, (n, n), 1)
    heads = range(MLSTM_HEADS)
    row = lambda base_row, h: mr[base_row + h:base_row + h + 1, :]

    qp, kh, st_t, inter_t, vt_aug = [], [], [], [], []
    for h in heads:
        pr = h // 2
        qp.append(sy[:, pr * LANES:(pr + 1) * LANES].astype(BF16))
        kp = sy[:, MLSTM_QK_WIDTH + pr * LANES:MLSTM_QK_WIDTH + (pr + 1) * LANES] * (MLSTM_QK_DIM ** -0.5)
        kh.append(jnp.where((lane < MLSTM_QK_DIM) if h % 2 == 0 else (lane >= MLSTM_QK_DIM), kp, 0.0))
        st_t.append(lax.dot_general(kh[h].astype(BF16), qp[h], NT_DIMS, preferred_element_type=F32))
        inter_t.append(lax.dot_general(st_ref[h].astype(BF16), qp[h], NT_DIMS, preferred_element_type=F32))
        vt_aug.append(jnp.concatenate([vt_ref[h * dv:(h + 1) * dv, :], ones_rows], axis=0))

    for h in heads:
        hs = slice(h * dv, (h + 1) * dv)
        a_col = mc[:, LANE_A + h:LANE_A + h + 1]
        sc_t = st_t[h] * jnp.exp(jnp.where(valid, a_col + row(ROW_G, h), NEG_INF))
        res = (jnp.dot(vt_aug[h], sc_t.astype(BF16), preferred_element_type=F32)
               + jnp.exp(row(ROW_U, h)) * inter_t[h])
        den = res[dv:dv + 1, :]
        ht = res[:dv] * (1.0 / jnp.maximum(jnp.abs(den), jnp.exp(row(ROW_NEGM, h))))
        hb = ht * _sigmoid(ogt_ref[hs, :])
        hn = hb * lax.rsqrt(jnp.mean(hb * hb, axis=0, keepdims=True) + NORM_EPS) * nwb_ref[hs, :]
        z = zt_ref[hs, :]
        out_ref[hs, :] = (hn * (z * _sigmoid(z))).astype(BF16)

    for h in heads:
        a_col = mc[:, LANE_A + h:LANE_A + h + 1]
        g_last = mc[n - 1:n, LANE_G + h:LANE_G + h + 1]
        u_last = mc[n - 1:n, LANE_U + h:LANE_U + h + 1]
        kw = (kh[h] * jnp.exp(a_col + g_last)).astype(BF16)
        upd = jnp.dot(vt_aug[h], kw, preferred_element_type=F32)
        st_ref[h] = jnp.exp(u_last) * st_ref[h] + upd


def _mlstm(r32, t16, t32, mcol, mrow, cw, cb, nwb):
    b, s, _ = r32.shape
    n = MLSTM_L

    def t_map(which):
        return lambda bi, c: (bi, which, c)

    return pl.pallas_call(
        _mlstm_kernel,
        grid=(b, s // n),
        in_specs=[
            pl.BlockSpec((None, n, SEG), lambda bi, c: (bi, c, 0)),
            pl.BlockSpec((None, SEG, n), t_map(2)),
            pl.BlockSpec((None, SEG, n), t_map(1)),
            pl.BlockSpec((None, SEG, n), t_map(2)),
            pl.BlockSpec((None, n, LANES), lambda bi, c: (bi, c, 0)),
            pl.BlockSpec((None, GATE_ROWS, n), lambda bi, c: (bi, 0, c)),
            pl.BlockSpec((CONV_WIDTH, SEG), lambda bi, c: (0, 0)),
            pl.BlockSpec((1, SEG), lambda bi, c: (0, 0)),
            pl.BlockSpec((SEG, n), lambda bi, c: (0, 0)),
        ],
        out_specs=pl.BlockSpec((None, SEG, n), lambda bi, c: (bi, 0, c)),
        out_shape=jax.ShapeDtypeStruct((b, MLSTM_V_WIDTH, s), BF16),
        scratch_shapes=[pltpu.VMEM((n + 8, SEG), F32),
                        pltpu.VMEM((MLSTM_HEADS, MLSTM_V_DIM + ONES_ROWS, LANES), F32)],
        compiler_params=_params(("arbitrary", "arbitrary")),
        name="mlstm",
    )(r32, t16, t32, t32, mcol, mrow, cw, cb, nwb)


def _outproj_kernel(yat_ref, ybt_ref, wa_ref, wb_ref, x_ref, fw_ref, o_ref, *, final):
    acc = (lax.dot_general(yat_ref[...], wa_ref[...], TN_DIMS, preferred_element_type=F32)
           + lax.dot_general(ybt_ref[...], wb_ref[...], TN_DIMS, preferred_element_type=F32))
    r = x_ref[...] + acc
    if final:
        r = r * lax.rsqrt(jnp.mean(r * r, axis=-1, keepdims=True) + NORM_EPS) * fw_ref[...]
    o_ref[...] = r


def _outproj(yat, ybt, wa, wb, x2, fw, final):
    m = x2.shape[0]
    _, _, seq = yat.shape
    tm = OUT_TM
    per_seq = seq // tm
    y_map = lambda i: (i // per_seq, 0, i % per_seq)
    return pl.pallas_call(
        functools.partial(_outproj_kernel, final=final),
        grid=(m // tm,),
        in_specs=[
            pl.BlockSpec((None, FOX_WIDTH, tm), y_map),
            pl.BlockSpec((None, MLSTM_V_WIDTH, tm), y_map),
            pl.BlockSpec((FOX_WIDTH, D_MODEL), lambda i: (0, 0), pipeline_mode=pl.Buffered(1)),
            pl.BlockSpec((MLSTM_V_WIDTH, D_MODEL), lambda i: (0, 0), pipeline_mode=pl.Buffered(1)),
            pl.BlockSpec((tm, D_MODEL), lambda i: (i, 0)),
            pl.BlockSpec((1, D_MODEL), lambda i: (0, 0)),
        ],
        out_specs=pl.BlockSpec((tm, D_MODEL), lambda i: (i, 0)),
        out_shape=jax.ShapeDtypeStruct((m, D_MODEL), F32),
        compiler_params=_params(("arbitrary",), BIG_TILE_VMEM_LIMIT),
        name="outproj",
    )(yat, ybt, wa, wb, x2, fw)


SEG_STARTS = (IN_OFFS[0], IN_OFFS[1], IN_OFFS[5], IN_OFFS[2], IN_OFFS[7], IN_OFFS[3], IN_OFFS[8], IN_OFFS[9])
SEG_SHIFT = FOX_HEADS
assert all(st % SEG in (0, SEG_SHIFT) for st in SEG_STARTS) and IN_OFFS[6] == IN_OFFS[5] + MLSTM_QK_WIDTH
F32_SUBLANES = 8
assert SEG_SHIFT == F32_SUBLANES and MLSTM_HEADS == F32_SUBLANES

J_FOX_GATE = 2
assert SEG_STARTS[J_FOX_GATE] - SEG_SHIFT == IN_OFFS[4]
GW_FOX_F, GW_MLSTM_F, GW_MLSTM_I = 0, 8, LANES + 8


def _wprep_kernel(main_blk_ref, extra_blk_ref, main_ref, extra_ref, mi_ref, mf_ref, o_ref, gw_ref, gwf_ref):
    del main_blk_ref, extra_blk_ref
    j = pl.program_id(0)
    shifted = functools.reduce(jnp.logical_or, [j == k for k, st in enumerate(SEG_STARTS) if st % SEG])

    @pl.when(j == 0)
    def _():
        gwf_ref[...] = jnp.zeros_like(gwf_ref)
        gwf_ref[GW_MLSTM_I:GW_MLSTM_I + MLSTM_HEADS, :] = mi_ref[...]
        gwf_ref[GW_MLSTM_F:GW_MLSTM_F + MLSTM_HEADS, :] = mf_ref[...]

    @pl.when(jnp.logical_not(shifted))
    def _():
        o_ref[...] = main_ref[...].astype(BF16)

    @pl.when(shifted)
    def _():
        full = jnp.concatenate([main_ref[...], extra_ref[...]], axis=0)
        o_ref[...] = full[SEG_SHIFT:SEG_SHIFT + SEG, :].astype(BF16)

    @pl.when(j == J_FOX_GATE)
    def _():
        gwf_ref[GW_FOX_F:GW_FOX_F + FOX_HEADS, :] = main_ref[0:SEG_SHIFT, :]

    @pl.when(j == len(SEG_STARTS) - 1)
    def _():
        gw_ref[...] = gwf_ref[...].astype(BF16)


def _wprep(w_t):
    d = w_t.shape[1]
    main_blk = jnp.asarray([st // SEG for st in SEG_STARTS], jnp.int32)
    extra_blk = jnp.asarray([(st // SEG + 1) * (SEG // F32_SUBLANES) for st in SEG_STARTS], jnp.int32)
    grid_spec = pltpu.PrefetchScalarGridSpec(
        num_scalar_prefetch=2,
        grid=(len(SEG_STARTS),),
        in_specs=[
            pl.BlockSpec((SEG, d), lambda j, mb, eb: (mb[j], 0)),
            pl.BlockSpec((F32_SUBLANES, d), lambda j, mb, eb: (eb[j], 0)),
            pl.BlockSpec((MLSTM_HEADS, d), lambda j, mb, eb: (IN_OFFS[10] // F32_SUBLANES, 0)),
            pl.BlockSpec((MLSTM_HEADS, d), lambda j, mb, eb: (IN_OFFS[11] // F32_SUBLANES, 0)),
        ],
        out_specs=[
            pl.BlockSpec((SEG, d), lambda j, mb, eb: (j, 0)),
            pl.BlockSpec((2 * LANES, d), lambda j, mb, eb: (0, 0)),
        ],
        scratch_shapes=[pltpu.VMEM((2 * LANES, d), F32)],
    )
    return pl.pallas_call(
        _wprep_kernel,
        grid_spec=grid_spec,
        out_shape=[jax.ShapeDtypeStruct((len(SEG_STARTS) * SEG, d), BF16),
                   jax.ShapeDtypeStruct((2 * LANES, d), BF16)],
        compiler_params=_params(("arbitrary",)),
        name="weight_layout",
    )(main_blk, extra_blk, w_t, w_t, w_t, w_t)


def _lane_broadcast(v, width):
    return jnp.broadcast_to(v[:, None], (v.shape[0], width))


def kernel(x, norm_w, w_in, fox_f_bias, conv_w, conv_b, mlstm_i_bias, mlstm_f_bias,
           fox_out_norm_w, mlstm_out_norm_w, w_out, final_norm_w):
    b, s, d = x.shape
    depth = norm_w.shape[0]
    x2 = x.reshape(b * s, d)
    for l in range(depth):
        w_t, w_gate = _wprep(w_in[l].T)
        bias_vec = jnp.concatenate([
            fox_f_bias[l], mlstm_f_bias[l], jnp.zeros((LANES - 16,), F32),
            jnp.zeros((8,), F32), mlstm_i_bias[l], jnp.zeros((LANES - 16,), F32)]).reshape(1, 2 * LANES)

        kr, qkr, t16, t32, g = _inproj(x2, norm_w[l].reshape(1, d), w_t, w_gate, s)
        mcol, mrow = _gates(g.reshape(b, s, 2 * LANES), bias_vec)

        yat = _fox(kr.reshape(b, s, SEG), t16, t32, mcol, _lane_broadcast(fox_out_norm_w[l], ATT_T))
        ybt = _mlstm(qkr.reshape(b, s, SEG), t16, t32, mcol, mrow, conv_w[l], conv_b[l].reshape(1, SEG),
                     _lane_broadcast(mlstm_out_norm_w[l], MLSTM_L))

        wo = w_out[l].astype(BF16)
        x2 = _outproj(yat, ybt, wo[:FOX_WIDTH], wo[FOX_WIDTH:], x2, final_norm_w.reshape(1, d),
                      final=(l == depth - 1))
    return x2.reshape(b, s, d)
```

```python
import functools

import numpy as np
import jax
import jax.numpy as jnp
from jax import lax
from jax.experimental import pallas as pl
from jax.experimental.pallas import tpu as pltpu

D_MODEL = 2048
NORM_EPS = 1e-6
NEG_INF = -1e30

FOX_HEADS = 8
FOX_HEAD_DIM = 128
FOX_WIDTH = FOX_HEADS * FOX_HEAD_DIM
MLSTM_HEADS = 8
MLSTM_V_DIM = 128
MLSTM_QK_DIM = 64
MLSTM_V_WIDTH = MLSTM_HEADS * MLSTM_V_DIM
MLSTM_QK_WIDTH = MLSTM_HEADS * MLSTM_QK_DIM
CONV_WIDTH = 4
IN_SIZES = (FOX_WIDTH, FOX_WIDTH, FOX_WIDTH, FOX_WIDTH, FOX_HEADS,
            MLSTM_QK_WIDTH, MLSTM_QK_WIDTH, MLSTM_V_WIDTH, MLSTM_V_WIDTH, MLSTM_V_WIDTH,
            MLSTM_HEADS, MLSTM_HEADS)
IN_OFFS = tuple(int(v) for v in np.cumsum((0,) + IN_SIZES))

LANES = 128
F32_SUBLANES = 8
SEG = 1024
VMEM_LIMIT = 56 * 1024 * 1024
BIG_TILE_VMEM_LIMIT = 62 * 1024 * 1024

PROJ_TM = 256
GATE_R = 1024
MLSTM_L = 256
ATT_T = 512
OUT_TM = 1024

LOG2E = 1.4426950408889634
FOX_Q_SCALE = FOX_HEAD_DIM ** -0.5 * LOG2E
ONES_ROWS = 16
ATT_QSPLIT = 1
ATT_LOOKAHEAD = 1

LANE_G, LANE_U, LANE_A, LANE_NEGM = 8, 16, 24, 32
ROW_G, ROW_U, ROW_NEGM = 0, 8, 24
GATE_ROWS = 32

BF16 = jnp.bfloat16
F32 = jnp.float32
NT_DIMS = (((1,), (1,)), ((), ()))
TN_DIMS = (((0,), (0,)), ((), ()))


def _sigmoid(x):
    return 1.0 / (1.0 + jnp.exp(-x))


def _params(sem, vmem_limit=VMEM_LIMIT):
    return pltpu.CompilerParams(dimension_semantics=sem, vmem_limit_bytes=vmem_limit)


def _inproj_kernel(x_ref, nw_ref, wt_ref, wg_ref, kr_ref, qkr_ref, t16_ref, t32_ref, g_ref):
    xf = x_ref[...]
    ms = jnp.mean(xf * xf, axis=-1, keepdims=True)
    hb = (xf * lax.rsqrt(ms + NORM_EPS) * nw_ref[...]).astype(BF16)

    def seg_t(j):
        return lax.dot_general(wt_ref[j * SEG:(j + 1) * SEG, :], hb, NT_DIMS, preferred_element_type=F32)

    t16_ref[0:SEG, :] = (seg_t(0) * FOX_Q_SCALE).astype(BF16)
    kr_ref[...] = seg_t(1).T.astype(BF16)
    qkr_ref[...] = seg_t(2).T
    t16_ref[SEG:2 * SEG, :] = seg_t(3).astype(BF16)
    t16_ref[2 * SEG:3 * SEG, :] = seg_t(4).astype(BF16)
    for r in range(3):
        t32_ref[r * SEG:(r + 1) * SEG, :] = seg_t(5 + r)
    g_ref[...] = lax.dot_general(hb, wg_ref[...], NT_DIMS, preferred_element_type=F32)


def _inproj(x2, nw, w_t, w_gate, seq):
    m = x2.shape[0]
    tm = PROJ_TM
    per_seq = seq // tm
    t_map = lambda i: (i // per_seq, 0, i % per_seq)
    return pl.pallas_call(
        _inproj_kernel,
        grid=(m // tm,),
        in_specs=[
            pl.BlockSpec((tm, D_MODEL), lambda i: (i, 0)),
            pl.BlockSpec((1, D_MODEL), lambda i: (0, 0)),
            pl.BlockSpec(w_t.shape, lambda i: (0, 0)),
            pl.BlockSpec((2 * LANES, D_MODEL), lambda i: (0, 0)),
        ],
        out_specs=[
            pl.BlockSpec((tm, SEG), lambda i: (i, 0)),
            pl.BlockSpec((tm, SEG), lambda i: (i, 0)),
            pl.BlockSpec((None, 3 * SEG, tm), t_map),
            pl.BlockSpec((None, 3 * SEG, tm), t_map),
            pl.BlockSpec((tm, 2 * LANES), lambda i: (i, 0)),
        ],
        out_shape=[
            jax.ShapeDtypeStruct((m, SEG), BF16),
            jax.ShapeDtypeStruct((m, SEG), F32),
            jax.ShapeDtypeStruct((m // seq, 3 * SEG, seq), BF16),
            jax.ShapeDtypeStruct((m // seq, 3 * SEG, seq), F32),
            jax.ShapeDtypeStruct((m, 2 * LANES), F32),
        ],
        compiler_params=_params(("arbitrary",), BIG_TILE_VMEM_LIMIT),
        name="inproj",
    )(x2, nw, w_t, w_gate)


def _split3(v):
    hi = v.astype(BF16)
    r1 = v - hi.astype(F32)
    mid = r1.astype(BF16)
    lo = (r1 - mid.astype(F32)).astype(BF16)
    return hi, mid, lo


def _gates_kernel(g_ref, b_ref, mcol_ref, mrow_ref, carry_ref, mprev_ref):
    rows, chunk = GATE_R, MLSTM_L

    @pl.when(pl.program_id(1) == 0)
    def _():
        carry_ref[...] = jnp.zeros_like(carry_ref)
        mprev_ref[...] = jnp.zeros_like(mprev_ref)

    v = g_ref[...] + b_ref[...]
    v1 = v[:, :LANES]
    v2 = v[:, LANES:]
    ls = jnp.minimum(v1, 0.0) - jnp.log1p(jnp.exp(-jnp.abs(v1)))

    tri = (lax.broadcasted_iota(jnp.int32, (chunk, chunk), 1)
           <= lax.broadcasted_iota(jnp.int32, (chunk, chunk), 0)).astype(BF16)
    hi, mid, lo = _split3(ls)
    cuml = jnp.concatenate([
        jnp.dot(tri, hi[c * chunk:(c + 1) * chunk], preferred_element_type=F32)
        + jnp.dot(tri, mid[c * chunk:(c + 1) * chunk], preferred_element_type=F32)
        + jnp.dot(tri, lo[c * chunk:(c + 1) * chunk], preferred_element_type=F32)
        for c in range(rows // chunk)], axis=0)
    carry = carry_ref[...]
    cumf = []
    for c in range(rows // chunk):
        cumf.append(cuml[c * chunk:(c + 1) * chunk] + carry)
        carry = cumf[c][chunk - 1:chunk, :]
    carry_ref[...] = carry
    cumf = jnp.concatenate(cumf, axis=0)

    a = v2 - cuml
    t_in = lax.broadcasted_iota(jnp.int32, (rows, LANES), 0) % chunk
    cm = a
    s = 1
    while s < chunk:
        cm = jnp.where(t_in >= s, jnp.maximum(cm, pltpu.roll(cm, s, axis=0)), cm)
        s *= 2

    lane = lax.broadcasted_iota(jnp.int32, (chunk, LANES), 1)
    mprev = mprev_ref[...]
    for c in range(rows // chunk):
        sl = slice(c * chunk, (c + 1) * chunk)
        b = cuml[sl]
        mt = b + jnp.maximum(mprev, cm[sl])
        gt = b - mt
        ut = (b + mprev) - mt
        tile = jnp.where(lane < LANE_G, cumf[sl] * LOG2E,
               jnp.where(lane < LANE_U, gt,
               jnp.where(lane < LANE_A, pltpu.roll(ut, LANE_U - LANE_G, axis=1),
               jnp.where(lane < LANE_NEGM, pltpu.roll(a[sl], LANE_A - LANE_G, axis=1),
                         pltpu.roll(-mt, LANE_NEGM - LANE_G, axis=1)))))
        mprev = mt[chunk - 1:chunk, :]
        mcol_ref[sl, :] = tile
        mrow_ref[:, sl] = tile.T[LANE_G:LANE_G + GATE_ROWS, :]
    mprev_ref[...] = mprev


def _gates(g3, bias_vec):
    b, s, _ = g3.shape
    r = GATE_R
    return pl.pallas_call(
        _gates_kernel,
        grid=(b, s // r),
        in_specs=[
            pl.BlockSpec((None, r, 2 * LANES), lambda bi, ri: (bi, ri, 0)),
            pl.BlockSpec((1, 2 * LANES), lambda bi, ri: (0, 0)),
        ],
        out_specs=[
            pl.BlockSpec((None, r, LANES), lambda bi, ri: (bi, ri, 0)),
            pl.BlockSpec((None, GATE_ROWS, r), lambda bi, ri: (bi, 0, ri)),
        ],
        out_shape=[
            jax.ShapeDtypeStruct((b, s, LANES), F32),
            jax.ShapeDtypeStruct((b, GATE_ROWS, s), F32),
        ],
        scratch_shapes=[pltpu.VMEM((1, LANES), F32), pltpu.VMEM((1, LANES), F32)],
        compiler_params=_params(("arbitrary", "arbitrary")),
        name="gates",
    )(g3, bias_vec)


def _fox_kernel(it_ref, jt_ref, qt_ref, k_ref, vt_ref, gc_ref, zt_ref, nwb_ref, o_ref, m_ref, acc_ref):
    step_id = pl.program_id(1)
    i = it_ref[step_id]
    j = jt_ref[step_id]
    t = ATT_T
    hd = FOX_HEAD_DIM

    @pl.when(j == 0)
    def _():
        m_ref[...] = jnp.full_like(m_ref, NEG_INF)
        acc_ref[...] = jnp.zeros_like(acc_ref)

    def step(masked):
        gc = gc_ref[...]
        tqs = t // ATT_QSPLIT
        units = [(h, c) for h in range(FOX_HEADS) for c in range(ATT_QSPLIT)]
        nkeys = lambda c: (c + 1) * tqs if masked else t

        def scores(h, c):
            hs = slice(h * hd, (h + 1) * hd)
            nk = nkeys(c)
            st = jnp.dot(k_ref[0:nk, hs], qt_ref[hs, c * tqs:(c + 1) * tqs],
                         preferred_element_type=F32) - gc[0:nk, h:h + 1]
            if masked:
                valid = (lax.broadcasted_iota(jnp.int32, (nk, tqs), 0)
                         <= lax.broadcasted_iota(jnp.int32, (nk, tqs), 1) + c * tqs)
                st = jnp.where(valid, st, NEG_INF)
            return st

        def probs(h, c, st):
            qs = slice(c * tqs, (c + 1) * tqs)
            m_prev = m_ref[h:h + 1, qs]
            m_new = jnp.maximum(m_prev, jnp.max(st, axis=0, keepdims=True))
            m_ref[h:h + 1, qs] = m_new
            return jnp.exp2(m_prev - m_new), jnp.exp2(st - m_new).astype(BF16)

        def accumulate(h, c, alpha, pt):
            qs = slice(c * tqs, (c + 1) * tqs)
            nk = nkeys(c)
            vt_aug = jnp.concatenate([vt_ref[h * hd:(h + 1) * hd, 0:nk], jnp.ones((ONES_ROWS, nk), BF16)], axis=0)
            acc_ref[h, :, qs] = alpha * acc_ref[h, :, qs] + jnp.dot(vt_aug, pt, preferred_element_type=F32)

        st, ap = {}, {}
        for u in range(-ATT_LOOKAHEAD, len(units) + 1):
            if 0 <= u + ATT_LOOKAHEAD < len(units):
                st[u + ATT_LOOKAHEAD] = scores(*units[u + ATT_LOOKAHEAD])
            if 0 <= u < len(units):
                ap[u] = probs(*units[u], st.pop(u))
            if 0 <= u - 1 < len(units):
                accumulate(*units[u - 1], *ap.pop(u - 1))

    @pl.when(j < i)
    def _():
        step(False)

    @pl.when(j == i)
    def _():
        step(True)
        for h in range(FOX_HEADS):
            hs = slice(h * hd, (h + 1) * hd)
            acc = acc_ref[h]
            o = acc[:hd] * (1.0 / acc[hd:hd + 1])
            o = o * lax.rsqrt(jnp.mean(o * o, axis=0, keepdims=True) + NORM_EPS) * nwb_ref[hs, :]
            z = zt_ref[hs, :]
            o_ref[hs, :] = (o * (z * _sigmoid(z))).astype(BF16)


def _fox(kr, t16, t32, mcol, nwb):
    b, s, _ = kr.shape
    t = ATT_T
    n = s // t
    pairs = [(i, j) for i in range(n) for j in range(i + 1)]
    it = jnp.asarray([p[0] for p in pairs], jnp.int32)
    jt = jnp.asarray([p[1] for p in pairs], jnp.int32)

    grid_spec = pltpu.PrefetchScalarGridSpec(
        num_scalar_prefetch=2,
        grid=(b, len(pairs)),
        in_specs=[
            pl.BlockSpec((None, FOX_WIDTH, t), lambda bi, p, it, jt: (bi, 0, it[p])),
            pl.BlockSpec((None, t, FOX_WIDTH), lambda bi, p, it, jt: (bi, jt[p], 0)),
            pl.BlockSpec((None, FOX_WIDTH, t), lambda bi, p, it, jt: (bi, 1, jt[p])),
            pl.BlockSpec((None, t, LANES), lambda bi, p, it, jt: (bi, jt[p], 0)),
            pl.BlockSpec((None, FOX_WIDTH, t), lambda bi, p, it, jt: (bi, 0, it[p])),
            pl.BlockSpec((FOX_WIDTH, t), lambda bi, p, it, jt: (0, 0)),
        ],
        out_specs=pl.BlockSpec((None, FOX_WIDTH, t), lambda bi, p, it, jt: (bi, 0, it[p])),
        scratch_shapes=[pltpu.VMEM((FOX_HEADS, t), F32),
                        pltpu.VMEM((FOX_HEADS, FOX_HEAD_DIM + ONES_ROWS, t), F32)],
    )
    return pl.pallas_call(
        _fox_kernel,
        grid_spec=grid_spec,
        out_shape=jax.ShapeDtypeStruct((b, FOX_WIDTH, s), BF16),
        compiler_params=_params(("arbitrary", "arbitrary")),
        name="fox_attention",
    )(it, jt, t16, kr, t16, mcol, t32, nwb)


def _mlstm_kernel(qk_ref, vt_ref, ogt_ref, zt_ref, mcol_ref, mrow_ref, cw_ref, cb_ref, nwb_ref,
                  out_ref, hist_ref, st_ref):
    n = MLSTM_L
    dv = MLSTM_V_DIM

    @pl.when(pl.program_id(1) == 0)
    def _():
        hist_ref[...] = jnp.zeros_like(hist_ref)
        st_ref[...] = jnp.zeros_like(st_ref)

    u = qk_ref[...]
    ext = jnp.concatenate([hist_ref[...], u], axis=0)
    hist_ref[...] = u[n - F32_SUBLANES:n, :]

    def tap(jj):
        back = CONV_WIDTH - 1 - jj
        return pltpu.roll(ext, back, axis=0)[F32_SUBLANES:, :] if back else u

    y = tap(0) * cw_ref[0:1, :]
    for jj in range(1, CONV_WIDTH):
        y = y + tap(jj) * cw_ref[jj:jj + 1, :]
    y = y + cb_ref[...]
    sy = y * _sigmoid(y)

    mc = mcol_ref[...]
    mr = mrow_ref[...]
    lane = lax.broadcasted_iota(jnp.int32, (n, LANES), 1)
    ones_rows = jnp.where(lax.broadcasted_iota(jnp.int32, (ONES_ROWS, n), 0) == 0, 1.0, 0.0).astype(BF16)
    valid = lax.broadcasted_iota(jnp.int32, (n, n), 0) <= lax.broadcasted_iota(jnp.int32, (n, n), 1)
    heads = range(MLSTM_HEADS)
    row = lambda base_row, h: mr[base_row + h:base_row + h + 1, :]

    qp, kh, st_t, inter_t, vt_aug = [], [], [], [], []
    for h in heads:
        pr = h // 2
        qp.append(sy[:, pr * LANES:(pr + 1) * LANES].astype(BF16))
        kp = sy[:, MLSTM_QK_WIDTH + pr * LANES:MLSTM_QK_WIDTH + (pr + 1) * LANES] * (MLSTM_QK_DIM ** -0.5)
        kh.append(jnp.where((lane < MLSTM_QK_DIM) if h % 2 == 0 else (lane >= MLSTM_QK_DIM), kp, 0.0))
        st_t.append(lax.dot_general(kh[h].astype(BF16), qp[h], NT_DIMS, preferred_element_type=F32))
        inter_t.append(lax.dot_general(st_ref[h].astype(BF16), qp[h], NT_DIMS, preferred_element_type=F32))
        vt_aug.append(jnp.concatenate([vt_ref[h * dv:(h + 1) * dv, :], ones_rows], axis=0))

    for h in heads:
        hs = slice(h * dv, (h + 1) * dv)
        a_col = mc[:, LANE_A + h:LANE_A + h + 1]
        sc_t = st_t[h] * jnp.exp(jnp.where(valid, a_col + row(ROW_G, h), NEG_INF))
        res = (jnp.dot(vt_aug[h], sc_t.astype(BF16), preferred_element_type=F32)
               + jnp.exp(row(ROW_U, h)) * inter_t[h])
        den = res[dv:dv + 1, :]
        ht = res[:dv] * (1.0 / jnp.maximum(jnp.abs(den), jnp.exp(row(ROW_NEGM, h))))
        hb = ht * _sigmoid(ogt_ref[hs, :])
        hn = hb * lax.rsqrt(jnp.mean(hb * hb, axis=0, keepdims=True) + NORM_EPS) * nwb_ref[hs, :]
        z = zt_ref[hs, :]
        out_ref[hs, :] = (hn * (z * _sigmoid(z))).astype(BF16)

    for h in heads:
        a_col = mc[:, LANE_A + h:LANE_A + h + 1]
        g_last = mc[n - 1:n, LANE_G + h:LANE_G + h + 1]
        u_last = mc[n - 1:n, LANE_U + h:LANE_U + h + 1]
        kw = (kh[h] * jnp.exp(a_col + g_last)).astype(BF16)
        upd = jnp.dot(vt_aug[h], kw, preferred_element_type=F32)
        st_ref[h] = jnp.exp(u_last) * st_ref[h] + upd


def _mlstm(r32, t16, t32, mcol, mrow, cw, cb, nwb):
    b, s, _ = r32.shape
    n = MLSTM_L

    def t_map(which):
        return lambda bi, c: (bi, which, c)

    return pl.pallas_call(
        _mlstm_kernel,
        grid=(b, s // n),
        in_specs=[
            pl.BlockSpec((None, n, SEG), lambda bi, c: (bi, c, 0)),
            pl.BlockSpec((None, SEG, n), t_map(2)),
            pl.BlockSpec((None, SEG, n), t_map(1)),
            pl.BlockSpec((None, SEG, n), t_map(2)),
            pl.BlockSpec((None, n, LANES), lambda bi, c: (bi, c, 0)),
            pl.BlockSpec((None, GATE_ROWS, n), lambda bi, c: (bi, 0, c)),
            pl.BlockSpec((CONV_WIDTH, SEG), lambda bi, c: (0, 0)),
            pl.BlockSpec((1, SEG), lambda bi, c: (0, 0)),
            pl.BlockSpec((SEG, n), lambda bi, c: (0, 0)),
        ],
        out_specs=pl.BlockSpec((None, SEG, n), lambda bi, c: (bi, 0, c)),
        out_shape=jax.ShapeDtypeStruct((b, MLSTM_V_WIDTH, s), BF16),
        scratch_shapes=[pltpu.VMEM((F32_SUBLANES, SEG), F32),
                        pltpu.VMEM((MLSTM_HEADS, MLSTM_V_DIM + ONES_ROWS, LANES), F32)],
        compiler_params=_params(("arbitrary", "arbitrary")),
        name="mlstm",
    )(r32, t16, t32, t32, mcol, mrow, cw, cb, nwb)


def _outproj_kernel(yat_ref, ybt_ref, wa_ref, wb_ref, x_ref, fw_ref, o_ref, *, final):
    acc = (lax.dot_general(yat_ref[...], wa_ref[...], TN_DIMS, preferred_element_type=F32)
           + lax.dot_general(ybt_ref[...], wb_ref[...], TN_DIMS, preferred_element_type=F32))
    r = x_ref[...] + acc
    if final:
        r = r * lax.rsqrt(jnp.mean(r * r, axis=-1, keepdims=True) + NORM_EPS) * fw_ref[...]
    o_ref[...] = r


def _outproj(yat, ybt, wa, wb, x2, fw, final):
    m = x2.shape[0]
    _, _, seq = yat.shape
    tm = OUT_TM
    per_seq = seq // tm
    y_map = lambda i: (i // per_seq, 0, i % per_seq)
    return pl.pallas_call(
        functools.partial(_outproj_kernel, final=final),
        grid=(m // tm,),
        in_specs=[
            pl.BlockSpec((None, FOX_WIDTH, tm), y_map),
            pl.BlockSpec((None, MLSTM_V_WIDTH, tm), y_map),
            pl.BlockSpec((FOX_WIDTH, D_MODEL), lambda i: (0, 0)),
            pl.BlockSpec((MLSTM_V_WIDTH, D_MODEL), lambda i: (0, 0)),
            pl.BlockSpec((tm, D_MODEL), lambda i: (i, 0)),
            pl.BlockSpec((1, D_MODEL), lambda i: (0, 0)),
        ],
        out_specs=pl.BlockSpec((tm, D_MODEL), lambda i: (i, 0)),
        out_shape=jax.ShapeDtypeStruct((m, D_MODEL), F32),
        compiler_params=_params(("arbitrary",), BIG_TILE_VMEM_LIMIT),
        name="outproj",
    )(yat, ybt, wa, wb, x2, fw)


SEG_STARTS = (IN_OFFS[0], IN_OFFS[1], IN_OFFS[5], IN_OFFS[2], IN_OFFS[7], IN_OFFS[3], IN_OFFS[8], IN_OFFS[9])
SEG_SHIFT = FOX_HEADS
assert all(st % SEG in (0, SEG_SHIFT) for st in SEG_STARTS) and IN_OFFS[6] == IN_OFFS[5] + MLSTM_QK_WIDTH
assert SEG_SHIFT == F32_SUBLANES and MLSTM_HEADS == F32_SUBLANES

J_FOX_GATE = 2
assert SEG_STARTS[J_FOX_GATE] - SEG_SHIFT == IN_OFFS[4]
GW_FOX_F, GW_MLSTM_F, GW_MLSTM_I = 0, 8, LANES + 8


def _wprep_kernel(main_blk_ref, extra_blk_ref, main_ref, extra_ref, mi_ref, mf_ref, o_ref, gw_ref, gwf_ref):
    del main_blk_ref, extra_blk_ref
    j = pl.program_id(0)
    shifted = functools.reduce(jnp.logical_or, [j == k for k, st in enumerate(SEG_STARTS) if st % SEG])

    @pl.when(j == 0)
    def _():
        gwf_ref[...] = jnp.zeros_like(gwf_ref)
        gwf_ref[GW_MLSTM_I:GW_MLSTM_I + MLSTM_HEADS, :] = mi_ref[...]
        gwf_ref[GW_MLSTM_F:GW_MLSTM_F + MLSTM_HEADS, :] = mf_ref[...]

    @pl.when(jnp.logical_not(shifted))
    def _():
        o_ref[...] = main_ref[...].astype(BF16)

    @pl.when(shifted)
    def _():
        full = jnp.concatenate([main_ref[...], extra_ref[...]], axis=0)
        o_ref[...] = full[SEG_SHIFT:SEG_SHIFT + SEG, :].astype(BF16)

    @pl.when(j == J_FOX_GATE)
    def _():
        gwf_ref[GW_FOX_F:GW_FOX_F + FOX_HEADS, :] = main_ref[0:SEG_SHIFT, :]

    @pl.when(j == len(SEG_STARTS) - 1)
    def _():
        gw_ref[...] = gwf_ref[...].astype(BF16)


def _wprep(w_t):
    d = w_t.shape[1]
    main_blk = jnp.asarray([st // SEG for st in SEG_STARTS], jnp.int32)
    extra_blk = jnp.asarray([(st // SEG + 1) * (SEG // F32_SUBLANES) for st in SEG_STARTS], jnp.int32)
    grid_spec = pltpu.PrefetchScalarGridSpec(
        num_scalar_prefetch=2,
        grid=(len(SEG_STARTS),),
        in_specs=[
            pl.BlockSpec((SEG, d), lambda j, mb, eb: (mb[j], 0)),
            pl.BlockSpec((F32_SUBLANES, d), lambda j, mb, eb: (eb[j], 0)),
            pl.BlockSpec((MLSTM_HEADS, d), lambda j, mb, eb: (IN_OFFS[10] // F32_SUBLANES, 0)),
            pl.BlockSpec((MLSTM_HEADS, d), lambda j, mb, eb: (IN_OFFS[11] // F32_SUBLANES, 0)),
        ],
        out_specs=[
            pl.BlockSpec((SEG, d), lambda j, mb, eb: (j, 0)),
            pl.BlockSpec((2 * LANES, d), lambda j, mb, eb: (0, 0)),
        ],
        scratch_shapes=[pltpu.VMEM((2 * LANES, d), F32)],
    )
    return pl.pallas_call(
        _wprep_kernel,
        grid_spec=grid_spec,
        out_shape=[jax.ShapeDtypeStruct((len(SEG_STARTS) * SEG, d), BF16),
                   jax.ShapeDtypeStruct((2 * LANES, d), BF16)],
        compiler_params=_params(("arbitrary",)),
        name="weight_layout",
    )(main_blk, extra_blk, w_t, w_t, w_t, w_t)


def _lane_broadcast(v, width):
    return jnp.broadcast_to(v[:, None], (v.shape[0], width))


def kernel(x, norm_w, w_in, fox_f_bias, conv_w, conv_b, mlstm_i_bias, mlstm_f_bias,
           fox_out_norm_w, mlstm_out_norm_w, w_out, final_norm_w):
    b, s, d = x.shape
    depth = norm_w.shape[0]
    x2 = x.reshape(b * s, d)
    for l in range(depth):
        w_t, w_gate = _wprep(w_in[l].T)
        bias_vec = jnp.concatenate([
            fox_f_bias[l], mlstm_f_bias[l], jnp.zeros((LANES - 16,), F32),
            jnp.zeros((8,), F32), mlstm_i_bias[l], jnp.zeros((LANES - 16,), F32)]).reshape(1, 2 * LANES)

        kr, qkr, t16, t32, g = _inproj(x2, norm_w[l].reshape(1, d), w_t, w_gate, s)
        mcol, mrow = _gates(g.reshape(b, s, 2 * LANES), bias_vec)

        yat = _fox(kr.reshape(b, s, SEG), t16, t32, mcol, _lane_broadcast(fox_out_norm_w[l], ATT_T))
        ybt = _mlstm(qkr.reshape(b, s, SEG), t16, t32, mcol, mrow, conv_w[l], conv_b[l].reshape(1, SEG),
                     _lane_broadcast(mlstm_out_norm_w[l], MLSTM_L))

        wo = w_out[l].astype(BF16)
        x2 = _outproj(yat, ybt, wo[:FOX_WIDTH], wo[FOX_WIDTH:], x2, final_norm_w.reshape(1, d),
                      final=(l == depth - 1))
    return x2.reshape(b, s, d)
```

```python
import functools

import numpy as np
import jax
import jax.numpy as jnp
from jax import lax
from jax.experimental import pallas as pl
from jax.experimental.pallas import tpu as pltpu

D_MODEL = 2048
NORM_EPS = 1e-6
NEG_INF = -1e30

FOX_HEADS = 8
FOX_HEAD_DIM = 128
FOX_WIDTH = FOX_HEADS * FOX_HEAD_DIM
MLSTM_HEADS = 8
MLSTM_V_DIM = 128
MLSTM_QK_DIM = 64
MLSTM_V_WIDTH = MLSTM_HEADS * MLSTM_V_DIM
MLSTM_QK_WIDTH = MLSTM_HEADS * MLSTM_QK_DIM
CONV_WIDTH = 4
IN_SIZES = (FOX_WIDTH, FOX_WIDTH, FOX_WIDTH, FOX_WIDTH, FOX_HEADS,
            MLSTM_QK_WIDTH, MLSTM_QK_WIDTH, MLSTM_V_WIDTH, MLSTM_V_WIDTH, MLSTM_V_WIDTH,
            MLSTM_HEADS, MLSTM_HEADS)
IN_OFFS = tuple(int(v) for v in np.cumsum((0,) + IN_SIZES))

LANES = 128
F32_SUBLANES = 8
SEG = 1024
VMEM_LIMIT = 56 * 1024 * 1024
BIG_TILE_VMEM_LIMIT = 62 * 1024 * 1024

PROJ_TM = 256
GATE_R = 1024
MLSTM_L = 256
ATT_T = 512
OUT_TM = 1024

LOG2E = 1.4426950408889634
FOX_Q_SCALE = FOX_HEAD_DIM ** -0.5 * LOG2E
ONES_ROWS = 16
ATT_QBLOCKS = 2
ATT_LOOKAHEAD = 1

LANE_G, LANE_U, LANE_A, LANE_NEGM = 8, 16, 24, 32
ROW_G, ROW_U, ROW_NEGM = 0, 8, 24
GATE_ROWS = 32

BF16 = jnp.bfloat16
F32 = jnp.float32
NT_DIMS = (((1,), (1,)), ((), ()))
TN_DIMS = (((0,), (0,)), ((), ()))


def _sigmoid(x):
    return 1.0 / (1.0 + jnp.exp(-x))


def _params(sem, vmem_limit=VMEM_LIMIT):
    return pltpu.CompilerParams(dimension_semantics=sem, vmem_limit_bytes=vmem_limit)


def _inproj_kernel(x_ref, nw_ref, wt_ref, wg_ref, kr_ref, qkr_ref, t16_ref, t32_ref, g_ref):
    xf = x_ref[...]
    ms = jnp.mean(xf * xf, axis=-1, keepdims=True)
    hb = (xf * lax.rsqrt(ms + NORM_EPS) * nw_ref[...]).astype(BF16)

    def seg_t(j):
        return lax.dot_general(wt_ref[j * SEG:(j + 1) * SEG, :], hb, NT_DIMS, preferred_element_type=F32)

    t16_ref[0:SEG, :] = (seg_t(0) * FOX_Q_SCALE).astype(BF16)
    kr_ref[...] = seg_t(1).T.astype(BF16)
    qkr_ref[...] = seg_t(2).T
    t16_ref[SEG:2 * SEG, :] = seg_t(3).astype(BF16)
    t16_ref[2 * SEG:3 * SEG, :] = seg_t(4).astype(BF16)
    for r in range(3):
        t32_ref[r * SEG:(r + 1) * SEG, :] = seg_t(5 + r)
    g_ref[...] = lax.dot_general(hb, wg_ref[...], NT_DIMS, preferred_element_type=F32)


def _inproj(x2, nw, w_t, w_gate, seq):
    m = x2.shape[0]
    tm = PROJ_TM
    per_seq = seq // tm
    t_map = lambda i: (i // per_seq, 0, i % per_seq)
    return pl.pallas_call(
        _inproj_kernel,
        grid=(m // tm,),
        in_specs=[
            pl.BlockSpec((tm, D_MODEL), lambda i: (i, 0)),
            pl.BlockSpec((1, D_MODEL), lambda i: (0, 0)),
            pl.BlockSpec(w_t.shape, lambda i: (0, 0)),
            pl.BlockSpec((2 * LANES, D_MODEL), lambda i: (0, 0)),
        ],
        out_specs=[
            pl.BlockSpec((tm, SEG), lambda i: (i, 0)),
            pl.BlockSpec((tm, SEG), lambda i: (i, 0)),
            pl.BlockSpec((None, 3 * SEG, tm), t_map),
            pl.BlockSpec((None, 3 * SEG, tm), t_map),
            pl.BlockSpec((tm, 2 * LANES), lambda i: (i, 0)),
        ],
        out_shape=[
            jax.ShapeDtypeStruct((m, SEG), BF16),
            jax.ShapeDtypeStruct((m, SEG), F32),
            jax.ShapeDtypeStruct((m // seq, 3 * SEG, seq), BF16),
            jax.ShapeDtypeStruct((m // seq, 3 * SEG, seq), F32),
            jax.ShapeDtypeStruct((m, 2 * LANES), F32),
        ],
        compiler_params=_params(("arbitrary",), BIG_TILE_VMEM_LIMIT),
        name="inproj",
    )(x2, nw, w_t, w_gate)


def _split3(v):
    hi = v.astype(BF16)
    r1 = v - hi.astype(F32)
    mid = r1.astype(BF16)
    lo = (r1 - mid.astype(F32)).astype(BF16)
    return hi, mid, lo


def _gates_kernel(g_ref, b_ref, mcol_ref, mrow_ref, carry_ref, mprev_ref):
    rows, chunk = GATE_R, MLSTM_L

    @pl.when(pl.program_id(1) == 0)
    def _():
        carry_ref[...] = jnp.zeros_like(carry_ref)
        mprev_ref[...] = jnp.zeros_like(mprev_ref)

    v = g_ref[...] + b_ref[...]
    v1 = v[:, :LANES]
    v2 = v[:, LANES:]
    ls = jnp.minimum(v1, 0.0) - jnp.log1p(jnp.exp(-jnp.abs(v1)))

    tri = (lax.broadcasted_iota(jnp.int32, (chunk, chunk), 1)
           <= lax.broadcasted_iota(jnp.int32, (chunk, chunk), 0)).astype(BF16)
    hi, mid, lo = _split3(ls)
    cuml = jnp.concatenate([
        jnp.dot(tri, hi[c * chunk:(c + 1) * chunk], preferred_element_type=F32)
        + jnp.dot(tri, mid[c * chunk:(c + 1) * chunk], preferred_element_type=F32)
        + jnp.dot(tri, lo[c * chunk:(c + 1) * chunk], preferred_element_type=F32)
        for c in range(rows // chunk)], axis=0)
    carry = carry_ref[...]
    cumf = []
    for c in range(rows // chunk):
        cumf.append(cuml[c * chunk:(c + 1) * chunk] + carry)
        carry = cumf[c][chunk - 1:chunk, :]
    carry_ref[...] = carry
    cumf = jnp.concatenate(cumf, axis=0)

    a = v2 - cuml
    t_in = lax.broadcasted_iota(jnp.int32, (rows, LANES), 0) % chunk
    cm = a
    s = 1
    while s < chunk:
        cm = jnp.where(t_in >= s, jnp.maximum(cm, pltpu.roll(cm, s, axis=0)), cm)
        s *= 2

    lane = lax.broadcasted_iota(jnp.int32, (chunk, LANES), 1)
    mprev = mprev_ref[...]
    for c in range(rows // chunk):
        sl = slice(c * chunk, (c + 1) * chunk)
        b = cuml[sl]
        mt = b + jnp.maximum(mprev, cm[sl])
        gt = b - mt
        ut = (b + mprev) - mt
        tile = jnp.where(lane < LANE_G, cumf[sl] * LOG2E,
               jnp.where(lane < LANE_U, gt,
               jnp.where(lane < LANE_A, pltpu.roll(ut, LANE_U - LANE_G, axis=1),
               jnp.where(lane < LANE_NEGM, pltpu.roll(a[sl], LANE_A - LANE_G, axis=1),
                         pltpu.roll(-mt, LANE_NEGM - LANE_G, axis=1)))))
        mprev = mt[chunk - 1:chunk, :]
        mcol_ref[sl, :] = tile
        mrow_ref[:, sl] = tile.T[LANE_G:LANE_G + GATE_ROWS, :]
    mprev_ref[...] = mprev


def _gates(g3, bias_vec):
    b, s, _ = g3.shape
    r = GATE_R
    return pl.pallas_call(
        _gates_kernel,
        grid=(b, s // r),
        in_specs=[
            pl.BlockSpec((None, r, 2 * LANES), lambda bi, ri: (bi, ri, 0)),
            pl.BlockSpec((1, 2 * LANES), lambda bi, ri: (0, 0)),
        ],
        out_specs=[
            pl.BlockSpec((None, r, LANES), lambda bi, ri: (bi, ri, 0)),
            pl.BlockSpec((None, GATE_ROWS, r), lambda bi, ri: (bi, 0, ri)),
        ],
        out_shape=[
            jax.ShapeDtypeStruct((b, s, LANES), F32),
            jax.ShapeDtypeStruct((b, GATE_ROWS, s), F32),
        ],
        scratch_shapes=[pltpu.VMEM((1, LANES), F32), pltpu.VMEM((1, LANES), F32)],
        compiler_params=_params(("arbitrary", "arbitrary")),
        name="gates",
    )(g3, bias_vec)


def _fox_kernel(it_ref, jt_ref, qt_ref, k_ref, vt_ref, gc_ref, zt_ref, nwb_ref, o_ref, m_ref, acc_ref):
    step_id = pl.program_id(1)
    i = it_ref[step_id]
    j = jt_ref[step_id]
    t = ATT_T
    hd = FOX_HEAD_DIM

    @pl.when(j == 0)
    def _():
        m_ref[...] = jnp.full_like(m_ref, NEG_INF)
        acc_ref[...] = jnp.zeros_like(acc_ref)

    def step(modes):
        gc = gc_ref[...]
        ones = jnp.ones((ONES_ROWS, t), BF16)
        units = [(h, c) for h in range(FOX_HEADS) for c in range(ATT_QBLOCKS) if modes[c]]

        def scores(h, c):
            hs = slice(h * hd, (h + 1) * hd)
            st = jnp.dot(k_ref[:, hs], qt_ref[hs, c * t:(c + 1) * t],
                         preferred_element_type=F32) - gc[:, h:h + 1]
            if modes[c] == 'diag':
                valid = (lax.broadcasted_iota(jnp.int32, (t, t), 0)
                         <= lax.broadcasted_iota(jnp.int32, (t, t), 1))
                st = jnp.where(valid, st, NEG_INF)
            return st

        def probs(h, c, st):
            qs = slice(c * t, (c + 1) * t)
            m_prev = m_ref[h:h + 1, qs]
            m_new = jnp.maximum(m_prev, jnp.max(st, axis=0, keepdims=True))
            m_ref[h:h + 1, qs] = m_new
            return jnp.exp2(m_prev - m_new), jnp.exp2(st - m_new).astype(BF16)

        def accumulate(h, c, alpha, pt):
            qs = slice(c * t, (c + 1) * t)
            vt_aug = jnp.concatenate([vt_ref[h * hd:(h + 1) * hd, :], ones], axis=0)
            acc_ref[h, :, qs] = alpha * acc_ref[h, :, qs] + jnp.dot(vt_aug, pt, preferred_element_type=F32)

        st, ap = {}, {}
        for u in range(-ATT_LOOKAHEAD, len(units) + 1):
            if 0 <= u + ATT_LOOKAHEAD < len(units):
                st[u + ATT_LOOKAHEAD] = scores(*units[u + ATT_LOOKAHEAD])
            if 0 <= u < len(units):
                ap[u] = probs(*units[u], st.pop(u))
            if 0 <= u - 1 < len(units):
                accumulate(*units[u - 1], *ap.pop(u - 1))

    first = i * ATT_QBLOCKS

    @pl.when(j < first)
    def _():
        step(('full',) * ATT_QBLOCKS)

    for r in range(ATT_QBLOCKS):
        @pl.when(j == first + r)
        def _(r=r):
            step(tuple(None if c < r else 'diag' if c == r else 'full' for c in range(ATT_QBLOCKS)))
            if r == ATT_QBLOCKS - 1:
                for h in range(FOX_HEADS):
                    hs = slice(h * hd, (h + 1) * hd)
                    for c in range(ATT_QBLOCKS):
                        qs = slice(c * t, (c + 1) * t)
                        acc = acc_ref[h, :, qs]
                        o = acc[:hd] * (1.0 / acc[hd:hd + 1])
                        o = o * lax.rsqrt(jnp.mean(o * o, axis=0, keepdims=True) + NORM_EPS) * nwb_ref[hs, :]
                        z = zt_ref[hs, qs]
                        o_ref[hs, qs] = (o * (z * _sigmoid(z))).astype(BF16)


def _fox(kr, t16, t32, mcol, nwb):
    b, s, _ = kr.shape
    t = ATT_T
    tq = ATT_T * ATT_QBLOCKS
    pairs = [(i, j) for i in range(s // tq) for j in range((i + 1) * ATT_QBLOCKS)]
    it = jnp.asarray([p[0] for p in pairs], jnp.int32)
    jt = jnp.asarray([p[1] for p in pairs], jnp.int32)

    grid_spec = pltpu.PrefetchScalarGridSpec(
        num_scalar_prefetch=2,
        grid=(b, len(pairs)),
        in_specs=[
            pl.BlockSpec((None, FOX_WIDTH, tq), lambda bi, p, it, jt: (bi, 0, it[p])),
            pl.BlockSpec((None, t, FOX_WIDTH), lambda bi, p, it, jt: (bi, jt[p], 0)),
            pl.BlockSpec((None, FOX_WIDTH, t), lambda bi, p, it, jt: (bi, 1, jt[p])),
            pl.BlockSpec((None, t, LANES), lambda bi, p, it, jt: (bi, jt[p], 0)),
            pl.BlockSpec((None, FOX_WIDTH, tq), lambda bi, p, it, jt: (bi, 0, it[p])),
            pl.BlockSpec((FOX_WIDTH, t), lambda bi, p, it, jt: (0, 0)),
        ],
        out_specs=pl.BlockSpec((None, FOX_WIDTH, tq), lambda bi, p, it, jt: (bi, 0, it[p])),
        scratch_shapes=[pltpu.VMEM((FOX_HEADS, tq), F32),
                        pltpu.VMEM((FOX_HEADS, FOX_HEAD_DIM + ONES_ROWS, tq), F32)],
    )
    return pl.pallas_call(
        _fox_kernel,
        grid_spec=grid_spec,
        out_shape=jax.ShapeDtypeStruct((b, FOX_WIDTH, s), BF16),
        compiler_params=_params(("arbitrary", "arbitrary")),
        name="fox_attention",
    )(it, jt, t16, kr, t16, mcol, t32, nwb)


def _mlstm_kernel(qk_ref, vt_ref, ogt_ref, zt_ref, mcol_ref, mrow_ref, cw_ref, cb_ref, nwb_ref,
                  out_ref, hist_ref, st_ref):
    n = MLSTM_L
    dv = MLSTM_V_DIM

    @pl.when(pl.program_id(1) == 0)
    def _():
        hist_ref[...] = jnp.zeros_like(hist_ref)
        st_ref[...] = jnp.zeros_like(st_ref)

    u = qk_ref[...]
    ext = jnp.concatenate([hist_ref[...], u], axis=0)
    hist_ref[...] = u[n - F32_SUBLANES:n, :]

    def tap(jj):
        back = CONV_WIDTH - 1 - jj
        return pltpu.roll(ext, back, axis=0)[F32_SUBLANES:, :] if back else u

    y = tap(0) * cw_ref[0:1, :]
    for jj in range(1, CONV_WIDTH):
        y = y + tap(jj) * cw_ref[jj:jj + 1, :]
    y = y + cb_ref[...]
    sy = y * _sigmoid(y)

    mc = mcol_ref[...]
    mr = mrow_ref[...]
    lane = lax.broadcasted_iota(jnp.int32, (n, LANES), 1)
    ones_rows = jnp.where(lax.broadcasted_iota(jnp.int32, (ONES_ROWS, n), 0) == 0, 1.0, 0.0).astype(BF16)
    valid = lax.broadcasted_iota(jnp.int32, (n, n), 0) <= lax.broadcasted_iota(jnp.int32, (n, n), 1)
    heads = range(MLSTM_HEADS)
    row = lambda base_row, h: mr[base_row + h:base_row + h + 1, :]

    qp, kh, st_t, inter_t, vt_aug = [], [], [], [], []
    for h in heads:
        pr = h // 2
        qp.append(sy[:, pr * LANES:(pr + 1) * LANES].astype(BF16))
        kp = sy[:, MLSTM_QK_WIDTH + pr * LANES:MLSTM_QK_WIDTH + (pr + 1) * LANES] * (MLSTM_QK_DIM ** -0.5)
        kh.append(jnp.where((lane < MLSTM_QK_DIM) if h % 2 == 0 else (lane >= MLSTM_QK_DIM), kp, 0.0))
        st_t.append(lax.dot_general(kh[h].astype(BF16), qp[h], NT_DIMS, preferred_element_type=F32))
        inter_t.append(lax.dot_general(st_ref[h].astype(BF16), qp[h], NT_DIMS, preferred_element_type=F32))
        vt_aug.append(jnp.concatenate([vt_ref[h * dv:(h + 1) * dv, :], ones_rows], axis=0))

    for h in heads:
        hs = slice(h * dv, (h + 1) * dv)
        a_col = mc[:, LANE_A + h:LANE_A + h + 1]
        sc_t = st_t[h] * jnp.exp(jnp.where(valid, a_col + row(ROW_G, h), NEG_INF))
        res = (jnp.dot(vt_aug[h], sc_t.astype(BF16), preferred_element_type=F32)
               + jnp.exp(row(ROW_U, h)) * inter_t[h])
        den = res[dv:dv + 1, :]
        ht = res[:dv] * (1.0 / jnp.maximum(jnp.abs(den), jnp.exp(row(ROW_NEGM, h))))
        hb = ht * _sigmoid(ogt_ref[hs, :])
        hn = hb * lax.rsqrt(jnp.mean(hb * hb, axis=0, keepdims=True) + NORM_EPS) * nwb_ref[hs, :]
        z = zt_ref[hs, :]
        out_ref[hs, :] = (hn * (z * _sigmoid(z))).astype(BF16)

    for h in heads:
        a_col = mc[:, LANE_A + h:LANE_A + h + 1]
        g_last = mc[n - 1:n, LANE_G + h:LANE_G + h + 1]
        u_last = mc[n - 1:n, LANE_U + h:LANE_U + h + 1]
        kw = (kh[h] * jnp.exp(a_col + g_last)).astype(BF16)
        upd = jnp.dot(vt_aug[h], kw, preferred_element_type=F32)
        st_ref[h] = jnp.exp(u_last) * st_ref[h] + upd


def _mlstm(r32, t16, t32, mcol, mrow, cw, cb, nwb):
    b, s, _ = r32.shape
    n = MLSTM_L

    def t_map(which):
        return lambda bi, c: (bi, which, c)

    return pl.pallas_call(
        _mlstm_kernel,
        grid=(b, s // n),
        in_specs=[
            pl.BlockSpec((None, n, SEG), lambda bi, c: (bi, c, 0)),
            pl.BlockSpec((None, SEG, n), t_map(2)),
            pl.BlockSpec((None, SEG, n), t_map(1)),
            pl.BlockSpec((None, SEG, n), t_map(2)),
            pl.BlockSpec((None, n, LANES), lambda bi, c: (bi, c, 0)),
            pl.BlockSpec((None, GATE_ROWS, n), lambda bi, c: (bi, 0, c)),
            pl.BlockSpec((CONV_WIDTH, SEG), lambda bi, c: (0, 0)),
            pl.BlockSpec((1, SEG), lambda bi, c: (0, 0)),
            pl.BlockSpec((SEG, n), lambda bi, c: (0, 0)),
        ],
        out_specs=pl.BlockSpec((None, SEG, n), lambda bi, c: (bi, 0, c)),
        out_shape=jax.ShapeDtypeStruct((b, MLSTM_V_WIDTH, s), BF16),
        scratch_shapes=[pltpu.VMEM((F32_SUBLANES, SEG), F32),
                        pltpu.VMEM((MLSTM_HEADS, MLSTM_V_DIM + ONES_ROWS, LANES), F32)],
        compiler_params=_params(("arbitrary", "arbitrary")),
        name="mlstm",
    )(r32, t16, t32, t32, mcol, mrow, cw, cb, nwb)


def _outproj_kernel(yat_ref, ybt_ref, wa_ref, wb_ref, x_ref, fw_ref, o_ref, *, final):
    acc = (lax.dot_general(yat_ref[...], wa_ref[...], TN_DIMS, preferred_element_type=F32)
           + lax.dot_general(ybt_ref[...], wb_ref[...], TN_DIMS, preferred_element_type=F32))
    r = x_ref[...] + acc
    if final:
        r = r * lax.rsqrt(jnp.mean(r * r, axis=-1, keepdims=True) + NORM_EPS) * fw_ref[...]
    o_ref[...] = r


def _outproj(yat, ybt, wa, wb, x2, fw, final):
    m = x2.shape[0]
    _, _, seq = yat.shape
    tm = OUT_TM
    per_seq = seq // tm
    y_map = lambda i: (i // per_seq, 0, i % per_seq)
    return pl.pallas_call(
        functools.partial(_outproj_kernel, final=final),
        grid=(m // tm,),
        in_specs=[
            pl.BlockSpec((None, FOX_WIDTH, tm), y_map),
            pl.BlockSpec((None, MLSTM_V_WIDTH, tm), y_map),
            pl.BlockSpec((FOX_WIDTH, D_MODEL), lambda i: (0, 0)),
            pl.BlockSpec((MLSTM_V_WIDTH, D_MODEL), lambda i: (0, 0)),
            pl.BlockSpec((tm, D_MODEL), lambda i: (i, 0)),
            pl.BlockSpec((1, D_MODEL), lambda i: (0, 0)),
        ],
        out_specs=pl.BlockSpec((tm, D_MODEL), lambda i: (i, 0)),
        out_shape=jax.ShapeDtypeStruct((m, D_MODEL), F32),
        compiler_params=_params(("arbitrary",), BIG_TILE_VMEM_LIMIT),
        name="outproj",
    )(yat, ybt, wa, wb, x2, fw)


SEG_STARTS = (IN_OFFS[0], IN_OFFS[1], IN_OFFS[5], IN_OFFS[2], IN_OFFS[7], IN_OFFS[3], IN_OFFS[8], IN_OFFS[9])
SEG_SHIFT = FOX_HEADS
assert all(st % SEG in (0, SEG_SHIFT) for st in SEG_STARTS) and IN_OFFS[6] == IN_OFFS[5] + MLSTM_QK_WIDTH
assert SEG_SHIFT == F32_SUBLANES and MLSTM_HEADS == F32_SUBLANES

J_FOX_GATE = 2
assert SEG_STARTS[J_FOX_GATE] - SEG_SHIFT == IN_OFFS[4]
GW_FOX_F, GW_MLSTM_F, GW_MLSTM_I = 0, 8, LANES + 8


def _wprep_kernel(main_blk_ref, extra_blk_ref, main_ref, extra_ref, mi_ref, mf_ref, o_ref, gw_ref, gwf_ref):
    del main_blk_ref, extra_blk_ref
    j = pl.program_id(0)
    shifted = functools.reduce(jnp.logical_or, [j == k for k, st in enumerate(SEG_STARTS) if st % SEG])

    @pl.when(j == 0)
    def _():
        gwf_ref[...] = jnp.zeros_like(gwf_ref)
        gwf_ref[GW_MLSTM_I:GW_MLSTM_I + MLSTM_HEADS, :] = mi_ref[...]
        gwf_ref[GW_MLSTM_F:GW_MLSTM_F + MLSTM_HEADS, :] = mf_ref[...]

    @pl.when(jnp.logical_not(shifted))
    def _():
        o_ref[...] = main_ref[...].astype(BF16)

    @pl.when(shifted)
    def _():
        full = jnp.concatenate([main_ref[...], extra_ref[...]], axis=0)
        o_ref[...] = full[SEG_SHIFT:SEG_SHIFT + SEG, :].astype(BF16)

    @pl.when(j == J_FOX_GATE)
    def _():
        gwf_ref[GW_FOX_F:GW_FOX_F + FOX_HEADS, :] = main_ref[0:SEG_SHIFT, :]

    @pl.when(j == len(SEG_STARTS) - 1)
    def _():
        gw_ref[...] = gwf_ref[...].astype(BF16)


def _wprep(w_t):
    d = w_t.shape[1]
    main_blk = jnp.asarray([st // SEG for st in SEG_STARTS], jnp.int32)
    extra_blk = jnp.asarray([(st // SEG + 1) * (SEG // F32_SUBLANES) for st in SEG_STARTS], jnp.int32)
    grid_spec = pltpu.PrefetchScalarGridSpec(
        num_scalar_prefetch=2,
        grid=(len(SEG_STARTS),),
        in_specs=[
            pl.BlockSpec((SEG, d), lambda j, mb, eb: (mb[j], 0)),
            pl.BlockSpec((F32_SUBLANES, d), lambda j, mb, eb: (eb[j], 0)),
            pl.BlockSpec((MLSTM_HEADS, d), lambda j, mb, eb: (IN_OFFS[10] // F32_SUBLANES, 0)),
            pl.BlockSpec((MLSTM_HEADS, d), lambda j, mb, eb: (IN_OFFS[11] // F32_SUBLANES, 0)),
        ],
        out_specs=[
            pl.BlockSpec((SEG, d), lambda j, mb, eb: (j, 0)),
            pl.BlockSpec((2 * LANES, d), lambda j, mb, eb: (0, 0)),
        ],
        scratch_shapes=[pltpu.VMEM((2 * LANES, d), F32)],
    )
    return pl.pallas_call(
        _wprep_kernel,
        grid_spec=grid_spec,
        out_shape=[jax.ShapeDtypeStruct((len(SEG_STARTS) * SEG, d), BF16),
                   jax.ShapeDtypeStruct((2 * LANES, d), BF16)],
        compiler_params=_params(("arbitrary",)),
        name="weight_layout",
    )(main_blk, extra_blk, w_t, w_t, w_t, w_t)


def _lane_broadcast(v, width):
    return jnp.broadcast_to(v[:, None], (v.shape[0], width))


def kernel(x, norm_w, w_in, fox_f_bias, conv_w, conv_b, mlstm_i_bias, mlstm_f_bias,
           fox_out_norm_w, mlstm_out_norm_w, w_out, final_norm_w):
    b, s, d = x.shape
    depth = norm_w.shape[0]
    x2 = x.reshape(b * s, d)
    for l in range(depth):
        w_t, w_gate = _wprep(w_in[l].T)
        bias_vec = jnp.concatenate([
            fox_f_bias[l], mlstm_f_bias[l], jnp.zeros((LANES - 16,), F32),
            jnp.zeros((8,), F32), mlstm_i_bias[l], jnp.zeros((LANES - 16,), F32)]).reshape(1, 2 * LANES)

        kr, qkr, t16, t32, g = _inproj(x2, norm_w[l].reshape(1, d), w_t, w_gate, s)
        mcol, mrow = _gates(g.reshape(b, s, 2 * LANES), bias_vec)

        yat = _fox(kr.reshape(b, s, SEG), t16, t32, mcol, _lane_broadcast(fox_out_norm_w[l], ATT_T))
        ybt = _mlstm(qkr.reshape(b, s, SEG), t16, t32, mcol, mrow, conv_w[l], conv_b[l].reshape(1, SEG),
                     _lane_broadcast(mlstm_out_norm_w[l], MLSTM_L))

        wo = w_out[l].astype(BF16)
        x2 = _outproj(yat, ybt, wo[:FOX_WIDTH], wo[FOX_WIDTH:], x2, final_norm_w.reshape(1, d),
                      final=(l == depth - 1))
    return x2.reshape(b, s, d)
```

```python
import functools

import numpy as np
import jax
import jax.numpy as jnp
from jax import lax
from jax.experimental import pallas as pl
from jax.experimental.pallas import tpu as pltpu

D_MODEL = 2048
NORM_EPS = 1e-6
NEG_INF = -1e30

FOX_HEADS = 8
FOX_HEAD_DIM = 128
FOX_WIDTH = FOX_HEADS * FOX_HEAD_DIM
MLSTM_HEADS = 8
MLSTM_V_DIM = 128
MLSTM_QK_DIM = 64
MLSTM_V_WIDTH = MLSTM_HEADS * MLSTM_V_DIM
MLSTM_QK_WIDTH = MLSTM_HEADS * MLSTM_QK_DIM
CONV_WIDTH = 4
IN_SIZES = (FOX_WIDTH, FOX_WIDTH, FOX_WIDTH, FOX_WIDTH, FOX_HEADS,
            MLSTM_QK_WIDTH, MLSTM_QK_WIDTH, MLSTM_V_WIDTH, MLSTM_V_WIDTH, MLSTM_V_WIDTH,
            MLSTM_HEADS, MLSTM_HEADS)
IN_OFFS = tuple(int(v) for v in np.cumsum((0,) + IN_SIZES))

LANES = 128
F32_SUBLANES = 8
SEG = 1024
VMEM_LIMIT = 56 * 1024 * 1024
BIG_TILE_VMEM_LIMIT = 62 * 1024 * 1024

PROJ_TM = 256
GATE_R = 1024
MLSTM_L = 256
ATT_T = 512
OUT_TM = 1024
OUT_SUBBLOCKS = 4

LOG2E = 1.4426950408889634
FOX_Q_SCALE = FOX_HEAD_DIM ** -0.5 * LOG2E
ONES_ROWS = 16
ATT_QBLOCKS = 2
ATT_LOOKAHEAD = 1

LANE_G, LANE_U, LANE_A, LANE_NEGM = 8, 16, 24, 32
ROW_G, ROW_U, ROW_NEGM = 0, 8, 24
GATE_ROWS = 32

BF16 = jnp.bfloat16
F32 = jnp.float32
NT_DIMS = (((1,), (1,)), ((), ()))
TN_DIMS = (((0,), (0,)), ((), ()))


def _sigmoid(x):
    return 1.0 / (1.0 + jnp.exp(-x))


def _params(sem, vmem_limit=VMEM_LIMIT):
    return pltpu.CompilerParams(dimension_semantics=sem, vmem_limit_bytes=vmem_limit)


def _inproj_kernel(x0_ref, xn_ref, nw_ref, wt_ref, wg_ref, kr_ref, qkr_ref, t16_ref, t32_ref, g_ref, hn_ref):
    step = pl.program_id(0)

    def normed(x_ref):
        xf = x_ref[...]
        ms = jnp.mean(xf * xf, axis=-1, keepdims=True)
        return (xf * lax.rsqrt(ms + NORM_EPS) * nw_ref[...]).astype(BF16)

    @pl.when(step == 0)
    def _():
        hn_ref[0] = normed(x0_ref)

    hb = hn_ref[step % 2]

    def seg_t(j):
        return lax.dot_general(wt_ref[j * SEG:(j + 1) * SEG, :], hb, NT_DIMS, preferred_element_type=F32)

    t16_ref[0:SEG, :] = (seg_t(0) * FOX_Q_SCALE).astype(BF16)
    hn_ref[(step + 1) % 2] = normed(xn_ref)
    kr_ref[...] = seg_t(1).T.astype(BF16)
    qkr_ref[...] = seg_t(2).T
    t16_ref[SEG:2 * SEG, :] = seg_t(3).astype(BF16)
    t16_ref[2 * SEG:3 * SEG, :] = seg_t(4).astype(BF16)
    for r in range(3):
        t32_ref[r * SEG:(r + 1) * SEG, :] = seg_t(5 + r)
    g_ref[...] = lax.dot_general(hb, wg_ref[...], NT_DIMS, preferred_element_type=F32)


def _inproj(x2, nw, w_t, w_gate, seq):
    m = x2.shape[0]
    tm = PROJ_TM
    per_seq = seq // tm
    steps = m // tm
    t_map = lambda i: (i // per_seq, 0, i % per_seq)
    return pl.pallas_call(
        _inproj_kernel,
        grid=(steps,),
        in_specs=[
            pl.BlockSpec((tm, D_MODEL), lambda i: (0, 0)),
            pl.BlockSpec((tm, D_MODEL), lambda i: (jnp.minimum(i + 1, steps - 1), 0)),
            pl.BlockSpec((1, D_MODEL), lambda i: (0, 0)),
            pl.BlockSpec(w_t.shape, lambda i: (0, 0)),
            pl.BlockSpec((2 * LANES, D_MODEL), lambda i: (0, 0)),
        ],
        out_specs=[
            pl.BlockSpec((tm, SEG), lambda i: (i, 0)),
            pl.BlockSpec((tm, SEG), lambda i: (i, 0)),
            pl.BlockSpec((None, 3 * SEG, tm), t_map),
            pl.BlockSpec((None, 3 * SEG, tm), t_map),
            pl.BlockSpec((tm, 2 * LANES), lambda i: (i, 0)),
        ],
        out_shape=[
            jax.ShapeDtypeStruct((m, SEG), BF16),
            jax.ShapeDtypeStruct((m, SEG), F32),
            jax.ShapeDtypeStruct((m // seq, 3 * SEG, seq), BF16),
            jax.ShapeDtypeStruct((m // seq, 3 * SEG, seq), F32),
            jax.ShapeDtypeStruct((m, 2 * LANES), F32),
        ],
        scratch_shapes=[pltpu.VMEM((2, tm, D_MODEL), BF16)],
        compiler_params=_params(("arbitrary",), BIG_TILE_VMEM_LIMIT),
        name="inproj",
    )(x2, x2, nw, w_t, w_gate)


def _split3(v):
    hi = v.astype(BF16)
    r1 = v - hi.astype(F32)
    mid = r1.astype(BF16)
    lo = (r1 - mid.astype(F32)).astype(BF16)
    return hi, mid, lo


def _gates_kernel(g_ref, b_ref, mcol_ref, mrow_ref, carry_ref, mprev_ref):
    rows, chunk = GATE_R, MLSTM_L

    @pl.when(pl.program_id(1) == 0)
    def _():
        carry_ref[...] = jnp.zeros_like(carry_ref)
        mprev_ref[...] = jnp.zeros_like(mprev_ref)

    v = g_ref[...] + b_ref[...]
    v1 = v[:, :LANES]
    v2 = v[:, LANES:]
    ls = jnp.minimum(v1, 0.0) - jnp.log1p(jnp.exp(-jnp.abs(v1)))

    tri = (lax.broadcasted_iota(jnp.int32, (chunk, chunk), 1)
           <= lax.broadcasted_iota(jnp.int32, (chunk, chunk), 0)).astype(BF16)
    hi, mid, lo = _split3(ls)
    cuml = jnp.concatenate([
        jnp.dot(tri, hi[c * chunk:(c + 1) * chunk], preferred_element_type=F32)
        + jnp.dot(tri, mid[c * chunk:(c + 1) * chunk], preferred_element_type=F32)
        + jnp.dot(tri, lo[c * chunk:(c + 1) * chunk], preferred_element_type=F32)
        for c in range(rows // chunk)], axis=0)
    carry = carry_ref[...]
    cumf = []
    for c in range(rows // chunk):
        cumf.append(cuml[c * chunk:(c + 1) * chunk] + carry)
        carry = cumf[c][chunk - 1:chunk, :]
    carry_ref[...] = carry
    cumf = jnp.concatenate(cumf, axis=0)

    a = v2 - cuml
    t_in = lax.broadcasted_iota(jnp.int32, (rows, LANES), 0) % chunk
    cm = a
    s = 1
    while s < chunk:
        cm = jnp.where(t_in >= s, jnp.maximum(cm, pltpu.roll(cm, s, axis=0)), cm)
        s *= 2

    lane = lax.broadcasted_iota(jnp.int32, (chunk, LANES), 1)
    mprev = mprev_ref[...]
    for c in range(rows // chunk):
        sl = slice(c * chunk, (c + 1) * chunk)
        b = cuml[sl]
        mt = b + jnp.maximum(mprev, cm[sl])
        gt = b - mt
        ut = (b + mprev) - mt
        tile = jnp.where(lane < LANE_G, cumf[sl] * LOG2E,
               jnp.where(lane < LANE_U, gt,
               jnp.where(lane < LANE_A, pltpu.roll(ut, LANE_U - LANE_G, axis=1),
               jnp.where(lane < LANE_NEGM, pltpu.roll(a[sl], LANE_A - LANE_G, axis=1),
                         pltpu.roll(-mt, LANE_NEGM - LANE_G, axis=1)))))
        mprev = mt[chunk - 1:chunk, :]
        mcol_ref[sl, :] = tile
        mrow_ref[:, sl] = tile.T[LANE_G:LANE_G + GATE_ROWS, :]
    mprev_ref[...] = mprev


def _gates(g3, bias_vec):
    b, s, _ = g3.shape
    r = GATE_R
    return pl.pallas_call(
        _gates_kernel,
        grid=(b, s // r),
        in_specs=[
            pl.BlockSpec((None, r, 2 * LANES), lambda bi, ri: (bi, ri, 0)),
            pl.BlockSpec((1, 2 * LANES), lambda bi, ri: (0, 0)),
        ],
        out_specs=[
            pl.BlockSpec((None, r, LANES), lambda bi, ri: (bi, ri, 0)),
            pl.BlockSpec((None, GATE_ROWS, r), lambda bi, ri: (bi, 0, ri)),
        ],
        out_shape=[
            jax.ShapeDtypeStruct((b, s, LANES), F32),
            jax.ShapeDtypeStruct((b, GATE_ROWS, s), F32),
        ],
        scratch_shapes=[pltpu.VMEM((1, LANES), F32), pltpu.VMEM((1, LANES), F32)],
        compiler_params=_params(("arbitrary", "arbitrary")),
        name="gates",
    )(g3, bias_vec)


def _fox_kernel(it_ref, jt_ref, qt_ref, k_ref, vt_ref, gc_ref, zt_ref, nwb_ref, o_ref, m_ref, acc_ref):
    step_id = pl.program_id(1)
    i = it_ref[step_id]
    j = jt_ref[step_id]
    t = ATT_T
    hd = FOX_HEAD_DIM

    @pl.when(j == 0)
    def _():
        m_ref[...] = jnp.full_like(m_ref, NEG_INF)
        acc_ref[...] = jnp.zeros_like(acc_ref)

    def step(modes):
        gc = gc_ref[...]
        ones = jnp.ones((ONES_ROWS, t), BF16)
        units = [(h, c) for h in range(FOX_HEADS) for c in range(ATT_QBLOCKS) if modes[c]]

        def scores(h, c):
            hs = slice(h * hd, (h + 1) * hd)
            st = jnp.dot(k_ref[:, hs], qt_ref[hs, c * t:(c + 1) * t],
                         preferred_element_type=F32) - gc[:, h:h + 1]
            if modes[c] == 'diag':
                valid = (lax.broadcasted_iota(jnp.int32, (t, t), 0)
                         <= lax.broadcasted_iota(jnp.int32, (t, t), 1))
                st = jnp.where(valid, st, NEG_INF)
            return st

        def probs(h, c, st):
            qs = slice(c * t, (c + 1) * t)
            m_prev = m_ref[h:h + 1, qs]
            m_new = jnp.maximum(m_prev, jnp.max(st, axis=0, keepdims=True))
            m_ref[h:h + 1, qs] = m_new
            return jnp.exp2(m_prev - m_new), jnp.exp2(st - m_new).astype(BF16)

        def accumulate(h, c, alpha, pt):
            qs = slice(c * t, (c + 1) * t)
            vt_aug = jnp.concatenate([vt_ref[h * hd:(h + 1) * hd, :], ones], axis=0)
            acc_ref[h, :, qs] = alpha * acc_ref[h, :, qs] + jnp.dot(vt_aug, pt, preferred_element_type=F32)

        st, ap = {}, {}
        for u in range(-ATT_LOOKAHEAD, len(units) + 1):
            if 0 <= u + ATT_LOOKAHEAD < len(units):
                st[u + ATT_LOOKAHEAD] = scores(*units[u + ATT_LOOKAHEAD])
            if 0 <= u < len(units):
                ap[u] = probs(*units[u], st.pop(u))
            if 0 <= u - 1 < len(units):
                accumulate(*units[u - 1], *ap.pop(u - 1))

    first = i * ATT_QBLOCKS

    @pl.when(j < first)
    def _():
        step(('full',) * ATT_QBLOCKS)

    for r in range(ATT_QBLOCKS):
        @pl.when(j == first + r)
        def _(r=r):
            step(tuple(None if c < r else 'diag' if c == r else 'full' for c in range(ATT_QBLOCKS)))
            if r == ATT_QBLOCKS - 1:
                for h in range(FOX_HEADS):
                    hs = slice(h * hd, (h + 1) * hd)
                    for c in range(ATT_QBLOCKS):
                        qs = slice(c * t, (c + 1) * t)
                        acc = acc_ref[h, :, qs]
                        o = acc[:hd] * (1.0 / acc[hd:hd + 1])
                        o = o * lax.rsqrt(jnp.mean(o * o, axis=0, keepdims=True) + NORM_EPS) * nwb_ref[hs, :]
                        z = zt_ref[hs, qs]
                        o_ref[hs, qs] = (o * (z * _sigmoid(z))).astype(BF16)


def _fox(kr, t16, t32, mcol, nwb):
    b, s, _ = kr.shape
    t = ATT_T
    tq = ATT_T * ATT_QBLOCKS
    pairs = [(i, j) for i in range(s // tq) for j in range((i + 1) * ATT_QBLOCKS)]
    it = jnp.asarray([p[0] for p in pairs], jnp.int32)
    jt = jnp.asarray([p[1] for p in pairs], jnp.int32)

    grid_spec = pltpu.PrefetchScalarGridSpec(
        num_scalar_prefetch=2,
        grid=(b, len(pairs)),
        in_specs=[
            pl.BlockSpec((None, FOX_WIDTH, tq), lambda bi, p, it, jt: (bi, 0, it[p])),
            pl.BlockSpec((None, t, FOX_WIDTH), lambda bi, p, it, jt: (bi, jt[p], 0)),
            pl.BlockSpec((None, FOX_WIDTH, t), lambda bi, p, it, jt: (bi, 1, jt[p])),
            pl.BlockSpec((None, t, LANES), lambda bi, p, it, jt: (bi, jt[p], 0)),
            pl.BlockSpec((None, FOX_WIDTH, tq), lambda bi, p, it, jt: (bi, 0, it[p])),
            pl.BlockSpec((FOX_WIDTH, t), lambda bi, p, it, jt: (0, 0)),
        ],
        out_specs=pl.BlockSpec((None, FOX_WIDTH, tq), lambda bi, p, it, jt: (bi, 0, it[p])),
        scratch_shapes=[pltpu.VMEM((FOX_HEADS, tq), F32),
                        pltpu.VMEM((FOX_HEADS, FOX_HEAD_DIM + ONES_ROWS, tq), F32)],
    )
    return pl.pallas_call(
        _fox_kernel,
        grid_spec=grid_spec,
        out_shape=jax.ShapeDtypeStruct((b, FOX_WIDTH, s), BF16),
        compiler_params=_params(("arbitrary", "arbitrary")),
        name="fox_attention",
    )(it, jt, t16, kr, t16, mcol, t32, nwb)


def _mlstm_kernel(qk_ref, vt_ref, ogt_ref, zt_ref, mcol_ref, mrow_ref, cw_ref, cb_ref, nwb_ref,
                  out_ref, hist_ref, st_ref):
    n = MLSTM_L
    dv = MLSTM_V_DIM

    @pl.when(pl.program_id(1) == 0)
    def _():
        hist_ref[...] = jnp.zeros_like(hist_ref)
        st_ref[...] = jnp.zeros_like(st_ref)

    u = qk_ref[...]
    ext = jnp.concatenate([hist_ref[...], u], axis=0)
    hist_ref[...] = u[n - F32_SUBLANES:n, :]

    def tap(jj):
        back = CONV_WIDTH - 1 - jj
        return pltpu.roll(ext, back, axis=0)[F32_SUBLANES:, :] if back else u

    y = tap(0) * cw_ref[0:1, :]
    for jj in range(1, CONV_WIDTH):
        y = y + tap(jj) * cw_ref[jj:jj + 1, :]
    y = y + cb_ref[...]
    sy = y * _sigmoid(y)

    mc = mcol_ref[...]
    mr = mrow_ref[...]
    lane = lax.broadcasted_iota(jnp.int32, (n, LANES), 1)
    ones_rows = jnp.where(lax.broadcasted_iota(jnp.int32, (ONES_ROWS, n), 0) == 0, 1.0, 0.0).astype(BF16)
    valid = lax.broadcasted_iota(jnp.int32, (n, n), 0) <= lax.broadcasted_iota(jnp.int32, (n, n), 1)
    heads = range(MLSTM_HEADS)
    row = lambda base_row, h: mr[base_row + h:base_row + h + 1, :]

    qp, kh, st_t, inter_t, vt_aug = [], [], [], [], []
    for h in heads:
        pr = h // 2
        qp.append(sy[:, pr * LANES:(pr + 1) * LANES].astype(BF16))
        kp = sy[:, MLSTM_QK_WIDTH + pr * LANES:MLSTM_QK_WIDTH + (pr + 1) * LANES] * (MLSTM_QK_DIM ** -0.5)
        kh.append(jnp.where((lane < MLSTM_QK_DIM) if h % 2 == 0 else (lane >= MLSTM_QK_DIM), kp, 0.0))
        st_t.append(lax.dot_general(kh[h].astype(BF16), qp[h], NT_DIMS, preferred_element_type=F32))
        inter_t.append(lax.dot_general(st_ref[h].astype(BF16), qp[h], NT_DIMS, preferred_element_type=F32))
        vt_aug.append(jnp.concatenate([vt_ref[h * dv:(h + 1) * dv, :], ones_rows], axis=0))

    for h in heads:
        hs = slice(h * dv, (h + 1) * dv)
        a_col = mc[:, LANE_A + h:LANE_A + h + 1]
        sc_t = st_t[h] * jnp.exp(jnp.where(valid, a_col + row(ROW_G, h), NEG_INF))
        res = (jnp.dot(vt_aug[h], sc_t.astype(BF16), preferred_element_type=F32)
               + jnp.exp(row(ROW_U, h)) * inter_t[h])
        den = res[dv:dv + 1, :]
        ht = res[:dv] * (1.0 / jnp.maximum(jnp.abs(den), jnp.exp(row(ROW_NEGM, h))))
        hb = ht * _sigmoid(ogt_ref[hs, :])
        hn = hb * lax.rsqrt(jnp.mean(hb * hb, axis=0, keepdims=True) + NORM_EPS) * nwb_ref[hs, :]
        z = zt_ref[hs, :]
        out_ref[hs, :] = (hn * (z * _sigmoid(z))).astype(BF16)

    for h in heads:
        a_col = mc[:, LANE_A + h:LANE_A + h + 1]
        g_last = mc[n - 1:n, LANE_G + h:LANE_G + h + 1]
        u_last = mc[n - 1:n, LANE_U + h:LANE_U + h + 1]
        kw = (kh[h] * jnp.exp(a_col + g_last)).astype(BF16)
        upd = jnp.dot(vt_aug[h], kw, preferred_element_type=F32)
        st_ref[h] = jnp.exp(u_last) * st_ref[h] + upd


def _mlstm(r32, t16, t32, mcol, mrow, cw, cb, nwb):
    b, s, _ = r32.shape
    n = MLSTM_L

    def t_map(which):
        return lambda bi, c: (bi, which, c)

    return pl.pallas_call(
        _mlstm_kernel,
        grid=(b, s // n),
        in_specs=[
            pl.BlockSpec((None, n, SEG), lambda bi, c: (bi, c, 0)),
            pl.BlockSpec((None, SEG, n), t_map(2)),
            pl.BlockSpec((None, SEG, n), t_map(1)),
            pl.BlockSpec((None, SEG, n), t_map(2)),
            pl.BlockSpec((None, n, LANES), lambda bi, c: (bi, c, 0)),
            pl.BlockSpec((None, GATE_ROWS, n), lambda bi, c: (bi, 0, c)),
            pl.BlockSpec((CONV_WIDTH, SEG), lambda bi, c: (0, 0)),
            pl.BlockSpec((1, SEG), lambda bi, c: (0, 0)),
            pl.BlockSpec((SEG, n), lambda bi, c: (0, 0)),
        ],
        out_specs=pl.BlockSpec((None, SEG, n), lambda bi, c: (bi, 0, c)),
        out_shape=jax.ShapeDtypeStruct((b, MLSTM_V_WIDTH, s), BF16),
        scratch_shapes=[pltpu.VMEM((F32_SUBLANES, SEG), F32),
                        pltpu.VMEM((MLSTM_HEADS, MLSTM_V_DIM + ONES_ROWS, LANES), F32)],
        compiler_params=_params(("arbitrary", "arbitrary")),
        name="mlstm",
    )(r32, t16, t32, t32, mcol, mrow, cw, cb, nwb)


def _outproj_kernel(yat_ref, ybt_ref, wa_ref, wb_ref, x_ref, fw_ref, o_ref, *, final):
    sub = OUT_TM // OUT_SUBBLOCKS
    for c in range(OUT_SUBBLOCKS):
        rs = slice(c * sub, (c + 1) * sub)
        acc = (lax.dot_general(yat_ref[:, rs], wa_ref[...], TN_DIMS, preferred_element_type=F32)
               + lax.dot_general(ybt_ref[:, rs], wb_ref[...], TN_DIMS, preferred_element_type=F32))
        r = x_ref[rs, :] + acc
        if final:
            r = r * lax.rsqrt(jnp.mean(r * r, axis=-1, keepdims=True) + NORM_EPS) * fw_ref[...]
        o_ref[rs, :] = r


def _outproj(yat, ybt, wa, wb, x2, fw, final):
    m = x2.shape[0]
    _, _, seq = yat.shape
    tm = OUT_TM
    per_seq = seq // tm
    y_map = lambda i: (i // per_seq, 0, i % per_seq)
    return pl.pallas_call(
        functools.partial(_outproj_kernel, final=final),
        grid=(m // tm,),
        in_specs=[
            pl.BlockSpec((None, FOX_WIDTH, tm), y_map),
            pl.BlockSpec((None, MLSTM_V_WIDTH, tm), y_map),
            pl.BlockSpec((FOX_WIDTH, D_MODEL), lambda i: (0, 0)),
            pl.BlockSpec((MLSTM_V_WIDTH, D_MODEL), lambda i: (0, 0)),
            pl.BlockSpec((tm, D_MODEL), lambda i: (i, 0)),
            pl.BlockSpec((1, D_MODEL), lambda i: (0, 0)),
        ],
        out_specs=pl.BlockSpec((tm, D_MODEL), lambda i: (i, 0)),
        out_shape=jax.ShapeDtypeStruct((m, D_MODEL), F32),
        compiler_params=_params(("arbitrary",), BIG_TILE_VMEM_LIMIT),
        name="outproj",
    )(yat, ybt, wa, wb, x2, fw)


SEG_STARTS = (IN_OFFS[0], IN_OFFS[1], IN_OFFS[5], IN_OFFS[2], IN_OFFS[7], IN_OFFS[3], IN_OFFS[8], IN_OFFS[9])
SEG_SHIFT = FOX_HEADS
assert all(st % SEG in (0, SEG_SHIFT) for st in SEG_STARTS) and IN_OFFS[6] == IN_OFFS[5] + MLSTM_QK_WIDTH
assert SEG_SHIFT == F32_SUBLANES and MLSTM_HEADS == F32_SUBLANES

J_FOX_GATE = 2
assert SEG_STARTS[J_FOX_GATE] - SEG_SHIFT == IN_OFFS[4]
GW_FOX_F, GW_MLSTM_F, GW_MLSTM_I = 0, 8, LANES + 8


def _wprep_kernel(main_blk_ref, extra_blk_ref, main_ref, extra_ref, mi_ref, mf_ref, o_ref, gw_ref, gwf_ref):
    del main_blk_ref, extra_blk_ref
    j = pl.program_id(0)
    shifted = functools.reduce(jnp.logical_or, [j == k for k, st in enumerate(SEG_STARTS) if st % SEG])

    @pl.when(j == 0)
    def _():
        gwf_ref[...] = jnp.zeros_like(gwf_ref)
        gwf_ref[GW_MLSTM_I:GW_MLSTM_I + MLSTM_HEADS, :] = mi_ref[...]
        gwf_ref[GW_MLSTM_F:GW_MLSTM_F + MLSTM_HEADS, :] = mf_ref[...]

    @pl.when(jnp.logical_not(shifted))
    def _():
        o_ref[...] = main_ref[...].astype(BF16)

    @pl.when(shifted)
    def _():
        full = jnp.concatenate([main_ref[...], extra_ref[...]], axis=0)
        o_ref[...] = full[SEG_SHIFT:SEG_SHIFT + SEG, :].astype(BF16)

    @pl.when(j == J_FOX_GATE)
    def _():
        gwf_ref[GW_FOX_F:GW_FOX_F + FOX_HEADS, :] = main_ref[0:SEG_SHIFT, :]

    @pl.when(j == len(SEG_STARTS) - 1)
    def _():
        gw_ref[...] = gwf_ref[...].astype(BF16)


def _wprep(w_t):
    d = w_t.shape[1]
    main_blk = jnp.asarray([st // SEG for st in SEG_STARTS], jnp.int32)
    extra_blk = jnp.asarray([(st // SEG + 1) * (SEG // F32_SUBLANES) for st in SEG_STARTS], jnp.int32)
    grid_spec = pltpu.PrefetchScalarGridSpec(
        num_scalar_prefetch=2,
        grid=(len(SEG_STARTS),),
        in_specs=[
            pl.BlockSpec((SEG, d), lambda j, mb, eb: (mb[j], 0)),
            pl.BlockSpec((F32_SUBLANES, d), lambda j, mb, eb: (eb[j], 0)),
            pl.BlockSpec((MLSTM_HEADS, d), lambda j, mb, eb: (IN_OFFS[10] // F32_SUBLANES, 0)),
            pl.BlockSpec((MLSTM_HEADS, d), lambda j, mb, eb: (IN_OFFS[11] // F32_SUBLANES, 0)),
        ],
        out_specs=[
            pl.BlockSpec((SEG, d), lambda j, mb, eb: (j, 0)),
            pl.BlockSpec((2 * LANES, d), lambda j, mb, eb: (0, 0)),
        ],
        scratch_shapes=[pltpu.VMEM((2 * LANES, d), F32)],
    )
    return pl.pallas_call(
        _wprep_kernel,
        grid_spec=grid_spec,
        out_shape=[jax.ShapeDtypeStruct((len(SEG_STARTS) * SEG, d), BF16),
                   jax.ShapeDtypeStruct((2 * LANES, d), BF16)],
        compiler_params=_params(("arbitrary",)),
        name="weight_layout",
    )(main_blk, extra_blk, w_t, w_t, w_t, w_t)


def _lane_broadcast(v, width):
    return jnp.broadcast_to(v[:, None], (v.shape[0], width))


def kernel(x, norm_w, w_in, fox_f_bias, conv_w, conv_b, mlstm_i_bias, mlstm_f_bias,
           fox_out_norm_w, mlstm_out_norm_w, w_out, final_norm_w):
    b, s, d = x.shape
    depth = norm_w.shape[0]
    x2 = x.reshape(b * s, d)
    for l in range(depth):
        w_t, w_gate = _wprep(w_in[l].T)
        bias_vec = jnp.concatenate([
            fox_f_bias[l], mlstm_f_bias[l], jnp.zeros((LANES - 16,), F32),
            jnp.zeros((8,), F32), mlstm_i_bias[l], jnp.zeros((LANES - 16,), F32)]).reshape(1, 2 * LANES)

        kr, qkr, t16, t32, g = _inproj(x2, norm_w[l].reshape(1, d), w_t, w_gate, s)
        mcol, mrow = _gates(g.reshape(b, s, 2 * LANES), bias_vec)

        yat = _fox(kr.reshape(b, s, SEG), t16, t32, mcol, _lane_broadcast(fox_out_norm_w[l], ATT_T))
        ybt = _mlstm(qkr.reshape(b, s, SEG), t16, t32, mcol, mrow, conv_w[l], conv_b[l].reshape(1, SEG),
                     _lane_broadcast(mlstm_out_norm_w[l], MLSTM_L))

        wo = w_out[l].astype(BF16)
        x2 = _outproj(yat, ybt, wo[:FOX_WIDTH], wo[FOX_WIDTH:], x2, final_norm_w.reshape(1, d),
                      final=(l == depth - 1))
    return x2.reshape(b, s, d)
```

```python
import functools

import numpy as np
import jax
import jax.numpy as jnp
from jax import lax
from jax.experimental import pallas as pl
from jax.experimental.pallas import tpu as pltpu

D_MODEL = 2048
NORM_EPS = 1e-6
NEG_INF = -1e30

FOX_HEADS = 8
FOX_HEAD_DIM = 128
FOX_WIDTH = FOX_HEADS * FOX_HEAD_DIM
MLSTM_HEADS = 8
MLSTM_V_DIM = 128
MLSTM_QK_DIM = 64
MLSTM_V_WIDTH = MLSTM_HEADS * MLSTM_V_DIM
MLSTM_QK_WIDTH = MLSTM_HEADS * MLSTM_QK_DIM
CONV_WIDTH = 4
IN_SIZES = (FOX_WIDTH, FOX_WIDTH, FOX_WIDTH, FOX_WIDTH, FOX_HEADS,
            MLSTM_QK_WIDTH, MLSTM_QK_WIDTH, MLSTM_V_WIDTH, MLSTM_V_WIDTH, MLSTM_V_WIDTH,
            MLSTM_HEADS, MLSTM_HEADS)
IN_OFFS = tuple(int(v) for v in np.cumsum((0,) + IN_SIZES))

LANES = 128
F32_SUBLANES = 8
SEG = 1024
VMEM_LIMIT = 56 * 1024 * 1024
BIG_TILE_VMEM_LIMIT = 62 * 1024 * 1024

PROJ_TM = 256
GATE_R = 1024
MLSTM_L = 256
ATT_T = 512
FUSE_TM = 512

LOG2E = 1.4426950408889634
FOX_Q_SCALE = FOX_HEAD_DIM ** -0.5 * LOG2E
ONES_ROWS = 16
ATT_QBLOCKS = 2
ATT_LOOKAHEAD = 1

LANE_G, LANE_U, LANE_A, LANE_NEGM = 8, 16, 24, 32
ROW_G, ROW_U, ROW_NEGM = 0, 8, 24
GATE_ROWS = 32

BF16 = jnp.bfloat16
F32 = jnp.float32
NT_DIMS = (((1,), (1,)), ((), ()))
TN_DIMS = (((0,), (0,)), ((), ()))


def _sigmoid(x):
    return 1.0 / (1.0 + jnp.exp(-x))


def _params(sem, vmem_limit=VMEM_LIMIT):
    return pltpu.CompilerParams(dimension_semantics=sem, vmem_limit_bytes=vmem_limit)


def _inproj_kernel(x_ref, nw_ref, wt_ref, wg_ref, kr_ref, qkr_ref, t16_ref, t32_ref, g_ref):
    xf = x_ref[...]
    ms = jnp.mean(xf * xf, axis=-1, keepdims=True)
    hb = (xf * lax.rsqrt(ms + NORM_EPS) * nw_ref[...]).astype(BF16)

    def seg_t(j):
        return lax.dot_general(wt_ref[j * SEG:(j + 1) * SEG, :], hb, NT_DIMS, preferred_element_type=F32)

    t16_ref[0:SEG, :] = (seg_t(0) * FOX_Q_SCALE).astype(BF16)
    kr_ref[...] = seg_t(1).T.astype(BF16)
    qkr_ref[...] = seg_t(2).T
    t16_ref[SEG:2 * SEG, :] = seg_t(3).astype(BF16)
    t16_ref[2 * SEG:3 * SEG, :] = seg_t(4).astype(BF16)
    for r in range(3):
        t32_ref[r * SEG:(r + 1) * SEG, :] = seg_t(5 + r)
    g_ref[...] = lax.dot_general(hb, wg_ref[...], NT_DIMS, preferred_element_type=F32)


def _inproj(x2, nw, w_t, w_gate, seq):
    m = x2.shape[0]
    tm = PROJ_TM
    per_seq = seq // tm
    t_map = lambda i: (i // per_seq, 0, i % per_seq)
    return pl.pallas_call(
        _inproj_kernel,
        grid=(m // tm,),
        in_specs=[
            pl.BlockSpec((tm, D_MODEL), lambda i: (i, 0)),
            pl.BlockSpec((1, D_MODEL), lambda i: (0, 0)),
            pl.BlockSpec(w_t.shape, lambda i: (0, 0)),
            pl.BlockSpec((2 * LANES, D_MODEL), lambda i: (0, 0)),
        ],
        out_specs=[
            pl.BlockSpec((tm, SEG), lambda i: (i, 0)),
            pl.BlockSpec((tm, SEG), lambda i: (i, 0)),
            pl.BlockSpec((None, 3 * SEG, tm), t_map),
            pl.BlockSpec((None, 3 * SEG, tm), t_map),
            pl.BlockSpec((tm, 2 * LANES), lambda i: (i, 0)),
        ],
        out_shape=[
            jax.ShapeDtypeStruct((m, SEG), BF16),
            jax.ShapeDtypeStruct((m, SEG), F32),
            jax.ShapeDtypeStruct((m // seq, 3 * SEG, seq), BF16),
            jax.ShapeDtypeStruct((m // seq, 3 * SEG, seq), F32),
            jax.ShapeDtypeStruct((m, 2 * LANES), F32),
        ],
        compiler_params=_params(("arbitrary",), BIG_TILE_VMEM_LIMIT),
        name="inproj",
    )(x2, nw, w_t, w_gate)


def _split3(v):
    hi = v.astype(BF16)
    r1 = v - hi.astype(F32)
    mid = r1.astype(BF16)
    lo = (r1 - mid.astype(F32)).astype(BF16)
    return hi, mid, lo


def _gates_kernel(g_ref, b_ref, mcol_ref, mrow_ref, carry_ref, mprev_ref):
    rows, chunk = GATE_R, MLSTM_L

    @pl.when(pl.program_id(1) == 0)
    def _():
        carry_ref[...] = jnp.zeros_like(carry_ref)
        mprev_ref[...] = jnp.zeros_like(mprev_ref)

    v = g_ref[...] + b_ref[...]
    v1 = v[:, :LANES]
    v2 = v[:, LANES:]
    ls = jnp.minimum(v1, 0.0) - jnp.log1p(jnp.exp(-jnp.abs(v1)))

    tri = (lax.broadcasted_iota(jnp.int32, (chunk, chunk), 1)
           <= lax.broadcasted_iota(jnp.int32, (chunk, chunk), 0)).astype(BF16)
    hi, mid, lo = _split3(ls)
    cuml = jnp.concatenate([
        jnp.dot(tri, hi[c * chunk:(c + 1) * chunk], preferred_element_type=F32)
        + jnp.dot(tri, mid[c * chunk:(c + 1) * chunk], preferred_element_type=F32)
        + jnp.dot(tri, lo[c * chunk:(c + 1) * chunk], preferred_element_type=F32)
        for c in range(rows // chunk)], axis=0)
    carry = carry_ref[...]
    cumf = []
    for c in range(rows // chunk):
        cumf.append(cuml[c * chunk:(c + 1) * chunk] + carry)
        carry = cumf[c][chunk - 1:chunk, :]
    carry_ref[...] = carry
    cumf = jnp.concatenate(cumf, axis=0)

    a = v2 - cuml
    t_in = lax.broadcasted_iota(jnp.int32, (rows, LANES), 0) % chunk
    cm = a
    s = 1
    while s < chunk:
        cm = jnp.where(t_in >= s, jnp.maximum(cm, pltpu.roll(cm, s, axis=0)), cm)
        s *= 2

    lane = lax.broadcasted_iota(jnp.int32, (chunk, LANES), 1)
    mprev = mprev_ref[...]
    for c in range(rows // chunk):
        sl = slice(c * chunk, (c + 1) * chunk)
        b = cuml[sl]
        mt = b + jnp.maximum(mprev, cm[sl])
        gt = b - mt
        ut = (b + mprev) - mt
        tile = jnp.where(lane < LANE_G, cumf[sl] * LOG2E,
               jnp.where(lane < LANE_U, gt,
               jnp.where(lane < LANE_A, pltpu.roll(ut, LANE_U - LANE_G, axis=1),
               jnp.where(lane < LANE_NEGM, pltpu.roll(a[sl], LANE_A - LANE_G, axis=1),
                         pltpu.roll(-mt, LANE_NEGM - LANE_G, axis=1)))))
        mprev = mt[chunk - 1:chunk, :]
        mcol_ref[sl, :] = tile
        mrow_ref[:, sl] = tile.T[LANE_G:LANE_G + GATE_ROWS, :]
    mprev_ref[...] = mprev


def _gates(g3, bias_vec):
    b, s, _ = g3.shape
    r = GATE_R
    return pl.pallas_call(
        _gates_kernel,
        grid=(b, s // r),
        in_specs=[
            pl.BlockSpec((None, r, 2 * LANES), lambda bi, ri: (bi, ri, 0)),
            pl.BlockSpec((1, 2 * LANES), lambda bi, ri: (0, 0)),
        ],
        out_specs=[
            pl.BlockSpec((None, r, LANES), lambda bi, ri: (bi, ri, 0)),
            pl.BlockSpec((None, GATE_ROWS, r), lambda bi, ri: (bi, 0, ri)),
        ],
        out_shape=[
            jax.ShapeDtypeStruct((b, s, LANES), F32),
            jax.ShapeDtypeStruct((b, GATE_ROWS, s), F32),
        ],
        scratch_shapes=[pltpu.VMEM((1, LANES), F32), pltpu.VMEM((1, LANES), F32)],
        compiler_params=_params(("arbitrary", "arbitrary")),
        name="gates",
    )(g3, bias_vec)


def _fox_kernel(it_ref, jt_ref, qt_ref, k_ref, vt_ref, gc_ref, zt_ref, nwb_ref, o_ref, m_ref, acc_ref):
    step_id = pl.program_id(1)
    i = it_ref[step_id]
    j = jt_ref[step_id]
    t = ATT_T
    hd = FOX_HEAD_DIM

    @pl.when(j == 0)
    def _():
        m_ref[...] = jnp.full_like(m_ref, NEG_INF)
        acc_ref[...] = jnp.zeros_like(acc_ref)

    def step(modes):
        gc = gc_ref[...]
        ones = jnp.ones((ONES_ROWS, t), BF16)
        units = [(h, c) for h in range(FOX_HEADS) for c in range(ATT_QBLOCKS) if modes[c]]

        def scores(h, c):
            hs = slice(h * hd, (h + 1) * hd)
            st = jnp.dot(k_ref[:, hs], qt_ref[hs, c * t:(c + 1) * t],
                         preferred_element_type=F32) - gc[:, h:h + 1]
            if modes[c] == 'diag':
                valid = (lax.broadcasted_iota(jnp.int32, (t, t), 0)
                         <= lax.broadcasted_iota(jnp.int32, (t, t), 1))
                st = jnp.where(valid, st, NEG_INF)
            return st

        def probs(h, c, st):
            qs = slice(c * t, (c + 1) * t)
            m_prev = m_ref[h:h + 1, qs]
            m_new = jnp.maximum(m_prev, jnp.max(st, axis=0, keepdims=True))
            m_ref[h:h + 1, qs] = m_new
            return jnp.exp2(m_prev - m_new), jnp.exp2(st - m_new).astype(BF16)

        def accumulate(h, c, alpha, pt):
            qs = slice(c * t, (c + 1) * t)
            vt_aug = jnp.concatenate([vt_ref[h * hd:(h + 1) * hd, :], ones], axis=0)
            acc_ref[h, :, qs] = alpha * acc_ref[h, :, qs] + jnp.dot(vt_aug, pt, preferred_element_type=F32)

        st, ap = {}, {}
        for u in range(-ATT_LOOKAHEAD, len(units) + 1):
            if 0 <= u + ATT_LOOKAHEAD < len(units):
                st[u + ATT_LOOKAHEAD] = scores(*units[u + ATT_LOOKAHEAD])
            if 0 <= u < len(units):
                ap[u] = probs(*units[u], st.pop(u))
            if 0 <= u - 1 < len(units):
                accumulate(*units[u - 1], *ap.pop(u - 1))

    first = i * ATT_QBLOCKS

    @pl.when(j < first)
    def _():
        step(('full',) * ATT_QBLOCKS)

    for r in range(ATT_QBLOCKS):
        @pl.when(j == first + r)
        def _(r=r):
            step(tuple(None if c < r else 'diag' if c == r else 'full' for c in range(ATT_QBLOCKS)))
            if r == ATT_QBLOCKS - 1:
                for h in range(FOX_HEADS):
                    hs = slice(h * hd, (h + 1) * hd)
                    for c in range(ATT_QBLOCKS):
                        qs = slice(c * t, (c + 1) * t)
                        acc = acc_ref[h, :, qs]
                        o = acc[:hd] * (1.0 / acc[hd:hd + 1])
                        o = o * lax.rsqrt(jnp.mean(o * o, axis=0, keepdims=True) + NORM_EPS) * nwb_ref[hs, :]
                        z = zt_ref[hs, qs]
                        o_ref[hs, qs] = (o * (z * _sigmoid(z))).astype(BF16)


def _fox(kr, t16, t32, mcol, nwb):
    b, s, _ = kr.shape
    t = ATT_T
    tq = ATT_T * ATT_QBLOCKS
    pairs = [(i, j) for i in range(s // tq) for j in range((i + 1) * ATT_QBLOCKS)]
    it = jnp.asarray([p[0] for p in pairs], jnp.int32)
    jt = jnp.asarray([p[1] for p in pairs], jnp.int32)

    grid_spec = pltpu.PrefetchScalarGridSpec(
        num_scalar_prefetch=2,
        grid=(b, len(pairs)),
        in_specs=[
            pl.BlockSpec((None, FOX_WIDTH, tq), lambda bi, p, it, jt: (bi, 0, it[p])),
            pl.BlockSpec((None, t, FOX_WIDTH), lambda bi, p, it, jt: (bi, jt[p], 0)),
            pl.BlockSpec((None, FOX_WIDTH, t), lambda bi, p, it, jt: (bi, 1, jt[p])),
            pl.BlockSpec((None, t, LANES), lambda bi, p, it, jt: (bi, jt[p], 0)),
            pl.BlockSpec((None, FOX_WIDTH, tq), lambda bi, p, it, jt: (bi, 0, it[p])),
            pl.BlockSpec((FOX_WIDTH, t), lambda bi, p, it, jt: (0, 0)),
        ],
        out_specs=pl.BlockSpec((None, FOX_WIDTH, tq), lambda bi, p, it, jt: (bi, 0, it[p])),
        scratch_shapes=[pltpu.VMEM((FOX_HEADS, tq), F32),
                        pltpu.VMEM((FOX_HEADS, FOX_HEAD_DIM + ONES_ROWS, tq), F32)],
    )
    return pl.pallas_call(
        _fox_kernel,
        grid_spec=grid_spec,
        out_shape=jax.ShapeDtypeStruct((b, FOX_WIDTH, s), BF16),
        compiler_params=_params(("arbitrary", "arbitrary")),
        name="fox_attention",
    )(it, jt, t16, kr, t16, mcol, t32, nwb)


def _mlstm_conv(c, qk_ref, cw_ref, cb_ref, hist_ref):
    n = MLSTM_L
    u = qk_ref[c * n:(c + 1) * n, :]
    ext = jnp.concatenate([hist_ref[...], u], axis=0)
    hist_ref[...] = u[n - F32_SUBLANES:n, :]

    def tap(jj):
        back = CONV_WIDTH - 1 - jj
        return pltpu.roll(ext, back, axis=0)[F32_SUBLANES:, :] if back else u

    y = tap(0) * cw_ref[0:1, :]
    for jj in range(1, CONV_WIDTH):
        y = y + tap(jj) * cw_ref[jj:jj + 1, :]
    y = y + cb_ref[...]
    return y * _sigmoid(y)


def _mlstm_chunk(c, sy, vt_ref, ogt_ref, zt_ref, mcol_ref, mrow_ref, nwb_ref, write_out, st_ref):
    n = MLSTM_L
    dv = MLSTM_V_DIM
    ts = slice(c * n, (c + 1) * n)

    mc = mcol_ref[ts, :]
    mr = mrow_ref[:, ts]
    lane = lax.broadcasted_iota(jnp.int32, (n, LANES), 1)
    ones_rows = jnp.where(lax.broadcasted_iota(jnp.int32, (ONES_ROWS, n), 0) == 0, 1.0, 0.0).astype(BF16)
    valid = lax.broadcasted_iota(jnp.int32, (n, n), 0) <= lax.broadcasted_iota(jnp.int32, (n, n), 1)
    heads = range(MLSTM_HEADS)
    row = lambda base_row, h: mr[base_row + h:base_row + h + 1, :]

    qp, kh, st_t, inter_t, vt_aug = [], [], [], [], []
    for h in heads:
        pr = h // 2
        qp.append(sy[:, pr * LANES:(pr + 1) * LANES].astype(BF16))
        kp = sy[:, MLSTM_QK_WIDTH + pr * LANES:MLSTM_QK_WIDTH + (pr + 1) * LANES] * (MLSTM_QK_DIM ** -0.5)
        kh.append(jnp.where((lane < MLSTM_QK_DIM) if h % 2 == 0 else (lane >= MLSTM_QK_DIM), kp, 0.0))
        st_t.append(lax.dot_general(kh[h].astype(BF16), qp[h], NT_DIMS, preferred_element_type=F32))
        inter_t.append(lax.dot_general(st_ref[h].astype(BF16), qp[h], NT_DIMS, preferred_element_type=F32))
        vt_aug.append(jnp.concatenate([vt_ref[h * dv:(h + 1) * dv, ts], ones_rows], axis=0))

    for h in heads:
        hs = slice(h * dv, (h + 1) * dv)
        a_col = mc[:, LANE_A + h:LANE_A + h + 1]
        sc_t = st_t[h] * jnp.exp(jnp.where(valid, a_col + row(ROW_G, h), NEG_INF))
        res = (jnp.dot(vt_aug[h], sc_t.astype(BF16), preferred_element_type=F32)
               + jnp.exp(row(ROW_U, h)) * inter_t[h])
        den = res[dv:dv + 1, :]
        ht = res[:dv] * (1.0 / jnp.maximum(jnp.abs(den), jnp.exp(row(ROW_NEGM, h))))
        hb = ht * _sigmoid(ogt_ref[hs, ts])
        hn = hb * lax.rsqrt(jnp.mean(hb * hb, axis=0, keepdims=True) + NORM_EPS) * nwb_ref[hs, :]
        z = zt_ref[hs, ts]
        write_out(hs, ts, (hn * (z * _sigmoid(z))).astype(BF16))

    for h in heads:
        a_col = mc[:, LANE_A + h:LANE_A + h + 1]
        g_last = mc[n - 1:n, LANE_G + h:LANE_G + h + 1]
        u_last = mc[n - 1:n, LANE_U + h:LANE_U + h + 1]
        kw = (kh[h] * jnp.exp(a_col + g_last)).astype(BF16)
        upd = jnp.dot(vt_aug[h], kw, preferred_element_type=F32)
        st_ref[h] = jnp.exp(u_last) * st_ref[h] + upd


def _mlstm_out_kernel(qk_ref, vt_ref, ogt_ref, zt_ref, mcol_ref, mrow_ref, cw_ref, cb_ref, nwb_ref,
                      yat_ref, wa_ref, wb_ref, x_ref, fw_ref, o_ref, hist_ref, st_ref, ybt_ref,
                      *, per_seq, final):
    step = pl.program_id(0)
    cur = step % 2
    n = MLSTM_L

    @pl.when(step % per_seq == 0)
    def _():
        hist_ref[...] = jnp.zeros_like(hist_ref)
        st_ref[...] = jnp.zeros_like(st_ref)

    @pl.when(step == 0)
    def _():
        ybt_ref[1] = jnp.zeros_like(ybt_ref[1])

    def write_yb(hs, ts, val):
        ybt_ref[cur, hs, ts] = val

    prev = ybt_ref.at[1 - cur]
    for c in range(FUSE_TM // n):
        rs = slice(c * n, (c + 1) * n)
        sy = _mlstm_conv(c, qk_ref, cw_ref, cb_ref, hist_ref)
        acc = (lax.dot_general(yat_ref[:, rs], wa_ref[...], TN_DIMS, preferred_element_type=F32)
               + lax.dot_general(prev[:, rs], wb_ref[...], TN_DIMS, preferred_element_type=F32))
        _mlstm_chunk(c, sy, vt_ref, ogt_ref, zt_ref, mcol_ref, mrow_ref, nwb_ref, write_yb, st_ref)
        r = x_ref[rs, :] + acc
        if final:
            r = r * lax.rsqrt(jnp.mean(r * r, axis=-1, keepdims=True) + NORM_EPS) * fw_ref[...]
        o_ref[rs, :] = r


def _mlstm_out(qkr, t16, t32, mcol, mrow, cw, cb, nwb, yat, wa, wb, x2, fw, final):
    m = x2.shape[0]
    _, _, seq = yat.shape
    tm = FUSE_TM
    per_seq = seq // tm
    blocks = m // tm
    cur = lambda s: jnp.minimum(s, blocks - 1)
    prv = lambda s: jnp.maximum(s - 1, 0)
    t_cur = lambda which: (lambda s: (cur(s) // per_seq, which, cur(s) % per_seq))
    return pl.pallas_call(
        functools.partial(_mlstm_out_kernel, per_seq=per_seq, final=final),
        grid=(blocks + 1,),
        in_specs=[
            pl.BlockSpec((tm, SEG), lambda s: (cur(s), 0)),
            pl.BlockSpec((None, SEG, tm), t_cur(2)),
            pl.BlockSpec((None, SEG, tm), t_cur(1)),
            pl.BlockSpec((None, SEG, tm), t_cur(2)),
            pl.BlockSpec((tm, LANES), lambda s: (cur(s), 0)),
            pl.BlockSpec((None, GATE_ROWS, tm), lambda s: (cur(s) // per_seq, 0, cur(s) % per_seq)),
            pl.BlockSpec((CONV_WIDTH, SEG), lambda s: (0, 0)),
            pl.BlockSpec((1, SEG), lambda s: (0, 0)),
            pl.BlockSpec((SEG, MLSTM_L), lambda s: (0, 0)),
            pl.BlockSpec((None, FOX_WIDTH, tm), lambda s: (prv(s) // per_seq, 0, prv(s) % per_seq)),
            pl.BlockSpec((FOX_WIDTH, D_MODEL), lambda s: (0, 0)),
            pl.BlockSpec((MLSTM_V_WIDTH, D_MODEL), lambda s: (0, 0)),
            pl.BlockSpec((tm, D_MODEL), lambda s: (prv(s), 0)),
            pl.BlockSpec((1, D_MODEL), lambda s: (0, 0)),
        ],
        out_specs=pl.BlockSpec((tm, D_MODEL), lambda s: (prv(s), 0)),
        out_shape=jax.ShapeDtypeStruct((m, D_MODEL), F32),
        scratch_shapes=[pltpu.VMEM((F32_SUBLANES, SEG), F32),
                        pltpu.VMEM((MLSTM_HEADS, MLSTM_V_DIM + ONES_ROWS, LANES), F32),
                        pltpu.VMEM((2, MLSTM_V_WIDTH, tm), BF16)],
        compiler_params=_params(("arbitrary",)),
        name="mlstm_outproj",
    )(qkr, t16, t32, t32, mcol, mrow, cw, cb, nwb, yat, wa, wb, x2, fw)


SEG_STARTS = (IN_OFFS[0], IN_OFFS[1], IN_OFFS[5], IN_OFFS[2], IN_OFFS[7], IN_OFFS[3], IN_OFFS[8], IN_OFFS[9])
SEG_SHIFT = FOX_HEADS
assert all(st % SEG in (0, SEG_SHIFT) for st in SEG_STARTS) and IN_OFFS[6] == IN_OFFS[5] + MLSTM_QK_WIDTH
assert SEG_SHIFT == F32_SUBLANES and MLSTM_HEADS == F32_SUBLANES

J_FOX_GATE = 2
assert SEG_STARTS[J_FOX_GATE] - SEG_SHIFT == IN_OFFS[4]
GW_FOX_F, GW_MLSTM_F, GW_MLSTM_I = 0, 8, LANES + 8


def _wprep_kernel(main_blk_ref, extra_blk_ref, main_ref, extra_ref, mi_ref, mf_ref, o_ref, gw_ref, gwf_ref):
    del main_blk_ref, extra_blk_ref
    j = pl.program_id(0)
    shifted = functools.reduce(jnp.logical_or, [j == k for k, st in enumerate(SEG_STARTS) if st % SEG])

    @pl.when(j == 0)
    def _():
        gwf_ref[...] = jnp.zeros_like(gwf_ref)
        gwf_ref[GW_MLSTM_I:GW_MLSTM_I + MLSTM_HEADS, :] = mi_ref[...]
        gwf_ref[GW_MLSTM_F:GW_MLSTM_F + MLSTM_HEADS, :] = mf_ref[...]

    @pl.when(jnp.logical_not(shifted))
    def _():
        o_ref[...] = main_ref[...].astype(BF16)

    @pl.when(shifted)
    def _():
        full = jnp.concatenate([main_ref[...], extra_ref[...]], axis=0)
        o_ref[...] = full[SEG_SHIFT:SEG_SHIFT + SEG, :].astype(BF16)

    @pl.when(j == J_FOX_GATE)
    def _():
        gwf_ref[GW_FOX_F:GW_FOX_F + FOX_HEADS, :] = main_ref[0:SEG_SHIFT, :]

    @pl.when(j == len(SEG_STARTS) - 1)
    def _():
        gw_ref[...] = gwf_ref[...].astype(BF16)


def _wprep(w_t):
    d = w_t.shape[1]
    main_blk = jnp.asarray([st // SEG for st in SEG_STARTS], jnp.int32)
    extra_blk = jnp.asarray([(st // SEG + 1) * (SEG // F32_SUBLANES) for st in SEG_STARTS], jnp.int32)
    grid_spec = pltpu.PrefetchScalarGridSpec(
        num_scalar_prefetch=2,
        grid=(len(SEG_STARTS),),
        in_specs=[
            pl.BlockSpec((SEG, d), lambda j, mb, eb: (mb[j], 0)),
            pl.BlockSpec((F32_SUBLANES, d), lambda j, mb, eb: (eb[j], 0)),
            pl.BlockSpec((MLSTM_HEADS, d), lambda j, mb, eb: (IN_OFFS[10] // F32_SUBLANES, 0)),
            pl.BlockSpec((MLSTM_HEADS, d), lambda j, mb, eb: (IN_OFFS[11] // F32_SUBLANES, 0)),
        ],
        out_specs=[
            pl.BlockSpec((SEG, d), lambda j, mb, eb: (j, 0)),
            pl.BlockSpec((2 * LANES, d), lambda j, mb, eb: (0, 0)),
        ],
        scratch_shapes=[pltpu.VMEM((2 * LANES, d), F32)],
    )
    return pl.pallas_call(
        _wprep_kernel,
        grid_spec=grid_spec,
        out_shape=[jax.ShapeDtypeStruct((len(SEG_STARTS) * SEG, d), BF16),
                   jax.ShapeDtypeStruct((2 * LANES, d), BF16)],
        compiler_params=_params(("arbitrary",)),
        name="weight_layout",
    )(main_blk, extra_blk, w_t, w_t, w_t, w_t)


def _lane_broadcast(v, width):
    return jnp.broadcast_to(v[:, None], (v.shape[0], width))


def kernel(x, norm_w, w_in, fox_f_bias, conv_w, conv_b, mlstm_i_bias, mlstm_f_bias,
           fox_out_norm_w, mlstm_out_norm_w, w_out, final_norm_w):
    b, s, d = x.shape
    depth = norm_w.shape[0]
    x2 = x.reshape(b * s, d)
    for l in range(depth):
        w_t, w_gate = _wprep(w_in[l].T)
        bias_vec = jnp.concatenate([
            fox_f_bias[l], mlstm_f_bias[l], jnp.zeros((LANES - 16,), F32),
            jnp.zeros((8,), F32), mlstm_i_bias[l], jnp.zeros((LANES - 16,), F32)]).reshape(1, 2 * LANES)

        kr, qkr, t16, t32, g = _inproj(x2, norm_w[l].reshape(1, d), w_t, w_gate, s)
        mcol, mrow = _gates(g.reshape(b, s, 2 * LANES), bias_vec)

        yat = _fox(kr.reshape(b, s, SEG), t16, t32, mcol, _lane_broadcast(fox_out_norm_w[l], ATT_T))
        wo = w_out[l].astype(BF16)
        x2 = _mlstm_out(qkr, t16, t32, mcol.reshape(b * s, LANES), mrow, conv_w[l], conv_b[l].reshape(1, SEG),
                        _lane_broadcast(mlstm_out_norm_w[l], MLSTM_L),
                        yat, wo[:FOX_WIDTH], wo[FOX_WIDTH:], x2, final_norm_w.reshape(1, d),
                        final=(l == depth - 1))
    return x2.reshape(b, s, d)
```

```python
import functools

import numpy as np
import jax
import jax.numpy as jnp
from jax import lax
from jax.experimental import pallas as pl
from jax.experimental.pallas import tpu as pltpu

D_MODEL = 2048
NORM_EPS = 1e-6
NEG_INF = -1e30

FOX_HEADS = 8
FOX_HEAD_DIM = 128
FOX_WIDTH = FOX_HEADS * FOX_HEAD_DIM
MLSTM_HEADS = 8
MLSTM_V_DIM = 128
MLSTM_QK_DIM = 64
MLSTM_V_WIDTH = MLSTM_HEADS * MLSTM_V_DIM
MLSTM_QK_WIDTH = MLSTM_HEADS * MLSTM_QK_DIM
CONV_WIDTH = 4
IN_SIZES = (FOX_WIDTH, FOX_WIDTH, FOX_WIDTH, FOX_WIDTH, FOX_HEADS,
            MLSTM_QK_WIDTH, MLSTM_QK_WIDTH, MLSTM_V_WIDTH, MLSTM_V_WIDTH, MLSTM_V_WIDTH,
            MLSTM_HEADS, MLSTM_HEADS)
IN_OFFS = tuple(int(v) for v in np.cumsum((0,) + IN_SIZES))

LANES = 128
F32_SUBLANES = 8
SEG = 1024
VMEM_LIMIT = 56 * 1024 * 1024
BIG_TILE_VMEM_LIMIT = 62 * 1024 * 1024

PROJ_TM = 256
GATE_R = 1024
MLSTM_L = 256
ATT_T = 512
FUSE_TM = 512

LOG2E = 1.4426950408889634
FOX_Q_SCALE = FOX_HEAD_DIM ** -0.5 * LOG2E
ONES_ROWS = 16
ATT_QBLOCKS = 2
ATT_LOOKAHEAD = 1

LANE_G, LANE_U, LANE_A, LANE_NEGM = 8, 16, 24, 32
ROW_G, ROW_U, ROW_NEGM = 0, 8, 24
GATE_ROWS = 32

BF16 = jnp.bfloat16
F32 = jnp.float32
NT_DIMS = (((1,), (1,)), ((), ()))
TN_DIMS = (((0,), (0,)), ((), ()))


def _sigmoid(x):
    return 1.0 / (1.0 + jnp.exp(-x))


def _params(sem, vmem_limit=VMEM_LIMIT):
    return pltpu.CompilerParams(dimension_semantics=sem, vmem_limit_bytes=vmem_limit)


def _inproj_kernel(x_ref, nw_ref, wt_ref, wg_ref, kr_ref, qkr_ref, t16_ref, t32_ref, g_ref):
    xf = x_ref[...]
    ms = jnp.mean(xf * xf, axis=-1, keepdims=True)
    hb = (xf * lax.rsqrt(ms + NORM_EPS) * nw_ref[...]).astype(BF16)

    def seg_t(j):
        return lax.dot_general(wt_ref[j * SEG:(j + 1) * SEG, :], hb, NT_DIMS, preferred_element_type=F32)

    t16_ref[0:SEG, :] = (seg_t(0) * FOX_Q_SCALE).astype(BF16)
    kr_ref[...] = seg_t(1).T.astype(BF16)
    qkr_ref[...] = seg_t(2).T
    t16_ref[SEG:2 * SEG, :] = seg_t(3).astype(BF16)
    t16_ref[2 * SEG:3 * SEG, :] = seg_t(4).astype(BF16)
    for r in range(3):
        t32_ref[r * SEG:(r + 1) * SEG, :] = seg_t(5 + r)
    g_ref[...] = lax.dot_general(hb, wg_ref[...], NT_DIMS, preferred_element_type=F32)


def _inproj(x2, nw, w_t, w_gate, seq):
    m = x2.shape[0]
    tm = PROJ_TM
    per_seq = seq // tm
    t_map = lambda i: (i // per_seq, 0, i % per_seq)
    return pl.pallas_call(
        _inproj_kernel,
        grid=(m // tm,),
        in_specs=[
            pl.BlockSpec((tm, D_MODEL), lambda i: (i, 0)),
            pl.BlockSpec((1, D_MODEL), lambda i: (0, 0)),
            pl.BlockSpec(w_t.shape, lambda i: (0, 0)),
            pl.BlockSpec((2 * LANES, D_MODEL), lambda i: (0, 0)),
        ],
        out_specs=[
            pl.BlockSpec((tm, SEG), lambda i: (i, 0)),
            pl.BlockSpec((tm, SEG), lambda i: (i, 0)),
            pl.BlockSpec((None, 3 * SEG, tm), t_map),
            pl.BlockSpec((None, 3 * SEG, tm), t_map),
            pl.BlockSpec((tm, 2 * LANES), lambda i: (i, 0)),
        ],
        out_shape=[
            jax.ShapeDtypeStruct((m, SEG), BF16),
            jax.ShapeDtypeStruct((m, SEG), F32),
            jax.ShapeDtypeStruct((m // seq, 3 * SEG, seq), BF16),
            jax.ShapeDtypeStruct((m // seq, 3 * SEG, seq), F32),
            jax.ShapeDtypeStruct((m, 2 * LANES), F32),
        ],
        compiler_params=_params(("arbitrary",), BIG_TILE_VMEM_LIMIT),
        name="inproj",
    )(x2, nw, w_t, w_gate)


def _split3(v):
    hi = v.astype(BF16)
    r1 = v - hi.astype(F32)
    mid = r1.astype(BF16)
    lo = (r1 - mid.astype(F32)).astype(BF16)
    return hi, mid, lo


def _gates_kernel(g_ref, b_ref, mcol_ref, mrow_ref, carry_ref, mprev_ref):
    rows, chunk = GATE_R, MLSTM_L

    @pl.when(pl.program_id(1) == 0)
    def _():
        carry_ref[...] = jnp.zeros_like(carry_ref)
        mprev_ref[...] = jnp.zeros_like(mprev_ref)

    v = g_ref[...] + b_ref[...]
    v1 = v[:, :LANES]
    v2 = v[:, LANES:]
    ls = jnp.minimum(v1, 0.0) - jnp.log1p(jnp.exp(-jnp.abs(v1)))

    tri = (lax.broadcasted_iota(jnp.int32, (chunk, chunk), 1)
           <= lax.broadcasted_iota(jnp.int32, (chunk, chunk), 0)).astype(BF16)
    hi, mid, lo = _split3(ls)
    cuml = jnp.concatenate([
        jnp.dot(tri, hi[c * chunk:(c + 1) * chunk], preferred_element_type=F32)
        + jnp.dot(tri, mid[c * chunk:(c + 1) * chunk], preferred_element_type=F32)
        + jnp.dot(tri, lo[c * chunk:(c + 1) * chunk], preferred_element_type=F32)
        for c in range(rows // chunk)], axis=0)
    carry = carry_ref[...]
    cumf = []
    for c in range(rows // chunk):
        cumf.append(cuml[c * chunk:(c + 1) * chunk] + carry)
        carry = cumf[c][chunk - 1:chunk, :]
    carry_ref[...] = carry
    cumf = jnp.concatenate(cumf, axis=0)

    a = v2 - cuml
    t_in = lax.broadcasted_iota(jnp.int32, (rows, LANES), 0) % chunk
    cm = a
    s = 1
    while s < chunk:
        cm = jnp.where(t_in >= s, jnp.maximum(cm, pltpu.roll(cm, s, axis=0)), cm)
        s *= 2

    lane = lax.broadcasted_iota(jnp.int32, (chunk, LANES), 1)
    mprev = mprev_ref[...]
    for c in range(rows // chunk):
        sl = slice(c * chunk, (c + 1) * chunk)
        b = cuml[sl]
        mt = b + jnp.maximum(mprev, cm[sl])
        gt = b - mt
        ut = (b + mprev) - mt
        tile = jnp.where(lane < LANE_G, cumf[sl] * LOG2E,
               jnp.where(lane < LANE_U, gt,
               jnp.where(lane < LANE_A, pltpu.roll(ut, LANE_U - LANE_G, axis=1),
               jnp.where(lane < LANE_NEGM, pltpu.roll(a[sl], LANE_A - LANE_G, axis=1),
                         pltpu.roll(-mt, LANE_NEGM - LANE_G, axis=1)))))
        mprev = mt[chunk - 1:chunk, :]
        mcol_ref[sl, :] = tile
        mrow_ref[:, sl] = tile.T[LANE_G:LANE_G + GATE_ROWS, :]
    mprev_ref[...] = mprev


def _gates(g3, bias_vec):
    b, s, _ = g3.shape
    r = GATE_R
    return pl.pallas_call(
        _gates_kernel,
        grid=(b, s // r),
        in_specs=[
            pl.BlockSpec((None, r, 2 * LANES), lambda bi, ri: (bi, ri, 0)),
            pl.BlockSpec((1, 2 * LANES), lambda bi, ri: (0, 0)),
        ],
        out_specs=[
            pl.BlockSpec((None, r, LANES), lambda bi, ri: (bi, ri, 0)),
            pl.BlockSpec((None, GATE_ROWS, r), lambda bi, ri: (bi, 0, ri)),
        ],
        out_shape=[
            jax.ShapeDtypeStruct((b, s, LANES), F32),
            jax.ShapeDtypeStruct((b, GATE_ROWS, s), F32),
        ],
        scratch_shapes=[pltpu.VMEM((1, LANES), F32), pltpu.VMEM((1, LANES), F32)],
        compiler_params=_params(("arbitrary", "arbitrary")),
        name="gates",
    )(g3, bias_vec)


def _fox_kernel(it_ref, jt_ref, qt_ref, k_ref, vt_ref, gc_ref, zt_ref, nwb_ref, o_ref, m_ref, acc_ref):
    step_id = pl.program_id(1)
    i = it_ref[step_id]
    j = jt_ref[step_id]
    t = ATT_T
    hd = FOX_HEAD_DIM

    @pl.when(j == 0)
    def _():
        m_ref[...] = jnp.full_like(m_ref, NEG_INF)
        acc_ref[...] = jnp.zeros_like(acc_ref)

    def step(modes):
        gc = gc_ref[...]
        ones = jnp.ones((ONES_ROWS, t), BF16)
        units = [(h, c) for h in range(FOX_HEADS) for c in range(ATT_QBLOCKS) if modes[c]]

        def scores(h, c):
            hs = slice(h * hd, (h + 1) * hd)
            st = jnp.dot(k_ref[:, hs], qt_ref[hs, c * t:(c + 1) * t],
                         preferred_element_type=F32) - gc[:, h:h + 1]
            if modes[c] == 'diag':
                valid = (lax.broadcasted_iota(jnp.int32, (t, t), 0)
                         <= lax.broadcasted_iota(jnp.int32, (t, t), 1))
                st = jnp.where(valid, st, NEG_INF)
            return st

        def probs(h, c, st):
            qs = slice(c * t, (c + 1) * t)
            m_prev = m_ref[h:h + 1, qs]
            m_new = jnp.maximum(m_prev, jnp.max(st, axis=0, keepdims=True))
            m_ref[h:h + 1, qs] = m_new
            return jnp.exp2(m_prev - m_new), jnp.exp2(st - m_new).astype(BF16)

        def accumulate(h, c, alpha, pt):
            qs = slice(c * t, (c + 1) * t)
            vt_aug = jnp.concatenate([vt_ref[h * hd:(h + 1) * hd, :], ones], axis=0)
            acc_ref[h, :, qs] = alpha * acc_ref[h, :, qs] + jnp.dot(vt_aug, pt, preferred_element_type=F32)

        st, ap = {}, {}
        for u in range(-ATT_LOOKAHEAD, len(units) + 1):
            if 0 <= u + ATT_LOOKAHEAD < len(units):
                st[u + ATT_LOOKAHEAD] = scores(*units[u + ATT_LOOKAHEAD])
            if 0 <= u < len(units):
                ap[u] = probs(*units[u], st.pop(u))
            if 0 <= u - 1 < len(units):
                accumulate(*units[u - 1], *ap.pop(u - 1))

    first = i * ATT_QBLOCKS

    @pl.when(j < first)
    def _():
        step(('full',) * ATT_QBLOCKS)

    for r in range(ATT_QBLOCKS):
        @pl.when(j == first + r)
        def _(r=r):
            step(tuple(None if c < r else 'diag' if c == r else 'full' for c in range(ATT_QBLOCKS)))
            if r == ATT_QBLOCKS - 1:
                for h in range(FOX_HEADS):
                    hs = slice(h * hd, (h + 1) * hd)
                    for c in range(ATT_QBLOCKS):
                        qs = slice(c * t, (c + 1) * t)
                        acc = acc_ref[h, :, qs]
                        o = acc[:hd] * (1.0 / acc[hd:hd + 1])
                        o = o * lax.rsqrt(jnp.mean(o * o, axis=0, keepdims=True) + NORM_EPS) * nwb_ref[hs, :]
                        z = zt_ref[hs, qs]
                        o_ref[hs, qs] = (o * (z * _sigmoid(z))).astype(BF16)


def _fox(kr, t16, t32, mcol, nwb):
    b, s, _ = kr.shape
    t = ATT_T
    tq = ATT_T * ATT_QBLOCKS
    pairs = [(i, j) for i in range(s // tq) for j in range((i + 1) * ATT_QBLOCKS)]
    it = jnp.asarray([p[0] for p in pairs], jnp.int32)
    jt = jnp.asarray([p[1] for p in pairs], jnp.int32)

    grid_spec = pltpu.PrefetchScalarGridSpec(
        num_scalar_prefetch=2,
        grid=(b, len(pairs)),
        in_specs=[
            pl.BlockSpec((None, FOX_WIDTH, tq), lambda bi, p, it, jt: (bi, 0, it[p])),
            pl.BlockSpec((None, t, FOX_WIDTH), lambda bi, p, it, jt: (bi, jt[p], 0)),
            pl.BlockSpec((None, FOX_WIDTH, t), lambda bi, p, it, jt: (bi, 1, jt[p])),
            pl.BlockSpec((None, t, LANES), lambda bi, p, it, jt: (bi, jt[p], 0)),
            pl.BlockSpec((None, FOX_WIDTH, tq), lambda bi, p, it, jt: (bi, 0, it[p])),
            pl.BlockSpec((FOX_WIDTH, t), lambda bi, p, it, jt: (0, 0)),
        ],
        out_specs=pl.BlockSpec((None, FOX_WIDTH, tq), lambda bi, p, it, jt: (bi, 0, it[p])),
        scratch_shapes=[pltpu.VMEM((FOX_HEADS, tq), F32),
                        pltpu.VMEM((FOX_HEADS, FOX_HEAD_DIM + ONES_ROWS, tq), F32)],
    )
    return pl.pallas_call(
        _fox_kernel,
        grid_spec=grid_spec,
        out_shape=jax.ShapeDtypeStruct((b, FOX_WIDTH, s), BF16),
        compiler_params=_params(("arbitrary", "arbitrary")),
        name="fox_attention",
    )(it, jt, t16, kr, t16, mcol, t32, nwb)


def _mlstm_conv_tile(c, t, n_chunks, qk_ref, cw_ref, cb_ref, hist_ref):
    n = MLSTM_L
    ls = slice(t * LANES, (t + 1) * LANES)
    u = qk_ref[c * n:(c + 1) * n, ls]
    before = hist_ref[:, ls] if c == 0 else qk_ref[c * n - F32_SUBLANES:c * n, ls]
    ext = jnp.concatenate([before, u], axis=0)
    if c == n_chunks - 1:
        hist_ref[:, ls] = u[n - F32_SUBLANES:n, :]

    def tap(jj):
        back = CONV_WIDTH - 1 - jj
        return pltpu.roll(ext, back, axis=0)[F32_SUBLANES:, :] if back else u

    y = tap(0) * cw_ref[0:1, ls]
    for jj in range(1, CONV_WIDTH):
        y = y + tap(jj) * cw_ref[jj:jj + 1, ls]
    y = y + cb_ref[:, ls]
    return y * _sigmoid(y)


def _mlstm_chunk(c, sy, vt_ref, ogt_ref, zt_ref, mcol_ref, mrow_ref, nwb_ref, write_out, st_ref,
                 after_head=lambda h: None):
    n = MLSTM_L
    dv = MLSTM_V_DIM
    ts = slice(c * n, (c + 1) * n)

    mc = mcol_ref[ts, :]
    mr = mrow_ref[:, ts]
    lane = lax.broadcasted_iota(jnp.int32, (n, LANES), 1)
    ones_rows = jnp.where(lax.broadcasted_iota(jnp.int32, (ONES_ROWS, n), 0) == 0, 1.0, 0.0).astype(BF16)
    valid = lax.broadcasted_iota(jnp.int32, (n, n), 0) <= lax.broadcasted_iota(jnp.int32, (n, n), 1)
    heads = range(MLSTM_HEADS)
    row = lambda base_row, h: mr[base_row + h:base_row + h + 1, :]

    qp, kh, st_t, inter_t, vt_aug = [], [], [], [], []
    for h in heads:
        pr = h // 2
        qp.append(sy[pr].astype(BF16))
        kp = sy[MLSTM_QK_WIDTH // LANES + pr] * (MLSTM_QK_DIM ** -0.5)
        kh.append(jnp.where((lane < MLSTM_QK_DIM) if h % 2 == 0 else (lane >= MLSTM_QK_DIM), kp, 0.0))
        st_t.append(lax.dot_general(kh[h].astype(BF16), qp[h], NT_DIMS, preferred_element_type=F32))
        inter_t.append(lax.dot_general(st_ref[h].astype(BF16), qp[h], NT_DIMS, preferred_element_type=F32))
        vt_aug.append(jnp.concatenate([vt_ref[h * dv:(h + 1) * dv, ts], ones_rows], axis=0))

    for h in heads:
        hs = slice(h * dv, (h + 1) * dv)
        a_col = mc[:, LANE_A + h:LANE_A + h + 1]
        sc_t = st_t[h] * jnp.exp(jnp.where(valid, a_col + row(ROW_G, h), NEG_INF))
        pv = jnp.dot(vt_aug[h], sc_t.astype(BF16), preferred_element_type=F32)
        after_head(h)
        res = pv + jnp.exp(row(ROW_U, h)) * inter_t[h]
        den = res[dv:dv + 1, :]
        ht = res[:dv] * (1.0 / jnp.maximum(jnp.abs(den), jnp.exp(row(ROW_NEGM, h))))
        hb = ht * _sigmoid(ogt_ref[hs, ts])
        hn = hb * lax.rsqrt(jnp.mean(hb * hb, axis=0, keepdims=True) + NORM_EPS) * nwb_ref[hs, :]
        z = zt_ref[hs, ts]
        write_out(hs, ts, (hn * (z * _sigmoid(z))).astype(BF16))

    for h in heads:
        a_col = mc[:, LANE_A + h:LANE_A + h + 1]
        g_last = mc[n - 1:n, LANE_G + h:LANE_G + h + 1]
        u_last = mc[n - 1:n, LANE_U + h:LANE_U + h + 1]
        kw = (kh[h] * jnp.exp(a_col + g_last)).astype(BF16)
        upd = jnp.dot(vt_aug[h], kw, preferred_element_type=F32)
        st_ref[h] = jnp.exp(u_last) * st_ref[h] + upd


def _mlstm_out_kernel(qk_ref, vt_ref, ogt_ref, zt_ref, mcol_ref, mrow_ref, cw_ref, cb_ref, nwb_ref,
                      yat_ref, wa_ref, wb_ref, x_ref, fw_ref, o_ref, hist_ref, st_ref, ybt_ref,
                      *, per_seq, final):
    step = pl.program_id(0)
    cur = step % 2
    n = MLSTM_L

    @pl.when(step % per_seq == 0)
    def _():
        hist_ref[...] = jnp.zeros_like(hist_ref)
        st_ref[...] = jnp.zeros_like(st_ref)

    @pl.when(step == 0)
    def _():
        ybt_ref[1] = jnp.zeros_like(ybt_ref[1])

    def write_yb(hs, ts, val):
        ybt_ref[cur, hs, ts] = val

    prev = ybt_ref.at[1 - cur]
    n_chunks = FUSE_TM // n
    q_tiles = MLSTM_QK_WIDTH // LANES
    conv_tile = lambda c, t: _mlstm_conv_tile(c, t, n_chunks, qk_ref, cw_ref, cb_ref, hist_ref)
    cols = D_MODEL // MLSTM_HEADS
    for c in range(n_chunks):
        rs = slice(c * n, (c + 1) * n)
        sy = [conv_tile(c, t) for t in range(2 * q_tiles)]
        proj = []
        ya = yat_ref[:, rs].T
        yb = prev[:, rs].T

        def woven(h, proj=proj, ya=ya, yb=yb):
            cs = slice(h * cols, (h + 1) * cols)
            proj.append(jnp.dot(ya, wa_ref[:, cs], preferred_element_type=F32)
                        + jnp.dot(yb, wb_ref[:, cs], preferred_element_type=F32))

        _mlstm_chunk(c, sy, vt_ref, ogt_ref, zt_ref, mcol_ref, mrow_ref, nwb_ref, write_yb, st_ref,
                     after_head=woven)
        acc = jnp.concatenate(proj, axis=1)
        r = x_ref[rs, :] + acc
        if final:
            r = r * lax.rsqrt(jnp.mean(r * r, axis=-1, keepdims=True) + NORM_EPS) * fw_ref[...]
        o_ref[rs, :] = r


def _mlstm_out(qkr, t16, t32, mcol, mrow, cw, cb, nwb, yat, wo, x2, fw, final):
    assert FOX_WIDTH == MLSTM_V_WIDTH and wo.shape == (FOX_WIDTH + MLSTM_V_WIDTH, D_MODEL)
    m = x2.shape[0]
    _, _, seq = yat.shape
    tm = FUSE_TM
    per_seq = seq // tm
    blocks = m // tm
    cur = lambda s: jnp.minimum(s, blocks - 1)
    prv = lambda s: jnp.maximum(s - 1, 0)
    t_cur = lambda which: (lambda s: (cur(s) // per_seq, which, cur(s) % per_seq))
    return pl.pallas_call(
        functools.partial(_mlstm_out_kernel, per_seq=per_seq, final=final),
        grid=(blocks + 1,),
        in_specs=[
            pl.BlockSpec((tm, SEG), lambda s: (cur(s), 0)),
            pl.BlockSpec((None, SEG, tm), t_cur(2)),
            pl.BlockSpec((None, SEG, tm), t_cur(1)),
            pl.BlockSpec((None, SEG, tm), t_cur(2)),
            pl.BlockSpec((tm, LANES), lambda s: (cur(s), 0)),
            pl.BlockSpec((None, GATE_ROWS, tm), lambda s: (cur(s) // per_seq, 0, cur(s) % per_seq)),
            pl.BlockSpec((CONV_WIDTH, SEG), lambda s: (0, 0)),
            pl.BlockSpec((1, SEG), lambda s: (0, 0)),
            pl.BlockSpec((SEG, MLSTM_L), lambda s: (0, 0)),
            pl.BlockSpec((None, FOX_WIDTH, tm), lambda s: (prv(s) // per_seq, 0, prv(s) % per_seq)),
            pl.BlockSpec((FOX_WIDTH, D_MODEL), lambda s: (0, 0)),
            pl.BlockSpec((MLSTM_V_WIDTH, D_MODEL), lambda s: (1, 0)),
            pl.BlockSpec((tm, D_MODEL), lambda s: (prv(s), 0)),
            pl.BlockSpec((1, D_MODEL), lambda s: (0, 0)),
        ],
        out_specs=pl.BlockSpec((tm, D_MODEL), lambda s: (prv(s), 0)),
        out_shape=jax.ShapeDtypeStruct((m, D_MODEL), F32),
        scratch_shapes=[pltpu.VMEM((F32_SUBLANES, SEG), F32),
                        pltpu.VMEM((MLSTM_HEADS, MLSTM_V_DIM + ONES_ROWS, LANES), F32),
                        pltpu.VMEM((2, MLSTM_V_WIDTH, tm), BF16)],
        compiler_params=_params(("arbitrary",)),
        name="mlstm_outproj",
    )(qkr, t16, t32, t32, mcol, mrow, cw, cb, nwb, yat, wo, wo, x2, fw)


SEG_STARTS = (IN_OFFS[0], IN_OFFS[1], IN_OFFS[5], IN_OFFS[2], IN_OFFS[7], IN_OFFS[3], IN_OFFS[8], IN_OFFS[9])
SEG_SHIFT = FOX_HEADS
assert all(st % SEG in (0, SEG_SHIFT) for st in SEG_STARTS) and IN_OFFS[6] == IN_OFFS[5] + MLSTM_QK_WIDTH
assert SEG_SHIFT == F32_SUBLANES and MLSTM_HEADS == F32_SUBLANES

J_FOX_GATE = 2
assert SEG_STARTS[J_FOX_GATE] - SEG_SHIFT == IN_OFFS[4]
GW_FOX_F, GW_MLSTM_F, GW_MLSTM_I = 0, 8, LANES + 8


def _wprep_kernel(main_blk_ref, extra_blk_ref, main_ref, extra_ref, mi_ref, mf_ref, wout_ref,
                  o_ref, gw_ref, wo_ref, gwf_ref):
    del main_blk_ref, extra_blk_ref
    j = pl.program_id(0)
    in_seg = j < len(SEG_STARTS)
    shifted = functools.reduce(jnp.logical_or, [j == k for k, st in enumerate(SEG_STARTS) if st % SEG])

    @pl.when(jnp.logical_not(in_seg))
    def _():
        wo_ref[...] = wout_ref[...].astype(BF16)

    @pl.when(j == 0)
    def _():
        gwf_ref[...] = jnp.zeros_like(gwf_ref)
        gwf_ref[GW_MLSTM_I:GW_MLSTM_I + MLSTM_HEADS, :] = mi_ref[...]
        gwf_ref[GW_MLSTM_F:GW_MLSTM_F + MLSTM_HEADS, :] = mf_ref[...]

    @pl.when(in_seg & jnp.logical_not(shifted))
    def _():
        o_ref[...] = main_ref[...].astype(BF16)

    @pl.when(shifted)
    def _():
        full = jnp.concatenate([main_ref[...], extra_ref[...]], axis=0)
        o_ref[...] = full[SEG_SHIFT:SEG_SHIFT + SEG, :].astype(BF16)

    @pl.when(j == J_FOX_GATE)
    def _():
        gwf_ref[GW_FOX_F:GW_FOX_F + FOX_HEADS, :] = main_ref[0:SEG_SHIFT, :]

    @pl.when(j == len(SEG_STARTS) - 1)
    def _():
        gw_ref[...] = gwf_ref[...].astype(BF16)


def _wprep(w_t, w_out):
    d = w_t.shape[1]
    nseg = len(SEG_STARTS)
    out_steps = w_out.shape[0] // SEG
    pad = [SEG_STARTS[-1]] * out_steps
    main_blk = jnp.asarray([st // SEG for st in SEG_STARTS + tuple(pad)], jnp.int32)
    extra_blk = jnp.asarray([(st // SEG + 1) * (SEG // F32_SUBLANES) for st in SEG_STARTS + tuple(pad)],
                            jnp.int32)
    out_blk = lambda j: jnp.clip(j - nseg, 0, out_steps - 1)
    grid_spec = pltpu.PrefetchScalarGridSpec(
        num_scalar_prefetch=2,
        grid=(nseg + out_steps,),
        in_specs=[
            pl.BlockSpec((SEG, d), lambda j, mb, eb: (mb[j], 0)),
            pl.BlockSpec((F32_SUBLANES, d), lambda j, mb, eb: (eb[j], 0)),
            pl.BlockSpec((MLSTM_HEADS, d), lambda j, mb, eb: (IN_OFFS[10] // F32_SUBLANES, 0)),
            pl.BlockSpec((MLSTM_HEADS, d), lambda j, mb, eb: (IN_OFFS[11] // F32_SUBLANES, 0)),
            pl.BlockSpec((SEG, w_out.shape[1]), lambda j, mb, eb: (out_blk(j), 0)),
        ],
        out_specs=[
            pl.BlockSpec((SEG, d), lambda j, mb, eb: (jnp.minimum(j, nseg - 1), 0)),
            pl.BlockSpec((2 * LANES, d), lambda j, mb, eb: (0, 0)),
            pl.BlockSpec((SEG, w_out.shape[1]), lambda j, mb, eb: (out_blk(j), 0)),
        ],
        scratch_shapes=[pltpu.VMEM((2 * LANES, d), F32)],
    )
    return pl.pallas_call(
        _wprep_kernel,
        grid_spec=grid_spec,
        out_shape=[jax.ShapeDtypeStruct((nseg * SEG, d), BF16),
                   jax.ShapeDtypeStruct((2 * LANES, d), BF16),
                   jax.ShapeDtypeStruct(w_out.shape, BF16)],
        compiler_params=_params(("arbitrary",)),
        name="weight_layout",
    )(main_blk, extra_blk, w_t, w_t, w_t, w_t, w_out)


def _lane_broadcast(v, width):
    return jnp.broadcast_to(v[:, None], (v.shape[0], width))


def kernel(x, norm_w, w_in, fox_f_bias, conv_w, conv_b, mlstm_i_bias, mlstm_f_bias,
           fox_out_norm_w, mlstm_out_norm_w, w_out, final_norm_w):
    b, s, d = x.shape
    depth = norm_w.shape[0]
    x2 = x.reshape(b * s, d)
    for l in range(depth):
        w_t, w_gate, wo = _wprep(w_in[l].T, w_out[l])
        bias_vec = jnp.concatenate([
            fox_f_bias[l], mlstm_f_bias[l], jnp.zeros((LANES - 16,), F32),
            jnp.zeros((8,), F32), mlstm_i_bias[l], jnp.zeros((LANES - 16,), F32)]).reshape(1, 2 * LANES)

        kr, qkr, t16, t32, g = _inproj(x2, norm_w[l].reshape(1, d), w_t, w_gate, s)
        mcol, mrow = _gates(g.reshape(b, s, 2 * LANES), bias_vec)

        yat = _fox(kr.reshape(b, s, SEG), t16, t32, mcol, _lane_broadcast(fox_out_norm_w[l], ATT_T))
        x2 = _mlstm_out(qkr, t16, t32, mcol.reshape(b * s, LANES), mrow, conv_w[l], conv_b[l].reshape(1, SEG),
                        _lane_broadcast(mlstm_out_norm_w[l], MLSTM_L),
                        yat, wo, x2, final_norm_w.reshape(1, d),
                        final=(l == depth - 1))
    return x2.reshape(b, s, d)
```

```python
import functools

import numpy as np
import jax
import jax.numpy as jnp
from jax import lax
from jax.experimental import pallas as pl
from jax.experimental.pallas import tpu as pltpu

D_MODEL = 2048
NORM_EPS = 1e-6
NEG_INF = -1e30

FOX_HEADS = 8
FOX_HEAD_DIM = 128
FOX_WIDTH = FOX_HEADS * FOX_HEAD_DIM
MLSTM_HEADS = 8
MLSTM_V_DIM = 128
MLSTM_QK_DIM = 64
MLSTM_V_WIDTH = MLSTM_HEADS * MLSTM_V_DIM
MLSTM_QK_WIDTH = MLSTM_HEADS * MLSTM_QK_DIM
CONV_WIDTH = 4
IN_SIZES = (FOX_WIDTH, FOX_WIDTH, FOX_WIDTH, FOX_WIDTH, FOX_HEADS,
            MLSTM_QK_WIDTH, MLSTM_QK_WIDTH, MLSTM_V_WIDTH, MLSTM_V_WIDTH, MLSTM_V_WIDTH,
            MLSTM_HEADS, MLSTM_HEADS)
IN_OFFS = tuple(int(v) for v in np.cumsum((0,) + IN_SIZES))

LANES = 128
F32_SUBLANES = 8
SEG = 1024
VMEM_LIMIT = 56 * 1024 * 1024

PROJ_TM = 256
GATE_R = 2048
MLSTM_L = 256
ATT_T = 512
FUSE_TM = 512

LOG2E = 1.4426950408889634
FOX_Q_SCALE = FOX_HEAD_DIM ** -0.5 * LOG2E
ONES_ROWS = 16
ATT_QBLOCKS = 2
ATT_LOOKAHEAD = 1

LANE_G, LANE_U, LANE_A, LANE_NEGM = 8, 16, 24, 32
ROW_G, ROW_U, ROW_NEGM = 0, 8, 24
GATE_ROWS = 32

BF16 = jnp.bfloat16
F32 = jnp.float32
NT_DIMS = (((1,), (1,)), ((), ()))


def _sigmoid(x):
    return 1.0 / (1.0 + jnp.exp(-x))


def _params(sem):
    return pltpu.CompilerParams(dimension_semantics=sem, vmem_limit_bytes=VMEM_LIMIT)


def _inproj_kernel(x_ref, nw_ref, wt_ref, wg_ref, kr_ref, qkr_ref, t16_ref, t32_ref, g_ref):
    xf = x_ref[...]
    ms = jnp.mean(xf * xf, axis=-1, keepdims=True)
    hb = (xf * lax.rsqrt(ms + NORM_EPS) * nw_ref[...]).astype(BF16)

    def seg_t(j):
        return lax.dot_general(wt_ref[j * SEG:(j + 1) * SEG, :], hb, NT_DIMS, preferred_element_type=F32)

    t16_ref[0:SEG, :] = (seg_t(0) * FOX_Q_SCALE).astype(BF16)
    kr_ref[...] = seg_t(1).T.astype(BF16)
    qkr_ref[...] = seg_t(2).T
    t16_ref[SEG:2 * SEG, :] = seg_t(3).astype(BF16)
    t16_ref[2 * SEG:3 * SEG, :] = seg_t(4).astype(BF16)
    for r in range(3):
        t32_ref[r * SEG:(r + 1) * SEG, :] = seg_t(5 + r)
    g_ref[...] = lax.dot_general(hb, wg_ref[...], NT_DIMS, preferred_element_type=F32)


def _inproj(x2, nw, w_t, w_gate, seq):
    m = x2.shape[0]
    tm = PROJ_TM
    per_seq = seq // tm
    t_map = lambda i: (i // per_seq, 0, i % per_seq)
    return pl.pallas_call(
        _inproj_kernel,
        grid=(m // tm,),
        in_specs=[
            pl.BlockSpec((tm, D_MODEL), lambda i: (i, 0)),
            pl.BlockSpec((1, D_MODEL), lambda i: (0, 0)),
            pl.BlockSpec(w_t.shape, lambda i: (0, 0)),
            pl.BlockSpec((2 * LANES, D_MODEL), lambda i: (0, 0)),
        ],
        out_specs=[
            pl.BlockSpec((tm, SEG), lambda i: (i, 0)),
            pl.BlockSpec((tm, SEG), lambda i: (i, 0)),
            pl.BlockSpec((None, 3 * SEG, tm), t_map),
            pl.BlockSpec((None, 3 * SEG, tm), t_map),
            pl.BlockSpec((tm, 2 * LANES), lambda i: (i, 0)),
        ],
        out_shape=[
            jax.ShapeDtypeStruct((m, SEG), BF16),
            jax.ShapeDtypeStruct((m, SEG), F32),
            jax.ShapeDtypeStruct((m // seq, 3 * SEG, seq), BF16),
            jax.ShapeDtypeStruct((m // seq, 3 * SEG, seq), F32),
            jax.ShapeDtypeStruct((m, 2 * LANES), F32),
        ],
        compiler_params=_params(("arbitrary",)),
        name="inproj",
    )(x2, nw, w_t, w_gate)


def _split3(v):
    hi = v.astype(BF16)
    r1 = v - hi.astype(F32)
    mid = r1.astype(BF16)
    lo = (r1 - mid.astype(F32)).astype(BF16)
    return hi, mid, lo


def _gates_kernel(g_ref, b_ref, mcol_ref, mrow_ref, carry_ref, mprev_ref):
    rows, chunk = GATE_R, MLSTM_L

    @pl.when(pl.program_id(1) == 0)
    def _():
        carry_ref[...] = jnp.zeros_like(carry_ref)
        mprev_ref[...] = jnp.zeros_like(mprev_ref)

    v = g_ref[...] + b_ref[...]
    v1 = v[:, :LANES]
    v2 = v[:, LANES:]
    ls = jnp.minimum(v1, 0.0) - jnp.log1p(jnp.exp(-jnp.abs(v1)))

    tri = (lax.broadcasted_iota(jnp.int32, (chunk, chunk), 1)
           <= lax.broadcasted_iota(jnp.int32, (chunk, chunk), 0)).astype(BF16)
    hi, mid, lo = _split3(ls)
    cuml = jnp.concatenate([
        jnp.dot(tri, hi[c * chunk:(c + 1) * chunk], preferred_element_type=F32)
        + jnp.dot(tri, mid[c * chunk:(c + 1) * chunk], preferred_element_type=F32)
        + jnp.dot(tri, lo[c * chunk:(c + 1) * chunk], preferred_element_type=F32)
        for c in range(rows // chunk)], axis=0)
    carry = carry_ref[...]
    cumf = []
    for c in range(rows // chunk):
        cumf.append(cuml[c * chunk:(c + 1) * chunk] + carry)
        carry = cumf[c][chunk - 1:chunk, :]
    carry_ref[...] = carry
    cumf = jnp.concatenate(cumf, axis=0)

    a = v2 - cuml
    t_in = lax.broadcasted_iota(jnp.int32, (rows, LANES), 0) % chunk
    cm = a
    s = 1
    while s < chunk:
        cm = jnp.where(t_in >= s, jnp.maximum(cm, pltpu.roll(cm, s, axis=0)), cm)
        s *= 2

    lane = lax.broadcasted_iota(jnp.int32, (chunk, LANES), 1)
    mprev = mprev_ref[...]
    for c in range(rows // chunk):
        sl = slice(c * chunk, (c + 1) * chunk)
        b = cuml[sl]
        mt = b + jnp.maximum(mprev, cm[sl])
        gt = b - mt
        ut = (b + mprev) - mt
        tile = jnp.where(lane < LANE_G, cumf[sl] * LOG2E,
               jnp.where(lane < LANE_U, gt,
               jnp.where(lane < LANE_A, pltpu.roll(ut, LANE_U - LANE_G, axis=1),
               jnp.where(lane < LANE_NEGM, pltpu.roll(a[sl], LANE_A - LANE_G, axis=1),
                         pltpu.roll(-mt, LANE_NEGM - LANE_G, axis=1)))))
        mprev = mt[chunk - 1:chunk, :]
        mcol_ref[sl, :] = tile
        mrow_ref[:, sl] = tile.T[LANE_G:LANE_G + GATE_ROWS, :]
    mprev_ref[...] = mprev


def _gates(g3, bias_vec):
    b, s, _ = g3.shape
    r = GATE_R
    return pl.pallas_call(
        _gates_kernel,
        grid=(b, s // r),
        in_specs=[
            pl.BlockSpec((None, r, 2 * LANES), lambda bi, ri: (bi, ri, 0)),
            pl.BlockSpec((1, 2 * LANES), lambda bi, ri: (0, 0)),
        ],
        out_specs=[
            pl.BlockSpec((None, r, LANES), lambda bi, ri: (bi, ri, 0)),
            pl.BlockSpec((None, GATE_ROWS, r), lambda bi, ri: (bi, 0, ri)),
        ],
        out_shape=[
            jax.ShapeDtypeStruct((b, s, LANES), F32),
            jax.ShapeDtypeStruct((b, GATE_ROWS, s), F32),
        ],
        scratch_shapes=[pltpu.VMEM((1, LANES), F32), pltpu.VMEM((1, LANES), F32)],
        compiler_params=_params(("arbitrary", "arbitrary")),
        name="gates",
    )(g3, bias_vec)


def _fox_kernel(it_ref, jt_ref, qt_ref, k_ref, vt_ref, gc_ref, zt_ref, nwb_ref, o_ref, m_ref, acc_ref):
    step_id = pl.program_id(1)
    i = it_ref[step_id]
    j = jt_ref[step_id]
    t = ATT_T
    hd = FOX_HEAD_DIM

    @pl.when(j == 0)
    def _():
        m_ref[...] = jnp.full_like(m_ref, NEG_INF)
        acc_ref[...] = jnp.zeros_like(acc_ref)

    def step(modes):
        gc = gc_ref[...]
        ones = jnp.ones((ONES_ROWS, t), BF16)
        units = [(h, c) for h in range(FOX_HEADS) for c in range(ATT_QBLOCKS) if modes[c]]

        def scores(h, c):
            hs = slice(h * hd, (h + 1) * hd)
            st = jnp.dot(k_ref[:, hs], qt_ref[hs, c * t:(c + 1) * t],
                         preferred_element_type=F32) - gc[:, h:h + 1]
            if modes[c] == 'diag':
                valid = (lax.broadcasted_iota(jnp.int32, (t, t), 0)
                         <= lax.broadcasted_iota(jnp.int32, (t, t), 1))
                st = jnp.where(valid, st, NEG_INF)
            return st

        def probs(h, c, st):
            qs = slice(c * t, (c + 1) * t)
            m_prev = m_ref[h:h + 1, qs]
            m_new = jnp.maximum(m_prev, jnp.max(st, axis=0, keepdims=True))
            m_ref[h:h + 1, qs] = m_new
            return jnp.exp2(m_prev - m_new), jnp.exp2(st - m_new).astype(BF16)

        def accumulate(h, c, alpha, pt):
            qs = slice(c * t, (c + 1) * t)
            vt_aug = jnp.concatenate([vt_ref[h * hd:(h + 1) * hd, :], ones], axis=0)
            acc_ref[h, :, qs] = alpha * acc_ref[h, :, qs] + jnp.dot(vt_aug, pt, preferred_element_type=F32)

        st, ap = {}, {}
        for u in range(-ATT_LOOKAHEAD, len(units) + 1):
            if 0 <= u + ATT_LOOKAHEAD < len(units):
                st[u + ATT_LOOKAHEAD] = scores(*units[u + ATT_LOOKAHEAD])
            if 0 <= u < len(units):
                ap[u] = probs(*units[u], st.pop(u))
            if 0 <= u - 1 < len(units):
                accumulate(*units[u - 1], *ap.pop(u - 1))

    first = i * ATT_QBLOCKS

    @pl.when(j < first)
    def _():
        step(('full',) * ATT_QBLOCKS)

    for r in range(ATT_QBLOCKS):
        @pl.when(j == first + r)
        def _(r=r):
            step(tuple(None if c < r else 'diag' if c == r else 'full' for c in range(ATT_QBLOCKS)))
            if r == ATT_QBLOCKS - 1:
                for h in range(FOX_HEADS):
                    hs = slice(h * hd, (h + 1) * hd)
                    for c in range(ATT_QBLOCKS):
                        qs = slice(c * t, (c + 1) * t)
                        acc = acc_ref[h, :, qs]
                        o = acc[:hd] * (1.0 / acc[hd:hd + 1])
                        o = o * lax.rsqrt(jnp.mean(o * o, axis=0, keepdims=True) + NORM_EPS) * nwb_ref[hs, :]
                        z = zt_ref[hs, qs]
                        o_ref[hs, qs] = (o * (z * _sigmoid(z))).astype(BF16)


def _fox(kr, t16, t32, mcol, nwb):
    b, s, _ = kr.shape
    t = ATT_T
    tq = ATT_T * ATT_QBLOCKS
    pairs = [(i, j) for i in range(s // tq) for j in range((i + 1) * ATT_QBLOCKS)]
    it = jnp.asarray([p[0] for p in pairs], jnp.int32)
    jt = jnp.asarray([p[1] for p in pairs], jnp.int32)

    grid_spec = pltpu.PrefetchScalarGridSpec(
        num_scalar_prefetch=2,
        grid=(b, len(pairs)),
        in_specs=[
            pl.BlockSpec((None, FOX_WIDTH, tq), lambda bi, p, it, jt: (bi, 0, it[p])),
            pl.BlockSpec((None, t, FOX_WIDTH), lambda bi, p, it, jt: (bi, jt[p], 0)),
            pl.BlockSpec((None, FOX_WIDTH, t), lambda bi, p, it, jt: (bi, 1, jt[p])),
            pl.BlockSpec((None, t, LANES), lambda bi, p, it, jt: (bi, jt[p], 0)),
            pl.BlockSpec((None, FOX_WIDTH, tq), lambda bi, p, it, jt: (bi, 0, it[p])),
            pl.BlockSpec((FOX_WIDTH, t), lambda bi, p, it, jt: (0, 0)),
        ],
        out_specs=pl.BlockSpec((None, FOX_WIDTH, tq), lambda bi, p, it, jt: (bi, 0, it[p])),
        scratch_shapes=[pltpu.VMEM((FOX_HEADS, tq), F32),
                        pltpu.VMEM((FOX_HEADS, FOX_HEAD_DIM + ONES_ROWS, tq), F32)],
    )
    return pl.pallas_call(
        _fox_kernel,
        grid_spec=grid_spec,
        out_shape=jax.ShapeDtypeStruct((b, FOX_WIDTH, s), BF16),
        compiler_params=_params(("arbitrary", "arbitrary")),
        name="fox_attention",
    )(it, jt, t16, kr, t16, mcol, t32, nwb)


def _mlstm_conv_tile(c, t, n_chunks, qk_ref, cw_ref, cb_ref, hist_ref):
    n = MLSTM_L
    ls = slice(t * LANES, (t + 1) * LANES)
    u = qk_ref[c * n:(c + 1) * n, ls]
    before = hist_ref[:, ls] if c == 0 else qk_ref[c * n - F32_SUBLANES:c * n, ls]
    ext = jnp.concatenate([before, u], axis=0)
    if c == n_chunks - 1:
        hist_ref[:, ls] = u[n - F32_SUBLANES:n, :]

    def tap(jj):
        back = CONV_WIDTH - 1 - jj
        return pltpu.roll(ext, back, axis=0)[F32_SUBLANES:, :] if back else u

    y = tap(0) * cw_ref[0:1, ls]
    for jj in range(1, CONV_WIDTH):
        y = y + tap(jj) * cw_ref[jj:jj + 1, ls]
    y = y + cb_ref[:, ls]
    return y * _sigmoid(y)


def _mlstm_chunk(c, sy, vt_ref, ogt_ref, zt_ref, mcol_ref, mrow_ref, nwb_ref, write_out, st_ref,
                 after_head=lambda h: None):
    n = MLSTM_L
    dv = MLSTM_V_DIM
    ts = slice(c * n, (c + 1) * n)

    mc = mcol_ref[ts, :]
    mr = mrow_ref[:, ts]
    lane = lax.broadcasted_iota(jnp.int32, (n, LANES), 1)
    ones_rows = jnp.where(lax.broadcasted_iota(jnp.int32, (ONES_ROWS, n), 0) == 0, 1.0, 0.0).astype(BF16)
    valid = lax.broadcasted_iota(jnp.int32, (n, n), 0) <= lax.broadcasted_iota(jnp.int32, (n, n), 1)
    heads = range(MLSTM_HEADS)
    row = lambda base_row, h: mr[base_row + h:base_row + h + 1, :]

    qp, kh, st_t, inter_t, vt_aug = [], [], [], [], []
    for h in heads:
        pr = h // 2
        qp.append(sy[pr].astype(BF16))
        kp = sy[MLSTM_QK_WIDTH // LANES + pr] * (MLSTM_QK_DIM ** -0.5)
        kh.append(jnp.where((lane < MLSTM_QK_DIM) if h % 2 == 0 else (lane >= MLSTM_QK_DIM), kp, 0.0))
        st_t.append(lax.dot_general(kh[h].astype(BF16), qp[h], NT_DIMS, preferred_element_type=F32))
        inter_t.append(lax.dot_general(st_ref[h].astype(BF16), qp[h], NT_DIMS, preferred_element_type=F32))
        vt_aug.append(jnp.concatenate([vt_ref[h * dv:(h + 1) * dv, ts], ones_rows], axis=0))

    for h in heads:
        hs = slice(h * dv, (h + 1) * dv)
        a_col = mc[:, LANE_A + h:LANE_A + h + 1]
        sc_t = st_t[h] * jnp.exp(jnp.where(valid, a_col + row(ROW_G, h), NEG_INF))
        pv = jnp.dot(vt_aug[h], sc_t.astype(BF16), preferred_element_type=F32)
        after_head(h)
        res = pv + jnp.exp(row(ROW_U, h)) * inter_t[h]
        den = res[dv:dv + 1, :]
        ht = res[:dv] * (1.0 / jnp.maximum(jnp.abs(den), jnp.exp(row(ROW_NEGM, h))))
        hb = ht * _sigmoid(ogt_ref[hs, ts])
        hn = hb * lax.rsqrt(jnp.mean(hb * hb, axis=0, keepdims=True) + NORM_EPS) * nwb_ref[hs, :]
        z = zt_ref[hs, ts]
        write_out(hs, ts, (hn * (z * _sigmoid(z))).astype(BF16))

    for h in heads:
        a_col = mc[:, LANE_A + h:LANE_A + h + 1]
        g_last = mc[n - 1:n, LANE_G + h:LANE_G + h + 1]
        u_last = mc[n - 1:n, LANE_U + h:LANE_U + h + 1]
        kw = (kh[h] * jnp.exp(a_col + g_last)).astype(BF16)
        upd = jnp.dot(vt_aug[h], kw, preferred_element_type=F32)
        st_ref[h] = jnp.exp(u_last) * st_ref[h] + upd


def _mlstm_out_kernel(qk_ref, vt_ref, ogt_ref, zt_ref, mcol_ref, mrow_ref, cw_ref, cb_ref, nwb_ref,
                      yat_ref, wa_ref, wb_ref, x_ref, fw_ref, o_ref, hist_ref, st_ref, ybt_ref,
                      *, per_seq, final):
    step = pl.program_id(0)
    cur = step % 2
    n = MLSTM_L

    @pl.when(step % per_seq == 0)
    def _():
        hist_ref[...] = jnp.zeros_like(hist_ref)
        st_ref[...] = jnp.zeros_like(st_ref)

    @pl.when(step == 0)
    def _():
        ybt_ref[1] = jnp.zeros_like(ybt_ref[1])

    def write_yb(hs, ts, val):
        ybt_ref[cur, hs, ts] = val

    prev = ybt_ref.at[1 - cur]
    n_chunks = FUSE_TM // n
    q_tiles = MLSTM_QK_WIDTH // LANES
    conv_tile = lambda c, t: _mlstm_conv_tile(c, t, n_chunks, qk_ref, cw_ref, cb_ref, hist_ref)
    cols = D_MODEL // MLSTM_HEADS
    for c in range(n_chunks):
        rs = slice(c * n, (c + 1) * n)
        sy = [conv_tile(c, t) for t in range(2 * q_tiles)]
        proj = []
        ya = yat_ref[:, rs].T
        yb = prev[:, rs].T

        def woven(h, proj=proj, ya=ya, yb=yb):
            cs = slice(h * cols, (h + 1) * cols)
            proj.append(jnp.dot(ya, wa_ref[:, cs], preferred_element_type=F32)
                        + jnp.dot(yb, wb_ref[:, cs], preferred_element_type=F32))

        _mlstm_chunk(c, sy, vt_ref, ogt_ref, zt_ref, mcol_ref, mrow_ref, nwb_ref, write_yb, st_ref,
                     after_head=woven)
        acc = jnp.concatenate(proj, axis=1)
        r = x_ref[rs, :] + acc
        if final:
            r = r * lax.rsqrt(jnp.mean(r * r, axis=-1, keepdims=True) + NORM_EPS) * fw_ref[...]
        o_ref[rs, :] = r


def _mlstm_out(qkr, t16, t32, mcol, mrow, cw, cb, nwb, yat, wo, x2, fw, final):
    assert FOX_WIDTH == MLSTM_V_WIDTH and wo.shape == (FOX_WIDTH + MLSTM_V_WIDTH, D_MODEL)
    m = x2.shape[0]
    _, _, seq = yat.shape
    tm = FUSE_TM
    per_seq = seq // tm
    blocks = m // tm
    cur = lambda s: jnp.minimum(s, blocks - 1)
    prv = lambda s: jnp.maximum(s - 1, 0)
    t_cur = lambda which: (lambda s: (cur(s) // per_seq, which, cur(s) % per_seq))
    return pl.pallas_call(
        functools.partial(_mlstm_out_kernel, per_seq=per_seq, final=final),
        grid=(blocks + 1,),
        in_specs=[
            pl.BlockSpec((tm, SEG), lambda s: (cur(s), 0)),
            pl.BlockSpec((None, SEG, tm), t_cur(2)),
            pl.BlockSpec((None, SEG, tm), t_cur(1)),
            pl.BlockSpec((None, SEG, tm), t_cur(2)),
            pl.BlockSpec((tm, LANES), lambda s: (cur(s), 0)),
            pl.BlockSpec((None, GATE_ROWS, tm), lambda s: (cur(s) // per_seq, 0, cur(s) % per_seq)),
            pl.BlockSpec((CONV_WIDTH, SEG), lambda s: (0, 0)),
            pl.BlockSpec((1, SEG), lambda s: (0, 0)),
            pl.BlockSpec((SEG, MLSTM_L), lambda s: (0, 0)),
            pl.BlockSpec((None, FOX_WIDTH, tm), lambda s: (prv(s) // per_seq, 0, prv(s) % per_seq)),
            pl.BlockSpec((FOX_WIDTH, D_MODEL), lambda s: (0, 0)),
            pl.BlockSpec((MLSTM_V_WIDTH, D_MODEL), lambda s: (1, 0)),
            pl.BlockSpec((tm, D_MODEL), lambda s: (prv(s), 0)),
            pl.BlockSpec((1, D_MODEL), lambda s: (0, 0)),
        ],
        out_specs=pl.BlockSpec((tm, D_MODEL), lambda s: (prv(s), 0)),
        out_shape=jax.ShapeDtypeStruct((m, D_MODEL), F32),
        scratch_shapes=[pltpu.VMEM((F32_SUBLANES, SEG), F32),
                        pltpu.VMEM((MLSTM_HEADS, MLSTM_V_DIM + ONES_ROWS, LANES), F32),
                        pltpu.VMEM((2, MLSTM_V_WIDTH, tm), BF16)],
        compiler_params=_params(("arbitrary",)),
        name="mlstm_outproj",
    )(qkr, t16, t32, t32, mcol, mrow, cw, cb, nwb, yat, wo, wo, x2, fw)


SEG_STARTS = (IN_OFFS[0], IN_OFFS[1], IN_OFFS[5], IN_OFFS[2], IN_OFFS[7], IN_OFFS[3], IN_OFFS[8], IN_OFFS[9])
SEG_SHIFT = FOX_HEADS
assert all(st % SEG in (0, SEG_SHIFT) for st in SEG_STARTS) and IN_OFFS[6] == IN_OFFS[5] + MLSTM_QK_WIDTH
assert SEG_SHIFT == F32_SUBLANES and MLSTM_HEADS == F32_SUBLANES

J_FOX_GATE = 2
assert SEG_STARTS[J_FOX_GATE] - SEG_SHIFT == IN_OFFS[4]
GW_FOX_F, GW_MLSTM_F, GW_MLSTM_I = 0, 8, LANES + 8


def _wprep_kernel(main_blk_ref, extra_blk_ref, main_ref, extra_ref, mi_ref, mf_ref, wout_ref,
                  o_ref, gw_ref, wo_ref, gwf_ref):
    del main_blk_ref, extra_blk_ref
    j = pl.program_id(0)
    in_seg = j < len(SEG_STARTS)
    shifted = functools.reduce(jnp.logical_or, [j == k for k, st in enumerate(SEG_STARTS) if st % SEG])

    @pl.when(jnp.logical_not(in_seg))
    def _():
        wo_ref[...] = wout_ref[...].astype(BF16)

    @pl.when(j == 0)
    def _():
        gwf_ref[...] = jnp.zeros_like(gwf_ref)
        gwf_ref[GW_MLSTM_I:GW_MLSTM_I + MLSTM_HEADS, :] = mi_ref[...]
        gwf_ref[GW_MLSTM_F:GW_MLSTM_F + MLSTM_HEADS, :] = mf_ref[...]

    @pl.when(in_seg & jnp.logical_not(shifted))
    def _():
        o_ref[...] = main_ref[...].astype(BF16)

    @pl.when(shifted)
    def _():
        full = jnp.concatenate([main_ref[...], extra_ref[...]], axis=0)
        o_ref[...] = full[SEG_SHIFT:SEG_SHIFT + SEG, :].astype(BF16)

    @pl.when(j == J_FOX_GATE)
    def _():
        gwf_ref[GW_FOX_F:GW_FOX_F + FOX_HEADS, :] = main_ref[0:SEG_SHIFT, :]

    @pl.when(j == len(SEG_STARTS) - 1)
    def _():
        gw_ref[...] = gwf_ref[...].astype(BF16)


def _wprep(w_t, w_out):
    d = w_t.shape[1]
    nseg = len(SEG_STARTS)
    out_steps = w_out.shape[0] // SEG
    pad = [SEG_STARTS[-1]] * out_steps
    main_blk = jnp.asarray([st // SEG for st in SEG_STARTS + tuple(pad)], jnp.int32)
    extra_blk = jnp.asarray([(st // SEG + 1) * (SEG // F32_SUBLANES) for st in SEG_STARTS + tuple(pad)],
                            jnp.int32)
    out_blk = lambda j: jnp.clip(j - nseg, 0, out_steps - 1)
    grid_spec = pltpu.PrefetchScalarGridSpec(
        num_scalar_prefetch=2,
        grid=(nseg + out_steps,),
        in_specs=[
            pl.BlockSpec((SEG, d), lambda j, mb, eb: (mb[j], 0)),
            pl.BlockSpec((F32_SUBLANES, d), lambda j, mb, eb: (eb[j], 0)),
            pl.BlockSpec((MLSTM_HEADS, d), lambda j, mb, eb: (IN_OFFS[10] // F32_SUBLANES, 0)),
            pl.BlockSpec((MLSTM_HEADS, d), lambda j, mb, eb: (IN_OFFS[11] // F32_SUBLANES, 0)),
            pl.BlockSpec((SEG, w_out.shape[1]), lambda j, mb, eb: (out_blk(j), 0)),
        ],
        out_specs=[
            pl.BlockSpec((SEG, d), lambda j, mb, eb: (jnp.minimum(j, nseg - 1), 0)),
            pl.BlockSpec((2 * LANES, d), lambda j, mb, eb: (0, 0)),
            pl.BlockSpec((SEG, w_out.shape[1]), lambda j, mb, eb: (out_blk(j), 0)),
        ],
        scratch_shapes=[pltpu.VMEM((2 * LANES, d), F32)],
    )
    return pl.pallas_call(
        _wprep_kernel,
        grid_spec=grid_spec,
        out_shape=[jax.ShapeDtypeStruct((nseg * SEG, d), BF16),
                   jax.ShapeDtypeStruct((2 * LANES, d), BF16),
                   jax.ShapeDtypeStruct(w_out.shape, BF16)],
        compiler_params=_params(("arbitrary",)),
        name="weight_layout",
    )(main_blk, extra_blk, w_t, w_t, w_t, w_t, w_out)


def _lane_broadcast(v, width):
    return jnp.broadcast_to(v[:, None], (v.shape[0], width))


def kernel(x, norm_w, w_in, fox_f_bias, conv_w, conv_b, mlstm_i_bias, mlstm_f_bias,
           fox_out_norm_w, mlstm_out_norm_w, w_out, final_norm_w):
    b, s, d = x.shape
    depth = norm_w.shape[0]
    x2 = x.reshape(b * s, d)
    for l in range(depth):
        w_t, w_gate, wo = _wprep(w_in[l].T, w_out[l])
        bias_vec = jnp.concatenate([
            fox_f_bias[l], mlstm_f_bias[l], jnp.zeros((LANES - 16,), F32),
            jnp.zeros((8,), F32), mlstm_i_bias[l], jnp.zeros((LANES - 16,), F32)]).reshape(1, 2 * LANES)

        kr, qkr, t16, t32, g = _inproj(x2, norm_w[l].reshape(1, d), w_t, w_gate, s)
        mcol, mrow = _gates(g.reshape(b, s, 2 * LANES), bias_vec)

        yat = _fox(kr.reshape(b, s, SEG), t16, t32, mcol, _lane_broadcast(fox_out_norm_w[l], ATT_T))
        x2 = _mlstm_out(qkr, t16, t32, mcol.reshape(b * s, LANES), mrow, conv_w[l], conv_b[l].reshape(1, SEG),
                        _lane_broadcast(mlstm_out_norm_w[l], MLSTM_L),
                        yat, wo, x2, final_norm_w.reshape(1, d),
                        final=(l == depth - 1))
    return x2.reshape(b, s, d)
```

```python
import functools

import numpy as np
import jax
import jax.numpy as jnp
from jax import lax
from jax.experimental import pallas as pl
from jax.experimental.pallas import tpu as pltpu

D_MODEL = 2048
NORM_EPS = 1e-6
NEG_INF = -1e30

FOX_HEADS = 8
FOX_HEAD_DIM = 128
FOX_WIDTH = FOX_HEADS * FOX_HEAD_DIM
MLSTM_HEADS = 8
MLSTM_V_DIM = 128
MLSTM_QK_DIM = 64
MLSTM_V_WIDTH = MLSTM_HEADS * MLSTM_V_DIM
MLSTM_QK_WIDTH = MLSTM_HEADS * MLSTM_QK_DIM
CONV_WIDTH = 4
IN_SIZES = (FOX_WIDTH, FOX_WIDTH, FOX_WIDTH, FOX_WIDTH, FOX_HEADS,
            MLSTM_QK_WIDTH, MLSTM_QK_WIDTH, MLSTM_V_WIDTH, MLSTM_V_WIDTH, MLSTM_V_WIDTH,
            MLSTM_HEADS, MLSTM_HEADS)
IN_OFFS = tuple(int(v) for v in np.cumsum((0,) + IN_SIZES))

LANES = 128
F32_SUBLANES = 8
SEG = 1024
VMEM_LIMIT = 56 * 1024 * 1024

PROJ_TM = 256
GATE_R = 2048
MLSTM_L = 256
ATT_T = 512
FUSE_TM = 512

LOG2E = 1.4426950408889634
FOX_Q_SCALE = FOX_HEAD_DIM ** -0.5 * LOG2E
ONES_ROWS = 16
ATT_QBLOCKS = 2
ATT_LOOKAHEAD = 1

GW_FOX_F, GW_MLSTM_F, GW_MLSTM_I = 0, 8, 16
GATE_ROWS = 32
ROW_G, ROW_U, ROW_A, ROW_NEGM = 0, 8, 16, 24
LANE_FOX = 0
LANE_G, LANE_U, LANE_A, LANE_NEGM = (8 + r for r in (ROW_G, ROW_U, ROW_A, ROW_NEGM))

BF16 = jnp.bfloat16
F32 = jnp.float32
NT_DIMS = (((1,), (1,)), ((), ()))


def _sigmoid(x):
    return 1.0 / (1.0 + jnp.exp(-x))


def _params(sem):
    return pltpu.CompilerParams(dimension_semantics=sem, vmem_limit_bytes=VMEM_LIMIT)


def _inproj_kernel(x_ref, nw_ref, wt_ref, wg_ref, kr_ref, qkr_ref, t16_ref, t32_ref, g_ref):
    xf = x_ref[...]
    ms = jnp.mean(xf * xf, axis=-1, keepdims=True)
    hb = (xf * lax.rsqrt(ms + NORM_EPS) * nw_ref[...]).astype(BF16)

    def seg_t(j):
        return lax.dot_general(wt_ref[j * SEG:(j + 1) * SEG, :], hb, NT_DIMS, preferred_element_type=F32)

    t16_ref[0:SEG, :] = (seg_t(0) * FOX_Q_SCALE).astype(BF16)
    kr_ref[...] = seg_t(1).T.astype(BF16)
    qkr_ref[...] = seg_t(2).T
    t16_ref[SEG:2 * SEG, :] = seg_t(3).astype(BF16)
    t16_ref[2 * SEG:3 * SEG, :] = seg_t(4).astype(BF16)
    for r in range(2):
        t32_ref[r * SEG:(r + 1) * SEG, :] = seg_t(5 + r)
    last = jnp.concatenate([wt_ref[7 * SEG:8 * SEG, :], wg_ref[...]], axis=0)
    acc = lax.dot_general(last, hb, NT_DIMS, preferred_element_type=F32)
    t32_ref[2 * SEG:3 * SEG, :] = acc[:SEG]
    g_ref[...] = acc[SEG:]


def _inproj(x2, nw, w_t, w_gate, seq):
    m = x2.shape[0]
    tm = PROJ_TM
    per_seq = seq // tm
    t_map = lambda i: (i // per_seq, 0, i % per_seq)
    return pl.pallas_call(
        _inproj_kernel,
        grid=(m // tm,),
        in_specs=[
            pl.BlockSpec((tm, D_MODEL), lambda i: (i, 0)),
            pl.BlockSpec((1, D_MODEL), lambda i: (0, 0)),
            pl.BlockSpec(w_t.shape, lambda i: (0, 0)),
            pl.BlockSpec((GATE_ROWS, D_MODEL), lambda i: (0, 0)),
        ],
        out_specs=[
            pl.BlockSpec((tm, SEG), lambda i: (i, 0)),
            pl.BlockSpec((tm, SEG), lambda i: (i, 0)),
            pl.BlockSpec((None, 3 * SEG, tm), t_map),
            pl.BlockSpec((None, 3 * SEG, tm), t_map),
            pl.BlockSpec((None, GATE_ROWS, tm), t_map),
        ],
        out_shape=[
            jax.ShapeDtypeStruct((m, SEG), BF16),
            jax.ShapeDtypeStruct((m, SEG), F32),
            jax.ShapeDtypeStruct((m // seq, 3 * SEG, seq), BF16),
            jax.ShapeDtypeStruct((m // seq, 3 * SEG, seq), F32),
            jax.ShapeDtypeStruct((m // seq, GATE_ROWS, seq), F32),
        ],
        compiler_params=_params(("arbitrary",)),
        name="inproj",
    )(x2, nw, w_t, w_gate)


def _split3(v):
    hi = v.astype(BF16)
    r1 = v - hi.astype(F32)
    mid = r1.astype(BF16)
    lo = (r1 - mid.astype(F32)).astype(BF16)
    return hi, mid, lo


def _gates_kernel(g_ref, b_ref, mcol_ref, mrow_ref, carry_ref, mprev_ref):
    width, chunk = GATE_R, MLSTM_L
    nh = FOX_HEADS
    lanes_of = lambda tile, n: jnp.concatenate([tile] * (n // LANES), axis=1)

    @pl.when(pl.program_id(1) == 0)
    def _():
        carry_ref[...] = jnp.zeros_like(carry_ref)
        mprev_ref[...] = jnp.zeros_like(mprev_ref)

    v = g_ref[...] + lanes_of(b_ref[...], width)
    vf = v[GW_FOX_F:GW_MLSTM_F + nh]
    ls = jnp.minimum(vf, 0.0) - jnp.log1p(jnp.exp(-jnp.abs(vf)))

    upper = (lax.broadcasted_iota(jnp.int32, (chunk, chunk), 0)
             <= lax.broadcasted_iota(jnp.int32, (chunk, chunk), 1)).astype(BF16)
    hi, mid, lo = _split3(ls)
    cuml = jnp.concatenate([
        jnp.dot(hi[:, c * chunk:(c + 1) * chunk], upper, preferred_element_type=F32)
        + jnp.dot(mid[:, c * chunk:(c + 1) * chunk], upper, preferred_element_type=F32)
        + jnp.dot(lo[:, c * chunk:(c + 1) * chunk], upper, preferred_element_type=F32)
        for c in range(width // chunk)], axis=1)
    b_all = cuml[GW_MLSTM_F:GW_MLSTM_F + nh]
    a = v[GW_MLSTM_I:GW_MLSTM_I + nh] - b_all

    t_in = lax.broadcasted_iota(jnp.int32, (nh, width), 1) % chunk
    cm = a
    s = 1
    while s < chunk:
        cm = jnp.where(t_in >= s, jnp.maximum(cm, pltpu.roll(cm, s, axis=1)), cm)
        s *= 2

    last = lambda x: jnp.broadcast_to(x[:, chunk - 1:chunk], (nh, LANES))
    carry = carry_ref[...]
    mprev = mprev_ref[...]
    for c in range(width // chunk):
        sl = slice(c * chunk, (c + 1) * chunk)
        cumf = cuml[GW_FOX_F:GW_FOX_F + nh, sl] + lanes_of(carry, chunk)
        carry = last(cumf)
        b = b_all[:, sl]
        mp = lanes_of(mprev, chunk)
        mt = b + jnp.maximum(mp, cm[:, sl])
        rows = {ROW_G: b - mt, ROW_U: (b + mp) - mt, ROW_A: a[:, sl], ROW_NEGM: -mt}
        mprev = last(mt)
        for r0, val in rows.items():
            mrow_ref[r0:r0 + nh, sl] = val
        tile_t = jnp.concatenate([cumf * LOG2E] + [rows[r0] for r0 in (ROW_G, ROW_U, ROW_A, ROW_NEGM)]
                                 + [jnp.zeros((LANES - 5 * nh, chunk), F32)], axis=0)
        mcol_ref[sl, :] = tile_t.T
    carry_ref[...] = carry
    mprev_ref[...] = mprev


def _gates(gt, bias_rows):
    b, _, s = gt.shape
    r = GATE_R
    return pl.pallas_call(
        _gates_kernel,
        grid=(b, s // r),
        in_specs=[
            pl.BlockSpec((None, GATE_ROWS, r), lambda bi, ri: (bi, 0, ri)),
            pl.BlockSpec((GATE_ROWS, LANES), lambda bi, ri: (0, 0)),
        ],
        out_specs=[
            pl.BlockSpec((None, r, LANES), lambda bi, ri: (bi, ri, 0)),
            pl.BlockSpec((None, GATE_ROWS, r), lambda bi, ri: (bi, 0, ri)),
        ],
        out_shape=[
            jax.ShapeDtypeStruct((b, s, LANES), F32),
            jax.ShapeDtypeStruct((b, GATE_ROWS, s), F32),
        ],
        scratch_shapes=[pltpu.VMEM((FOX_HEADS, LANES), F32), pltpu.VMEM((MLSTM_HEADS, LANES), F32)],
        compiler_params=_params(("arbitrary", "arbitrary")),
        name="gates",
    )(gt, bias_rows)


def _fox_kernel(it_ref, jt_ref, qt_ref, k_ref, vt_ref, gc_ref, zt_ref, nwb_ref, o_ref, m_ref, acc_ref):
    step_id = pl.program_id(1)
    i = it_ref[step_id]
    j = jt_ref[step_id]
    t = ATT_T
    hd = FOX_HEAD_DIM

    @pl.when(j == 0)
    def _():
        m_ref[...] = jnp.full_like(m_ref, NEG_INF)
        acc_ref[...] = jnp.zeros_like(acc_ref)

    def step(modes):
        gc = gc_ref[...]
        ones = jnp.ones((ONES_ROWS, t), BF16)
        units = [(h, c) for h in range(FOX_HEADS) for c in range(ATT_QBLOCKS) if modes[c]]

        def scores(h, c):
            hs = slice(h * hd, (h + 1) * hd)
            st = jnp.dot(k_ref[:, hs], qt_ref[hs, c * t:(c + 1) * t],
                         preferred_element_type=F32) - gc[:, h:h + 1]
            if modes[c] == 'diag':
                valid = (lax.broadcasted_iota(jnp.int32, (t, t), 0)
                         <= lax.broadcasted_iota(jnp.int32, (t, t), 1))
                st = jnp.where(valid, st, NEG_INF)
            return st

        def probs(h, c, st):
            qs = slice(c * t, (c + 1) * t)
            m_prev = m_ref[h:h + 1, qs]
            m_new = jnp.maximum(m_prev, jnp.max(st, axis=0, keepdims=True))
            m_ref[h:h + 1, qs] = m_new
            return jnp.exp2(m_prev - m_new), jnp.exp2(st - m_new).astype(BF16)

        def accumulate(h, c, alpha, pt):
            qs = slice(c * t, (c + 1) * t)
            vt_aug = jnp.concatenate([vt_ref[h * hd:(h + 1) * hd, :], ones], axis=0)
            acc_ref[h, :, qs] = alpha * acc_ref[h, :, qs] + jnp.dot(vt_aug, pt, preferred_element_type=F32)

        st, ap = {}, {}
        for u in range(-ATT_LOOKAHEAD, len(units) + 1):
            if 0 <= u + ATT_LOOKAHEAD < len(units):
                st[u + ATT_LOOKAHEAD] = scores(*units[u + ATT_LOOKAHEAD])
            if 0 <= u < len(units):
                ap[u] = probs(*units[u], st.pop(u))
            if 0 <= u - 1 < len(units):
                accumulate(*units[u - 1], *ap.pop(u - 1))

    first = i * ATT_QBLOCKS

    @pl.when(j < first)
    def _():
        step(('full',) * ATT_QBLOCKS)

    for r in range(ATT_QBLOCKS):
        @pl.when(j == first + r)
        def _(r=r):
            step(tuple(None if c < r else 'diag' if c == r else 'full' for c in range(ATT_QBLOCKS)))
            if r == ATT_QBLOCKS - 1:
                for h in range(FOX_HEADS):
                    hs = slice(h * hd, (h + 1) * hd)
                    for c in range(ATT_QBLOCKS):
                        qs = slice(c * t, (c + 1) * t)
                        acc = acc_ref[h, :, qs]
                        o = acc[:hd] * (1.0 / acc[hd:hd + 1])
                        o = o * lax.rsqrt(jnp.mean(o * o, axis=0, keepdims=True) + NORM_EPS) * nwb_ref[hs, :]
                        z = zt_ref[hs, qs]
                        o_ref[hs, qs] = (o * (z * _sigmoid(z))).astype(BF16)


def _fox(kr, t16, t32, mcol, nwb):
    b, s, _ = kr.shape
    t = ATT_T
    tq = ATT_T * ATT_QBLOCKS
    pairs = [(i, j) for i in range(s // tq) for j in range((i + 1) * ATT_QBLOCKS)]
    it = jnp.asarray([p[0] for p in pairs], jnp.int32)
    jt = jnp.asarray([p[1] for p in pairs], jnp.int32)

    grid_spec = pltpu.PrefetchScalarGridSpec(
        num_scalar_prefetch=2,
        grid=(b, len(pairs)),
        in_specs=[
            pl.BlockSpec((None, FOX_WIDTH, tq), lambda bi, p, it, jt: (bi, 0, it[p])),
            pl.BlockSpec((None, t, FOX_WIDTH), lambda bi, p, it, jt: (bi, jt[p], 0)),
            pl.BlockSpec((None, FOX_WIDTH, t), lambda bi, p, it, jt: (bi, 1, jt[p])),
            pl.BlockSpec((None, t, LANES), lambda bi, p, it, jt: (bi, jt[p], 0)),
            pl.BlockSpec((None, FOX_WIDTH, tq), lambda bi, p, it, jt: (bi, 0, it[p])),
            pl.BlockSpec((FOX_WIDTH, t), lambda bi, p, it, jt: (0, 0)),
        ],
        out_specs=pl.BlockSpec((None, FOX_WIDTH, tq), lambda bi, p, it, jt: (bi, 0, it[p])),
        scratch_shapes=[pltpu.VMEM((FOX_HEADS, tq), F32),
                        pltpu.VMEM((FOX_HEADS, FOX_HEAD_DIM + ONES_ROWS, tq), F32)],
    )
    return pl.pallas_call(
        _fox_kernel,
        grid_spec=grid_spec,
        out_shape=jax.ShapeDtypeStruct((b, FOX_WIDTH, s), BF16),
        compiler_params=_params(("arbitrary", "arbitrary")),
        name="fox_attention",
    )(it, jt, t16, kr, t16, mcol, t32, nwb)


def _mlstm_conv_tile(c, t, n_chunks, qk_ref, cw_ref, cb_ref, hist_ref):
    n = MLSTM_L
    ls = slice(t * LANES, (t + 1) * LANES)
    u = qk_ref[c * n:(c + 1) * n, ls]
    before = hist_ref[:, ls] if c == 0 else qk_ref[c * n - F32_SUBLANES:c * n, ls]
    ext = jnp.concatenate([before, u], axis=0)
    if c == n_chunks - 1:
        hist_ref[:, ls] = u[n - F32_SUBLANES:n, :]

    def tap(jj):
        back = CONV_WIDTH - 1 - jj
        return pltpu.roll(ext, back, axis=0)[F32_SUBLANES:, :] if back else u

    y = tap(0) * cw_ref[0:1, ls]
    for jj in range(1, CONV_WIDTH):
        y = y + tap(jj) * cw_ref[jj:jj + 1, ls]
    y = y + cb_ref[:, ls]
    return y * _sigmoid(y)


def _mlstm_chunk(c, sy, vt_ref, ogt_ref, zt_ref, mcol_ref, mrow_ref, nwb_ref, write_out, st_ref,
                 after_head=lambda h: None):
    n = MLSTM_L
    dv = MLSTM_V_DIM
    ts = slice(c * n, (c + 1) * n)

    mc = mcol_ref[ts, :]
    mr = mrow_ref[:, ts]
    lane = lax.broadcasted_iota(jnp.int32, (n, LANES), 1)
    ones_rows = jnp.where(lax.broadcasted_iota(jnp.int32, (ONES_ROWS, n), 0) == 0, 1.0, 0.0).astype(BF16)
    valid = lax.broadcasted_iota(jnp.int32, (n, n), 0) <= lax.broadcasted_iota(jnp.int32, (n, n), 1)
    heads = range(MLSTM_HEADS)
    row = lambda base_row, h: mr[base_row + h:base_row + h + 1, :]

    qp, kh, st_t, inter_t, vt_aug = [], [], [], [], []
    for h in heads:
        pr = h // 2
        qp.append(sy[pr].astype(BF16))
        kp = sy[MLSTM_QK_WIDTH // LANES + pr] * (MLSTM_QK_DIM ** -0.5)
        kh.append(jnp.where((lane < MLSTM_QK_DIM) if h % 2 == 0 else (lane >= MLSTM_QK_DIM), kp, 0.0))
        st_t.append(lax.dot_general(kh[h].astype(BF16), qp[h], NT_DIMS, preferred_element_type=F32))
        inter_t.append(lax.dot_general(st_ref[h].astype(BF16), qp[h], NT_DIMS, preferred_element_type=F32))
        vt_aug.append(jnp.concatenate([vt_ref[h * dv:(h + 1) * dv, ts], ones_rows], axis=0))

    for h in heads:
        hs = slice(h * dv, (h + 1) * dv)
        a_col = mc[:, LANE_A + h:LANE_A + h + 1]
        sc_t = st_t[h] * jnp.exp(jnp.where(valid, a_col + row(ROW_G, h), NEG_INF))
        pv = jnp.dot(vt_aug[h], sc_t.astype(BF16), preferred_element_type=F32)
        after_head(h)
        res = pv + jnp.exp(row(ROW_U, h)) * inter_t[h]
        den = res[dv:dv + 1, :]
        ht = res[:dv] * (1.0 / jnp.maximum(jnp.abs(den), jnp.exp(row(ROW_NEGM, h))))
        hb = ht * _sigmoid(ogt_ref[hs, ts])
        hn = hb * lax.rsqrt(jnp.mean(hb * hb, axis=0, keepdims=True) + NORM_EPS) * nwb_ref[hs, :]
        z = zt_ref[hs, ts]
        write_out(hs, ts, (hn * (z * _sigmoid(z))).astype(BF16))

    for h in heads:
        a_col = mc[:, LANE_A + h:LANE_A + h + 1]
        g_last = mc[n - 1:n, LANE_G + h:LANE_G + h + 1]
        u_last = mc[n - 1:n, LANE_U + h:LANE_U + h + 1]
        kw = (kh[h] * jnp.exp(a_col + g_last)).astype(BF16)
        upd = jnp.dot(vt_aug[h], kw, preferred_element_type=F32)
        st_ref[h] = jnp.exp(u_last) * st_ref[h] + upd


def _mlstm_out_kernel(qk_ref, vt_ref, ogt_ref, zt_ref, mcol_ref, mrow_ref, cw_ref, cb_ref, nwb_ref,
                      yat_ref, wa_ref, wb_ref, x_ref, fw_ref, o_ref, hist_ref, st_ref, ybt_ref,
                      *, per_seq, final):
    step = pl.program_id(0)
    cur = step % 2
    n = MLSTM_L

    @pl.when(step % per_seq == 0)
    def _():
        hist_ref[...] = jnp.zeros_like(hist_ref)
        st_ref[...] = jnp.zeros_like(st_ref)

    @pl.when(step == 0)
    def _():
        ybt_ref[1] = jnp.zeros_like(ybt_ref[1])

    def write_yb(hs, ts, val):
        ybt_ref[cur, hs, ts] = val

    prev = ybt_ref.at[1 - cur]
    n_chunks = FUSE_TM // n
    q_tiles = MLSTM_QK_WIDTH // LANES
    conv_tile = lambda c, t: _mlstm_conv_tile(c, t, n_chunks, qk_ref, cw_ref, cb_ref, hist_ref)
    cols = D_MODEL // MLSTM_HEADS
    for c in range(n_chunks):
        rs = slice(c * n, (c + 1) * n)
        sy = [conv_tile(c, t) for t in range(2 * q_tiles)]
        proj = []
        ya = yat_ref[:, rs].T
        yb = prev[:, rs].T

        def woven(h, proj=proj, ya=ya, yb=yb):
            cs = slice(h * cols, (h + 1) * cols)
            proj.append(jnp.dot(ya, wa_ref[:, cs], preferred_element_type=F32)
                        + jnp.dot(yb, wb_ref[:, cs], preferred_element_type=F32))

        _mlstm_chunk(c, sy, vt_ref, ogt_ref, zt_ref, mcol_ref, mrow_ref, nwb_ref, write_yb, st_ref,
                     after_head=woven)
        acc = jnp.concatenate(proj, axis=1)
        r = x_ref[rs, :] + acc
        if final:
            r = r * lax.rsqrt(jnp.mean(r * r, axis=-1, keepdims=True) + NORM_EPS) * fw_ref[...]
        o_ref[rs, :] = r


def _mlstm_out(qkr, t16, t32, mcol, mrow, cw, cb, nwb, yat, wo, x2, fw, final):
    assert FOX_WIDTH == MLSTM_V_WIDTH and wo.shape == (FOX_WIDTH + MLSTM_V_WIDTH, D_MODEL)
    m = x2.shape[0]
    _, _, seq = yat.shape
    tm = FUSE_TM
    per_seq = seq // tm
    blocks = m // tm
    cur = lambda s: jnp.minimum(s, blocks - 1)
    prv = lambda s: jnp.maximum(s - 1, 0)
    t_cur = lambda which: (lambda s: (cur(s) // per_seq, which, cur(s) % per_seq))
    return pl.pallas_call(
        functools.partial(_mlstm_out_kernel, per_seq=per_seq, final=final),
        grid=(blocks + 1,),
        in_specs=[
            pl.BlockSpec((tm, SEG), lambda s: (cur(s), 0)),
            pl.BlockSpec((None, SEG, tm), t_cur(2)),
            pl.BlockSpec((None, SEG, tm), t_cur(1)),
            pl.BlockSpec((None, SEG, tm), t_cur(2)),
            pl.BlockSpec((tm, LANES), lambda s: (cur(s), 0)),
            pl.BlockSpec((None, GATE_ROWS, tm), lambda s: (cur(s) // per_seq, 0, cur(s) % per_seq)),
            pl.BlockSpec((CONV_WIDTH, SEG), lambda s: (0, 0)),
            pl.BlockSpec((1, SEG), lambda s: (0, 0)),
            pl.BlockSpec((SEG, MLSTM_L), lambda s: (0, 0)),
            pl.BlockSpec((None, FOX_WIDTH, tm), lambda s: (prv(s) // per_seq, 0, prv(s) % per_seq)),
            pl.BlockSpec((FOX_WIDTH, D_MODEL), lambda s: (0, 0)),
            pl.BlockSpec((MLSTM_V_WIDTH, D_MODEL), lambda s: (1, 0)),
            pl.BlockSpec((tm, D_MODEL), lambda s: (prv(s), 0)),
            pl.BlockSpec((1, D_MODEL), lambda s: (0, 0)),
        ],
        out_specs=pl.BlockSpec((tm, D_MODEL), lambda s: (prv(s), 0)),
        out_shape=jax.ShapeDtypeStruct((m, D_MODEL), F32),
        scratch_shapes=[pltpu.VMEM((F32_SUBLANES, SEG), F32),
                        pltpu.VMEM((MLSTM_HEADS, MLSTM_V_DIM + ONES_ROWS, LANES), F32),
                        pltpu.VMEM((2, MLSTM_V_WIDTH, tm), BF16)],
        compiler_params=_params(("arbitrary",)),
        name="mlstm_outproj",
    )(qkr, t16, t32, t32, mcol, mrow, cw, cb, nwb, yat, wo, wo, x2, fw)


SEG_STARTS = (IN_OFFS[0], IN_OFFS[1], IN_OFFS[5], IN_OFFS[2], IN_OFFS[7], IN_OFFS[3], IN_OFFS[8], IN_OFFS[9])
SEG_SHIFT = FOX_HEADS
assert all(st % SEG in (0, SEG_SHIFT) for st in SEG_STARTS) and IN_OFFS[6] == IN_OFFS[5] + MLSTM_QK_WIDTH
assert SEG_SHIFT == F32_SUBLANES and MLSTM_HEADS == F32_SUBLANES

J_FOX_GATE = 2
assert SEG_STARTS[J_FOX_GATE] - SEG_SHIFT == IN_OFFS[4]


def _wprep_kernel(main_blk_ref, extra_blk_ref, main_ref, extra_ref, mi_ref, mf_ref, wout_ref,
                  o_ref, gw_ref, wo_ref, gwf_ref):
    del main_blk_ref, extra_blk_ref
    j = pl.program_id(0)
    in_seg = j < len(SEG_STARTS)
    shifted = functools.reduce(jnp.logical_or, [j == k for k, st in enumerate(SEG_STARTS) if st % SEG])

    @pl.when(jnp.logical_not(in_seg))
    def _():
        wo_ref[...] = wout_ref[...].astype(BF16)

    @pl.when(j == 0)
    def _():
        gwf_ref[...] = jnp.zeros_like(gwf_ref)
        gwf_ref[GW_MLSTM_I:GW_MLSTM_I + MLSTM_HEADS, :] = mi_ref[...]
        gwf_ref[GW_MLSTM_F:GW_MLSTM_F + MLSTM_HEADS, :] = mf_ref[...]

    @pl.when(in_seg & jnp.logical_not(shifted))
    def _():
        o_ref[...] = main_ref[...].astype(BF16)

    @pl.when(shifted)
    def _():
        full = jnp.concatenate([main_ref[...], extra_ref[...]], axis=0)
        o_ref[...] = full[SEG_SHIFT:SEG_SHIFT + SEG, :].astype(BF16)

    @pl.when(j == J_FOX_GATE)
    def _():
        gwf_ref[GW_FOX_F:GW_FOX_F + FOX_HEADS, :] = main_ref[0:SEG_SHIFT, :]

    @pl.when(j == len(SEG_STARTS) - 1)
    def _():
        gw_ref[...] = gwf_ref[...].astype(BF16)


def _wprep(w_t, w_out):
    d = w_t.shape[1]
    nseg = len(SEG_STARTS)
    out_steps = w_out.shape[0] // SEG
    pad = [SEG_STARTS[-1]] * out_steps
    main_blk = jnp.asarray([st // SEG for st in SEG_STARTS + tuple(pad)], jnp.int32)
    extra_blk = jnp.asarray([(st // SEG + 1) * (SEG // F32_SUBLANES) for st in SEG_STARTS + tuple(pad)],
                            jnp.int32)
    out_blk = lambda j: jnp.clip(j - nseg, 0, out_steps - 1)
    grid_spec = pltpu.PrefetchScalarGridSpec(
        num_scalar_prefetch=2,
        grid=(nseg + out_steps,),
        in_specs=[
            pl.BlockSpec((SEG, d), lambda j, mb, eb: (mb[j], 0)),
            pl.BlockSpec((F32_SUBLANES, d), lambda j, mb, eb: (eb[j], 0)),
            pl.BlockSpec((MLSTM_HEADS, d), lambda j, mb, eb: (IN_OFFS[10] // F32_SUBLANES, 0)),
            pl.BlockSpec((MLSTM_HEADS, d), lambda j, mb, eb: (IN_OFFS[11] // F32_SUBLANES, 0)),
            pl.BlockSpec((SEG, w_out.shape[1]), lambda j, mb, eb: (out_blk(j), 0)),
        ],
        out_specs=[
            pl.BlockSpec((SEG, d), lambda j, mb, eb: (jnp.minimum(j, nseg - 1), 0)),
            pl.BlockSpec((GATE_ROWS, d), lambda j, mb, eb: (0, 0)),
            pl.BlockSpec((SEG, w_out.shape[1]), lambda j, mb, eb: (out_blk(j), 0)),
        ],
        scratch_shapes=[pltpu.VMEM((GATE_ROWS, d), F32)],
    )
    return pl.pallas_call(
        _wprep_kernel,
        grid_spec=grid_spec,
        out_shape=[jax.ShapeDtypeStruct((nseg * SEG, d), BF16),
                   jax.ShapeDtypeStruct((GATE_ROWS, d), BF16),
                   jax.ShapeDtypeStruct(w_out.shape, BF16)],
        compiler_params=_params(("arbitrary",)),
        name="weight_layout",
    )(main_blk, extra_blk, w_t, w_t, w_t, w_t, w_out)


def _lane_broadcast(v, width):
    return jnp.broadcast_to(v[:, None], (v.shape[0], width))


def kernel(x, norm_w, w_in, fox_f_bias, conv_w, conv_b, mlstm_i_bias, mlstm_f_bias,
           fox_out_norm_w, mlstm_out_norm_w, w_out, final_norm_w):
    b, s, d = x.shape
    depth = norm_w.shape[0]
    x2 = x.reshape(b * s, d)
    for l in range(depth):
        w_t, w_gate, wo = _wprep(w_in[l].T, w_out[l])
        bias_rows = _lane_broadcast(jnp.concatenate([
            fox_f_bias[l], mlstm_f_bias[l], mlstm_i_bias[l], jnp.zeros((GATE_ROWS - 24,), F32)]), LANES)

        kr, qkr, t16, t32, g = _inproj(x2, norm_w[l].reshape(1, d), w_t, w_gate, s)
        mcol, mrow = _gates(g, bias_rows)

        yat = _fox(kr.reshape(b, s, SEG), t16, t32, mcol, _lane_broadcast(fox_out_norm_w[l], ATT_T))
        x2 = _mlstm_out(qkr, t16, t32, mcol.reshape(b * s, LANES), mrow, conv_w[l], conv_b[l].reshape(1, SEG),
                        _lane_broadcast(mlstm_out_norm_w[l], MLSTM_L),
                        yat, wo, x2, final_norm_w.reshape(1, d),
                        final=(l == depth - 1))
    return x2.reshape(b, s, d)
```

```python
import functools

import numpy as np
import jax
import jax.numpy as jnp
from jax import lax
from jax.experimental import pallas as pl
from jax.experimental.pallas import tpu as pltpu

D_MODEL = 2048
NORM_EPS = 1e-6
NEG_INF = -1e30

FOX_HEADS = 8
FOX_HEAD_DIM = 128
FOX_WIDTH = FOX_HEADS * FOX_HEAD_DIM
MLSTM_HEADS = 8
MLSTM_V_DIM = 128
MLSTM_QK_DIM = 64
MLSTM_V_WIDTH = MLSTM_HEADS * MLSTM_V_DIM
MLSTM_QK_WIDTH = MLSTM_HEADS * MLSTM_QK_DIM
CONV_WIDTH = 4
IN_SIZES = (FOX_WIDTH, FOX_WIDTH, FOX_WIDTH, FOX_WIDTH, FOX_HEADS,
            MLSTM_QK_WIDTH, MLSTM_QK_WIDTH, MLSTM_V_WIDTH, MLSTM_V_WIDTH, MLSTM_V_WIDTH,
            MLSTM_HEADS, MLSTM_HEADS)
IN_OFFS = tuple(int(v) for v in np.cumsum((0,) + IN_SIZES))

LANES = 128
F32_SUBLANES = 8
SEG = 1024
VMEM_LIMIT = 56 * 1024 * 1024

PROJ_TM = 256
GATE_R = 2048
MLSTM_L = 256
ATT_T = 512
FUSE_TM = 512

LOG2E = 1.4426950408889634
FOX_Q_SCALE = FOX_HEAD_DIM ** -0.5 * LOG2E
ONES_ROWS = 16
ATT_QBLOCKS = 2
ATT_LOOKAHEAD = 1

GW_FOX_F, GW_MLSTM_F, GW_MLSTM_I = 0, 8, 16
GATE_ROWS = 32
ROW_G, ROW_U, ROW_A, ROW_NEGM = 0, 8, 16, 24
LANE_FOX = 0
LANE_G, LANE_U, LANE_A, LANE_NEGM = (8 + r for r in (ROW_G, ROW_U, ROW_A, ROW_NEGM))

BF16 = jnp.bfloat16
F32 = jnp.float32
NT_DIMS = (((1,), (1,)), ((), ()))


def _sigmoid(x):
    return 1.0 / (1.0 + jnp.exp(-x))


def _silu(x):
    h = 0.5 * x
    return h * (1.0 + jnp.tanh(h))


def _params(sem):
    return pltpu.CompilerParams(dimension_semantics=sem, vmem_limit_bytes=VMEM_LIMIT)


def _inproj_kernel(x_ref, nw_ref, wt_ref, wg_ref, kr_ref, qkr_ref, t16_ref, t32_ref, g_ref):
    xf = x_ref[...]
    ms = jnp.mean(xf * xf, axis=-1, keepdims=True)
    hb = (xf * lax.rsqrt(ms + NORM_EPS) * nw_ref[...]).astype(BF16)

    def seg_t(j):
        return lax.dot_general(wt_ref[j * SEG:(j + 1) * SEG, :], hb, NT_DIMS, preferred_element_type=F32)

    t16_ref[0:SEG, :] = (seg_t(0) * FOX_Q_SCALE).astype(BF16)
    kr_ref[...] = seg_t(1).T.astype(BF16)
    qkr_ref[...] = seg_t(2).T
    t16_ref[SEG:2 * SEG, :] = seg_t(3).astype(BF16)
    t16_ref[2 * SEG:3 * SEG, :] = seg_t(4).astype(BF16)
    for r in range(2):
        t32_ref[r * SEG:(r + 1) * SEG, :] = seg_t(5 + r)
    last = jnp.concatenate([wt_ref[7 * SEG:8 * SEG, :], wg_ref[...]], axis=0)
    acc = lax.dot_general(last, hb, NT_DIMS, preferred_element_type=F32)
    t32_ref[2 * SEG:3 * SEG, :] = acc[:SEG]
    g_ref[...] = acc[SEG:]


def _inproj(x2, nw, w_t, w_gate, seq):
    m = x2.shape[0]
    tm = PROJ_TM
    per_seq = seq // tm
    t_map = lambda i: (i // per_seq, 0, i % per_seq)
    return pl.pallas_call(
        _inproj_kernel,
        grid=(m // tm,),
        in_specs=[
            pl.BlockSpec((tm, D_MODEL), lambda i: (i, 0)),
            pl.BlockSpec((1, D_MODEL), lambda i: (0, 0)),
            pl.BlockSpec(w_t.shape, lambda i: (0, 0)),
            pl.BlockSpec((GATE_ROWS, D_MODEL), lambda i: (0, 0)),
        ],
        out_specs=[
            pl.BlockSpec((tm, SEG), lambda i: (i, 0)),
            pl.BlockSpec((tm, SEG), lambda i: (i, 0)),
            pl.BlockSpec((None, 3 * SEG, tm), t_map),
            pl.BlockSpec((None, 3 * SEG, tm), t_map),
            pl.BlockSpec((None, GATE_ROWS, tm), t_map),
        ],
        out_shape=[
            jax.ShapeDtypeStruct((m, SEG), BF16),
            jax.ShapeDtypeStruct((m, SEG), F32),
            jax.ShapeDtypeStruct((m // seq, 3 * SEG, seq), BF16),
            jax.ShapeDtypeStruct((m // seq, 3 * SEG, seq), F32),
            jax.ShapeDtypeStruct((m // seq, GATE_ROWS, seq), F32),
        ],
        compiler_params=_params(("arbitrary",)),
        name="inproj",
    )(x2, nw, w_t, w_gate)


def _split3(v):
    hi = v.astype(BF16)
    r1 = v - hi.astype(F32)
    mid = r1.astype(BF16)
    lo = (r1 - mid.astype(F32)).astype(BF16)
    return hi, mid, lo


def _gates_kernel(g_ref, b_ref, mcol_ref, mrow_ref, carry_ref, mprev_ref):
    width, chunk = GATE_R, MLSTM_L
    nh = FOX_HEADS
    lanes_of = lambda tile, n: jnp.concatenate([tile] * (n // LANES), axis=1)

    @pl.when(pl.program_id(1) == 0)
    def _():
        carry_ref[...] = jnp.zeros_like(carry_ref)
        mprev_ref[...] = jnp.zeros_like(mprev_ref)

    v = g_ref[...] + lanes_of(b_ref[...], width)
    vf = v[GW_FOX_F:GW_MLSTM_F + nh]
    ls = jnp.minimum(vf, 0.0) - jnp.log1p(jnp.exp(-jnp.abs(vf)))

    upper = (lax.broadcasted_iota(jnp.int32, (chunk, chunk), 0)
             <= lax.broadcasted_iota(jnp.int32, (chunk, chunk), 1)).astype(BF16)
    hi, mid, lo = _split3(ls)
    cuml = jnp.concatenate([
        jnp.dot(hi[:, c * chunk:(c + 1) * chunk], upper, preferred_element_type=F32)
        + jnp.dot(mid[:, c * chunk:(c + 1) * chunk], upper, preferred_element_type=F32)
        + jnp.dot(lo[:, c * chunk:(c + 1) * chunk], upper, preferred_element_type=F32)
        for c in range(width // chunk)], axis=1)
    b_all = cuml[GW_MLSTM_F:GW_MLSTM_F + nh]
    a = v[GW_MLSTM_I:GW_MLSTM_I + nh] - b_all

    t_in = lax.broadcasted_iota(jnp.int32, (nh, width), 1) % chunk
    cm = a
    s = 1
    while s < chunk:
        cm = jnp.where(t_in >= s, jnp.maximum(cm, pltpu.roll(cm, s, axis=1)), cm)
        s *= 2

    last = lambda x: jnp.broadcast_to(x[:, chunk - 1:chunk], (nh, LANES))
    carry = carry_ref[...]
    mprev = mprev_ref[...]
    for c in range(width // chunk):
        sl = slice(c * chunk, (c + 1) * chunk)
        cumf = cuml[GW_FOX_F:GW_FOX_F + nh, sl] + lanes_of(carry, chunk)
        carry = last(cumf)
        b = b_all[:, sl]
        mp = lanes_of(mprev, chunk)
        mt = b + jnp.maximum(mp, cm[:, sl])
        rows = {ROW_G: b - mt, ROW_U: (b + mp) - mt, ROW_A: a[:, sl], ROW_NEGM: -mt}
        mprev = last(mt)
        for r0, val in rows.items():
            mrow_ref[r0:r0 + nh, sl] = val
        tile_t = jnp.concatenate([cumf * LOG2E] + [rows[r0] for r0 in (ROW_G, ROW_U, ROW_A, ROW_NEGM)]
                                 + [jnp.zeros((LANES - 5 * nh, chunk), F32)], axis=0)
        mcol_ref[sl, :] = tile_t.T
    carry_ref[...] = carry
    mprev_ref[...] = mprev


def _gates(gt, bias_rows):
    b, _, s = gt.shape
    r = GATE_R
    return pl.pallas_call(
        _gates_kernel,
        grid=(b, s // r),
        in_specs=[
            pl.BlockSpec((None, GATE_ROWS, r), lambda bi, ri: (bi, 0, ri)),
            pl.BlockSpec((GATE_ROWS, LANES), lambda bi, ri: (0, 0)),
        ],
        out_specs=[
            pl.BlockSpec((None, r, LANES), lambda bi, ri: (bi, ri, 0)),
            pl.BlockSpec((None, GATE_ROWS, r), lambda bi, ri: (bi, 0, ri)),
        ],
        out_shape=[
            jax.ShapeDtypeStruct((b, s, LANES), F32),
            jax.ShapeDtypeStruct((b, GATE_ROWS, s), F32),
        ],
        scratch_shapes=[pltpu.VMEM((FOX_HEADS, LANES), F32), pltpu.VMEM((MLSTM_HEADS, LANES), F32)],
        compiler_params=_params(("arbitrary", "arbitrary")),
        name="gates",
    )(gt, bias_rows)


def _fox_kernel(it_ref, jt_ref, qt_ref, k_ref, vt_ref, gc_ref, zt_ref, nwb_ref, o_ref, m_ref, acc_ref):
    step_id = pl.program_id(1)
    i = it_ref[step_id]
    j = jt_ref[step_id]
    t = ATT_T
    hd = FOX_HEAD_DIM

    @pl.when(j == 0)
    def _():
        m_ref[...] = jnp.full_like(m_ref, NEG_INF)
        acc_ref[...] = jnp.zeros_like(acc_ref)

    def step(modes):
        gc = gc_ref[...]
        ones = jnp.ones((ONES_ROWS, t), BF16)
        units = [(h, c) for h in range(FOX_HEADS) for c in range(ATT_QBLOCKS) if modes[c]]

        def scores(h, c):
            hs = slice(h * hd, (h + 1) * hd)
            st = jnp.dot(k_ref[:, hs], qt_ref[hs, c * t:(c + 1) * t],
                         preferred_element_type=F32) - gc[:, h:h + 1]
            if modes[c] == 'diag':
                valid = (lax.broadcasted_iota(jnp.int32, (t, t), 0)
                         <= lax.broadcasted_iota(jnp.int32, (t, t), 1))
                st = jnp.where(valid, st, NEG_INF)
            return st

        def probs(h, c, st):
            qs = slice(c * t, (c + 1) * t)
            m_prev = m_ref[h:h + 1, qs]
            m_new = jnp.maximum(m_prev, jnp.max(st, axis=0, keepdims=True))
            m_ref[h:h + 1, qs] = m_new
            return jnp.exp2(m_prev - m_new), jnp.exp2(st - m_new).astype(BF16)

        def accumulate(h, c, alpha, pt):
            qs = slice(c * t, (c + 1) * t)
            vt_aug = jnp.concatenate([vt_ref[h * hd:(h + 1) * hd, :], ones], axis=0)
            acc_ref[h, :, qs] = alpha * acc_ref[h, :, qs] + jnp.dot(vt_aug, pt, preferred_element_type=F32)

        st, ap = {}, {}
        for u in range(-ATT_LOOKAHEAD, len(units) + 1):
            if 0 <= u + ATT_LOOKAHEAD < len(units):
                st[u + ATT_LOOKAHEAD] = scores(*units[u + ATT_LOOKAHEAD])
            if 0 <= u < len(units):
                ap[u] = probs(*units[u], st.pop(u))
            if 0 <= u - 1 < len(units):
                accumulate(*units[u - 1], *ap.pop(u - 1))

    first = i * ATT_QBLOCKS

    @pl.when(j < first)
    def _():
        step(('full',) * ATT_QBLOCKS)

    for r in range(ATT_QBLOCKS):
        @pl.when(j == first + r)
        def _(r=r):
            step(tuple(None if c < r else 'diag' if c == r else 'full' for c in range(ATT_QBLOCKS)))
            if r == ATT_QBLOCKS - 1:
                for h in range(FOX_HEADS):
                    hs = slice(h * hd, (h + 1) * hd)
                    for c in range(ATT_QBLOCKS):
                        qs = slice(c * t, (c + 1) * t)
                        acc = acc_ref[h, :, qs]
                        o = acc[:hd] * (1.0 / acc[hd:hd + 1])
                        o = o * lax.rsqrt(jnp.mean(o * o, axis=0, keepdims=True) + NORM_EPS) * nwb_ref[hs, :]
                        z = zt_ref[hs, qs]
                        o_ref[hs, qs] = (o * _silu(z)).astype(BF16)


def _fox(kr, t16, t32, mcol, nwb):
    b, s, _ = kr.shape
    t = ATT_T
    tq = ATT_T * ATT_QBLOCKS
    pairs = [(i, j) for i in range(s // tq) for j in range((i + 1) * ATT_QBLOCKS)]
    it = jnp.asarray([p[0] for p in pairs], jnp.int32)
    jt = jnp.asarray([p[1] for p in pairs], jnp.int32)

    grid_spec = pltpu.PrefetchScalarGridSpec(
        num_scalar_prefetch=2,
        grid=(b, len(pairs)),
        in_specs=[
            pl.BlockSpec((None, FOX_WIDTH, tq), lambda bi, p, it, jt: (bi, 0, it[p])),
            pl.BlockSpec((None, t, FOX_WIDTH), lambda bi, p, it, jt: (bi, jt[p], 0)),
            pl.BlockSpec((None, FOX_WIDTH, t), lambda bi, p, it, jt: (bi, 1, jt[p])),
            pl.BlockSpec((None, t, LANES), lambda bi, p, it, jt: (bi, jt[p], 0)),
            pl.BlockSpec((None, FOX_WIDTH, tq), lambda bi, p, it, jt: (bi, 0, it[p])),
            pl.BlockSpec((FOX_WIDTH, t), lambda bi, p, it, jt: (0, 0)),
        ],
        out_specs=pl.BlockSpec((None, FOX_WIDTH, tq), lambda bi, p, it, jt: (bi, 0, it[p])),
        scratch_shapes=[pltpu.VMEM((FOX_HEADS, tq), F32),
                        pltpu.VMEM((FOX_HEADS, FOX_HEAD_DIM + ONES_ROWS, tq), F32)],
    )
    return pl.pallas_call(
        _fox_kernel,
        grid_spec=grid_spec,
        out_shape=jax.ShapeDtypeStruct((b, FOX_WIDTH, s), BF16),
        compiler_params=_params(("arbitrary", "arbitrary")),
        name="fox_attention",
    )(it, jt, t16, kr, t16, mcol, t32, nwb)


def _mlstm_conv_tile(c, t, n_chunks, qk_ref, cw_ref, cb_ref, hist_ref):
    n = MLSTM_L
    ls = slice(t * LANES, (t + 1) * LANES)
    u = qk_ref[c * n:(c + 1) * n, ls]
    before = hist_ref[:, ls] if c == 0 else qk_ref[c * n - F32_SUBLANES:c * n, ls]
    ext = jnp.concatenate([before, u], axis=0)
    if c == n_chunks - 1:
        hist_ref[:, ls] = u[n - F32_SUBLANES:n, :]

    def tap(jj):
        back = CONV_WIDTH - 1 - jj
        return pltpu.roll(ext, back, axis=0)[F32_SUBLANES:, :] if back else u

    y = tap(0) * cw_ref[0:1, ls]
    for jj in range(1, CONV_WIDTH):
        y = y + tap(jj) * cw_ref[jj:jj + 1, ls]
    y = y + cb_ref[:, ls]
    return _silu(y)


def _mlstm_chunk(c, sy, vt_ref, ogt_ref, zt_ref, mcol_ref, mrow_ref, nwb_ref, write_out, st_ref,
                 after_head=lambda h: None):
    n = MLSTM_L
    dv = MLSTM_V_DIM
    ts = slice(c * n, (c + 1) * n)

    mc = mcol_ref[ts, :]
    mr = mrow_ref[:, ts]
    lane = lax.broadcasted_iota(jnp.int32, (n, LANES), 1)
    ones_rows = jnp.where(lax.broadcasted_iota(jnp.int32, (ONES_ROWS, n), 0) == 0, 1.0, 0.0).astype(BF16)
    valid = lax.broadcasted_iota(jnp.int32, (n, n), 0) <= lax.broadcasted_iota(jnp.int32, (n, n), 1)
    heads = range(MLSTM_HEADS)
    row = lambda base_row, h: mr[base_row + h:base_row + h + 1, :]

    q_pair = [sy[pr].astype(BF16) for pr in range(MLSTM_HEADS // 2)]
    k_pair = [sy[MLSTM_QK_WIDTH // LANES + pr] * (MLSTM_QK_DIM ** -0.5) for pr in range(MLSTM_HEADS // 2)]
    qp, kh, st_t, inter_t, vt_aug = [], [], [], [], []
    for h in heads:
        qp.append(q_pair[h // 2])
        kh.append(jnp.where((lane < MLSTM_QK_DIM) if h % 2 == 0 else (lane >= MLSTM_QK_DIM), k_pair[h // 2], 0.0))
        st_t.append(lax.dot_general(kh[h].astype(BF16), qp[h], NT_DIMS, preferred_element_type=F32))
        inter_t.append(lax.dot_general(st_ref[h].astype(BF16), qp[h], NT_DIMS, preferred_element_type=F32))
        vt_aug.append(jnp.concatenate([vt_ref[h * dv:(h + 1) * dv, ts], ones_rows], axis=0))

    for h in heads:
        hs = slice(h * dv, (h + 1) * dv)
        a_col = mc[:, LANE_A + h:LANE_A + h + 1]
        sc_t = st_t[h] * jnp.exp(jnp.where(valid, a_col + row(ROW_G, h), NEG_INF))
        pv = jnp.dot(vt_aug[h], sc_t.astype(BF16), preferred_element_type=F32)
        after_head(h)
        res = pv + jnp.exp(row(ROW_U, h)) * inter_t[h]
        den = res[dv:dv + 1, :]
        ht = res[:dv] * (1.0 / jnp.maximum(jnp.abs(den), jnp.exp(row(ROW_NEGM, h))))
        hb = ht * _sigmoid(ogt_ref[hs, ts])
        hn = hb * lax.rsqrt(jnp.mean(hb * hb, axis=0, keepdims=True) + NORM_EPS) * nwb_ref[hs, :]
        z = zt_ref[hs, ts]
        write_out(hs, ts, (hn * _silu(z)).astype(BF16))

    for h in heads:
        a_col = mc[:, LANE_A + h:LANE_A + h + 1]
        g_last = mc[n - 1:n, LANE_G + h:LANE_G + h + 1]
        u_last = mc[n - 1:n, LANE_U + h:LANE_U + h + 1]
        kw = (kh[h] * jnp.exp(a_col + g_last)).astype(BF16)
        upd = jnp.dot(vt_aug[h], kw, preferred_element_type=F32)
        st_ref[h] = jnp.exp(u_last) * st_ref[h] + upd


def _mlstm_out_kernel(qk_ref, vt_ref, ogt_ref, zt_ref, mcol_ref, mrow_ref, cw_ref, cb_ref, nwb_ref,
                      yat_ref, wa_ref, wb_ref, x_ref, fw_ref, o_ref, hist_ref, st_ref, ybt_ref,
                      *, per_seq, final):
    step = pl.program_id(0)
    cur = step % 2
    n = MLSTM_L

    @pl.when(step % per_seq == 0)
    def _():
        hist_ref[...] = jnp.zeros_like(hist_ref)
        st_ref[...] = jnp.zeros_like(st_ref)

    @pl.when(step == 0)
    def _():
        ybt_ref[1] = jnp.zeros_like(ybt_ref[1])

    def write_yb(hs, ts, val):
        ybt_ref[cur, hs, ts] = val

    prev = ybt_ref.at[1 - cur]
    n_chunks = FUSE_TM // n
    q_tiles = MLSTM_QK_WIDTH // LANES
    conv_tile = lambda c, t: _mlstm_conv_tile(c, t, n_chunks, qk_ref, cw_ref, cb_ref, hist_ref)
    cols = D_MODEL // MLSTM_HEADS
    for c in range(n_chunks):
        rs = slice(c * n, (c + 1) * n)
        sy = [conv_tile(c, t) for t in range(2 * q_tiles)]
        proj = []
        ya = yat_ref[:, rs].T
        yb = prev[:, rs].T

        def woven(h, proj=proj, ya=ya, yb=yb):
            cs = slice(h * cols, (h + 1) * cols)
            proj.append(jnp.dot(ya, wa_ref[:, cs], preferred_element_type=F32)
                        + jnp.dot(yb, wb_ref[:, cs], preferred_element_type=F32))

        _mlstm_chunk(c, sy, vt_ref, ogt_ref, zt_ref, mcol_ref, mrow_ref, nwb_ref, write_yb, st_ref,
                     after_head=woven)
        acc = jnp.concatenate(proj, axis=1)
        r = x_ref[rs, :] + acc
        if final:
            r = r * lax.rsqrt(jnp.mean(r * r, axis=-1, keepdims=True) + NORM_EPS) * fw_ref[...]
        o_ref[rs, :] = r


def _mlstm_out(qkr, t16, t32, mcol, mrow, cw, cb, nwb, yat, wo, x2, fw, final):
    assert FOX_WIDTH == MLSTM_V_WIDTH and wo.shape == (FOX_WIDTH + MLSTM_V_WIDTH, D_MODEL)
    m = x2.shape[0]
    _, _, seq = yat.shape
    tm = FUSE_TM
    per_seq = seq // tm
    blocks = m // tm
    cur = lambda s: jnp.minimum(s, blocks - 1)
    prv = lambda s: jnp.maximum(s - 1, 0)
    t_cur = lambda which: (lambda s: (cur(s) // per_seq, which, cur(s) % per_seq))
    return pl.pallas_call(
        functools.partial(_mlstm_out_kernel, per_seq=per_seq, final=final),
        grid=(blocks + 1,),
        in_specs=[
            pl.BlockSpec((tm, SEG), lambda s: (cur(s), 0)),
            pl.BlockSpec((None, SEG, tm), t_cur(2)),
            pl.BlockSpec((None, SEG, tm), t_cur(1)),
            pl.BlockSpec((None, SEG, tm), t_cur(2)),
            pl.BlockSpec((tm, LANES), lambda s: (cur(s), 0)),
            pl.BlockSpec((None, GATE_ROWS, tm), lambda s: (cur(s) // per_seq, 0, cur(s) % per_seq)),
            pl.BlockSpec((CONV_WIDTH, SEG), lambda s: (0, 0)),
            pl.BlockSpec((1, SEG), lambda s: (0, 0)),
            pl.BlockSpec((SEG, MLSTM_L), lambda s: (0, 0)),
            pl.BlockSpec((None, FOX_WIDTH, tm), lambda s: (prv(s) // per_seq, 0, prv(s) % per_seq)),
            pl.BlockSpec((FOX_WIDTH, D_MODEL), lambda s: (0, 0)),
            pl.BlockSpec((MLSTM_V_WIDTH, D_MODEL), lambda s: (1, 0)),
            pl.BlockSpec((tm, D_MODEL), lambda s: (prv(s), 0)),
            pl.BlockSpec((1, D_MODEL), lambda s: (0, 0)),
        ],
        out_specs=pl.BlockSpec((tm, D_MODEL), lambda s: (prv(s), 0)),
        out_shape=jax.ShapeDtypeStruct((m, D_MODEL), F32),
        scratch_shapes=[pltpu.VMEM((F32_SUBLANES, SEG), F32),
                        pltpu.VMEM((MLSTM_HEADS, MLSTM_V_DIM + ONES_ROWS, LANES), F32),
                        pltpu.VMEM((2, MLSTM_V_WIDTH, tm), BF16)],
        compiler_params=_params(("arbitrary",)),
        name="mlstm_outproj",
    )(qkr, t16, t32, t32, mcol, mrow, cw, cb, nwb, yat, wo, wo, x2, fw)


SEG_STARTS = (IN_OFFS[0], IN_OFFS[1], IN_OFFS[5], IN_OFFS[2], IN_OFFS[7], IN_OFFS[3], IN_OFFS[8], IN_OFFS[9])
SEG_SHIFT = FOX_HEADS
assert all(st % SEG in (0, SEG_SHIFT) for st in SEG_STARTS) and IN_OFFS[6] == IN_OFFS[5] + MLSTM_QK_WIDTH
assert SEG_SHIFT == F32_SUBLANES and MLSTM_HEADS == F32_SUBLANES

J_FOX_GATE = 2
assert SEG_STARTS[J_FOX_GATE] - SEG_SHIFT == IN_OFFS[4]


def _wprep_kernel(main_blk_ref, extra_blk_ref, main_ref, extra_ref, mi_ref, mf_ref, wout_ref,
                  o_ref, gw_ref, wo_ref, gwf_ref):
    del main_blk_ref, extra_blk_ref
    j = pl.program_id(0)
    in_seg = j < len(SEG_STARTS)
    shifted = functools.reduce(jnp.logical_or, [j == k for k, st in enumerate(SEG_STARTS) if st % SEG])

    @pl.when(jnp.logical_not(in_seg))
    def _():
        wo_ref[...] = wout_ref[...].astype(BF16)

    @pl.when(j == 0)
    def _():
        gwf_ref[...] = jnp.zeros_like(gwf_ref)
        gwf_ref[GW_MLSTM_I:GW_MLSTM_I + MLSTM_HEADS, :] = mi_ref[...]
        gwf_ref[GW_MLSTM_F:GW_MLSTM_F + MLSTM_HEADS, :] = mf_ref[...]

    @pl.when(in_seg & jnp.logical_not(shifted))
    def _():
        o_ref[...] = main_ref[...].astype(BF16)

    @pl.when(shifted)
    def _():
        full = jnp.concatenate([main_ref[...], extra_ref[...]], axis=0)
        o_ref[...] = full[SEG_SHIFT:SEG_SHIFT + SEG, :].astype(BF16)

    @pl.when(j == J_FOX_GATE)
    def _():
        gwf_ref[GW_FOX_F:GW_FOX_F + FOX_HEADS, :] = main_ref[0:SEG_SHIFT, :]

    @pl.when(j == len(SEG_STARTS) - 1)
    def _():
        gw_ref[...] = gwf_ref[...].astype(BF16)


def _wprep(w_t, w_out):
    d = w_t.shape[1]
    nseg = len(SEG_STARTS)
    out_steps = w_out.shape[0] // SEG
    pad = [SEG_STARTS[-1]] * out_steps
    main_blk = jnp.asarray([st // SEG for st in SEG_STARTS + tuple(pad)], jnp.int32)
    extra_blk = jnp.asarray([(st // SEG + 1) * (SEG // F32_SUBLANES) for st in SEG_STARTS + tuple(pad)],
                            jnp.int32)
    out_blk = lambda j: jnp.clip(j - nseg, 0, out_steps - 1)
    grid_spec = pltpu.PrefetchScalarGridSpec(
        num_scalar_prefetch=2,
        grid=(nseg + out_steps,),
        in_specs=[
            pl.BlockSpec((SEG, d), lambda j, mb, eb: (mb[j], 0)),
            pl.BlockSpec((F32_SUBLANES, d), lambda j, mb, eb: (eb[j], 0)),
            pl.BlockSpec((MLSTM_HEADS, d), lambda j, mb, eb: (IN_OFFS[10] // F32_SUBLANES, 0)),
            pl.BlockSpec((MLSTM_HEADS, d), lambda j, mb, eb: (IN_OFFS[11] // F32_SUBLANES, 0)),
            pl.BlockSpec((SEG, w_out.shape[1]), lambda j, mb, eb: (out_blk(j), 0)),
        ],
        out_specs=[
            pl.BlockSpec((SEG, d), lambda j, mb, eb: (jnp.minimum(j, nseg - 1), 0)),
            pl.BlockSpec((GATE_ROWS, d), lambda j, mb, eb: (0, 0)),
            pl.BlockSpec((SEG, w_out.shape[1]), lambda j, mb, eb: (out_blk(j), 0)),
        ],
        scratch_shapes=[pltpu.VMEM((GATE_ROWS, d), F32)],
    )
    return pl.pallas_call(
        _wprep_kernel,
        grid_spec=grid_spec,
        out_shape=[jax.ShapeDtypeStruct((nseg * SEG, d), BF16),
                   jax.ShapeDtypeStruct((GATE_ROWS, d), BF16),
                   jax.ShapeDtypeStruct(w_out.shape, BF16)],
        compiler_params=_params(("arbitrary",)),
        name="weight_layout",
    )(main_blk, extra_blk, w_t, w_t, w_t, w_t, w_out)


def _lane_broadcast(v, width):
    return jnp.broadcast_to(v[:, None], (v.shape[0], width))


def kernel(x, norm_w, w_in, fox_f_bias, conv_w, conv_b, mlstm_i_bias, mlstm_f_bias,
           fox_out_norm_w, mlstm_out_norm_w, w_out, final_norm_w):
    b, s, d = x.shape
    depth = norm_w.shape[0]
    x2 = x.reshape(b * s, d)
    for l in range(depth):
        w_t, w_gate, wo = _wprep(w_in[l].T, w_out[l])
        bias_rows = _lane_broadcast(jnp.concatenate([
            fox_f_bias[l], mlstm_f_bias[l], mlstm_i_bias[l], jnp.zeros((GATE_ROWS - 24,), F32)]), LANES)

        kr, qkr, t16, t32, g = _inproj(x2, norm_w[l].reshape(1, d), w_t, w_gate, s)
        mcol, mrow = _gates(g, bias_rows)

        yat = _fox(kr.reshape(b, s, SEG), t16, t32, mcol, _lane_broadcast(fox_out_norm_w[l], ATT_T))
        x2 = _mlstm_out(qkr, t16, t32, mcol.reshape(b * s, LANES), mrow, conv_w[l], conv_b[l].reshape(1, SEG),
                        _lane_broadcast(mlstm_out_norm_w[l], MLSTM_L),
                        yat, wo, x2, final_norm_w.reshape(1, d),
                        final=(l == depth - 1))
    return x2.reshape(b, s, d)
```

```python
import functools

import numpy as np
import jax
import jax.numpy as jnp
from jax import lax
from jax.experimental import pallas as pl
from jax.experimental.pallas import tpu as pltpu

D_MODEL = 2048
NORM_EPS = 1e-6
NEG_INF = -1e30

FOX_HEADS = 8
FOX_HEAD_DIM = 128
FOX_WIDTH = FOX_HEADS * FOX_HEAD_DIM
MLSTM_HEADS = 8
MLSTM_V_DIM = 128
MLSTM_QK_DIM = 64
MLSTM_V_WIDTH = MLSTM_HEADS * MLSTM_V_DIM
MLSTM_QK_WIDTH = MLSTM_HEADS * MLSTM_QK_DIM
CONV_WIDTH = 4
IN_SIZES = (FOX_WIDTH, FOX_WIDTH, FOX_WIDTH, FOX_WIDTH, FOX_HEADS,
            MLSTM_QK_WIDTH, MLSTM_QK_WIDTH, MLSTM_V_WIDTH, MLSTM_V_WIDTH, MLSTM_V_WIDTH,
            MLSTM_HEADS, MLSTM_HEADS)
IN_OFFS = tuple(int(v) for v in np.cumsum((0,) + IN_SIZES))

LANES = 128
F32_SUBLANES = 8
SEG = 1024
VMEM_LIMIT = 56 * 1024 * 1024

PROJ_TM = 256
GATE_R = 2048
MLSTM_L = 256
ATT_T = 512
FUSE_TM = 512
FUSE_SLICES = 8

LOG2E = 1.4426950408889634
FOX_Q_SCALE = FOX_HEAD_DIM ** -0.5 * LOG2E
ONES_ROWS = 16
ATT_QBLOCKS = 2
ATT_LOOKAHEAD = 1

GW_FOX_F, GW_MLSTM_F, GW_MLSTM_I = 0, 8, 16
GATE_ROWS = 32
ROW_G, ROW_U, ROW_A, ROW_NEGM = 0, 8, 16, 24
LANE_FOX = 0
LANE_G, LANE_U, LANE_A, LANE_NEGM = (8 + r for r in (ROW_G, ROW_U, ROW_A, ROW_NEGM))

BF16 = jnp.bfloat16
F32 = jnp.float32
NT_DIMS = (((1,), (1,)), ((), ()))


def _sigmoid(x):
    return 1.0 / (1.0 + jnp.exp(-x))


def _silu(x):
    h = 0.5 * x
    return h * (1.0 + jnp.tanh(h))


def _params(sem):
    return pltpu.CompilerParams(dimension_semantics=sem, vmem_limit_bytes=VMEM_LIMIT)


def _inproj_kernel(x_ref, nw_ref, wt_ref, wg_ref, kr_ref, qkr_ref, t16_ref, t32_ref, g_ref):
    xf = x_ref[...]
    ms = jnp.mean(xf * xf, axis=-1, keepdims=True)
    hb = (xf * lax.rsqrt(ms + NORM_EPS) * nw_ref[...]).astype(BF16)

    def seg_t(j):
        return lax.dot_general(wt_ref[j * SEG:(j + 1) * SEG, :], hb, NT_DIMS, preferred_element_type=F32)

    t16_ref[0:SEG, :] = (seg_t(0) * FOX_Q_SCALE).astype(BF16)
    kr_ref[...] = seg_t(1).T.astype(BF16)
    qkr_ref[...] = seg_t(2).T
    t16_ref[SEG:2 * SEG, :] = seg_t(3).astype(BF16)
    t16_ref[2 * SEG:3 * SEG, :] = seg_t(4).astype(BF16)
    for r in range(2):
        t32_ref[r * SEG:(r + 1) * SEG, :] = seg_t(5 + r)
    last = jnp.concatenate([wt_ref[7 * SEG:8 * SEG, :], wg_ref[...]], axis=0)
    acc = lax.dot_general(last, hb, NT_DIMS, preferred_element_type=F32)
    t32_ref[2 * SEG:3 * SEG, :] = acc[:SEG]
    g_ref[...] = acc[SEG:]


def _inproj(x2, nw, w_t, w_gate, seq):
    m = x2.shape[0]
    tm = PROJ_TM
    per_seq = seq // tm
    t_map = lambda i: (i // per_seq, 0, i % per_seq)
    return pl.pallas_call(
        _inproj_kernel,
        grid=(m // tm,),
        in_specs=[
            pl.BlockSpec((tm, D_MODEL), lambda i: (i, 0)),
            pl.BlockSpec((1, D_MODEL), lambda i: (0, 0)),
            pl.BlockSpec(w_t.shape, lambda i: (0, 0)),
            pl.BlockSpec((GATE_ROWS, D_MODEL), lambda i: (0, 0)),
        ],
        out_specs=[
            pl.BlockSpec((tm, SEG), lambda i: (i, 0)),
            pl.BlockSpec((tm, SEG), lambda i: (i, 0)),
            pl.BlockSpec((None, 3 * SEG, tm), t_map),
            pl.BlockSpec((None, 3 * SEG, tm), t_map),
            pl.BlockSpec((None, GATE_ROWS, tm), t_map),
        ],
        out_shape=[
            jax.ShapeDtypeStruct((m, SEG), BF16),
            jax.ShapeDtypeStruct((m, SEG), F32),
            jax.ShapeDtypeStruct((m // seq, 3 * SEG, seq), BF16),
            jax.ShapeDtypeStruct((m // seq, 3 * SEG, seq), F32),
            jax.ShapeDtypeStruct((m // seq, GATE_ROWS, seq), F32),
        ],
        compiler_params=_params(("arbitrary",)),
        name="inproj",
    )(x2, nw, w_t, w_gate)


def _split3(v):
    hi = v.astype(BF16)
    r1 = v - hi.astype(F32)
    mid = r1.astype(BF16)
    lo = (r1 - mid.astype(F32)).astype(BF16)
    return hi, mid, lo


def _gates_kernel(g_ref, b_ref, mcol_ref, mrow_ref, carry_ref, mprev_ref):
    width, chunk = GATE_R, MLSTM_L
    nh = FOX_HEADS
    lanes_of = lambda tile, n: jnp.concatenate([tile] * (n // LANES), axis=1)

    @pl.when(pl.program_id(1) == 0)
    def _():
        carry_ref[...] = jnp.zeros_like(carry_ref)
        mprev_ref[...] = jnp.zeros_like(mprev_ref)

    v = g_ref[...] + lanes_of(b_ref[...], width)
    vf = v[GW_FOX_F:GW_MLSTM_F + nh]
    ls = jnp.minimum(vf, 0.0) - jnp.log1p(jnp.exp(-jnp.abs(vf)))

    upper = (lax.broadcasted_iota(jnp.int32, (chunk, chunk), 0)
             <= lax.broadcasted_iota(jnp.int32, (chunk, chunk), 1)).astype(BF16)
    hi, mid, lo = _split3(ls)
    cuml = jnp.concatenate([
        jnp.dot(hi[:, c * chunk:(c + 1) * chunk], upper, preferred_element_type=F32)
        + jnp.dot(mid[:, c * chunk:(c + 1) * chunk], upper, preferred_element_type=F32)
        + jnp.dot(lo[:, c * chunk:(c + 1) * chunk], upper, preferred_element_type=F32)
        for c in range(width // chunk)], axis=1)
    b_all = cuml[GW_MLSTM_F:GW_MLSTM_F + nh]
    a = v[GW_MLSTM_I:GW_MLSTM_I + nh] - b_all

    t_in = lax.broadcasted_iota(jnp.int32, (nh, width), 1) % chunk
    cm = a
    s = 1
    while s < chunk:
        cm = jnp.where(t_in >= s, jnp.maximum(cm, pltpu.roll(cm, s, axis=1)), cm)
        s *= 2

    last = lambda x: jnp.broadcast_to(x[:, chunk - 1:chunk], (nh, LANES))
    carry = carry_ref[...]
    mprev = mprev_ref[...]
    for c in range(width // chunk):
        sl = slice(c * chunk, (c + 1) * chunk)
        cumf = cuml[GW_FOX_F:GW_FOX_F + nh, sl] + lanes_of(carry, chunk)
        carry = last(cumf)
        b = b_all[:, sl]
        mp = lanes_of(mprev, chunk)
        mt = b + jnp.maximum(mp, cm[:, sl])
        rows = {ROW_G: b - mt, ROW_U: (b + mp) - mt, ROW_A: a[:, sl], ROW_NEGM: -mt}
        mprev = last(mt)
        for r0, val in rows.items():
            mrow_ref[r0:r0 + nh, sl] = val
        tile_t = jnp.concatenate([cumf * LOG2E] + [rows[r0] for r0 in (ROW_G, ROW_U, ROW_A, ROW_NEGM)]
                                 + [jnp.zeros((LANES - 5 * nh, chunk), F32)], axis=0)
        mcol_ref[sl, :] = tile_t.T
    carry_ref[...] = carry
    mprev_ref[...] = mprev


def _gates(gt, bias_rows):
    b, _, s = gt.shape
    r = GATE_R
    return pl.pallas_call(
        _gates_kernel,
        grid=(b, s // r),
        in_specs=[
            pl.BlockSpec((None, GATE_ROWS, r), lambda bi, ri: (bi, 0, ri)),
            pl.BlockSpec((GATE_ROWS, LANES), lambda bi, ri: (0, 0)),
        ],
        out_specs=[
            pl.BlockSpec((None, r, LANES), lambda bi, ri: (bi, ri, 0)),
            pl.BlockSpec((None, GATE_ROWS, r), lambda bi, ri: (bi, 0, ri)),
        ],
        out_shape=[
            jax.ShapeDtypeStruct((b, s, LANES), F32),
            jax.ShapeDtypeStruct((b, GATE_ROWS, s), F32),
        ],
        scratch_shapes=[pltpu.VMEM((FOX_HEADS, LANES), F32), pltpu.VMEM((MLSTM_HEADS, LANES), F32)],
        compiler_params=_params(("arbitrary", "arbitrary")),
        name="gates",
    )(gt, bias_rows)


def _fox_kernel(it_ref, jt_ref, qt_ref, k_ref, vt_ref, gc_ref, zt_ref, nwb_ref, o_ref, m_ref, acc_ref):
    step_id = pl.program_id(1)
    i = it_ref[step_id]
    j = jt_ref[step_id]
    t = ATT_T
    hd = FOX_HEAD_DIM

    @pl.when(j == 0)
    def _():
        m_ref[...] = jnp.full_like(m_ref, NEG_INF)
        acc_ref[...] = jnp.zeros_like(acc_ref)

    def step(modes):
        gc = gc_ref[...]
        ones = jnp.ones((ONES_ROWS, t), BF16)
        units = [(h, c) for h in range(FOX_HEADS) for c in range(ATT_QBLOCKS) if modes[c]]

        def scores(h, c):
            hs = slice(h * hd, (h + 1) * hd)
            st = jnp.dot(k_ref[:, hs], qt_ref[hs, c * t:(c + 1) * t],
                         preferred_element_type=F32) - gc[:, h:h + 1]
            if modes[c] == 'diag':
                valid = (lax.broadcasted_iota(jnp.int32, (t, t), 0)
                         <= lax.broadcasted_iota(jnp.int32, (t, t), 1))
                st = jnp.where(valid, st, NEG_INF)
            return st

        def probs(h, c, st):
            qs = slice(c * t, (c + 1) * t)
            m_prev = m_ref[h:h + 1, qs]
            m_new = jnp.maximum(m_prev, jnp.max(st, axis=0, keepdims=True))
            m_ref[h:h + 1, qs] = m_new
            return jnp.exp2(m_prev - m_new), jnp.exp2(st - m_new).astype(BF16)

        def accumulate(h, c, alpha, pt):
            qs = slice(c * t, (c + 1) * t)
            vt_aug = jnp.concatenate([vt_ref[h * hd:(h + 1) * hd, :], ones], axis=0)
            acc_ref[h, :, qs] = alpha * acc_ref[h, :, qs] + jnp.dot(vt_aug, pt, preferred_element_type=F32)

        st, ap = {}, {}
        for u in range(-ATT_LOOKAHEAD, len(units) + 1):
            if 0 <= u + ATT_LOOKAHEAD < len(units):
                st[u + ATT_LOOKAHEAD] = scores(*units[u + ATT_LOOKAHEAD])
            if 0 <= u < len(units):
                ap[u] = probs(*units[u], st.pop(u))
            if 0 <= u - 1 < len(units):
                accumulate(*units[u - 1], *ap.pop(u - 1))

    first = i * ATT_QBLOCKS

    @pl.when(j < first)
    def _():
        step(('full',) * ATT_QBLOCKS)

    for r in range(ATT_QBLOCKS):
        @pl.when(j == first + r)
        def _(r=r):
            step(tuple(None if c < r else 'diag' if c == r else 'full' for c in range(ATT_QBLOCKS)))
            if r == ATT_QBLOCKS - 1:
                for h in range(FOX_HEADS):
                    hs = slice(h * hd, (h + 1) * hd)
                    for c in range(ATT_QBLOCKS):
                        qs = slice(c * t, (c + 1) * t)
                        acc = acc_ref[h, :, qs]
                        o = acc[:hd] * (1.0 / acc[hd:hd + 1])
                        o = o * lax.rsqrt(jnp.mean(o * o, axis=0, keepdims=True) + NORM_EPS) * nwb_ref[hs, :]
                        z = zt_ref[hs, qs]
                        o_ref[hs, qs] = (o * _silu(z)).astype(BF16)


def _fox(kr, t16, t32, mcol, nwb):
    b, s, _ = kr.shape
    t = ATT_T
    tq = ATT_T * ATT_QBLOCKS
    pairs = [(i, j) for i in range(s // tq) for j in range((i + 1) * ATT_QBLOCKS)]
    it = jnp.asarray([p[0] for p in pairs], jnp.int32)
    jt = jnp.asarray([p[1] for p in pairs], jnp.int32)

    grid_spec = pltpu.PrefetchScalarGridSpec(
        num_scalar_prefetch=2,
        grid=(b, len(pairs)),
        in_specs=[
            pl.BlockSpec((None, FOX_WIDTH, tq), lambda bi, p, it, jt: (bi, 0, it[p])),
            pl.BlockSpec((None, t, FOX_WIDTH), lambda bi, p, it, jt: (bi, jt[p], 0)),
            pl.BlockSpec((None, FOX_WIDTH, t), lambda bi, p, it, jt: (bi, 1, jt[p])),
            pl.BlockSpec((None, t, LANES), lambda bi, p, it, jt: (bi, jt[p], 0)),
            pl.BlockSpec((None, FOX_WIDTH, tq), lambda bi, p, it, jt: (bi, 0, it[p])),
            pl.BlockSpec((FOX_WIDTH, t), lambda bi, p, it, jt: (0, 0)),
        ],
        out_specs=pl.BlockSpec((None, FOX_WIDTH, tq), lambda bi, p, it, jt: (bi, 0, it[p])),
        scratch_shapes=[pltpu.VMEM((FOX_HEADS, tq), F32),
                        pltpu.VMEM((FOX_HEADS, FOX_HEAD_DIM + ONES_ROWS, tq), F32)],
    )
    return pl.pallas_call(
        _fox_kernel,
        grid_spec=grid_spec,
        out_shape=jax.ShapeDtypeStruct((b, FOX_WIDTH, s), BF16),
        compiler_params=_params(("arbitrary", "arbitrary")),
        name="fox_attention",
    )(it, jt, t16, kr, t16, mcol, t32, nwb)


def _mlstm_conv_tile(c, t, n_chunks, qk_ref, cw_ref, cb_ref, hist_ref):
    n = MLSTM_L
    ls = slice(t * LANES, (t + 1) * LANES)
    u = qk_ref[c * n:(c + 1) * n, ls]
    before = hist_ref[:, ls] if c == 0 else qk_ref[c * n - F32_SUBLANES:c * n, ls]
    ext = jnp.concatenate([before, u], axis=0)
    if c == n_chunks - 1:
        hist_ref[:, ls] = u[n - F32_SUBLANES:n, :]

    def tap(jj):
        back = CONV_WIDTH - 1 - jj
        return pltpu.roll(ext, back, axis=0)[F32_SUBLANES:, :] if back else u

    y = tap(0) * cw_ref[0:1, ls]
    for jj in range(1, CONV_WIDTH):
        y = y + tap(jj) * cw_ref[jj:jj + 1, ls]
    y = y + cb_ref[:, ls]
    return _silu(y)


def _mlstm_chunk(c, sy, vt_ref, ogt_ref, zt_ref, mcol_ref, mrow_ref, nwb_ref, write_out, st_ref,
                 after_head=lambda h: None):
    n = MLSTM_L
    dv = MLSTM_V_DIM
    ts = slice(c * n, (c + 1) * n)

    mc = mcol_ref[ts, :]
    mr = mrow_ref[:, ts]
    lane = lax.broadcasted_iota(jnp.int32, (n, LANES), 1)
    ones_rows = jnp.where(lax.broadcasted_iota(jnp.int32, (ONES_ROWS, n), 0) == 0, 1.0, 0.0).astype(BF16)
    half = n // 2
    valid = lax.broadcasted_iota(jnp.int32, (half, half), 0) <= lax.broadcasted_iota(jnp.int32, (half, half), 1)
    heads = range(MLSTM_HEADS)
    row = lambda base_row, h: mr[base_row + h:base_row + h + 1, :]

    q_pair = [sy[pr].astype(BF16) for pr in range(MLSTM_HEADS // 2)]
    k_pair = [sy[MLSTM_QK_WIDTH // LANES + pr] * (MLSTM_QK_DIM ** -0.5) for pr in range(MLSTM_HEADS // 2)]
    qp, kh, st_t, inter_t, vt_aug = [], [], [], [], []
    for h in heads:
        qp.append(q_pair[h // 2])
        kh.append(jnp.where((lane < MLSTM_QK_DIM) if h % 2 == 0 else (lane >= MLSTM_QK_DIM), k_pair[h // 2], 0.0))
        st_t.append(lax.dot_general(kh[h].astype(BF16), qp[h], NT_DIMS, preferred_element_type=F32))
        inter_t.append(lax.dot_general(st_ref[h].astype(BF16), qp[h], NT_DIMS, preferred_element_type=F32))
        vt_aug.append(jnp.concatenate([vt_ref[h * dv:(h + 1) * dv, ts], ones_rows], axis=0))

    for h in heads:
        hs = slice(h * dv, (h + 1) * dv)
        a_col = mc[:, LANE_A + h:LANE_A + h + 1]
        g_row = row(ROW_G, h)
        w_ul = jnp.exp(jnp.where(valid, a_col[:half] + g_row[:, :half], NEG_INF))
        w_ur = jnp.exp(a_col[:half] + g_row[:, half:])
        w_lr = jnp.exp(jnp.where(valid, a_col[half:] + g_row[:, half:], NEG_INF))
        sc_t = jnp.concatenate([
            st_t[h][:half] * jnp.concatenate([w_ul, w_ur], axis=1),
            jnp.concatenate([jnp.zeros((half, half), F32), st_t[h][half:, half:] * w_lr], axis=1)], axis=0)
        pv = jnp.dot(vt_aug[h], sc_t.astype(BF16), preferred_element_type=F32)
        after_head(h)
        res = pv + jnp.exp(row(ROW_U, h)) * inter_t[h]
        den = res[dv:dv + 1, :]
        half_inv = 0.5 / jnp.maximum(jnp.abs(den), jnp.exp(row(ROW_NEGM, h)))
        hb = (res[:dv] * half_inv) * (1.0 + jnp.tanh(0.5 * ogt_ref[hs, ts]))
        hn = hb * lax.rsqrt(jnp.mean(hb * hb, axis=0, keepdims=True) + NORM_EPS) * nwb_ref[hs, :]
        z = zt_ref[hs, ts]
        write_out(hs, ts, (hn * _silu(z)).astype(BF16))

    for h in heads:
        a_col = mc[:, LANE_A + h:LANE_A + h + 1]
        g_last = mc[n - 1:n, LANE_G + h:LANE_G + h + 1]
        u_last = mc[n - 1:n, LANE_U + h:LANE_U + h + 1]
        kw = (kh[h] * jnp.exp(a_col + g_last)).astype(BF16)
        upd = jnp.dot(vt_aug[h], kw, preferred_element_type=F32)
        st_ref[h] = jnp.exp(u_last) * st_ref[h] + upd


def _mlstm_out_kernel(qk_ref, vt_ref, ogt_ref, zt_ref, mcol_ref, mrow_ref, cw_ref, cb_ref, nwb_ref,
                      yat_ref, wo_ref, x_ref, fw_ref, o_ref, hist_ref, st_ref, ybt_ref, lhs_ref,
                      *, per_seq, final):
    step = pl.program_id(0)
    cur = step % 2
    n = MLSTM_L

    @pl.when(step % per_seq == 0)
    def _():
        hist_ref[...] = jnp.zeros_like(hist_ref)
        st_ref[...] = jnp.zeros_like(st_ref)

    @pl.when(step == 0)
    def _():
        ybt_ref[1] = jnp.zeros_like(ybt_ref[1])

    def write_yb(hs, ts, val):
        ybt_ref[cur, hs, ts] = val

    prev = ybt_ref.at[1 - cur]
    n_chunks = FUSE_TM // n
    q_tiles = MLSTM_QK_WIDTH // LANES
    conv_tile = lambda c, t: _mlstm_conv_tile(c, t, n_chunks, qk_ref, cw_ref, cb_ref, hist_ref)
    every = MLSTM_HEADS // FUSE_SLICES
    cols = D_MODEL // FUSE_SLICES
    for c in range(n_chunks):
        rs = slice(c * n, (c + 1) * n)
        sy = [conv_tile(c, t) for t in range(2 * q_tiles)]
        proj = []
        lhs_ref[:, :FOX_WIDTH] = yat_ref[:, rs].T
        lhs_ref[:, FOX_WIDTH:] = prev[:, rs].T

        def woven(h, proj=proj):
            if h % every == 0:
                cs = slice(len(proj) * cols, (len(proj) + 1) * cols)
                proj.append(jnp.dot(lhs_ref[...], wo_ref[:, cs], preferred_element_type=F32))

        _mlstm_chunk(c, sy, vt_ref, ogt_ref, zt_ref, mcol_ref, mrow_ref, nwb_ref, write_yb, st_ref,
                     after_head=woven)
        acc = jnp.concatenate(proj, axis=1)
        r = x_ref[rs, :] + acc
        if final:
            r = r * lax.rsqrt(jnp.mean(r * r, axis=-1, keepdims=True) + NORM_EPS) * fw_ref[...]
        o_ref[rs, :] = r


def _mlstm_out(qkr, t16, t32, mcol, mrow, cw, cb, nwb, yat, wo, x2, fw, final):
    assert FOX_WIDTH == MLSTM_V_WIDTH and wo.shape == (FOX_WIDTH + MLSTM_V_WIDTH, D_MODEL)
    m = x2.shape[0]
    _, _, seq = yat.shape
    tm = FUSE_TM
    per_seq = seq // tm
    blocks = m // tm
    cur = lambda s: jnp.minimum(s, blocks - 1)
    prv = lambda s: jnp.maximum(s - 1, 0)
    t_cur = lambda which: (lambda s: (cur(s) // per_seq, which, cur(s) % per_seq))
    return pl.pallas_call(
        functools.partial(_mlstm_out_kernel, per_seq=per_seq, final=final),
        grid=(blocks + 1,),
        in_specs=[
            pl.BlockSpec((tm, SEG), lambda s: (cur(s), 0)),
            pl.BlockSpec((None, SEG, tm), t_cur(2)),
            pl.BlockSpec((None, SEG, tm), t_cur(1)),
            pl.BlockSpec((None, SEG, tm), t_cur(2)),
            pl.BlockSpec((tm, LANES), lambda s: (cur(s), 0)),
            pl.BlockSpec((None, GATE_ROWS, tm), lambda s: (cur(s) // per_seq, 0, cur(s) % per_seq)),
            pl.BlockSpec((CONV_WIDTH, SEG), lambda s: (0, 0)),
            pl.BlockSpec((1, SEG), lambda s: (0, 0)),
            pl.BlockSpec((SEG, MLSTM_L), lambda s: (0, 0)),
            pl.BlockSpec((None, FOX_WIDTH, tm), lambda s: (prv(s) // per_seq, 0, prv(s) % per_seq)),
            pl.BlockSpec(wo.shape, lambda s: (0, 0)),
            pl.BlockSpec((tm, D_MODEL), lambda s: (prv(s), 0)),
            pl.BlockSpec((1, D_MODEL), lambda s: (0, 0)),
        ],
        out_specs=pl.BlockSpec((tm, D_MODEL), lambda s: (prv(s), 0)),
        out_shape=jax.ShapeDtypeStruct((m, D_MODEL), F32),
        scratch_shapes=[pltpu.VMEM((F32_SUBLANES, SEG), F32),
                        pltpu.VMEM((MLSTM_HEADS, MLSTM_V_DIM + ONES_ROWS, LANES), F32),
                        pltpu.VMEM((2, MLSTM_V_WIDTH, tm), BF16),
                        pltpu.VMEM((MLSTM_L, FOX_WIDTH + MLSTM_V_WIDTH), BF16)],
        compiler_params=_params(("arbitrary",)),
        name="mlstm_outproj",
    )(qkr, t16, t32, t32, mcol, mrow, cw, cb, nwb, yat, wo, x2, fw)


SEG_STARTS = (IN_OFFS[0], IN_OFFS[1], IN_OFFS[5], IN_OFFS[2], IN_OFFS[7], IN_OFFS[3], IN_OFFS[8], IN_OFFS[9])
SEG_SHIFT = FOX_HEADS
assert all(st % SEG in (0, SEG_SHIFT) for st in SEG_STARTS) and IN_OFFS[6] == IN_OFFS[5] + MLSTM_QK_WIDTH
assert SEG_SHIFT == F32_SUBLANES and MLSTM_HEADS == F32_SUBLANES

J_FOX_GATE = 2
assert SEG_STARTS[J_FOX_GATE] - SEG_SHIFT == IN_OFFS[4]


def _wprep_kernel(main_blk_ref, extra_blk_ref, main_ref, extra_ref, mi_ref, mf_ref, wout_ref,
                  o_ref, gw_ref, wo_ref, gwf_ref):
    del main_blk_ref, extra_blk_ref
    j = pl.program_id(0)
    in_seg = j < len(SEG_STARTS)
    shifted = functools.reduce(jnp.logical_or, [j == k for k, st in enumerate(SEG_STARTS) if st % SEG])

    @pl.when(jnp.logical_not(in_seg))
    def _():
        wo_ref[...] = wout_ref[...].astype(BF16)

    @pl.when(j == 0)
    def _():
        gwf_ref[...] = jnp.zeros_like(gwf_ref)
        gwf_ref[GW_MLSTM_I:GW_MLSTM_I + MLSTM_HEADS, :] = mi_ref[...]
        gwf_ref[GW_MLSTM_F:GW_MLSTM_F + MLSTM_HEADS, :] = mf_ref[...]

    @pl.when(in_seg & jnp.logical_not(shifted))
    def _():
        o_ref[...] = main_ref[...].astype(BF16)

    @pl.when(shifted)
    def _():
        full = jnp.concatenate([main_ref[...], extra_ref[...]], axis=0)
        o_ref[...] = full[SEG_SHIFT:SEG_SHIFT + SEG, :].astype(BF16)

    @pl.when(j == J_FOX_GATE)
    def _():
        gwf_ref[GW_FOX_F:GW_FOX_F + FOX_HEADS, :] = main_ref[0:SEG_SHIFT, :]

    @pl.when(j == len(SEG_STARTS) - 1)
    def _():
        gw_ref[...] = gwf_ref[...].astype(BF16)


def _wprep(w_t, w_out):
    d = w_t.shape[1]
    nseg = len(SEG_STARTS)
    out_steps = w_out.shape[0] // SEG
    pad = [SEG_STARTS[-1]] * out_steps
    main_blk = jnp.asarray([st // SEG for st in SEG_STARTS + tuple(pad)], jnp.int32)
    extra_blk = jnp.asarray([(st // SEG + 1) * (SEG // F32_SUBLANES) for st in SEG_STARTS + tuple(pad)],
                            jnp.int32)
    out_blk = lambda j: jnp.clip(j - nseg, 0, out_steps - 1)
    grid_spec = pltpu.PrefetchScalarGridSpec(
        num_scalar_prefetch=2,
        grid=(nseg + out_steps,),
        in_specs=[
            pl.BlockSpec((SEG, d), lambda j, mb, eb: (mb[j], 0)),
            pl.BlockSpec((F32_SUBLANES, d), lambda j, mb, eb: (eb[j], 0)),
            pl.BlockSpec((MLSTM_HEADS, d), lambda j, mb, eb: (IN_OFFS[10] // F32_SUBLANES, 0)),
            pl.BlockSpec((MLSTM_HEADS, d), lambda j, mb, eb: (IN_OFFS[11] // F32_SUBLANES, 0)),
            pl.BlockSpec((SEG, w_out.shape[1]), lambda j, mb, eb: (out_blk(j), 0)),
        ],
        out_specs=[
            pl.BlockSpec((SEG, d), lambda j, mb, eb: (jnp.minimum(j, nseg - 1), 0)),
            pl.BlockSpec((GATE_ROWS, d), lambda j, mb, eb: (0, 0)),
            pl.BlockSpec((SEG, w_out.shape[1]), lambda j, mb, eb: (out_blk(j), 0)),
        ],
        scratch_shapes=[pltpu.VMEM((GATE_ROWS, d), F32)],
    )
    return pl.pallas_call(
        _wprep_kernel,
        grid_spec=grid_spec,
        out_shape=[jax.ShapeDtypeStruct((nseg * SEG, d), BF16),
                   jax.ShapeDtypeStruct((GATE_ROWS, d), BF16),
                   jax.ShapeDtypeStruct(w_out.shape, BF16)],
        compiler_params=_params(("arbitrary",)),
        name="weight_layout",
    )(main_blk, extra_blk, w_t, w_t, w_t, w_t, w_out)


def _lane_broadcast(v, width):
    return jnp.broadcast_to(v[:, None], (v.shape[0], width))


def kernel(x, norm_w, w_in, fox_f_bias, conv_w, conv_b, mlstm_i_bias, mlstm_f_bias,
           fox_out_norm_w, mlstm_out_norm_w, w_out, final_norm_w):
    b, s, d = x.shape
    depth = norm_w.shape[0]
    x2 = x.reshape(b * s, d)
    for l in range(depth):
        w_t, w_gate, wo = _wprep(w_in[l].T, w_out[l])
        bias_rows = _lane_broadcast(jnp.concatenate([
            fox_f_bias[l], mlstm_f_bias[l], mlstm_i_bias[l], jnp.zeros((GATE_ROWS - 24,), F32)]), LANES)

        kr, qkr, t16, t32, g = _inproj(x2, norm_w[l].reshape(1, d), w_t, w_gate, s)
        mcol, mrow = _gates(g, bias_rows)

        yat = _fox(kr.reshape(b, s, SEG), t16, t32, mcol, _lane_broadcast(fox_out_norm_w[l], ATT_T))
        x2 = _mlstm_out(qkr, t16, t32, mcol.reshape(b * s, LANES), mrow, conv_w[l], conv_b[l].reshape(1, SEG),
                        _lane_broadcast(mlstm_out_norm_w[l], MLSTM_L),
                        yat, wo, x2, final_norm_w.reshape(1, d),
                        final=(l == depth - 1))
    return x2.reshape(b, s, d)
```

```python
import functools

import numpy as np
import jax
import jax.numpy as jnp
from jax import lax
from jax.experimental import pallas as pl
from jax.experimental.pallas import tpu as pltpu

D_MODEL = 2048
NORM_EPS = 1e-6
NEG_INF = -1e30

FOX_HEADS = 8
FOX_HEAD_DIM = 128
FOX_WIDTH = FOX_HEADS * FOX_HEAD_DIM
MLSTM_HEADS = 8
MLSTM_V_DIM = 128
MLSTM_QK_DIM = 64
MLSTM_V_WIDTH = MLSTM_HEADS * MLSTM_V_DIM
MLSTM_QK_WIDTH = MLSTM_HEADS * MLSTM_QK_DIM
CONV_WIDTH = 4
IN_SIZES = (FOX_WIDTH, FOX_WIDTH, FOX_WIDTH, FOX_WIDTH, FOX_HEADS,
            MLSTM_QK_WIDTH, MLSTM_QK_WIDTH, MLSTM_V_WIDTH, MLSTM_V_WIDTH, MLSTM_V_WIDTH,
            MLSTM_HEADS, MLSTM_HEADS)
IN_OFFS = tuple(int(v) for v in np.cumsum((0,) + IN_SIZES))

LANES = 128
F32_SUBLANES = 8
SEG = 1024
VMEM_LIMIT = 56 * 1024 * 1024

PROJ_TM = 256
GATE_R = 2048
MLSTM_L = 256
ATT_T = 512
FUSE_TM = 512
FUSE_SLICES = 8

LOG2E = 1.4426950408889634
FOX_Q_SCALE = FOX_HEAD_DIM ** -0.5 * LOG2E
ONES_ROWS = 16
ATT_QBLOCKS = 2
ATT_LOOKAHEAD = 1

GW_FOX_F, GW_MLSTM_F, GW_MLSTM_I = 0, 8, 16
GATE_ROWS = 32
ROW_G, ROW_U, ROW_A, ROW_NEGM = 0, 8, 16, 24
LANE_FOX = 0
LANE_G, LANE_U, LANE_A, LANE_NEGM = (8 + r for r in (ROW_G, ROW_U, ROW_A, ROW_NEGM))

BF16 = jnp.bfloat16
F32 = jnp.float32
NT_DIMS = (((1,), (1,)), ((), ()))


def _sigmoid(x):
    return 1.0 / (1.0 + jnp.exp(-x))


def _silu(x):
    h = 0.5 * x
    return h * (1.0 + jnp.tanh(h))


def _params(sem):
    return pltpu.CompilerParams(dimension_semantics=sem, vmem_limit_bytes=VMEM_LIMIT)


def _inproj_kernel(x_ref, nw_ref, wt_ref, wg_ref, kr_ref, qkr_ref, t16_ref, t32_ref, g_ref):
    xf = x_ref[...]
    ms = jnp.mean(xf * xf, axis=-1, keepdims=True)
    hb = (xf * lax.rsqrt(ms + NORM_EPS) * nw_ref[...]).astype(BF16)

    def seg_t(j):
        return lax.dot_general(wt_ref[j * SEG:(j + 1) * SEG, :], hb, NT_DIMS, preferred_element_type=F32)

    t16_ref[0:SEG, :] = (seg_t(0) * FOX_Q_SCALE).astype(BF16)
    kr_ref[...] = seg_t(1).T.astype(BF16)
    qkr_ref[...] = seg_t(2).T
    t16_ref[SEG:2 * SEG, :] = seg_t(3).astype(BF16)
    t16_ref[2 * SEG:3 * SEG, :] = seg_t(4).astype(BF16)
    for r in range(2):
        t32_ref[r * SEG:(r + 1) * SEG, :] = seg_t(5 + r)
    last = jnp.concatenate([wt_ref[7 * SEG:8 * SEG, :], wg_ref[...]], axis=0)
    acc = lax.dot_general(last, hb, NT_DIMS, preferred_element_type=F32)
    t32_ref[2 * SEG:3 * SEG, :] = acc[:SEG]
    g_ref[...] = acc[SEG:]


def _inproj(x2, nw, w_t, w_gate, seq):
    m = x2.shape[0]
    tm = PROJ_TM
    per_seq = seq // tm
    t_map = lambda i: (i // per_seq, 0, i % per_seq)
    return pl.pallas_call(
        _inproj_kernel,
        grid=(m // tm,),
        in_specs=[
            pl.BlockSpec((tm, D_MODEL), lambda i: (i, 0)),
            pl.BlockSpec((1, D_MODEL), lambda i: (0, 0)),
            pl.BlockSpec(w_t.shape, lambda i: (0, 0)),
            pl.BlockSpec((GATE_ROWS, D_MODEL), lambda i: (0, 0)),
        ],
        out_specs=[
            pl.BlockSpec((tm, SEG), lambda i: (i, 0)),
            pl.BlockSpec((tm, SEG), lambda i: (i, 0)),
            pl.BlockSpec((None, 3 * SEG, tm), t_map),
            pl.BlockSpec((None, 3 * SEG, tm), t_map),
            pl.BlockSpec((None, GATE_ROWS, tm), t_map),
        ],
        out_shape=[
            jax.ShapeDtypeStruct((m, SEG), BF16),
            jax.ShapeDtypeStruct((m, SEG), F32),
            jax.ShapeDtypeStruct((m // seq, 3 * SEG, seq), BF16),
            jax.ShapeDtypeStruct((m // seq, 3 * SEG, seq), F32),
            jax.ShapeDtypeStruct((m // seq, GATE_ROWS, seq), F32),
        ],
        compiler_params=_params(("arbitrary",)),
        name="inproj",
    )(x2, nw, w_t, w_gate)


def _split3(v):
    hi = v.astype(BF16)
    r1 = v - hi.astype(F32)
    mid = r1.astype(BF16)
    lo = (r1 - mid.astype(F32)).astype(BF16)
    return hi, mid, lo


def _gates_kernel(g_ref, b_ref, mcol_ref, mrow_ref, carry_ref, mprev_ref):
    width, chunk = GATE_R, MLSTM_L
    nh = FOX_HEADS
    lanes_of = lambda tile, n: jnp.concatenate([tile] * (n // LANES), axis=1)

    @pl.when(pl.program_id(1) == 0)
    def _():
        carry_ref[...] = jnp.zeros_like(carry_ref)
        mprev_ref[...] = jnp.zeros_like(mprev_ref)

    v = g_ref[...] + lanes_of(b_ref[...], width)
    vf = v[GW_FOX_F:GW_MLSTM_F + nh]
    ls = jnp.minimum(vf, 0.0) - jnp.log1p(jnp.exp(-jnp.abs(vf)))

    upper = (lax.broadcasted_iota(jnp.int32, (chunk, chunk), 0)
             <= lax.broadcasted_iota(jnp.int32, (chunk, chunk), 1)).astype(BF16)
    hi, mid, lo = _split3(ls)
    cuml = jnp.concatenate([
        jnp.dot(hi[:, c * chunk:(c + 1) * chunk], upper, preferred_element_type=F32)
        + jnp.dot(mid[:, c * chunk:(c + 1) * chunk], upper, preferred_element_type=F32)
        + jnp.dot(lo[:, c * chunk:(c + 1) * chunk], upper, preferred_element_type=F32)
        for c in range(width // chunk)], axis=1)
    b_all = cuml[GW_MLSTM_F:GW_MLSTM_F + nh]
    a = v[GW_MLSTM_I:GW_MLSTM_I + nh] - b_all

    t_in = lax.broadcasted_iota(jnp.int32, (nh, width), 1) % chunk
    cm = a
    s = 1
    while s < chunk:
        cm = jnp.where(t_in >= s, jnp.maximum(cm, pltpu.roll(cm, s, axis=1)), cm)
        s *= 2

    last = lambda x: jnp.broadcast_to(x[:, chunk - 1:chunk], (nh, LANES))
    carry = carry_ref[...]
    mprev = mprev_ref[...]
    for c in range(width // chunk):
        sl = slice(c * chunk, (c + 1) * chunk)
        cumf = cuml[GW_FOX_F:GW_FOX_F + nh, sl] + lanes_of(carry, chunk)
        carry = last(cumf)
        b = b_all[:, sl]
        mp = lanes_of(mprev, chunk)
        mt = b + jnp.maximum(mp, cm[:, sl])
        rows = {ROW_G: b - mt, ROW_U: (b + mp) - mt, ROW_A: a[:, sl], ROW_NEGM: -mt}
        mprev = last(mt)
        for r0, val in rows.items():
            mrow_ref[r0:r0 + nh, sl] = val
        tile_t = jnp.concatenate([cumf * LOG2E] + [rows[r0] for r0 in (ROW_G, ROW_U, ROW_A, ROW_NEGM)]
                                 + [jnp.zeros((LANES - 5 * nh, chunk), F32)], axis=0)
        mcol_ref[sl, :] = tile_t.T
    carry_ref[...] = carry
    mprev_ref[...] = mprev


def _gates(gt, bias_rows):
    b, _, s = gt.shape
    r = GATE_R
    return pl.pallas_call(
        _gates_kernel,
        grid=(b, s // r),
        in_specs=[
            pl.BlockSpec((None, GATE_ROWS, r), lambda bi, ri: (bi, 0, ri)),
            pl.BlockSpec((GATE_ROWS, LANES), lambda bi, ri: (0, 0)),
        ],
        out_specs=[
            pl.BlockSpec((None, r, LANES), lambda bi, ri: (bi, ri, 0)),
            pl.BlockSpec((None, GATE_ROWS, r), lambda bi, ri: (bi, 0, ri)),
        ],
        out_shape=[
            jax.ShapeDtypeStruct((b, s, LANES), F32),
            jax.ShapeDtypeStruct((b, GATE_ROWS, s), F32),
        ],
        scratch_shapes=[pltpu.VMEM((FOX_HEADS, LANES), F32), pltpu.VMEM((MLSTM_HEADS, LANES), F32)],
        compiler_params=_params(("arbitrary", "arbitrary")),
        name="gates",
    )(gt, bias_rows)


def _fox_kernel(it_ref, jt_ref, qt_ref, k_ref, vt_ref, gc_ref, zt_ref, nwb_ref, sel_ref, o_ref, m_ref,
                acc_ref):
    step_id = pl.program_id(1)
    i = it_ref[step_id]
    j = jt_ref[step_id]
    t = ATT_T
    hd = FOX_HEAD_DIM

    @pl.when(j == 0)
    def _():
        m_ref[...] = jnp.full_like(m_ref, NEG_INF)
        acc_ref[...] = jnp.zeros_like(acc_ref)

    def step(modes):
        gc = gc_ref[...]
        lane = lax.broadcasted_iota(jnp.int32, gc.shape, 1)
        hi = gc.astype(BF16).astype(F32)
        r1 = gc - hi
        mid = r1.astype(BF16).astype(F32)
        lo = r1 - mid
        gk = jnp.where(lane < 8, hi,
             jnp.where(lane < 16, pltpu.roll(mid, 8, axis=1),
             jnp.where(lane < 24, pltpu.roll(lo, 16, axis=1), 0.0))).astype(BF16)
        ones = jnp.ones((ONES_ROWS, t), BF16)
        units = [(h, c) for h in range(FOX_HEADS) for c in range(ATT_QBLOCKS) if modes[c]]

        def scores(h, c):
            hs = slice(h * hd, (h + 1) * hd)
            sel = jnp.concatenate([sel_ref[h]] * (t // LANES), axis=1)
            st = jnp.dot(jnp.concatenate([k_ref[:, hs], gk], axis=1),
                         jnp.concatenate([qt_ref[hs, c * t:(c + 1) * t], sel], axis=0),
                         preferred_element_type=F32)
            if modes[c] == 'diag':
                valid = (lax.broadcasted_iota(jnp.int32, (t, t), 0)
                         <= lax.broadcasted_iota(jnp.int32, (t, t), 1))
                st = jnp.where(valid, st, NEG_INF)
            return st

        def probs(h, c, st):
            qs = slice(c * t, (c + 1) * t)
            m_prev = m_ref[h:h + 1, qs]
            m_new = jnp.maximum(m_prev, jnp.max(st, axis=0, keepdims=True))
            m_ref[h:h + 1, qs] = m_new
            return jnp.exp2(m_prev - m_new), jnp.exp2(st - m_new).astype(BF16)

        def accumulate(h, c, alpha, pt):
            qs = slice(c * t, (c + 1) * t)
            vt_aug = jnp.concatenate([vt_ref[h * hd:(h + 1) * hd, :], ones], axis=0)
            acc_ref[h, :, qs] = alpha * acc_ref[h, :, qs] + jnp.dot(vt_aug, pt, preferred_element_type=F32)

        st, ap = {}, {}
        for u in range(-ATT_LOOKAHEAD, len(units) + 1):
            if 0 <= u + ATT_LOOKAHEAD < len(units):
                st[u + ATT_LOOKAHEAD] = scores(*units[u + ATT_LOOKAHEAD])
            if 0 <= u < len(units):
                ap[u] = probs(*units[u], st.pop(u))
            if 0 <= u - 1 < len(units):
                accumulate(*units[u - 1], *ap.pop(u - 1))

    first = i * ATT_QBLOCKS

    @pl.when(j < first)
    def _():
        step(('full',) * ATT_QBLOCKS)

    for r in range(ATT_QBLOCKS):
        @pl.when(j == first + r)
        def _(r=r):
            step(tuple(None if c < r else 'diag' if c == r else 'full' for c in range(ATT_QBLOCKS)))
            if r == ATT_QBLOCKS - 1:
                for h in range(FOX_HEADS):
                    hs = slice(h * hd, (h + 1) * hd)
                    for c in range(ATT_QBLOCKS):
                        qs = slice(c * t, (c + 1) * t)
                        acc = acc_ref[h, :, qs]
                        o = acc[:hd] * (1.0 / acc[hd:hd + 1])
                        o = o * lax.rsqrt(jnp.mean(o * o, axis=0, keepdims=True) + NORM_EPS) * nwb_ref[hs, :]
                        z = zt_ref[hs, qs]
                        o_ref[hs, qs] = (o * _silu(z)).astype(BF16)


def _fox(kr, t16, t32, mcol, nwb):
    b, s, _ = kr.shape
    t = ATT_T
    tq = ATT_T * ATT_QBLOCKS
    pairs = [(i, j) for i in range(s // tq) for j in range((i + 1) * ATT_QBLOCKS)]
    it = jnp.asarray([p[0] for p in pairs], jnp.int32)
    jt = jnp.asarray([p[1] for p in pairs], jnp.int32)
    row = lax.broadcasted_iota(jnp.int32, (FOX_HEADS, LANES, LANES), 1)
    head = lax.broadcasted_iota(jnp.int32, (FOX_HEADS, LANES, LANES), 0)
    sel = jnp.where((row < 24) & (row % 8 == head), -1.0, 0.0).astype(BF16)

    grid_spec = pltpu.PrefetchScalarGridSpec(
        num_scalar_prefetch=2,
        grid=(b, len(pairs)),
        in_specs=[
            pl.BlockSpec((None, FOX_WIDTH, tq), lambda bi, p, it, jt: (bi, 0, it[p])),
            pl.BlockSpec((None, t, FOX_WIDTH), lambda bi, p, it, jt: (bi, jt[p], 0)),
            pl.BlockSpec((None, FOX_WIDTH, t), lambda bi, p, it, jt: (bi, 1, jt[p])),
            pl.BlockSpec((None, t, LANES), lambda bi, p, it, jt: (bi, jt[p], 0)),
            pl.BlockSpec((None, FOX_WIDTH, tq), lambda bi, p, it, jt: (bi, 0, it[p])),
            pl.BlockSpec((FOX_WIDTH, t), lambda bi, p, it, jt: (0, 0)),
            pl.BlockSpec((FOX_HEADS, LANES, LANES), lambda bi, p, it, jt: (0, 0, 0)),
        ],
        out_specs=pl.BlockSpec((None, FOX_WIDTH, tq), lambda bi, p, it, jt: (bi, 0, it[p])),
        scratch_shapes=[pltpu.VMEM((FOX_HEADS, tq), F32),
                        pltpu.VMEM((FOX_HEADS, FOX_HEAD_DIM + ONES_ROWS, tq), F32)],
    )
    return pl.pallas_call(
        _fox_kernel,
        grid_spec=grid_spec,
        out_shape=jax.ShapeDtypeStruct((b, FOX_WIDTH, s), BF16),
        compiler_params=_params(("arbitrary", "arbitrary")),
        name="fox_attention",
    )(it, jt, t16, kr, t16, mcol, t32, nwb, sel)


def _mlstm_conv_tile(c, t, n_chunks, qk_ref, cw_ref, cb_ref, hist_ref):
    n = MLSTM_L
    ls = slice(t * LANES, (t + 1) * LANES)
    u = qk_ref[c * n:(c + 1) * n, ls]
    before = hist_ref[:, ls] if c == 0 else qk_ref[c * n - F32_SUBLANES:c * n, ls]
    ext = jnp.concatenate([before, u], axis=0)
    if c == n_chunks - 1:
        hist_ref[:, ls] = u[n - F32_SUBLANES:n, :]

    def tap(jj):
        back = CONV_WIDTH - 1 - jj
        return pltpu.roll(ext, back, axis=0)[F32_SUBLANES:, :] if back else u

    y = tap(0) * cw_ref[0:1, ls]
    for jj in range(1, CONV_WIDTH):
        y = y + tap(jj) * cw_ref[jj:jj + 1, ls]
    y = y + cb_ref[:, ls]
    return _silu(y)


def _mlstm_chunk(c, sy, vt_ref, ogt_ref, zt_ref, mcol_ref, mrow_ref, nwb_ref, write_out, st_ref,
                 after_head=lambda h: None):
    n = MLSTM_L
    dv = MLSTM_V_DIM
    ts = slice(c * n, (c + 1) * n)

    mc = mcol_ref[ts, :]
    mr = mrow_ref[:, ts]
    lane = lax.broadcasted_iota(jnp.int32, (n, LANES), 1)
    ones_rows = jnp.where(lax.broadcasted_iota(jnp.int32, (ONES_ROWS, n), 0) == 0, 1.0, 0.0).astype(BF16)
    half = n // 2
    valid = lax.broadcasted_iota(jnp.int32, (half, half), 0) <= lax.broadcasted_iota(jnp.int32, (half, half), 1)
    heads = range(MLSTM_HEADS)
    row = lambda base_row, h: mr[base_row + h:base_row + h + 1, :]

    q_pair = [sy[pr].astype(BF16) for pr in range(MLSTM_HEADS // 2)]
    k_pair = [sy[MLSTM_QK_WIDTH // LANES + pr] * (MLSTM_QK_DIM ** -0.5) for pr in range(MLSTM_HEADS // 2)]
    qp, kh, st_t, inter_t, vt_aug = [], [], [], [], []
    for h in heads:
        qp.append(q_pair[h // 2])
        kh.append(jnp.where((lane < MLSTM_QK_DIM) if h % 2 == 0 else (lane >= MLSTM_QK_DIM), k_pair[h // 2], 0.0))
        st_t.append(lax.dot_general(kh[h].astype(BF16), qp[h], NT_DIMS, preferred_element_type=F32))
        inter_t.append(lax.dot_general(st_ref[h].astype(BF16), qp[h], NT_DIMS, preferred_element_type=F32))
        vt_aug.append(jnp.concatenate([vt_ref[h * dv:(h + 1) * dv, ts], ones_rows], axis=0))

    for h in heads:
        hs = slice(h * dv, (h + 1) * dv)
        a_col = mc[:, LANE_A + h:LANE_A + h + 1]
        g_row = row(ROW_G, h)
        w_ul = jnp.exp(jnp.where(valid, a_col[:half] + g_row[:, :half], NEG_INF))
        w_ur = jnp.exp(a_col[:half] + g_row[:, half:])
        w_lr = jnp.exp(jnp.where(valid, a_col[half:] + g_row[:, half:], NEG_INF))
        sc_t = jnp.concatenate([
            st_t[h][:half] * jnp.concatenate([w_ul, w_ur], axis=1),
            jnp.concatenate([jnp.zeros((half, half), F32), st_t[h][half:, half:] * w_lr], axis=1)], axis=0)
        pv = jnp.dot(vt_aug[h], sc_t.astype(BF16), preferred_element_type=F32)
        after_head(h)
        res = pv + jnp.exp(row(ROW_U, h)) * inter_t[h]
        den = res[dv:dv + 1, :]
        half_inv = 0.5 / jnp.maximum(jnp.abs(den), jnp.exp(row(ROW_NEGM, h)))
        hb = (res[:dv] * half_inv) * (1.0 + jnp.tanh(0.5 * ogt_ref[hs, ts]))
        hn = hb * lax.rsqrt(jnp.mean(hb * hb, axis=0, keepdims=True) + NORM_EPS) * nwb_ref[hs, :]
        z = zt_ref[hs, ts]
        write_out(hs, ts, (hn * _silu(z)).astype(BF16))

    for h in heads:
        a_col = mc[:, LANE_A + h:LANE_A + h + 1]
        g_last = mc[n - 1:n, LANE_G + h:LANE_G + h + 1]
        u_last = mc[n - 1:n, LANE_U + h:LANE_U + h + 1]
        kw = (kh[h] * jnp.exp(a_col + g_last)).astype(BF16)
        upd = jnp.dot(vt_aug[h], kw, preferred_element_type=F32)
        st_ref[h] = jnp.exp(u_last) * st_ref[h] + upd


def _mlstm_out_kernel(qk_ref, vt_ref, ogt_ref, zt_ref, mcol_ref, mrow_ref, cw_ref, cb_ref, nwb_ref,
                      yat_ref, wo_ref, x_ref, fw_ref, o_ref, hist_ref, st_ref, ybt_ref, lhs_ref,
                      *, per_seq, final):
    step = pl.program_id(0)
    cur = step % 2
    n = MLSTM_L

    @pl.when(step % per_seq == 0)
    def _():
        hist_ref[...] = jnp.zeros_like(hist_ref)
        st_ref[...] = jnp.zeros_like(st_ref)

    @pl.when(step == 0)
    def _():
        ybt_ref[1] = jnp.zeros_like(ybt_ref[1])

    def write_yb(hs, ts, val):
        ybt_ref[cur, hs, ts] = val

    prev = ybt_ref.at[1 - cur]
    n_chunks = FUSE_TM // n
    q_tiles = MLSTM_QK_WIDTH // LANES
    conv_tile = lambda c, t: _mlstm_conv_tile(c, t, n_chunks, qk_ref, cw_ref, cb_ref, hist_ref)
    every = MLSTM_HEADS // FUSE_SLICES
    cols = D_MODEL // FUSE_SLICES
    for c in range(n_chunks):
        rs = slice(c * n, (c + 1) * n)
        sy = [conv_tile(c, t) for t in range(2 * q_tiles)]
        proj = []
        lhs_ref[:, :FOX_WIDTH] = yat_ref[:, rs].T
        lhs_ref[:, FOX_WIDTH:] = prev[:, rs].T

        def woven(h, proj=proj):
            if h % every == 0:
                cs = slice(len(proj) * cols, (len(proj) + 1) * cols)
                proj.append(jnp.dot(lhs_ref[...], wo_ref[:, cs], preferred_element_type=F32))

        _mlstm_chunk(c, sy, vt_ref, ogt_ref, zt_ref, mcol_ref, mrow_ref, nwb_ref, write_yb, st_ref,
                     after_head=woven)
        acc = jnp.concatenate(proj, axis=1)
        r = x_ref[rs, :] + acc
        if final:
            r = r * lax.rsqrt(jnp.mean(r * r, axis=-1, keepdims=True) + NORM_EPS) * fw_ref[...]
        o_ref[rs, :] = r


def _mlstm_out(qkr, t16, t32, mcol, mrow, cw, cb, nwb, yat, wo, x2, fw, final):
    assert FOX_WIDTH == MLSTM_V_WIDTH and wo.shape == (FOX_WIDTH + MLSTM_V_WIDTH, D_MODEL)
    m = x2.shape[0]
    _, _, seq = yat.shape
    tm = FUSE_TM
    per_seq = seq // tm
    blocks = m // tm
    cur = lambda s: jnp.minimum(s, blocks - 1)
    prv = lambda s: jnp.maximum(s - 1, 0)
    t_cur = lambda which: (lambda s: (cur(s) // per_seq, which, cur(s) % per_seq))
    return pl.pallas_call(
        functools.partial(_mlstm_out_kernel, per_seq=per_seq, final=final),
        grid=(blocks + 1,),
        in_specs=[
            pl.BlockSpec((tm, SEG), lambda s: (cur(s), 0)),
            pl.BlockSpec((None, SEG, tm), t_cur(2)),
            pl.BlockSpec((None, SEG, tm), t_cur(1)),
            pl.BlockSpec((None, SEG, tm), t_cur(2)),
            pl.BlockSpec((tm, LANES), lambda s: (cur(s), 0)),
            pl.BlockSpec((None, GATE_ROWS, tm), lambda s: (cur(s) // per_seq, 0, cur(s) % per_seq)),
            pl.BlockSpec((CONV_WIDTH, SEG), lambda s: (0, 0)),
            pl.BlockSpec((1, SEG), lambda s: (0, 0)),
            pl.BlockSpec((SEG, MLSTM_L), lambda s: (0, 0)),
            pl.BlockSpec((None, FOX_WIDTH, tm), lambda s: (prv(s) // per_seq, 0, prv(s) % per_seq)),
            pl.BlockSpec(wo.shape, lambda s: (0, 0)),
            pl.BlockSpec((tm, D_MODEL), lambda s: (prv(s), 0)),
            pl.BlockSpec((1, D_MODEL), lambda s: (0, 0)),
        ],
        out_specs=pl.BlockSpec((tm, D_MODEL), lambda s: (prv(s), 0)),
        out_shape=jax.ShapeDtypeStruct((m, D_MODEL), F32),
        scratch_shapes=[pltpu.VMEM((F32_SUBLANES, SEG), F32),
                        pltpu.VMEM((MLSTM_HEADS, MLSTM_V_DIM + ONES_ROWS, LANES), F32),
                        pltpu.VMEM((2, MLSTM_V_WIDTH, tm), BF16),
                        pltpu.VMEM((MLSTM_L, FOX_WIDTH + MLSTM_V_WIDTH), BF16)],
        compiler_params=_params(("arbitrary",)),
        name="mlstm_outproj",
    )(qkr, t16, t32, t32, mcol, mrow, cw, cb, nwb, yat, wo, x2, fw)


SEG_STARTS = (IN_OFFS[0], IN_OFFS[1], IN_OFFS[5], IN_OFFS[2], IN_OFFS[7], IN_OFFS[3], IN_OFFS[8], IN_OFFS[9])
SEG_SHIFT = FOX_HEADS
assert all(st % SEG in (0, SEG_SHIFT) for st in SEG_STARTS) and IN_OFFS[6] == IN_OFFS[5] + MLSTM_QK_WIDTH
assert SEG_SHIFT == F32_SUBLANES and MLSTM_HEADS == F32_SUBLANES

J_FOX_GATE = 2
assert SEG_STARTS[J_FOX_GATE] - SEG_SHIFT == IN_OFFS[4]


def _wprep_kernel(main_blk_ref, extra_blk_ref, main_ref, extra_ref, mi_ref, mf_ref, wout_ref,
                  o_ref, gw_ref, wo_ref, gwf_ref):
    del main_blk_ref, extra_blk_ref
    j = pl.program_id(0)
    in_seg = j < len(SEG_STARTS)
    shifted = functools.reduce(jnp.logical_or, [j == k for k, st in enumerate(SEG_STARTS) if st % SEG])

    @pl.when(jnp.logical_not(in_seg))
    def _():
        wo_ref[...] = wout_ref[...].astype(BF16)

    @pl.when(j == 0)
    def _():
        gwf_ref[...] = jnp.zeros_like(gwf_ref)
        gwf_ref[GW_MLSTM_I:GW_MLSTM_I + MLSTM_HEADS, :] = mi_ref[...]
        gwf_ref[GW_MLSTM_F:GW_MLSTM_F + MLSTM_HEADS, :] = mf_ref[...]

    @pl.when(in_seg & jnp.logical_not(shifted))
    def _():
        o_ref[...] = main_ref[...].astype(BF16)

    @pl.when(shifted)
    def _():
        full = jnp.concatenate([main_ref[...], extra_ref[...]], axis=0)
        o_ref[...] = full[SEG_SHIFT:SEG_SHIFT + SEG, :].astype(BF16)

    @pl.when(j == J_FOX_GATE)
    def _():
        gwf_ref[GW_FOX_F:GW_FOX_F + FOX_HEADS, :] = main_ref[0:SEG_SHIFT, :]

    @pl.when(j == len(SEG_STARTS) - 1)
    def _():
        gw_ref[...] = gwf_ref[...].astype(BF16)


def _wprep(w_t, w_out):
    d = w_t.shape[1]
    nseg = len(SEG_STARTS)
    out_steps = w_out.shape[0] // SEG
    pad = [SEG_STARTS[-1]] * out_steps
    main_blk = jnp.asarray([st // SEG for st in SEG_STARTS + tuple(pad)], jnp.int32)
    extra_blk = jnp.asarray([(st // SEG + 1) * (SEG // F32_SUBLANES) for st in SEG_STARTS + tuple(pad)],
                            jnp.int32)
    out_blk = lambda j: jnp.clip(j - nseg, 0, out_steps - 1)
    grid_spec = pltpu.PrefetchScalarGridSpec(
        num_scalar_prefetch=2,
        grid=(nseg + out_steps,),
        in_specs=[
            pl.BlockSpec((SEG, d), lambda j, mb, eb: (mb[j], 0)),
            pl.BlockSpec((F32_SUBLANES, d), lambda j, mb, eb: (eb[j], 0)),
            pl.BlockSpec((MLSTM_HEADS, d), lambda j, mb, eb: (IN_OFFS[10] // F32_SUBLANES, 0)),
            pl.BlockSpec((MLSTM_HEADS, d), lambda j, mb, eb: (IN_OFFS[11] // F32_SUBLANES, 0)),
            pl.BlockSpec((SEG, w_out.shape[1]), lambda j, mb, eb: (out_blk(j), 0)),
        ],
        out_specs=[
            pl.BlockSpec((SEG, d), lambda j, mb, eb: (jnp.minimum(j, nseg - 1), 0)),
            pl.BlockSpec((GATE_ROWS, d), lambda j, mb, eb: (0, 0)),
            pl.BlockSpec((SEG, w_out.shape[1]), lambda j, mb, eb: (out_blk(j), 0)),
        ],
        scratch_shapes=[pltpu.VMEM((GATE_ROWS, d), F32)],
    )
    return pl.pallas_call(
        _wprep_kernel,
        grid_spec=grid_spec,
        out_shape=[jax.ShapeDtypeStruct((nseg * SEG, d), BF16),
                   jax.ShapeDtypeStruct((GATE_ROWS, d), BF16),
                   jax.ShapeDtypeStruct(w_out.shape, BF16)],
        compiler_params=_params(("arbitrary",)),
        name="weight_layout",
    )(main_blk, extra_blk, w_t, w_t, w_t, w_t, w_out)


def _lane_broadcast(v, width):
    return jnp.broadcast_to(v[:, None], (v.shape[0], width))


def kernel(x, norm_w, w_in, fox_f_bias, conv_w, conv_b, mlstm_i_bias, mlstm_f_bias,
           fox_out_norm_w, mlstm_out_norm_w, w_out, final_norm_w):
    b, s, d = x.shape
    depth = norm_w.shape[0]
    x2 = x.reshape(b * s, d)
    for l in range(depth):
        w_t, w_gate, wo = _wprep(w_in[l].T, w_out[l])
        bias_rows = _lane_broadcast(jnp.concatenate([
            fox_f_bias[l], mlstm_f_bias[l], mlstm_i_bias[l], jnp.zeros((GATE_ROWS - 24,), F32)]), LANES)

        kr, qkr, t16, t32, g = _inproj(x2, norm_w[l].reshape(1, d), w_t, w_gate, s)
        mcol, mrow = _gates(g, bias_rows)

        yat = _fox(kr.reshape(b, s, SEG), t16, t32, mcol, _lane_broadcast(fox_out_norm_w[l], ATT_T))
        x2 = _mlstm_out(qkr, t16, t32, mcol.reshape(b * s, LANES), mrow, conv_w[l], conv_b[l].reshape(1, SEG),
                        _lane_broadcast(mlstm_out_norm_w[l], MLSTM_L),
                        yat, wo, x2, final_norm_w.reshape(1, d),
                        final=(l == depth - 1))
    return x2.reshape(b, s, d)
```

```python
import functools

import numpy as np
import jax
import jax.numpy as jnp
from jax import lax
from jax.experimental import pallas as pl
from jax.experimental.pallas import tpu as pltpu

D_MODEL = 2048
NORM_EPS = 1e-6
NEG_INF = -1e30

FOX_HEADS = 8
FOX_HEAD_DIM = 128
FOX_WIDTH = FOX_HEADS * FOX_HEAD_DIM
MLSTM_HEADS = 8
MLSTM_V_DIM = 128
MLSTM_QK_DIM = 64
MLSTM_V_WIDTH = MLSTM_HEADS * MLSTM_V_DIM
MLSTM_QK_WIDTH = MLSTM_HEADS * MLSTM_QK_DIM
CONV_WIDTH = 4
IN_SIZES = (FOX_WIDTH, FOX_WIDTH, FOX_WIDTH, FOX_WIDTH, FOX_HEADS,
            MLSTM_QK_WIDTH, MLSTM_QK_WIDTH, MLSTM_V_WIDTH, MLSTM_V_WIDTH, MLSTM_V_WIDTH,
            MLSTM_HEADS, MLSTM_HEADS)
IN_OFFS = tuple(int(v) for v in np.cumsum((0,) + IN_SIZES))

LANES = 128
F32_SUBLANES = 8
SEG = 1024
VMEM_LIMIT = 56 * 1024 * 1024

PROJ_TM = 256
GATE_R = 2048
MLSTM_L = 256
ATT_T = 512
FUSE_TM = 512
FUSE_SLICES = 8

LOG2E = 1.4426950408889634
FOX_Q_SCALE = FOX_HEAD_DIM ** -0.5 * LOG2E
ONES_ROWS = 16
ATT_QBLOCKS = 2
ATT_LOOKAHEAD = 1

GW_FOX_F, GW_MLSTM_F, GW_MLSTM_I = 0, 8, 16
GATE_ROWS = 32
ROW_G, ROW_U, ROW_A, ROW_NEGM = 0, 8, 16, 24
LANE_FOX = 0
LANE_G, LANE_U, LANE_A, LANE_NEGM = (8 + r for r in (ROW_G, ROW_U, ROW_A, ROW_NEGM))

BF16 = jnp.bfloat16
F32 = jnp.float32
NT_DIMS = (((1,), (1,)), ((), ()))


def _sigmoid(x):
    return 1.0 / (1.0 + jnp.exp(-x))


def _silu(x):
    h = 0.5 * x
    return h * (1.0 + jnp.tanh(h))


def _params(sem):
    return pltpu.CompilerParams(dimension_semantics=sem, vmem_limit_bytes=VMEM_LIMIT)


def _inproj_kernel(x_ref, nw_ref, wt_ref, wg_ref, kr_ref, qkr_ref, t16_ref, t32_ref, g_ref):
    xf = x_ref[...]
    ms = jnp.mean(xf * xf, axis=-1, keepdims=True)
    hb = (xf * lax.rsqrt(ms + NORM_EPS) * nw_ref[...]).astype(BF16)

    def seg_t(j):
        return lax.dot_general(wt_ref[j * SEG:(j + 1) * SEG, :], hb, NT_DIMS, preferred_element_type=F32)

    t16_ref[0:SEG, :] = (seg_t(0) * FOX_Q_SCALE).astype(BF16)
    kr_ref[...] = seg_t(1).T.astype(BF16)
    qkr_ref[...] = seg_t(2).T
    t16_ref[SEG:2 * SEG, :] = seg_t(3).astype(BF16)
    t16_ref[2 * SEG:3 * SEG, :] = seg_t(4).astype(BF16)
    for r in range(2):
        t32_ref[r * SEG:(r + 1) * SEG, :] = seg_t(5 + r)
    last = jnp.concatenate([wt_ref[7 * SEG:8 * SEG, :], wg_ref[...]], axis=0)
    acc = lax.dot_general(last, hb, NT_DIMS, preferred_element_type=F32)
    t32_ref[2 * SEG:3 * SEG, :] = acc[:SEG]
    g_ref[...] = acc[SEG:]


def _inproj(x2, nw, w_t, w_gate, seq):
    m = x2.shape[0]
    tm = PROJ_TM
    per_seq = seq // tm
    t_map = lambda i: (i // per_seq, 0, i % per_seq)
    return pl.pallas_call(
        _inproj_kernel,
        grid=(m // tm,),
        in_specs=[
            pl.BlockSpec((tm, D_MODEL), lambda i: (i, 0)),
            pl.BlockSpec((1, D_MODEL), lambda i: (0, 0)),
            pl.BlockSpec(w_t.shape, lambda i: (0, 0)),
            pl.BlockSpec((GATE_ROWS, D_MODEL), lambda i: (0, 0)),
        ],
        out_specs=[
            pl.BlockSpec((tm, SEG), lambda i: (i, 0)),
            pl.BlockSpec((tm, SEG), lambda i: (i, 0)),
            pl.BlockSpec((None, 3 * SEG, tm), t_map),
            pl.BlockSpec((None, 3 * SEG, tm), t_map),
            pl.BlockSpec((None, GATE_ROWS, tm), t_map),
        ],
        out_shape=[
            jax.ShapeDtypeStruct((m, SEG), BF16),
            jax.ShapeDtypeStruct((m, SEG), F32),
            jax.ShapeDtypeStruct((m // seq, 3 * SEG, seq), BF16),
            jax.ShapeDtypeStruct((m // seq, 3 * SEG, seq), F32),
            jax.ShapeDtypeStruct((m // seq, GATE_ROWS, seq), F32),
        ],
        compiler_params=_params(("arbitrary",)),
        name="inproj",
    )(x2, nw, w_t, w_gate)


def _split3(v):
    hi = v.astype(BF16)
    r1 = v - hi.astype(F32)
    mid = r1.astype(BF16)
    lo = (r1 - mid.astype(F32)).astype(BF16)
    return hi, mid, lo


def _gates_kernel(g_ref, b_ref, mcol_ref, mrow_ref, carry_ref, mprev_ref):
    width, chunk = GATE_R, MLSTM_L
    nh = FOX_HEADS
    lanes_of = lambda tile, n: jnp.concatenate([tile] * (n // LANES), axis=1)

    @pl.when(pl.program_id(1) == 0)
    def _():
        carry_ref[...] = jnp.zeros_like(carry_ref)
        mprev_ref[...] = jnp.zeros_like(mprev_ref)

    v = g_ref[...] + lanes_of(b_ref[...], width)
    vf = v[GW_FOX_F:GW_MLSTM_F + nh]
    ls = jnp.minimum(vf, 0.0) - jnp.log1p(jnp.exp(-jnp.abs(vf)))

    upper = (lax.broadcasted_iota(jnp.int32, (chunk, chunk), 0)
             <= lax.broadcasted_iota(jnp.int32, (chunk, chunk), 1)).astype(BF16)
    hi, mid, lo = _split3(ls)
    cuml = jnp.concatenate([
        jnp.dot(hi[:, c * chunk:(c + 1) * chunk], upper, preferred_element_type=F32)
        + jnp.dot(mid[:, c * chunk:(c + 1) * chunk], upper, preferred_element_type=F32)
        + jnp.dot(lo[:, c * chunk:(c + 1) * chunk], upper, preferred_element_type=F32)
        for c in range(width // chunk)], axis=1)
    b_all = cuml[GW_MLSTM_F:GW_MLSTM_F + nh]
    a = v[GW_MLSTM_I:GW_MLSTM_I + nh] - b_all

    t_in = lax.broadcasted_iota(jnp.int32, (nh, width), 1) % chunk
    cm = a
    s = 1
    while s < chunk:
        cm = jnp.where(t_in >= s, jnp.maximum(cm, pltpu.roll(cm, s, axis=1)), cm)
        s *= 2

    last = lambda x: jnp.broadcast_to(x[:, chunk - 1:chunk], (nh, LANES))
    carry = carry_ref[...]
    mprev = mprev_ref[...]
    for c in range(width // chunk):
        sl = slice(c * chunk, (c + 1) * chunk)
        cumf = cuml[GW_FOX_F:GW_FOX_F + nh, sl] + lanes_of(carry, chunk)
        carry = last(cumf)
        b = b_all[:, sl]
        mp = lanes_of(mprev, chunk)
        mt = b + jnp.maximum(mp, cm[:, sl])
        rows = {ROW_G: b - mt, ROW_U: (b + mp) - mt, ROW_A: a[:, sl], ROW_NEGM: -mt}
        mprev = last(mt)
        for r0, val in rows.items():
            mrow_ref[r0:r0 + nh, sl] = val
        tile_t = jnp.concatenate([cumf * LOG2E] + [rows[r0] for r0 in (ROW_G, ROW_U, ROW_A, ROW_NEGM)]
                                 + [jnp.zeros((LANES - 5 * nh, chunk), F32)], axis=0)
        mcol_ref[sl, :] = tile_t.T
    carry_ref[...] = carry
    mprev_ref[...] = mprev


def _gates(gt, bias_rows):
    b, _, s = gt.shape
    r = GATE_R
    return pl.pallas_call(
        _gates_kernel,
        grid=(b, s // r),
        in_specs=[
            pl.BlockSpec((None, GATE_ROWS, r), lambda bi, ri: (bi, 0, ri)),
            pl.BlockSpec((GATE_ROWS, LANES), lambda bi, ri: (0, 0)),
        ],
        out_specs=[
            pl.BlockSpec((None, r, LANES), lambda bi, ri: (bi, ri, 0)),
            pl.BlockSpec((None, GATE_ROWS, r), lambda bi, ri: (bi, 0, ri)),
        ],
        out_shape=[
            jax.ShapeDtypeStruct((b, s, LANES), F32),
            jax.ShapeDtypeStruct((b, GATE_ROWS, s), F32),
        ],
        scratch_shapes=[pltpu.VMEM((FOX_HEADS, LANES), F32), pltpu.VMEM((MLSTM_HEADS, LANES), F32)],
        compiler_params=_params(("arbitrary", "arbitrary")),
        name="gates",
    )(gt, bias_rows)


def _fox_kernel(it_ref, jt_ref, qt_ref, k_ref, vt_ref, gc_ref, zt_ref, nwb_ref, sel_ref, o_ref, m_ref,
                acc_ref, st_ref):
    step_id = pl.program_id(1)
    i = it_ref[step_id]
    j = jt_ref[step_id]
    t = ATT_T
    hd = FOX_HEAD_DIM

    @pl.when(j == 0)
    def _():
        m_ref[...] = jnp.full_like(m_ref, NEG_INF)
        acc_ref[...] = jnp.zeros_like(acc_ref)

    def step(modes):
        gc = gc_ref[...]
        lane = lax.broadcasted_iota(jnp.int32, gc.shape, 1)
        hi = gc.astype(BF16).astype(F32)
        r1 = gc - hi
        mid = r1.astype(BF16).astype(F32)
        lo = r1 - mid
        gk = jnp.where(lane < 8, hi,
             jnp.where(lane < 16, pltpu.roll(mid, 8, axis=1),
             jnp.where(lane < 24, pltpu.roll(lo, 16, axis=1), 0.0))).astype(BF16)
        ones = jnp.ones((ONES_ROWS, t), BF16)
        units = [(h, c) for h in range(FOX_HEADS) for c in range(ATT_QBLOCKS) if modes[c]]

        def scores(h, c, slot):
            hs = slice(h * hd, (h + 1) * hd)
            sel = jnp.concatenate([sel_ref[h]] * (t // LANES), axis=1)
            st = jnp.dot(jnp.concatenate([k_ref[:, hs], gk], axis=1),
                         jnp.concatenate([qt_ref[hs, c * t:(c + 1) * t], sel], axis=0),
                         preferred_element_type=F32)
            if modes[c] == 'diag':
                valid = (lax.broadcasted_iota(jnp.int32, (t, t), 0)
                         <= lax.broadcasted_iota(jnp.int32, (t, t), 1))
                st = jnp.where(valid, st, NEG_INF)
            st_ref[slot] = st
            return slot, jnp.max(st, axis=0, keepdims=True)

        def probs(h, c, parked):
            slot, st_max = parked
            st = st_ref[slot]
            qs = slice(c * t, (c + 1) * t)
            m_prev = m_ref[h:h + 1, qs]
            m_new = jnp.maximum(m_prev, st_max)
            m_ref[h:h + 1, qs] = m_new
            return jnp.exp2(m_prev - m_new), jnp.exp2(st - m_new).astype(BF16)

        def accumulate(h, c, alpha, pt):
            qs = slice(c * t, (c + 1) * t)
            vt_aug = jnp.concatenate([vt_ref[h * hd:(h + 1) * hd, :], ones], axis=0)
            acc_ref[h, :, qs] = alpha * acc_ref[h, :, qs] + jnp.dot(vt_aug, pt, preferred_element_type=F32)

        st, ap = {}, {}
        for u in range(-ATT_LOOKAHEAD, len(units) + 1):
            if 0 <= u + ATT_LOOKAHEAD < len(units):
                st[u + ATT_LOOKAHEAD] = scores(*units[u + ATT_LOOKAHEAD], (u + ATT_LOOKAHEAD) % 2)
            if 0 <= u < len(units):
                ap[u] = probs(*units[u], st.pop(u))
            if 0 <= u - 1 < len(units):
                accumulate(*units[u - 1], *ap.pop(u - 1))

    first = i * ATT_QBLOCKS

    @pl.when(j < first)
    def _():
        step(('full',) * ATT_QBLOCKS)

    for r in range(ATT_QBLOCKS):
        @pl.when(j == first + r)
        def _(r=r):
            step(tuple(None if c < r else 'diag' if c == r else 'full' for c in range(ATT_QBLOCKS)))
            if r == ATT_QBLOCKS - 1:
                for h in range(FOX_HEADS):
                    hs = slice(h * hd, (h + 1) * hd)
                    for c in range(ATT_QBLOCKS):
                        qs = slice(c * t, (c + 1) * t)
                        acc = acc_ref[h, :, qs]
                        o = acc[:hd] * (1.0 / acc[hd:hd + 1])
                        o = o * lax.rsqrt(jnp.mean(o * o, axis=0, keepdims=True) + NORM_EPS) * nwb_ref[hs, :]
                        z = zt_ref[hs, qs]
                        o_ref[hs, qs] = (o * _silu(z)).astype(BF16)


def _fox(kr, t16, t32, mcol, nwb):
    b, s, _ = kr.shape
    t = ATT_T
    tq = ATT_T * ATT_QBLOCKS
    pairs = [(i, j) for i in range(s // tq) for j in range((i + 1) * ATT_QBLOCKS)]
    it = jnp.asarray([p[0] for p in pairs], jnp.int32)
    jt = jnp.asarray([p[1] for p in pairs], jnp.int32)
    row = lax.broadcasted_iota(jnp.int32, (FOX_HEADS, LANES, LANES), 1)
    head = lax.broadcasted_iota(jnp.int32, (FOX_HEADS, LANES, LANES), 0)
    sel = jnp.where((row < 24) & (row % 8 == head), -1.0, 0.0).astype(BF16)

    grid_spec = pltpu.PrefetchScalarGridSpec(
        num_scalar_prefetch=2,
        grid=(b, len(pairs)),
        in_specs=[
            pl.BlockSpec((None, FOX_WIDTH, tq), lambda bi, p, it, jt: (bi, 0, it[p])),
            pl.BlockSpec((None, t, FOX_WIDTH), lambda bi, p, it, jt: (bi, jt[p], 0)),
            pl.BlockSpec((None, FOX_WIDTH, t), lambda bi, p, it, jt: (bi, 1, jt[p])),
            pl.BlockSpec((None, t, LANES), lambda bi, p, it, jt: (bi, jt[p], 0)),
            pl.BlockSpec((None, FOX_WIDTH, tq), lambda bi, p, it, jt: (bi, 0, it[p])),
            pl.BlockSpec((FOX_WIDTH, t), lambda bi, p, it, jt: (0, 0)),
            pl.BlockSpec((FOX_HEADS, LANES, LANES), lambda bi, p, it, jt: (0, 0, 0)),
        ],
        out_specs=pl.BlockSpec((None, FOX_WIDTH, tq), lambda bi, p, it, jt: (bi, 0, it[p])),
        scratch_shapes=[pltpu.VMEM((FOX_HEADS, tq), F32),
                        pltpu.VMEM((FOX_HEADS, FOX_HEAD_DIM + ONES_ROWS, tq), F32),
                        pltpu.VMEM((2, t, t), F32)],
    )
    return pl.pallas_call(
        _fox_kernel,
        grid_spec=grid_spec,
        out_shape=jax.ShapeDtypeStruct((b, FOX_WIDTH, s), BF16),
        compiler_params=_params(("arbitrary", "arbitrary")),
        name="fox_attention",
    )(it, jt, t16, kr, t16, mcol, t32, nwb, sel)


def _mlstm_conv_tile(c, t, n_chunks, qk_ref, cw_ref, cb_ref, hist_ref):
    n = MLSTM_L
    ls = slice(t * LANES, (t + 1) * LANES)
    u = qk_ref[c * n:(c + 1) * n, ls]
    before = hist_ref[:, ls] if c == 0 else qk_ref[c * n - F32_SUBLANES:c * n, ls]
    ext = jnp.concatenate([before, u], axis=0)
    if c == n_chunks - 1:
        hist_ref[:, ls] = u[n - F32_SUBLANES:n, :]

    def tap(jj):
        back = CONV_WIDTH - 1 - jj
        return pltpu.roll(ext, back, axis=0)[F32_SUBLANES:, :] if back else u

    y = tap(0) * cw_ref[0:1, ls]
    for jj in range(1, CONV_WIDTH):
        y = y + tap(jj) * cw_ref[jj:jj + 1, ls]
    y = y + cb_ref[:, ls]
    return _silu(y)


def _mlstm_chunk(c, sy, vt_ref, ogt_ref, zt_ref, mcol_ref, mrow_ref, nwb_ref, write_out, st_ref,
                 after_head=lambda h: None):
    n = MLSTM_L
    dv = MLSTM_V_DIM
    ts = slice(c * n, (c + 1) * n)

    mc = mcol_ref[ts, :]
    mr = mrow_ref[:, ts]
    lane = lax.broadcasted_iota(jnp.int32, (n, LANES), 1)
    ones_rows = jnp.where(lax.broadcasted_iota(jnp.int32, (ONES_ROWS, n), 0) == 0, 1.0, 0.0).astype(BF16)
    half = n // 2
    valid = lax.broadcasted_iota(jnp.int32, (half, half), 0) <= lax.broadcasted_iota(jnp.int32, (half, half), 1)
    heads = range(MLSTM_HEADS)
    row = lambda base_row, h: mr[base_row + h:base_row + h + 1, :]

    q_pair = [sy[pr].astype(BF16) for pr in range(MLSTM_HEADS // 2)]
    k_pair = [sy[MLSTM_QK_WIDTH // LANES + pr] * (MLSTM_QK_DIM ** -0.5) for pr in range(MLSTM_HEADS // 2)]
    qp, kh, st_t, inter_t, vt_aug = [], [], [], [], []
    for h in heads:
        qp.append(q_pair[h // 2])
        kh.append(jnp.where((lane < MLSTM_QK_DIM) if h % 2 == 0 else (lane >= MLSTM_QK_DIM), k_pair[h // 2], 0.0))
        st_t.append(lax.dot_general(kh[h].astype(BF16), qp[h], NT_DIMS, preferred_element_type=F32))
        inter_t.append(lax.dot_general(st_ref[h].astype(BF16), qp[h], NT_DIMS, preferred_element_type=F32))
        vt_aug.append(jnp.concatenate([vt_ref[h * dv:(h + 1) * dv, ts], ones_rows], axis=0))

    for h in heads:
        hs = slice(h * dv, (h + 1) * dv)
        a_col = mc[:, LANE_A + h:LANE_A + h + 1]
        g_row = row(ROW_G, h)
        w_ul = jnp.exp(jnp.where(valid, a_col[:half] + g_row[:, :half], NEG_INF))
        w_ur = jnp.exp(a_col[:half] + g_row[:, half:])
        w_lr = jnp.exp(jnp.where(valid, a_col[half:] + g_row[:, half:], NEG_INF))
        sc_t = jnp.concatenate([
            st_t[h][:half] * jnp.concatenate([w_ul, w_ur], axis=1),
            jnp.concatenate([jnp.zeros((half, half), F32), st_t[h][half:, half:] * w_lr], axis=1)], axis=0)
        pv = jnp.dot(vt_aug[h], sc_t.astype(BF16), preferred_element_type=F32)
        after_head(h)
        res = pv + jnp.exp(row(ROW_U, h)) * inter_t[h]
        den = res[dv:dv + 1, :]
        half_inv = 0.5 / jnp.maximum(jnp.abs(den), jnp.exp(row(ROW_NEGM, h)))
        hb = (res[:dv] * half_inv) * (1.0 + jnp.tanh(0.5 * ogt_ref[hs, ts]))
        hn = hb * lax.rsqrt(jnp.mean(hb * hb, axis=0, keepdims=True) + NORM_EPS) * nwb_ref[hs, :]
        z = zt_ref[hs, ts]
        write_out(hs, ts, (hn * _silu(z)).astype(BF16))

    for h in heads:
        a_col = mc[:, LANE_A + h:LANE_A + h + 1]
        g_last = mc[n - 1:n, LANE_G + h:LANE_G + h + 1]
        u_last = mc[n - 1:n, LANE_U + h:LANE_U + h + 1]
        kw = (kh[h] * jnp.exp(a_col + g_last)).astype(BF16)
        upd = jnp.dot(vt_aug[h], kw, preferred_element_type=F32)
        st_ref[h] = jnp.exp(u_last) * st_ref[h] + upd


def _mlstm_out_kernel(qk_ref, vt_ref, ogt_ref, zt_ref, mcol_ref, mrow_ref, cw_ref, cb_ref, nwb_ref,
                      yat_ref, wo_ref, x_ref, fw_ref, o_ref, hist_ref, st_ref, ybt_ref, lhs_ref,
                      *, per_seq, final):
    step = pl.program_id(0)
    cur = step % 2
    n = MLSTM_L

    @pl.when(step % per_seq == 0)
    def _():
        hist_ref[...] = jnp.zeros_like(hist_ref)
        st_ref[...] = jnp.zeros_like(st_ref)

    @pl.when(step == 0)
    def _():
        ybt_ref[1] = jnp.zeros_like(ybt_ref[1])

    def write_yb(hs, ts, val):
        ybt_ref[cur, hs, ts] = val

    prev = ybt_ref.at[1 - cur]
    n_chunks = FUSE_TM // n
    q_tiles = MLSTM_QK_WIDTH // LANES
    conv_tile = lambda c, t: _mlstm_conv_tile(c, t, n_chunks, qk_ref, cw_ref, cb_ref, hist_ref)
    every = MLSTM_HEADS // FUSE_SLICES
    cols = D_MODEL // FUSE_SLICES
    for c in range(n_chunks):
        rs = slice(c * n, (c + 1) * n)
        sy = [conv_tile(c, t) for t in range(2 * q_tiles)]
        proj = []
        lhs_ref[:, :FOX_WIDTH] = yat_ref[:, rs].T
        lhs_ref[:, FOX_WIDTH:] = prev[:, rs].T

        def woven(h, proj=proj):
            if h % every == 0:
                cs = slice(len(proj) * cols, (len(proj) + 1) * cols)
                proj.append(jnp.dot(lhs_ref[...], wo_ref[:, cs], preferred_element_type=F32))

        _mlstm_chunk(c, sy, vt_ref, ogt_ref, zt_ref, mcol_ref, mrow_ref, nwb_ref, write_yb, st_ref,
                     after_head=woven)
        acc = jnp.concatenate(proj, axis=1)
        r = x_ref[rs, :] + acc
        if final:
            r = r * lax.rsqrt(jnp.mean(r * r, axis=-1, keepdims=True) + NORM_EPS) * fw_ref[...]
        o_ref[rs, :] = r


def _mlstm_out(qkr, t16, t32, mcol, mrow, cw, cb, nwb, yat, wo, x2, fw, final):
    assert FOX_WIDTH == MLSTM_V_WIDTH and wo.shape == (FOX_WIDTH + MLSTM_V_WIDTH, D_MODEL)
    m = x2.shape[0]
    _, _, seq = yat.shape
    tm = FUSE_TM
    per_seq = seq // tm
    blocks = m // tm
    cur = lambda s: jnp.minimum(s, blocks - 1)
    prv = lambda s: jnp.maximum(s - 1, 0)
    t_cur = lambda which: (lambda s: (cur(s) // per_seq, which, cur(s) % per_seq))
    return pl.pallas_call(
        functools.partial(_mlstm_out_kernel, per_seq=per_seq, final=final),
        grid=(blocks + 1,),
        in_specs=[
            pl.BlockSpec((tm, SEG), lambda s: (cur(s), 0)),
            pl.BlockSpec((None, SEG, tm), t_cur(2)),
            pl.BlockSpec((None, SEG, tm), t_cur(1)),
            pl.BlockSpec((None, SEG, tm), t_cur(2)),
            pl.BlockSpec((tm, LANES), lambda s: (cur(s), 0)),
            pl.BlockSpec((None, GATE_ROWS, tm), lambda s: (cur(s) // per_seq, 0, cur(s) % per_seq)),
            pl.BlockSpec((CONV_WIDTH, SEG), lambda s: (0, 0)),
            pl.BlockSpec((1, SEG), lambda s: (0, 0)),
            pl.BlockSpec((SEG, MLSTM_L), lambda s: (0, 0)),
            pl.BlockSpec((None, FOX_WIDTH, tm), lambda s: (prv(s) // per_seq, 0, prv(s) % per_seq)),
            pl.BlockSpec(wo.shape, lambda s: (0, 0)),
            pl.BlockSpec((tm, D_MODEL), lambda s: (prv(s), 0)),
            pl.BlockSpec((1, D_MODEL), lambda s: (0, 0)),
        ],
        out_specs=pl.BlockSpec((tm, D_MODEL), lambda s: (prv(s), 0)),
        out_shape=jax.ShapeDtypeStruct((m, D_MODEL), F32),
        scratch_shapes=[pltpu.VMEM((F32_SUBLANES, SEG), F32),
                        pltpu.VMEM((MLSTM_HEADS, MLSTM_V_DIM + ONES_ROWS, LANES), F32),
                        pltpu.VMEM((2, MLSTM_V_WIDTH, tm), BF16),
                        pltpu.VMEM((MLSTM_L, FOX_WIDTH + MLSTM_V_WIDTH), BF16)],
        compiler_params=_params(("arbitrary",)),
        name="mlstm_outproj",
    )(qkr, t16, t32, t32, mcol, mrow, cw, cb, nwb, yat, wo, x2, fw)


SEG_STARTS = (IN_OFFS[0], IN_OFFS[1], IN_OFFS[5], IN_OFFS[2], IN_OFFS[7], IN_OFFS[3], IN_OFFS[8], IN_OFFS[9])
SEG_SHIFT = FOX_HEADS
assert all(st % SEG in (0, SEG_SHIFT) for st in SEG_STARTS) and IN_OFFS[6] == IN_OFFS[5] + MLSTM_QK_WIDTH
assert SEG_SHIFT == F32_SUBLANES and MLSTM_HEADS == F32_SUBLANES

J_FOX_GATE = 2
assert SEG_STARTS[J_FOX_GATE] - SEG_SHIFT == IN_OFFS[4]


def _wprep_kernel(main_blk_ref, extra_blk_ref, main_ref, extra_ref, mi_ref, mf_ref, wout_ref,
                  o_ref, gw_ref, wo_ref, gwf_ref):
    del main_blk_ref, extra_blk_ref
    j = pl.program_id(0)
    in_seg = j < len(SEG_STARTS)
    shifted = functools.reduce(jnp.logical_or, [j == k for k, st in enumerate(SEG_STARTS) if st % SEG])

    @pl.when(jnp.logical_not(in_seg))
    def _():
        wo_ref[...] = wout_ref[...].astype(BF16)

    @pl.when(j == 0)
    def _():
        gwf_ref[...] = jnp.zeros_like(gwf_ref)
        gwf_ref[GW_MLSTM_I:GW_MLSTM_I + MLSTM_HEADS, :] = mi_ref[...]
        gwf_ref[GW_MLSTM_F:GW_MLSTM_F + MLSTM_HEADS, :] = mf_ref[...]

    @pl.when(in_seg & jnp.logical_not(shifted))
    def _():
        o_ref[...] = main_ref[...].astype(BF16)

    @pl.when(shifted)
    def _():
        full = jnp.concatenate([main_ref[...], extra_ref[...]], axis=0)
        o_ref[...] = full[SEG_SHIFT:SEG_SHIFT + SEG, :].astype(BF16)

    @pl.when(j == J_FOX_GATE)
    def _():
        gwf_ref[GW_FOX_F:GW_FOX_F + FOX_HEADS, :] = main_ref[0:SEG_SHIFT, :]

    @pl.when(j == len(SEG_STARTS) - 1)
    def _():
        gw_ref[...] = gwf_ref[...].astype(BF16)


def _wprep(w_t, w_out):
    d = w_t.shape[1]
    nseg = len(SEG_STARTS)
    out_steps = w_out.shape[0] // SEG
    pad = [SEG_STARTS[-1]] * out_steps
    main_blk = jnp.asarray([st // SEG for st in SEG_STARTS + tuple(pad)], jnp.int32)
    extra_blk = jnp.asarray([(st // SEG + 1) * (SEG // F32_SUBLANES) for st in SEG_STARTS + tuple(pad)],
                            jnp.int32)
    out_blk = lambda j: jnp.clip(j - nseg, 0, out_steps - 1)
    grid_spec = pltpu.PrefetchScalarGridSpec(
        num_scalar_prefetch=2,
        grid=(nseg + out_steps,),
        in_specs=[
            pl.BlockSpec((SEG, d), lambda j, mb, eb: (mb[j], 0)),
            pl.BlockSpec((F32_SUBLANES, d), lambda j, mb, eb: (eb[j], 0)),
            pl.BlockSpec((MLSTM_HEADS, d), lambda j, mb, eb: (IN_OFFS[10] // F32_SUBLANES, 0)),
            pl.BlockSpec((MLSTM_HEADS, d), lambda j, mb, eb: (IN_OFFS[11] // F32_SUBLANES, 0)),
            pl.BlockSpec((SEG, w_out.shape[1]), lambda j, mb, eb: (out_blk(j), 0)),
        ],
        out_specs=[
            pl.BlockSpec((SEG, d), lambda j, mb, eb: (jnp.minimum(j, nseg - 1), 0)),
            pl.BlockSpec((GATE_ROWS, d), lambda j, mb, eb: (0, 0)),
            pl.BlockSpec((SEG, w_out.shape[1]), lambda j, mb, eb: (out_blk(j), 0)),
        ],
        scratch_shapes=[pltpu.VMEM((GATE_ROWS, d), F32)],
    )
    return pl.pallas_call(
        _wprep_kernel,
        grid_spec=grid_spec,
        out_shape=[jax.ShapeDtypeStruct((nseg * SEG, d), BF16),
                   jax.ShapeDtypeStruct((GATE_ROWS, d), BF16),
                   jax.ShapeDtypeStruct(w_out.shape, BF16)],
        compiler_params=_params(("arbitrary",)),
        name="weight_layout",
    )(main_blk, extra_blk, w_t, w_t, w_t, w_t, w_out)


def _lane_broadcast(v, width):
    return jnp.broadcast_to(v[:, None], (v.shape[0], width))


def kernel(x, norm_w, w_in, fox_f_bias, conv_w, conv_b, mlstm_i_bias, mlstm_f_bias,
           fox_out_norm_w, mlstm_out_norm_w, w_out, final_norm_w):
    b, s, d = x.shape
    depth = norm_w.shape[0]
    x2 = x.reshape(b * s, d)
    for l in range(depth):
        w_t, w_gate, wo = _wprep(w_in[l].T, w_out[l])
        bias_rows = _lane_broadcast(jnp.concatenate([
            fox_f_bias[l], mlstm_f_bias[l], mlstm_i_bias[l], jnp.zeros((GATE_ROWS - 24,), F32)]), LANES)

        kr, qkr, t16, t32, g = _inproj(x2, norm_w[l].reshape(1, d), w_t, w_gate, s)
        mcol, mrow = _gates(g, bias_rows)

        yat = _fox(kr.reshape(b, s, SEG), t16, t32, mcol, _lane_broadcast(fox_out_norm_w[l], ATT_T))
        x2 = _mlstm_out(qkr, t16, t32, mcol.reshape(b * s, LANES), mrow, conv_w[l], conv_b[l].reshape(1, SEG),
                        _lane_broadcast(mlstm_out_norm_w[l], MLSTM_L),
                        yat, wo, x2, final_norm_w.reshape(1, d),
                        final=(l == depth - 1))
    return x2.reshape(b, s, d)
```

```python
import functools

import numpy as np
import jax
import jax.numpy as jnp
from jax import lax
from jax.experimental import pallas as pl
from jax.experimental.pallas import tpu as pltpu

D_MODEL = 2048
NORM_EPS = 1e-6
NEG_INF = -1e30

FOX_HEADS = 8
FOX_HEAD_DIM = 128
FOX_WIDTH = FOX_HEADS * FOX_HEAD_DIM
MLSTM_HEADS = 8
MLSTM_V_DIM = 128
MLSTM_QK_DIM = 64
MLSTM_V_WIDTH = MLSTM_HEADS * MLSTM_V_DIM
MLSTM_QK_WIDTH = MLSTM_HEADS * MLSTM_QK_DIM
CONV_WIDTH = 4
IN_SIZES = (FOX_WIDTH, FOX_WIDTH, FOX_WIDTH, FOX_WIDTH, FOX_HEADS,
            MLSTM_QK_WIDTH, MLSTM_QK_WIDTH, MLSTM_V_WIDTH, MLSTM_V_WIDTH, MLSTM_V_WIDTH,
            MLSTM_HEADS, MLSTM_HEADS)
IN_OFFS = tuple(int(v) for v in np.cumsum((0,) + IN_SIZES))

LANES = 128
F32_SUBLANES = 8
SEG = 1024
VMEM_LIMIT = 56 * 1024 * 1024

PROJ_TM = 256
GATE_R = 2048
MLSTM_L = 256
ATT_T = 512
FUSE_TM = 512
FUSE_SLICES = 8

LOG2E = 1.4426950408889634
FOX_Q_SCALE = FOX_HEAD_DIM ** -0.5 * LOG2E
ONES_ROWS = 16
ATT_QBLOCKS = 2
ATT_LOOKAHEAD = 1

GW_FOX_F, GW_MLSTM_F, GW_MLSTM_I = 0, 8, 16
GATE_ROWS = 32
ROW_G, ROW_U, ROW_A, ROW_NEGM = 0, 8, 16, 24
LANE_FOX = 0
LANE_G, LANE_U, LANE_A, LANE_NEGM = (8 + r for r in (ROW_G, ROW_U, ROW_A, ROW_NEGM))

BF16 = jnp.bfloat16
F32 = jnp.float32
NT_DIMS = (((1,), (1,)), ((), ()))


def _sigmoid(x):
    return 1.0 / (1.0 + jnp.exp(-x))


def _silu(x):
    h = 0.5 * x
    return h * (1.0 + jnp.tanh(h))


def _params(sem):
    return pltpu.CompilerParams(dimension_semantics=sem, vmem_limit_bytes=VMEM_LIMIT)


def _inproj_kernel(x_ref, nw_ref, wt_ref, wg_ref, kr_ref, qkr_ref, t16_ref, t32_ref, g_ref):
    xf = x_ref[...]
    ms = jnp.mean(xf * xf, axis=-1, keepdims=True)
    hb = (xf * lax.rsqrt(ms + NORM_EPS) * nw_ref[...]).astype(BF16)

    def seg_t(j):
        return lax.dot_general(wt_ref[j * SEG:(j + 1) * SEG, :], hb, NT_DIMS, preferred_element_type=F32)

    t16_ref[0:SEG, :] = (seg_t(0) * FOX_Q_SCALE).astype(BF16)
    kr_ref[...] = seg_t(1).T.astype(BF16)
    qkr_ref[...] = seg_t(2).T
    t16_ref[SEG:2 * SEG, :] = seg_t(3).astype(BF16)
    t16_ref[2 * SEG:3 * SEG, :] = seg_t(4).astype(BF16)
    for r in range(2):
        t32_ref[r * SEG:(r + 1) * SEG, :] = seg_t(5 + r)
    last = jnp.concatenate([wt_ref[7 * SEG:8 * SEG, :], wg_ref[...]], axis=0)
    acc = lax.dot_general(last, hb, NT_DIMS, preferred_element_type=F32)
    t32_ref[2 * SEG:3 * SEG, :] = acc[:SEG]
    g_ref[...] = acc[SEG:]


def _inproj(x2, nw, w_t, w_gate, seq):
    m = x2.shape[0]
    tm = PROJ_TM
    per_seq = seq // tm
    t_map = lambda i: (i // per_seq, 0, i % per_seq)
    return pl.pallas_call(
        _inproj_kernel,
        grid=(m // tm,),
        in_specs=[
            pl.BlockSpec((tm, D_MODEL), lambda i: (i, 0)),
            pl.BlockSpec((1, D_MODEL), lambda i: (0, 0)),
            pl.BlockSpec(w_t.shape, lambda i: (0, 0)),
            pl.BlockSpec((GATE_ROWS, D_MODEL), lambda i: (0, 0)),
        ],
        out_specs=[
            pl.BlockSpec((tm, SEG), lambda i: (i, 0)),
            pl.BlockSpec((tm, SEG), lambda i: (i, 0)),
            pl.BlockSpec((None, 3 * SEG, tm), t_map),
            pl.BlockSpec((None, 3 * SEG, tm), t_map),
            pl.BlockSpec((None, GATE_ROWS, tm), t_map),
        ],
        out_shape=[
            jax.ShapeDtypeStruct((m, SEG), BF16),
            jax.ShapeDtypeStruct((m, SEG), F32),
            jax.ShapeDtypeStruct((m // seq, 3 * SEG, seq), BF16),
            jax.ShapeDtypeStruct((m // seq, 3 * SEG, seq), F32),
            jax.ShapeDtypeStruct((m // seq, GATE_ROWS, seq), F32),
        ],
        compiler_params=_params(("arbitrary",)),
        name="inproj",
    )(x2, nw, w_t, w_gate)


def _split3(v):
    hi = v.astype(BF16)
    r1 = v - hi.astype(F32)
    mid = r1.astype(BF16)
    lo = (r1 - mid.astype(F32)).astype(BF16)
    return hi, mid, lo


def _gates_kernel(g_ref, b_ref, mcol_ref, mrow_ref, carry_ref, mprev_ref):
    width, chunk = GATE_R, MLSTM_L
    nh = FOX_HEADS
    lanes_of = lambda tile, n: jnp.concatenate([tile] * (n // LANES), axis=1)

    @pl.when(pl.program_id(1) == 0)
    def _():
        carry_ref[...] = jnp.zeros_like(carry_ref)
        mprev_ref[...] = jnp.zeros_like(mprev_ref)

    v = g_ref[...] + lanes_of(b_ref[...], width)
    vf = v[GW_FOX_F:GW_MLSTM_F + nh]
    ls = jnp.minimum(vf, 0.0) - jnp.log1p(jnp.exp(-jnp.abs(vf)))

    upper = (lax.broadcasted_iota(jnp.int32, (chunk, chunk), 0)
             <= lax.broadcasted_iota(jnp.int32, (chunk, chunk), 1)).astype(BF16)
    hi, mid, lo = _split3(ls)
    cuml = jnp.concatenate([
        jnp.dot(hi[:, c * chunk:(c + 1) * chunk], upper, preferred_element_type=F32)
        + jnp.dot(mid[:, c * chunk:(c + 1) * chunk], upper, preferred_element_type=F32)
        + jnp.dot(lo[:, c * chunk:(c + 1) * chunk], upper, preferred_element_type=F32)
        for c in range(width // chunk)], axis=1)
    b_all = cuml[GW_MLSTM_F:GW_MLSTM_F + nh]
    a = v[GW_MLSTM_I:GW_MLSTM_I + nh] - b_all

    t_in = lax.broadcasted_iota(jnp.int32, (nh, width), 1) % chunk
    cm = a
    s = 1
    while s < chunk:
        cm = jnp.where(t_in >= s, jnp.maximum(cm, pltpu.roll(cm, s, axis=1)), cm)
        s *= 2

    last = lambda x: jnp.broadcast_to(x[:, chunk - 1:chunk], (nh, LANES))
    carry = carry_ref[...]
    mprev = mprev_ref[...]
    for c in range(width // chunk):
        sl = slice(c * chunk, (c + 1) * chunk)
        cumf = cuml[GW_FOX_F:GW_FOX_F + nh, sl] + lanes_of(carry, chunk)
        carry = last(cumf)
        b = b_all[:, sl]
        mp = lanes_of(mprev, chunk)
        mt = b + jnp.maximum(mp, cm[:, sl])
        rows = {ROW_G: b - mt, ROW_U: (b + mp) - mt, ROW_A: a[:, sl], ROW_NEGM: -mt}
        mprev = last(mt)
        for r0, val in rows.items():
            mrow_ref[r0:r0 + nh, sl] = val
        tile_t = jnp.concatenate([cumf * LOG2E] + [rows[r0] for r0 in (ROW_G, ROW_U, ROW_A, ROW_NEGM)]
                                 + [jnp.zeros((LANES - 5 * nh, chunk), F32)], axis=0)
        mcol_ref[sl, :] = tile_t.T
    carry_ref[...] = carry
    mprev_ref[...] = mprev


def _gates(gt, bias_rows):
    b, _, s = gt.shape
    r = GATE_R
    return pl.pallas_call(
        _gates_kernel,
        grid=(b, s // r),
        in_specs=[
            pl.BlockSpec((None, GATE_ROWS, r), lambda bi, ri: (bi, 0, ri)),
            pl.BlockSpec((GATE_ROWS, LANES), lambda bi, ri: (0, 0)),
        ],
        out_specs=[
            pl.BlockSpec((None, r, LANES), lambda bi, ri: (bi, ri, 0)),
            pl.BlockSpec((None, GATE_ROWS, r), lambda bi, ri: (bi, 0, ri)),
        ],
        out_shape=[
            jax.ShapeDtypeStruct((b, s, LANES), F32),
            jax.ShapeDtypeStruct((b, GATE_ROWS, s), F32),
        ],
        scratch_shapes=[pltpu.VMEM((FOX_HEADS, LANES), F32), pltpu.VMEM((MLSTM_HEADS, LANES), F32)],
        compiler_params=_params(("arbitrary", "arbitrary")),
        name="gates",
    )(gt, bias_rows)


def _fox_kernel(it_ref, jt_ref, qt_ref, k_ref, vt_ref, gc_ref, zt_ref, nwb_ref, sel_ref, o_ref, m_ref,
                acc_ref, st_ref):
    step_id = pl.program_id(1)
    i = it_ref[step_id]
    j = jt_ref[step_id]
    t = ATT_T
    hd = FOX_HEAD_DIM

    @pl.when(j == 0)
    def _():
        m_ref[...] = jnp.full_like(m_ref, NEG_INF)
        acc_ref[...] = jnp.zeros_like(acc_ref)

    def step(modes):
        gc = gc_ref[...]
        lane = lax.broadcasted_iota(jnp.int32, gc.shape, 1)
        hi = gc.astype(BF16).astype(F32)
        r1 = gc - hi
        mid = r1.astype(BF16).astype(F32)
        lo = r1 - mid
        gk = jnp.where(lane < 8, hi,
             jnp.where(lane < 16, pltpu.roll(mid, 8, axis=1),
             jnp.where(lane < 24, pltpu.roll(lo, 16, axis=1), 0.0))).astype(BF16)
        ones = jnp.ones((ONES_ROWS, t), BF16)
        units = [(h, c, kb) for h in range(FOX_HEADS) for c in range(ATT_QBLOCKS)
                 for kb in range(ATT_QBLOCKS) if modes[c][kb]]

        def scores(h, c, kb, slot):
            hs = slice(h * hd, (h + 1) * hd)
            ks = slice(kb * t, (kb + 1) * t)
            sel = jnp.concatenate([sel_ref[h]] * (t // LANES), axis=1)
            st = jnp.dot(jnp.concatenate([k_ref[ks, hs], gk[ks]], axis=1),
                         jnp.concatenate([qt_ref[hs, c * t:(c + 1) * t], sel], axis=0),
                         preferred_element_type=F32)
            if modes[c][kb] == 'diag':
                valid = (lax.broadcasted_iota(jnp.int32, (t, t), 0)
                         <= lax.broadcasted_iota(jnp.int32, (t, t), 1))
                st = jnp.where(valid, st, NEG_INF)
            st_ref[slot] = st
            return slot, jnp.max(st, axis=0, keepdims=True)

        def probs(h, c, kb, parked):
            slot, st_max = parked
            st = st_ref[slot]
            qs = slice(c * t, (c + 1) * t)
            m_prev = m_ref[h:h + 1, qs]
            m_new = jnp.maximum(m_prev, st_max)
            m_ref[h:h + 1, qs] = m_new
            return jnp.exp2(m_prev - m_new), jnp.exp2(st - m_new).astype(BF16)

        def accumulate(h, c, kb, alpha, pt):
            qs = slice(c * t, (c + 1) * t)
            vt_aug = jnp.concatenate([vt_ref[h * hd:(h + 1) * hd, kb * t:(kb + 1) * t], ones], axis=0)
            acc_ref[h, :, qs] = alpha * acc_ref[h, :, qs] + jnp.dot(vt_aug, pt, preferred_element_type=F32)

        st, ap = {}, {}
        for u in range(-ATT_LOOKAHEAD, len(units) + 1):
            if 0 <= u + ATT_LOOKAHEAD < len(units):
                st[u + ATT_LOOKAHEAD] = scores(*units[u + ATT_LOOKAHEAD], (u + ATT_LOOKAHEAD) % (ATT_LOOKAHEAD + 1))
            if 0 <= u < len(units):
                ap[u] = probs(*units[u], st.pop(u))
            if 0 <= u - 1 < len(units):
                accumulate(*units[u - 1], *ap.pop(u - 1))

    @pl.when(j < i)
    def _():
        step((('full',) * ATT_QBLOCKS,) * ATT_QBLOCKS)

    @pl.when(j == i)
    def _():
        step(tuple(tuple('full' if kb < c else 'diag' if kb == c else None for kb in range(ATT_QBLOCKS))
                   for c in range(ATT_QBLOCKS)))
        for h in range(FOX_HEADS):
            hs = slice(h * hd, (h + 1) * hd)
            for c in range(ATT_QBLOCKS):
                qs = slice(c * t, (c + 1) * t)
                acc = acc_ref[h, :, qs]
                o = acc[:hd] * (1.0 / acc[hd:hd + 1])
                o = o * lax.rsqrt(jnp.mean(o * o, axis=0, keepdims=True) + NORM_EPS) * nwb_ref[hs, :]
                z = zt_ref[hs, qs]
                o_ref[hs, qs] = (o * _silu(z)).astype(BF16)


def _fox(kr, t16, t32, mcol, nwb):
    b, s, _ = kr.shape
    t = ATT_T
    tq = ATT_T * ATT_QBLOCKS
    pairs = [(i, j) for i in range(s // tq) for j in range(i + 1)]
    it = jnp.asarray([p[0] for p in pairs], jnp.int32)
    jt = jnp.asarray([p[1] for p in pairs], jnp.int32)
    row = lax.broadcasted_iota(jnp.int32, (FOX_HEADS, LANES, LANES), 1)
    head = lax.broadcasted_iota(jnp.int32, (FOX_HEADS, LANES, LANES), 0)
    sel = jnp.where((row < 24) & (row % 8 == head), -1.0, 0.0).astype(BF16)

    grid_spec = pltpu.PrefetchScalarGridSpec(
        num_scalar_prefetch=2,
        grid=(b, len(pairs)),
        in_specs=[
            pl.BlockSpec((None, FOX_WIDTH, tq), lambda bi, p, it, jt: (bi, 0, it[p])),
            pl.BlockSpec((None, tq, FOX_WIDTH), lambda bi, p, it, jt: (bi, jt[p], 0)),
            pl.BlockSpec((None, FOX_WIDTH, tq), lambda bi, p, it, jt: (bi, 1, jt[p])),
            pl.BlockSpec((None, tq, LANES), lambda bi, p, it, jt: (bi, jt[p], 0)),
            pl.BlockSpec((None, FOX_WIDTH, tq), lambda bi, p, it, jt: (bi, 0, it[p])),
            pl.BlockSpec((FOX_WIDTH, t), lambda bi, p, it, jt: (0, 0)),
            pl.BlockSpec((FOX_HEADS, LANES, LANES), lambda bi, p, it, jt: (0, 0, 0)),
        ],
        out_specs=pl.BlockSpec((None, FOX_WIDTH, tq), lambda bi, p, it, jt: (bi, 0, it[p])),
        scratch_shapes=[pltpu.VMEM((FOX_HEADS, tq), F32),
                        pltpu.VMEM((FOX_HEADS, FOX_HEAD_DIM + ONES_ROWS, tq), F32),
                        pltpu.VMEM((ATT_LOOKAHEAD + 1, t, t), F32)],
    )
    return pl.pallas_call(
        _fox_kernel,
        grid_spec=grid_spec,
        out_shape=jax.ShapeDtypeStruct((b, FOX_WIDTH, s), BF16),
        compiler_params=_params(("arbitrary", "arbitrary")),
        name="fox_attention",
    )(it, jt, t16, kr, t16, mcol, t32, nwb, sel)


def _mlstm_conv_tile(c, t, n_chunks, qk_ref, cw_ref, cb_ref, hist_ref):
    n = MLSTM_L
    ls = slice(t * LANES, (t + 1) * LANES)
    u = qk_ref[c * n:(c + 1) * n, ls]
    before = hist_ref[:, ls] if c == 0 else qk_ref[c * n - F32_SUBLANES:c * n, ls]
    ext = jnp.concatenate([before, u], axis=0)
    if c == n_chunks - 1:
        hist_ref[:, ls] = u[n - F32_SUBLANES:n, :]

    def tap(jj):
        back = CONV_WIDTH - 1 - jj
        return pltpu.roll(ext, back, axis=0)[F32_SUBLANES:, :] if back else u

    y = tap(0) * cw_ref[0:1, ls]
    for jj in range(1, CONV_WIDTH):
        y = y + tap(jj) * cw_ref[jj:jj + 1, ls]
    y = y + cb_ref[:, ls]
    return _silu(y)


def _mlstm_chunk(c, sy, vt_ref, ogt_ref, zt_ref, mcol_ref, mrow_ref, nwb_ref, write_out, st_ref,
                 after_head=lambda h: None):
    n = MLSTM_L
    dv = MLSTM_V_DIM
    ts = slice(c * n, (c + 1) * n)

    mc = mcol_ref[ts, :]
    mr = mrow_ref[:, ts]
    lane = lax.broadcasted_iota(jnp.int32, (n, LANES), 1)
    ones_rows = jnp.where(lax.broadcasted_iota(jnp.int32, (ONES_ROWS, n), 0) == 0, 1.0, 0.0).astype(BF16)
    half = n // 2
    valid = lax.broadcasted_iota(jnp.int32, (half, half), 0) <= lax.broadcasted_iota(jnp.int32, (half, half), 1)
    heads = range(MLSTM_HEADS)
    row = lambda base_row, h: mr[base_row + h:base_row + h + 1, :]

    q_pair = [sy[pr].astype(BF16) for pr in range(MLSTM_HEADS // 2)]
    k_pair = [sy[MLSTM_QK_WIDTH // LANES + pr] * (MLSTM_QK_DIM ** -0.5) for pr in range(MLSTM_HEADS // 2)]
    qp, kh, st_t, inter_t, vt_aug = [], [], [], [], []
    for h in heads:
        qp.append(q_pair[h // 2])
        kh.append(jnp.where((lane < MLSTM_QK_DIM) if h % 2 == 0 else (lane >= MLSTM_QK_DIM), k_pair[h // 2], 0.0))
        st_t.append(lax.dot_general(kh[h].astype(BF16), qp[h], NT_DIMS, preferred_element_type=F32))
        inter_t.append(lax.dot_general(st_ref[h].astype(BF16), qp[h], NT_DIMS, preferred_element_type=F32))
        vt_aug.append(jnp.concatenate([vt_ref[h * dv:(h + 1) * dv, ts], ones_rows], axis=0))

    for h in heads:
        hs = slice(h * dv, (h + 1) * dv)
        a_col = mc[:, LANE_A + h:LANE_A + h + 1]
        g_row = row(ROW_G, h)
        w_ul = jnp.exp(jnp.where(valid, a_col[:half] + g_row[:, :half], NEG_INF))
        w_ur = jnp.exp(a_col[:half] + g_row[:, half:])
        w_lr = jnp.exp(jnp.where(valid, a_col[half:] + g_row[:, half:], NEG_INF))
        sc_t = jnp.concatenate([
            st_t[h][:half] * jnp.concatenate([w_ul, w_ur], axis=1),
            jnp.concatenate([jnp.zeros((half, half), F32), st_t[h][half:, half:] * w_lr], axis=1)], axis=0)
        pv = jnp.dot(vt_aug[h], sc_t.astype(BF16), preferred_element_type=F32)
        after_head(h)
        res = pv + jnp.exp(row(ROW_U, h)) * inter_t[h]
        den = res[dv:dv + 1, :]
        half_inv = 0.5 / jnp.maximum(jnp.abs(den), jnp.exp(row(ROW_NEGM, h)))
        hb = (res[:dv] * half_inv) * (1.0 + jnp.tanh(0.5 * ogt_ref[hs, ts]))
        hn = hb * lax.rsqrt(jnp.mean(hb * hb, axis=0, keepdims=True) + NORM_EPS) * nwb_ref[hs, :]
        z = zt_ref[hs, ts]
        write_out(hs, ts, (hn * _silu(z)).astype(BF16))

    for h in heads:
        a_col = mc[:, LANE_A + h:LANE_A + h + 1]
        g_last = mc[n - 1:n, LANE_G + h:LANE_G + h + 1]
        u_last = mc[n - 1:n, LANE_U + h:LANE_U + h + 1]
        kw = (kh[h] * jnp.exp(a_col + g_last)).astype(BF16)
        upd = jnp.dot(vt_aug[h], kw, preferred_element_type=F32)
        st_ref[h] = jnp.exp(u_last) * st_ref[h] + upd


def _mlstm_out_kernel(qk_ref, vt_ref, ogt_ref, zt_ref, mcol_ref, mrow_ref, cw_ref, cb_ref, nwb_ref,
                      yat_ref, wo_ref, x_ref, fw_ref, o_ref, hist_ref, st_ref, ybt_ref, lhs_ref,
                      *, per_seq, final):
    step = pl.program_id(0)
    cur = step % 2
    n = MLSTM_L

    @pl.when(step % per_seq == 0)
    def _():
        hist_ref[...] = jnp.zeros_like(hist_ref)
        st_ref[...] = jnp.zeros_like(st_ref)

    @pl.when(step == 0)
    def _():
        ybt_ref[1] = jnp.zeros_like(ybt_ref[1])

    def write_yb(hs, ts, val):
        ybt_ref[cur, hs, ts] = val

    prev = ybt_ref.at[1 - cur]
    n_chunks = FUSE_TM // n
    q_tiles = MLSTM_QK_WIDTH // LANES
    conv_tile = lambda c, t: _mlstm_conv_tile(c, t, n_chunks, qk_ref, cw_ref, cb_ref, hist_ref)
    every = MLSTM_HEADS // FUSE_SLICES
    cols = D_MODEL // FUSE_SLICES
    for c in range(n_chunks):
        rs = slice(c * n, (c + 1) * n)
        sy = [conv_tile(c, t) for t in range(2 * q_tiles)]
        proj = []
        lhs_ref[:, :FOX_WIDTH] = yat_ref[:, rs].T
        lhs_ref[:, FOX_WIDTH:] = prev[:, rs].T

        def woven(h, proj=proj):
            if h % every == 0:
                cs = slice(len(proj) * cols, (len(proj) + 1) * cols)
                proj.append(jnp.dot(lhs_ref[...], wo_ref[:, cs], preferred_element_type=F32))

        _mlstm_chunk(c, sy, vt_ref, ogt_ref, zt_ref, mcol_ref, mrow_ref, nwb_ref, write_yb, st_ref,
                     after_head=woven)
        acc = jnp.concatenate(proj, axis=1)
        r = x_ref[rs, :] + acc
        if final:
            r = r * lax.rsqrt(jnp.mean(r * r, axis=-1, keepdims=True) + NORM_EPS) * fw_ref[...]
        o_ref[rs, :] = r


def _mlstm_out(qkr, t16, t32, mcol, mrow, cw, cb, nwb, yat, wo, x2, fw, final):
    assert FOX_WIDTH == MLSTM_V_WIDTH and wo.shape == (FOX_WIDTH + MLSTM_V_WIDTH, D_MODEL)
    m = x2.shape[0]
    _, _, seq = yat.shape
    tm = FUSE_TM
    per_seq = seq // tm
    blocks = m // tm
    cur = lambda s: jnp.minimum(s, blocks - 1)
    prv = lambda s: jnp.maximum(s - 1, 0)
    t_cur = lambda which: (lambda s: (cur(s) // per_seq, which, cur(s) % per_seq))
    return pl.pallas_call(
        functools.partial(_mlstm_out_kernel, per_seq=per_seq, final=final),
        grid=(blocks + 1,),
        in_specs=[
            pl.BlockSpec((tm, SEG), lambda s: (cur(s), 0)),
            pl.BlockSpec((None, SEG, tm), t_cur(2)),
            pl.BlockSpec((None, SEG, tm), t_cur(1)),
            pl.BlockSpec((None, SEG, tm), t_cur(2)),
            pl.BlockSpec((tm, LANES), lambda s: (cur(s), 0)),
            pl.BlockSpec((None, GATE_ROWS, tm), lambda s: (cur(s) // per_seq, 0, cur(s) % per_seq)),
            pl.BlockSpec((CONV_WIDTH, SEG), lambda s: (0, 0)),
            pl.BlockSpec((1, SEG), lambda s: (0, 0)),
            pl.BlockSpec((SEG, MLSTM_L), lambda s: (0, 0)),
            pl.BlockSpec((None, FOX_WIDTH, tm), lambda s: (prv(s) // per_seq, 0, prv(s) % per_seq)),
            pl.BlockSpec(wo.shape, lambda s: (0, 0)),
            pl.BlockSpec((tm, D_MODEL), lambda s: (prv(s), 0)),
            pl.BlockSpec((1, D_MODEL), lambda s: (0, 0)),
        ],
        out_specs=pl.BlockSpec((tm, D_MODEL), lambda s: (prv(s), 0)),
        out_shape=jax.ShapeDtypeStruct((m, D_MODEL), F32),
        scratch_shapes=[pltpu.VMEM((F32_SUBLANES, SEG), F32),
                        pltpu.VMEM((MLSTM_HEADS, MLSTM_V_DIM + ONES_ROWS, LANES), F32),
                        pltpu.VMEM((2, MLSTM_V_WIDTH, tm), BF16),
                        pltpu.VMEM((MLSTM_L, FOX_WIDTH + MLSTM_V_WIDTH), BF16)],
        compiler_params=_params(("arbitrary",)),
        name="mlstm_outproj",
    )(qkr, t16, t32, t32, mcol, mrow, cw, cb, nwb, yat, wo, x2, fw)


SEG_STARTS = (IN_OFFS[0], IN_OFFS[1], IN_OFFS[5], IN_OFFS[2], IN_OFFS[7], IN_OFFS[3], IN_OFFS[8], IN_OFFS[9])
SEG_SHIFT = FOX_HEADS
assert all(st % SEG in (0, SEG_SHIFT) for st in SEG_STARTS) and IN_OFFS[6] == IN_OFFS[5] + MLSTM_QK_WIDTH
assert SEG_SHIFT == F32_SUBLANES and MLSTM_HEADS == F32_SUBLANES

J_FOX_GATE = 2
assert SEG_STARTS[J_FOX_GATE] - SEG_SHIFT == IN_OFFS[4]


def _wprep_kernel(main_blk_ref, extra_blk_ref, main_ref, extra_ref, mi_ref, mf_ref, wout_ref,
                  o_ref, gw_ref, wo_ref, gwf_ref):
    del main_blk_ref, extra_blk_ref
    j = pl.program_id(0)
    in_seg = j < len(SEG_STARTS)
    shifted = functools.reduce(jnp.logical_or, [j == k for k, st in enumerate(SEG_STARTS) if st % SEG])

    @pl.when(jnp.logical_not(in_seg))
    def _():
        wo_ref[...] = wout_ref[...].astype(BF16)

    @pl.when(j == 0)
    def _():
        gwf_ref[...] = jnp.zeros_like(gwf_ref)
        gwf_ref[GW_MLSTM_I:GW_MLSTM_I + MLSTM_HEADS, :] = mi_ref[...]
        gwf_ref[GW_MLSTM_F:GW_MLSTM_F + MLSTM_HEADS, :] = mf_ref[...]

    @pl.when(in_seg & jnp.logical_not(shifted))
    def _():
        o_ref[...] = main_ref[...].astype(BF16)

    @pl.when(shifted)
    def _():
        full = jnp.concatenate([main_ref[...], extra_ref[...]], axis=0)
        o_ref[...] = full[SEG_SHIFT:SEG_SHIFT + SEG, :].astype(BF16)

    @pl.when(j == J_FOX_GATE)
    def _():
        gwf_ref[GW_FOX_F:GW_FOX_F + FOX_HEADS, :] = main_ref[0:SEG_SHIFT, :]

    @pl.when(j == len(SEG_STARTS) - 1)
    def _():
        gw_ref[...] = gwf_ref[...].astype(BF16)


def _wprep(w_t, w_out):
    d = w_t.shape[1]
    nseg = len(SEG_STARTS)
    out_steps = w_out.shape[0] // SEG
    pad = [SEG_STARTS[-1]] * out_steps
    main_blk = jnp.asarray([st // SEG for st in SEG_STARTS + tuple(pad)], jnp.int32)
    extra_blk = jnp.asarray([(st // SEG + 1) * (SEG // F32_SUBLANES) for st in SEG_STARTS + tuple(pad)],
                            jnp.int32)
    out_blk = lambda j: jnp.clip(j - nseg, 0, out_steps - 1)
    grid_spec = pltpu.PrefetchScalarGridSpec(
        num_scalar_prefetch=2,
        grid=(nseg + out_steps,),
        in_specs=[
            pl.BlockSpec((SEG, d), lambda j, mb, eb: (mb[j], 0)),
            pl.BlockSpec((F32_SUBLANES, d), lambda j, mb, eb: (eb[j], 0)),
            pl.BlockSpec((MLSTM_HEADS, d), lambda j, mb, eb: (IN_OFFS[10] // F32_SUBLANES, 0)),
            pl.BlockSpec((MLSTM_HEADS, d), lambda j, mb, eb: (IN_OFFS[11] // F32_SUBLANES, 0)),
            pl.BlockSpec((SEG, w_out.shape[1]), lambda j, mb, eb: (out_blk(j), 0)),
        ],
        out_specs=[
            pl.BlockSpec((SEG, d), lambda j, mb, eb: (jnp.minimum(j, nseg - 1), 0)),
            pl.BlockSpec((GATE_ROWS, d), lambda j, mb, eb: (0, 0)),
            pl.BlockSpec((SEG, w_out.shape[1]), lambda j, mb, eb: (out_blk(j), 0)),
        ],
        scratch_shapes=[pltpu.VMEM((GATE_ROWS, d), F32)],
    )
    return pl.pallas_call(
        _wprep_kernel,
        grid_spec=grid_spec,
        out_shape=[jax.ShapeDtypeStruct((nseg * SEG, d), BF16),
                   jax.ShapeDtypeStruct((GATE_ROWS, d), BF16),
                   jax.ShapeDtypeStruct(w_out.shape, BF16)],
        compiler_params=_params(("arbitrary",)),
        name="weight_layout",
    )(main_blk, extra_blk, w_t, w_t, w_t, w_t, w_out)


def _lane_broadcast(v, width):
    return jnp.broadcast_to(v[:, None], (v.shape[0], width))


def kernel(x, norm_w, w_in, fox_f_bias, conv_w, conv_b, mlstm_i_bias, mlstm_f_bias,
           fox_out_norm_w, mlstm_out_norm_w, w_out, final_norm_w):
    b, s, d = x.shape
    depth = norm_w.shape[0]
    x2 = x.reshape(b * s, d)
    for l in range(depth):
        w_t, w_gate, wo = _wprep(w_in[l].T, w_out[l])
        bias_rows = _lane_broadcast(jnp.concatenate([
            fox_f_bias[l], mlstm_f_bias[l], mlstm_i_bias[l], jnp.zeros((GATE_ROWS - 24,), F32)]), LANES)

        kr, qkr, t16, t32, g = _inproj(x2, norm_w[l].reshape(1, d), w_t, w_gate, s)
        mcol, mrow = _gates(g, bias_rows)

        yat = _fox(kr.reshape(b, s, SEG), t16, t32, mcol, _lane_broadcast(fox_out_norm_w[l], ATT_T))
        x2 = _mlstm_out(qkr, t16, t32, mcol.reshape(b * s, LANES), mrow, conv_w[l], conv_b[l].reshape(1, SEG),
                        _lane_broadcast(mlstm_out_norm_w[l], MLSTM_L),
                        yat, wo, x2, final_norm_w.reshape(1, d),
                        final=(l == depth - 1))
    return x2.reshape(b, s, d)
```

```python
import functools

import numpy as np
import jax
import jax.numpy as jnp
from jax import lax
from jax.experimental import pallas as pl
from jax.experimental.pallas import tpu as pltpu

D_MODEL = 2048
NORM_EPS = 1e-6
NEG_INF = -1e30

FOX_HEADS = 8
FOX_HEAD_DIM = 128
FOX_WIDTH = FOX_HEADS * FOX_HEAD_DIM
MLSTM_HEADS = 8
MLSTM_V_DIM = 128
MLSTM_QK_DIM = 64
MLSTM_V_WIDTH = MLSTM_HEADS * MLSTM_V_DIM
MLSTM_QK_WIDTH = MLSTM_HEADS * MLSTM_QK_DIM
CONV_WIDTH = 4
IN_SIZES = (FOX_WIDTH, FOX_WIDTH, FOX_WIDTH, FOX_WIDTH, FOX_HEADS,
            MLSTM_QK_WIDTH, MLSTM_QK_WIDTH, MLSTM_V_WIDTH, MLSTM_V_WIDTH, MLSTM_V_WIDTH,
            MLSTM_HEADS, MLSTM_HEADS)
IN_OFFS = tuple(int(v) for v in np.cumsum((0,) + IN_SIZES))

LANES = 128
F32_SUBLANES = 8
SEG = 1024
VMEM_LIMIT = 56 * 1024 * 1024

PROJ_TM = 256
GATE_R = 2048
MLSTM_L = 256
ATT_T = 512
FUSE_TM = 512
FUSE_SLICES = 8

LOG2E = 1.4426950408889634
FOX_Q_SCALE = FOX_HEAD_DIM ** -0.5 * LOG2E
ONES_ROWS = 16
ATT_QBLOCKS = 2
ATT_LOOKAHEAD = 1

GW_FOX_F, GW_MLSTM_F, GW_MLSTM_I = 0, 8, 16
GATE_ROWS = 32
ROW_G, ROW_U, ROW_A, ROW_NEGM = 0, 8, 16, 24
LANE_FOX = 0
LANE_G, LANE_U, LANE_A, LANE_NEGM = (8 + r for r in (ROW_G, ROW_U, ROW_A, ROW_NEGM))

BF16 = jnp.bfloat16
F32 = jnp.float32
NT_DIMS = (((1,), (1,)), ((), ()))


def _sigmoid(x):
    return 1.0 / (1.0 + jnp.exp(-x))


def _silu(x):
    h = 0.5 * x
    return h * (1.0 + jnp.tanh(h))


def _silu_of_half(h):
    return h * (1.0 + jnp.tanh(h))


def _params(sem):
    return pltpu.CompilerParams(dimension_semantics=sem, vmem_limit_bytes=VMEM_LIMIT)


def _inproj_kernel(x_ref, nw_ref, wt_ref, wg_ref, kr_ref, qkr_ref, t16_ref, t32_ref, g_ref):
    xf = x_ref[...]
    ms = jnp.mean(xf * xf, axis=-1, keepdims=True)
    hb = (xf * lax.rsqrt(ms + NORM_EPS) * nw_ref[...]).astype(BF16)

    def seg_t(j):
        return lax.dot_general(wt_ref[j * SEG:(j + 1) * SEG, :], hb, NT_DIMS, preferred_element_type=F32)

    t16_ref[0:SEG, :] = (seg_t(0) * FOX_Q_SCALE).astype(BF16)
    kr_ref[...] = seg_t(1).T.astype(BF16)
    qkr_ref[...] = seg_t(2).T
    t16_ref[SEG:2 * SEG, :] = seg_t(3).astype(BF16)
    t16_ref[2 * SEG:3 * SEG, :] = seg_t(4).astype(BF16)
    for r in range(2):
        t32_ref[r * SEG:(r + 1) * SEG, :] = seg_t(5 + r) * 0.5
    last = jnp.concatenate([wt_ref[7 * SEG:8 * SEG, :], wg_ref[...]], axis=0)
    acc = lax.dot_general(last, hb, NT_DIMS, preferred_element_type=F32)
    t32_ref[2 * SEG:3 * SEG, :] = acc[:SEG] * 0.5
    g_ref[...] = acc[SEG:]


def _inproj(x2, nw, w_t, w_gate, seq):
    m = x2.shape[0]
    tm = PROJ_TM
    per_seq = seq // tm
    t_map = lambda i: (i // per_seq, 0, i % per_seq)
    return pl.pallas_call(
        _inproj_kernel,
        grid=(m // tm,),
        in_specs=[
            pl.BlockSpec((tm, D_MODEL), lambda i: (i, 0)),
            pl.BlockSpec((1, D_MODEL), lambda i: (0, 0)),
            pl.BlockSpec(w_t.shape, lambda i: (0, 0)),
            pl.BlockSpec((GATE_ROWS, D_MODEL), lambda i: (0, 0)),
        ],
        out_specs=[
            pl.BlockSpec((tm, SEG), lambda i: (i, 0)),
            pl.BlockSpec((tm, SEG), lambda i: (i, 0)),
            pl.BlockSpec((None, 3 * SEG, tm), t_map),
            pl.BlockSpec((None, 3 * SEG, tm), t_map),
            pl.BlockSpec((None, GATE_ROWS, tm), t_map),
        ],
        out_shape=[
            jax.ShapeDtypeStruct((m, SEG), BF16),
            jax.ShapeDtypeStruct((m, SEG), F32),
            jax.ShapeDtypeStruct((m // seq, 3 * SEG, seq), BF16),
            jax.ShapeDtypeStruct((m // seq, 3 * SEG, seq), F32),
            jax.ShapeDtypeStruct((m // seq, GATE_ROWS, seq), F32),
        ],
        compiler_params=_params(("arbitrary",)),
        name="inproj",
    )(x2, nw, w_t, w_gate)


def _split3(v):
    hi = v.astype(BF16)
    r1 = v - hi.astype(F32)
    mid = r1.astype(BF16)
    lo = (r1 - mid.astype(F32)).astype(BF16)
    return hi, mid, lo


def _gates_kernel(g_ref, b_ref, mcol_ref, mrow_ref, carry_ref, mprev_ref):
    width, chunk = GATE_R, MLSTM_L
    nh = FOX_HEADS
    lanes_of = lambda tile, n: jnp.concatenate([tile] * (n // LANES), axis=1)

    @pl.when(pl.program_id(1) == 0)
    def _():
        carry_ref[...] = jnp.zeros_like(carry_ref)
        mprev_ref[...] = jnp.zeros_like(mprev_ref)

    v = g_ref[...] + lanes_of(b_ref[...], width)
    vf = v[GW_FOX_F:GW_MLSTM_F + nh]
    ls = jnp.minimum(vf, 0.0) - jnp.log1p(jnp.exp(-jnp.abs(vf)))

    upper = (lax.broadcasted_iota(jnp.int32, (chunk, chunk), 0)
             <= lax.broadcasted_iota(jnp.int32, (chunk, chunk), 1)).astype(BF16)
    hi, mid, lo = _split3(ls)
    cuml = jnp.concatenate([
        jnp.dot(hi[:, c * chunk:(c + 1) * chunk], upper, preferred_element_type=F32)
        + jnp.dot(mid[:, c * chunk:(c + 1) * chunk], upper, preferred_element_type=F32)
        + jnp.dot(lo[:, c * chunk:(c + 1) * chunk], upper, preferred_element_type=F32)
        for c in range(width // chunk)], axis=1)
    b_all = cuml[GW_MLSTM_F:GW_MLSTM_F + nh]
    a = v[GW_MLSTM_I:GW_MLSTM_I + nh] - b_all

    t_in = lax.broadcasted_iota(jnp.int32, (nh, width), 1) % chunk
    cm = a
    s = 1
    while s < chunk:
        cm = jnp.where(t_in >= s, jnp.maximum(cm, pltpu.roll(cm, s, axis=1)), cm)
        s *= 2

    last = lambda x: jnp.broadcast_to(x[:, chunk - 1:chunk], (nh, LANES))
    carry = carry_ref[...]
    mprev = mprev_ref[...]
    for c in range(width // chunk):
        sl = slice(c * chunk, (c + 1) * chunk)
        cumf = cuml[GW_FOX_F:GW_FOX_F + nh, sl] + lanes_of(carry, chunk)
        carry = last(cumf)
        b = b_all[:, sl]
        mp = lanes_of(mprev, chunk)
        mt = b + jnp.maximum(mp, cm[:, sl])
        rows = {ROW_G: b - mt, ROW_U: (b + mp) - mt, ROW_A: a[:, sl], ROW_NEGM: -mt}
        mprev = last(mt)
        for r0, val in rows.items():
            mrow_ref[r0:r0 + nh, sl] = val
        tile_t = jnp.concatenate([cumf * LOG2E] + [rows[r0] for r0 in (ROW_G, ROW_U, ROW_A, ROW_NEGM)]
                                 + [jnp.zeros((LANES - 5 * nh, chunk), F32)], axis=0)
        mcol_ref[sl, :] = tile_t.T
    carry_ref[...] = carry
    mprev_ref[...] = mprev


def _gates(gt, bias_rows):
    b, _, s = gt.shape
    r = GATE_R
    return pl.pallas_call(
        _gates_kernel,
        grid=(b, s // r),
        in_specs=[
            pl.BlockSpec((None, GATE_ROWS, r), lambda bi, ri: (bi, 0, ri)),
            pl.BlockSpec((GATE_ROWS, LANES), lambda bi, ri: (0, 0)),
        ],
        out_specs=[
            pl.BlockSpec((None, r, LANES), lambda bi, ri: (bi, ri, 0)),
            pl.BlockSpec((None, GATE_ROWS, r), lambda bi, ri: (bi, 0, ri)),
        ],
        out_shape=[
            jax.ShapeDtypeStruct((b, s, LANES), F32),
            jax.ShapeDtypeStruct((b, GATE_ROWS, s), F32),
        ],
        scratch_shapes=[pltpu.VMEM((FOX_HEADS, LANES), F32), pltpu.VMEM((MLSTM_HEADS, LANES), F32)],
        compiler_params=_params(("arbitrary", "arbitrary")),
        name="gates",
    )(gt, bias_rows)


def _fox_kernel(it_ref, jt_ref, qt_ref, k_ref, vt_ref, gc_ref, zt_ref, nwb_ref, sel_ref, o_ref, m_ref,
                acc_ref, st_ref):
    step_id = pl.program_id(1)
    i = it_ref[step_id]
    j = jt_ref[step_id]
    t = ATT_T
    hd = FOX_HEAD_DIM

    @pl.when(j == 0)
    def _():
        m_ref[...] = jnp.full_like(m_ref, NEG_INF)
        acc_ref[...] = jnp.zeros_like(acc_ref)

    def step(modes):
        gc = gc_ref[...]
        lane = lax.broadcasted_iota(jnp.int32, gc.shape, 1)
        hi = gc.astype(BF16).astype(F32)
        r1 = gc - hi
        mid = r1.astype(BF16).astype(F32)
        lo = r1 - mid
        gk = jnp.where(lane < 8, hi,
             jnp.where(lane < 16, pltpu.roll(mid, 8, axis=1),
             jnp.where(lane < 24, pltpu.roll(lo, 16, axis=1), 0.0))).astype(BF16)
        ones = jnp.ones((ONES_ROWS, t), BF16)
        units = [(h, c) for h in range(FOX_HEADS) for c in range(ATT_QBLOCKS) if modes[c]]

        def scores(h, c, slot):
            hs = slice(h * hd, (h + 1) * hd)
            sel = jnp.concatenate([sel_ref[h]] * (t // LANES), axis=1)
            st = jnp.dot(jnp.concatenate([k_ref[:, hs], gk], axis=1),
                         jnp.concatenate([qt_ref[hs, c * t:(c + 1) * t], sel], axis=0),
                         preferred_element_type=F32)
            if modes[c] == 'diag':
                valid = (lax.broadcasted_iota(jnp.int32, (t, t), 0)
                         <= lax.broadcasted_iota(jnp.int32, (t, t), 1))
                st = jnp.where(valid, st, NEG_INF)
            st_ref[slot] = st
            return slot, jnp.max(st, axis=0, keepdims=True)

        def probs(h, c, parked):
            slot, st_max = parked
            st = st_ref[slot]
            qs = slice(c * t, (c + 1) * t)
            m_prev = m_ref[h:h + 1, qs]
            m_new = jnp.maximum(m_prev, st_max)
            m_ref[h:h + 1, qs] = m_new
            return jnp.exp2(m_prev - m_new), jnp.exp2(st - m_new).astype(BF16)

        def accumulate(h, c, alpha, pt):
            qs = slice(c * t, (c + 1) * t)
            vt_aug = jnp.concatenate([vt_ref[h * hd:(h + 1) * hd, :], ones], axis=0)
            acc_ref[h, :, qs] = alpha * acc_ref[h, :, qs] + jnp.dot(vt_aug, pt, preferred_element_type=F32)

        st, ap = {}, {}
        for u in range(-ATT_LOOKAHEAD, len(units) + 1):
            if 0 <= u + ATT_LOOKAHEAD < len(units):
                st[u + ATT_LOOKAHEAD] = scores(*units[u + ATT_LOOKAHEAD], (u + ATT_LOOKAHEAD) % 2)
            if 0 <= u < len(units):
                ap[u] = probs(*units[u], st.pop(u))
            if 0 <= u - 1 < len(units):
                accumulate(*units[u - 1], *ap.pop(u - 1))

    first = i * ATT_QBLOCKS

    @pl.when(j < first)
    def _():
        step(('full',) * ATT_QBLOCKS)

    for r in range(ATT_QBLOCKS):
        @pl.when(j == first + r)
        def _(r=r):
            step(tuple(None if c < r else 'diag' if c == r else 'full' for c in range(ATT_QBLOCKS)))
            if r == ATT_QBLOCKS - 1:
                for h in range(FOX_HEADS):
                    hs = slice(h * hd, (h + 1) * hd)
                    for c in range(ATT_QBLOCKS):
                        qs = slice(c * t, (c + 1) * t)
                        acc = acc_ref[h, :, qs]
                        o = acc[:hd] * (1.0 / acc[hd:hd + 1])
                        o = o * lax.rsqrt(jnp.mean(o * o, axis=0, keepdims=True) + NORM_EPS) * nwb_ref[hs, :]
                        o_ref[hs, qs] = (o * _silu_of_half(zt_ref[hs, qs])).astype(BF16)


def _fox(kr, t16, t32, mcol, nwb):
    b, s, _ = kr.shape
    t = ATT_T
    tq = ATT_T * ATT_QBLOCKS
    pairs = [(i, j) for i in range(s // tq) for j in range((i + 1) * ATT_QBLOCKS)]
    it = jnp.asarray([p[0] for p in pairs], jnp.int32)
    jt = jnp.asarray([p[1] for p in pairs], jnp.int32)
    row = lax.broadcasted_iota(jnp.int32, (FOX_HEADS, LANES, LANES), 1)
    head = lax.broadcasted_iota(jnp.int32, (FOX_HEADS, LANES, LANES), 0)
    sel = jnp.where((row < 24) & (row % 8 == head), -1.0, 0.0).astype(BF16)

    grid_spec = pltpu.PrefetchScalarGridSpec(
        num_scalar_prefetch=2,
        grid=(b, len(pairs)),
        in_specs=[
            pl.BlockSpec((None, FOX_WIDTH, tq), lambda bi, p, it, jt: (bi, 0, it[p])),
            pl.BlockSpec((None, t, FOX_WIDTH), lambda bi, p, it, jt: (bi, jt[p], 0)),
            pl.BlockSpec((None, FOX_WIDTH, t), lambda bi, p, it, jt: (bi, 1, jt[p])),
            pl.BlockSpec((None, t, LANES), lambda bi, p, it, jt: (bi, jt[p], 0)),
            pl.BlockSpec((None, FOX_WIDTH, tq), lambda bi, p, it, jt: (bi, 0, it[p])),
            pl.BlockSpec((FOX_WIDTH, t), lambda bi, p, it, jt: (0, 0)),
            pl.BlockSpec((FOX_HEADS, LANES, LANES), lambda bi, p, it, jt: (0, 0, 0)),
        ],
        out_specs=pl.BlockSpec((None, FOX_WIDTH, tq), lambda bi, p, it, jt: (bi, 0, it[p])),
        scratch_shapes=[pltpu.VMEM((FOX_HEADS, tq), F32),
                        pltpu.VMEM((FOX_HEADS, FOX_HEAD_DIM + ONES_ROWS, tq), F32),
                        pltpu.VMEM((2, t, t), F32)],
    )
    return pl.pallas_call(
        _fox_kernel,
        grid_spec=grid_spec,
        out_shape=jax.ShapeDtypeStruct((b, FOX_WIDTH, s), BF16),
        compiler_params=_params(("arbitrary", "arbitrary")),
        name="fox_attention",
    )(it, jt, t16, kr, t16, mcol, t32, nwb, sel)


def _mlstm_conv_tile(c, t, n_chunks, qk_ref, cw_ref, cb_ref, hist_ref):
    n = MLSTM_L
    ls = slice(t * LANES, (t + 1) * LANES)
    u = qk_ref[c * n:(c + 1) * n, ls]
    before = hist_ref[:, ls] if c == 0 else qk_ref[c * n - F32_SUBLANES:c * n, ls]
    ext = jnp.concatenate([before, u], axis=0)
    if c == n_chunks - 1:
        hist_ref[:, ls] = u[n - F32_SUBLANES:n, :]

    def tap(jj):
        back = CONV_WIDTH - 1 - jj
        return pltpu.roll(ext, back, axis=0)[F32_SUBLANES:, :] if back else u

    y = tap(0) * cw_ref[0:1, ls]
    for jj in range(1, CONV_WIDTH):
        y = y + tap(jj) * cw_ref[jj:jj + 1, ls]
    y = y + cb_ref[:, ls]
    return _silu(y)


def _mlstm_chunk(c, sy, vt_ref, ogt_ref, zt_ref, mcol_ref, mrow_ref, nwb_ref, write_out, st_ref,
                 after_head=lambda h: None):
    n = MLSTM_L
    dv = MLSTM_V_DIM
    ts = slice(c * n, (c + 1) * n)

    mc = mcol_ref[ts, :]
    mr = mrow_ref[:, ts]
    lane = lax.broadcasted_iota(jnp.int32, (n, LANES), 1)
    ones_rows = jnp.where(lax.broadcasted_iota(jnp.int32, (ONES_ROWS, n), 0) == 0, 1.0, 0.0).astype(BF16)
    half = n // 2
    valid = lax.broadcasted_iota(jnp.int32, (half, half), 0) <= lax.broadcasted_iota(jnp.int32, (half, half), 1)
    heads = range(MLSTM_HEADS)
    row = lambda base_row, h: mr[base_row + h:base_row + h + 1, :]

    q_pair = [sy[pr].astype(BF16) for pr in range(MLSTM_HEADS // 2)]
    k_pair = [sy[MLSTM_QK_WIDTH // LANES + pr] * (MLSTM_QK_DIM ** -0.5) for pr in range(MLSTM_HEADS // 2)]
    qp, kh, st_t, inter_t, vt_aug = [], [], [], [], []
    for h in heads:
        qp.append(q_pair[h // 2])
        kh.append(jnp.where((lane < MLSTM_QK_DIM) if h % 2 == 0 else (lane >= MLSTM_QK_DIM), k_pair[h // 2], 0.0))
        st_t.append(lax.dot_general(kh[h].astype(BF16), qp[h], NT_DIMS, preferred_element_type=F32))
        inter_t.append(lax.dot_general(st_ref[h].astype(BF16), qp[h], NT_DIMS, preferred_element_type=F32))
        vt_aug.append(jnp.concatenate([vt_ref[h * dv:(h + 1) * dv, ts], ones_rows], axis=0))

    for h in heads:
        hs = slice(h * dv, (h + 1) * dv)
        a_col = mc[:, LANE_A + h:LANE_A + h + 1]
        g_row = row(ROW_G, h)
        w_ul = jnp.exp(jnp.where(valid, a_col[:half] + g_row[:, :half], NEG_INF))
        w_ur = jnp.exp(a_col[:half] + g_row[:, half:])
        w_lr = jnp.exp(jnp.where(valid, a_col[half:] + g_row[:, half:], NEG_INF))
        sc_t = jnp.concatenate([
            st_t[h][:half] * jnp.concatenate([w_ul, w_ur], axis=1),
            jnp.concatenate([jnp.zeros((half, half), F32), st_t[h][half:, half:] * w_lr], axis=1)], axis=0)
        pv = jnp.dot(vt_aug[h], sc_t.astype(BF16), preferred_element_type=F32)
        after_head(h)
        res = pv + jnp.exp(row(ROW_U, h)) * inter_t[h]
        den = res[dv:dv + 1, :]
        half_inv = 0.5 / jnp.maximum(jnp.abs(den), jnp.exp(row(ROW_NEGM, h)))
        hb = (res[:dv] * half_inv) * (1.0 + jnp.tanh(ogt_ref[hs, ts]))
        hn = hb * lax.rsqrt(jnp.mean(hb * hb, axis=0, keepdims=True) + NORM_EPS) * nwb_ref[hs, :]
        write_out(hs, ts, (hn * _silu_of_half(zt_ref[hs, ts])).astype(BF16))

    for h in heads:
        a_col = mc[:, LANE_A + h:LANE_A + h + 1]
        g_last = mc[n - 1:n, LANE_G + h:LANE_G + h + 1]
        u_last = mc[n - 1:n, LANE_U + h:LANE_U + h + 1]
        kw = (kh[h] * jnp.exp(a_col + g_last)).astype(BF16)
        upd = jnp.dot(vt_aug[h], kw, preferred_element_type=F32)
        st_ref[h] = jnp.exp(u_last) * st_ref[h] + upd


def _mlstm_out_kernel(qk_ref, vt_ref, ogt_ref, zt_ref, mcol_ref, mrow_ref, cw_ref, cb_ref, nwb_ref,
                      yat_ref, wo_ref, x_ref, fw_ref, o_ref, hist_ref, st_ref, ybt_ref, lhs_ref,
                      *, per_seq, final):
    step = pl.program_id(0)
    cur = step % 2
    n = MLSTM_L

    @pl.when(step % per_seq == 0)
    def _():
        hist_ref[...] = jnp.zeros_like(hist_ref)
        st_ref[...] = jnp.zeros_like(st_ref)

    @pl.when(step == 0)
    def _():
        ybt_ref[1] = jnp.zeros_like(ybt_ref[1])

    def write_yb(hs, ts, val):
        ybt_ref[cur, hs, ts] = val

    prev = ybt_ref.at[1 - cur]
    n_chunks = FUSE_TM // n
    q_tiles = MLSTM_QK_WIDTH // LANES
    conv_tile = lambda c, t: _mlstm_conv_tile(c, t, n_chunks, qk_ref, cw_ref, cb_ref, hist_ref)
    every = MLSTM_HEADS // FUSE_SLICES
    cols = D_MODEL // FUSE_SLICES
    for c in range(n_chunks):
        rs = slice(c * n, (c + 1) * n)
        sy = [conv_tile(c, t) for t in range(2 * q_tiles)]
        proj = []
        lhs_ref[:, :FOX_WIDTH] = yat_ref[:, rs].T
        lhs_ref[:, FOX_WIDTH:] = prev[:, rs].T

        def woven(h, proj=proj):
            if h % every == 0:
                cs = slice(len(proj) * cols, (len(proj) + 1) * cols)
                proj.append(jnp.dot(lhs_ref[...], wo_ref[:, cs], preferred_element_type=F32))

        _mlstm_chunk(c, sy, vt_ref, ogt_ref, zt_ref, mcol_ref, mrow_ref, nwb_ref, write_yb, st_ref,
                     after_head=woven)
        acc = jnp.concatenate(proj, axis=1)
        r = x_ref[rs, :] + acc
        if final:
            r = r * lax.rsqrt(jnp.mean(r * r, axis=-1, keepdims=True) + NORM_EPS) * fw_ref[...]
        o_ref[rs, :] = r


def _mlstm_out(qkr, t16, t32, mcol, mrow, cw, cb, nwb, yat, wo, x2, fw, final):
    assert FOX_WIDTH == MLSTM_V_WIDTH and wo.shape == (FOX_WIDTH + MLSTM_V_WIDTH, D_MODEL)
    m = x2.shape[0]
    _, _, seq = yat.shape
    tm = FUSE_TM
    per_seq = seq // tm
    blocks = m // tm
    cur = lambda s: jnp.minimum(s, blocks - 1)
    prv = lambda s: jnp.maximum(s - 1, 0)
    t_cur = lambda which: (lambda s: (cur(s) // per_seq, which, cur(s) % per_seq))
    return pl.pallas_call(
        functools.partial(_mlstm_out_kernel, per_seq=per_seq, final=final),
        grid=(blocks + 1,),
        in_specs=[
            pl.BlockSpec((tm, SEG), lambda s: (cur(s), 0)),
            pl.BlockSpec((None, SEG, tm), t_cur(2)),
            pl.BlockSpec((None, SEG, tm), t_cur(1)),
            pl.BlockSpec((None, SEG, tm), t_cur(2)),
            pl.BlockSpec((tm, LANES), lambda s: (cur(s), 0)),
            pl.BlockSpec((None, GATE_ROWS, tm), lambda s: (cur(s) // per_seq, 0, cur(s) % per_seq)),
            pl.BlockSpec((CONV_WIDTH, SEG), lambda s: (0, 0)),
            pl.BlockSpec((1, SEG), lambda s: (0, 0)),
            pl.BlockSpec((SEG, MLSTM_L), lambda s: (0, 0)),
            pl.BlockSpec((None, FOX_WIDTH, tm), lambda s: (prv(s) // per_seq, 0, prv(s) % per_seq)),
            pl.BlockSpec(wo.shape, lambda s: (0, 0)),
            pl.BlockSpec((tm, D_MODEL), lambda s: (prv(s), 0)),
            pl.BlockSpec((1, D_MODEL), lambda s: (0, 0)),
        ],
        out_specs=pl.BlockSpec((tm, D_MODEL), lambda s: (prv(s), 0)),
        out_shape=jax.ShapeDtypeStruct((m, D_MODEL), F32),
        scratch_shapes=[pltpu.VMEM((F32_SUBLANES, SEG), F32),
                        pltpu.VMEM((MLSTM_HEADS, MLSTM_V_DIM + ONES_ROWS, LANES), F32),
                        pltpu.VMEM((2, MLSTM_V_WIDTH, tm), BF16),
                        pltpu.VMEM((MLSTM_L, FOX_WIDTH + MLSTM_V_WIDTH), BF16)],
        compiler_params=_params(("arbitrary",)),
        name="mlstm_outproj",
    )(qkr, t16, t32, t32, mcol, mrow, cw, cb, nwb, yat, wo, x2, fw)


SEG_STARTS = (IN_OFFS[0], IN_OFFS[1], IN_OFFS[5], IN_OFFS[2], IN_OFFS[7], IN_OFFS[3], IN_OFFS[8], IN_OFFS[9])
SEG_SHIFT = FOX_HEADS
assert all(st % SEG in (0, SEG_SHIFT) for st in SEG_STARTS) and IN_OFFS[6] == IN_OFFS[5] + MLSTM_QK_WIDTH
assert SEG_SHIFT == F32_SUBLANES and MLSTM_HEADS == F32_SUBLANES

J_FOX_GATE = 2
assert SEG_STARTS[J_FOX_GATE] - SEG_SHIFT == IN_OFFS[4]


def _wprep_kernel(main_blk_ref, extra_blk_ref, main_ref, extra_ref, mi_ref, mf_ref, wout_ref,
                  o_ref, gw_ref, wo_ref, gwf_ref):
    del main_blk_ref, extra_blk_ref
    j = pl.program_id(0)
    in_seg = j < len(SEG_STARTS)
    shifted = functools.reduce(jnp.logical_or, [j == k for k, st in enumerate(SEG_STARTS) if st % SEG])

    @pl.when(jnp.logical_not(in_seg))
    def _():
        wo_ref[...] = wout_ref[...].astype(BF16)

    @pl.when(j == 0)
    def _():
        gwf_ref[...] = jnp.zeros_like(gwf_ref)
        gwf_ref[GW_MLSTM_I:GW_MLSTM_I + MLSTM_HEADS, :] = mi_ref[...]
        gwf_ref[GW_MLSTM_F:GW_MLSTM_F + MLSTM_HEADS, :] = mf_ref[...]

    @pl.when(in_seg & jnp.logical_not(shifted))
    def _():
        o_ref[...] = main_ref[...].astype(BF16)

    @pl.when(shifted)
    def _():
        full = jnp.concatenate([main_ref[...], extra_ref[...]], axis=0)
        o_ref[...] = full[SEG_SHIFT:SEG_SHIFT + SEG, :].astype(BF16)

    @pl.when(j == J_FOX_GATE)
    def _():
        gwf_ref[GW_FOX_F:GW_FOX_F + FOX_HEADS, :] = main_ref[0:SEG_SHIFT, :]

    @pl.when(j == len(SEG_STARTS) - 1)
    def _():
        gw_ref[...] = gwf_ref[...].astype(BF16)


def _wprep(w_t, w_out):
    d = w_t.shape[1]
    nseg = len(SEG_STARTS)
    out_steps = w_out.shape[0] // SEG
    pad = [SEG_STARTS[-1]] * out_steps
    main_blk = jnp.asarray([st // SEG for st in SEG_STARTS + tuple(pad)], jnp.int32)
    extra_blk = jnp.asarray([(st // SEG + 1) * (SEG // F32_SUBLANES) for st in SEG_STARTS + tuple(pad)],
                            jnp.int32)
    out_blk = lambda j: jnp.clip(j - nseg, 0, out_steps - 1)
    grid_spec = pltpu.PrefetchScalarGridSpec(
        num_scalar_prefetch=2,
        grid=(nseg + out_steps,),
        in_specs=[
            pl.BlockSpec((SEG, d), lambda j, mb, eb: (mb[j], 0)),
            pl.BlockSpec((F32_SUBLANES, d), lambda j, mb, eb: (eb[j], 0)),
            pl.BlockSpec((MLSTM_HEADS, d), lambda j, mb, eb: (IN_OFFS[10] // F32_SUBLANES, 0)),
            pl.BlockSpec((MLSTM_HEADS, d), lambda j, mb, eb: (IN_OFFS[11] // F32_SUBLANES, 0)),
            pl.BlockSpec((SEG, w_out.shape[1]), lambda j, mb, eb: (out_blk(j), 0)),
        ],
        out_specs=[
            pl.BlockSpec((SEG, d), lambda j, mb, eb: (jnp.minimum(j, nseg - 1), 0)),
            pl.BlockSpec((GATE_ROWS, d), lambda j, mb, eb: (0, 0)),
            pl.BlockSpec((SEG, w_out.shape[1]), lambda j, mb, eb: (out_blk(j), 0)),
        ],
        scratch_shapes=[pltpu.VMEM((GATE_ROWS, d), F32)],
    )
    return pl.pallas_call(
        _wprep_kernel,
        grid_spec=grid_spec,
        out_shape=[jax.ShapeDtypeStruct((nseg * SEG, d), BF16),
                   jax.ShapeDtypeStruct((GATE_ROWS, d), BF16),
                   jax.ShapeDtypeStruct(w_out.shape, BF16)],
        compiler_params=_params(("arbitrary",)),
        name="weight_layout",
    )(main_blk, extra_blk, w_t, w_t, w_t, w_t, w_out)


def _lane_broadcast(v, width):
    return jnp.broadcast_to(v[:, None], (v.shape[0], width))


def kernel(x, norm_w, w_in, fox_f_bias, conv_w, conv_b, mlstm_i_bias, mlstm_f_bias,
           fox_out_norm_w, mlstm_out_norm_w, w_out, final_norm_w):
    b, s, d = x.shape
    depth = norm_w.shape[0]
    x2 = x.reshape(b * s, d)
    for l in range(depth):
        w_t, w_gate, wo = _wprep(w_in[l].T, w_out[l])
        bias_rows = _lane_broadcast(jnp.concatenate([
            fox_f_bias[l], mlstm_f_bias[l], mlstm_i_bias[l], jnp.zeros((GATE_ROWS - 24,), F32)]), LANES)

        kr, qkr, t16, t32, g = _inproj(x2, norm_w[l].reshape(1, d), w_t, w_gate, s)
        mcol, mrow = _gates(g, bias_rows)

        yat = _fox(kr.reshape(b, s, SEG), t16, t32, mcol, _lane_broadcast(fox_out_norm_w[l], ATT_T))
        x2 = _mlstm_out(qkr, t16, t32, mcol.reshape(b * s, LANES), mrow, conv_w[l], conv_b[l].reshape(1, SEG),
                        _lane_broadcast(mlstm_out_norm_w[l], MLSTM_L),
                        yat, wo, x2, final_norm_w.reshape(1, d),
                        final=(l == depth - 1))
    return x2.reshape(b, s, d)
```

```python
import functools

import numpy as np
import jax
import jax.numpy as jnp
from jax import lax
from jax.experimental import pallas as pl
from jax.experimental.pallas import tpu as pltpu

D_MODEL = 2048
NORM_EPS = 1e-6
NEG_INF = -1e30

FOX_HEADS = 8
FOX_HEAD_DIM = 128
FOX_WIDTH = FOX_HEADS * FOX_HEAD_DIM
MLSTM_HEADS = 8
MLSTM_V_DIM = 128
MLSTM_QK_DIM = 64
MLSTM_V_WIDTH = MLSTM_HEADS * MLSTM_V_DIM
MLSTM_QK_WIDTH = MLSTM_HEADS * MLSTM_QK_DIM
CONV_WIDTH = 4
IN_SIZES = (FOX_WIDTH, FOX_WIDTH, FOX_WIDTH, FOX_WIDTH, FOX_HEADS,
            MLSTM_QK_WIDTH, MLSTM_QK_WIDTH, MLSTM_V_WIDTH, MLSTM_V_WIDTH, MLSTM_V_WIDTH,
            MLSTM_HEADS, MLSTM_HEADS)
IN_OFFS = tuple(int(v) for v in np.cumsum((0,) + IN_SIZES))

LANES = 128
F32_SUBLANES = 8
SEG = 1024
VMEM_LIMIT = 56 * 1024 * 1024

PROJ_TM = 256
GATE_R = 2048
MLSTM_L = 256
ATT_T = 512
FUSE_TM = 512
FUSE_SLICES = 8

LOG2E = 1.4426950408889634
FOX_Q_SCALE = FOX_HEAD_DIM ** -0.5 * LOG2E
ONES_ROWS = 16
ATT_QBLOCKS = 2
ATT_LOOKAHEAD = 1

GW_FOX_F, GW_MLSTM_F, GW_MLSTM_I = 0, 8, 16
GATE_ROWS = 32
ROW_G, ROW_U, ROW_A, ROW_NEGM = 0, 8, 16, 24
LANE_FOX = 0
LANE_G, LANE_U, LANE_A, LANE_NEGM = (8 + r for r in (ROW_G, ROW_U, ROW_A, ROW_NEGM))

BF16 = jnp.bfloat16
F32 = jnp.float32
NT_DIMS = (((1,), (1,)), ((), ()))


def _sigmoid(x):
    return 1.0 / (1.0 + jnp.exp(-x))


def _silu(x):
    h = 0.5 * x
    return h * (1.0 + jnp.tanh(h))


def _silu_of_half(h):
    return h * (1.0 + jnp.tanh(h))


def _params(sem):
    return pltpu.CompilerParams(dimension_semantics=sem, vmem_limit_bytes=VMEM_LIMIT)


def _inproj_kernel(x_ref, nw_ref, wt_ref, wg_ref, kr_ref, qkr_ref, t16_ref, t32_ref, g_ref):
    xf = x_ref[...]
    ms = jnp.mean(xf * xf, axis=-1, keepdims=True)
    hb = (xf * lax.rsqrt(ms + NORM_EPS) * nw_ref[...]).astype(BF16)

    def seg_t(j):
        return lax.dot_general(wt_ref[j * SEG:(j + 1) * SEG, :], hb, NT_DIMS, preferred_element_type=F32)

    t16_ref[0:SEG, :] = (seg_t(0) * FOX_Q_SCALE).astype(BF16)
    kr_ref[...] = seg_t(1).T.astype(BF16)
    qkr_ref[...] = seg_t(2).T
    t16_ref[SEG:2 * SEG, :] = seg_t(3).astype(BF16)
    t16_ref[2 * SEG:3 * SEG, :] = seg_t(4).astype(BF16)
    for r in range(2):
        t32_ref[r * SEG:(r + 1) * SEG, :] = seg_t(5 + r) * 0.5
    last = jnp.concatenate([wt_ref[7 * SEG:8 * SEG, :], wg_ref[...]], axis=0)
    acc = lax.dot_general(last, hb, NT_DIMS, preferred_element_type=F32)
    t32_ref[2 * SEG:3 * SEG, :] = acc[:SEG] * 0.5
    g_ref[...] = acc[SEG:]


def _inproj(x2, nw, w_t, w_gate, seq):
    m = x2.shape[0]
    tm = PROJ_TM
    per_seq = seq // tm
    t_map = lambda i: (i // per_seq, 0, i % per_seq)
    return pl.pallas_call(
        _inproj_kernel,
        grid=(m // tm,),
        in_specs=[
            pl.BlockSpec((tm, D_MODEL), lambda i: (i, 0)),
            pl.BlockSpec((1, D_MODEL), lambda i: (0, 0)),
            pl.BlockSpec(w_t.shape, lambda i: (0, 0)),
            pl.BlockSpec((GATE_ROWS, D_MODEL), lambda i: (0, 0)),
        ],
        out_specs=[
            pl.BlockSpec((tm, SEG), lambda i: (i, 0)),
            pl.BlockSpec((tm, SEG), lambda i: (i, 0)),
            pl.BlockSpec((None, 3 * SEG, tm), t_map),
            pl.BlockSpec((None, 3 * SEG, tm), t_map),
            pl.BlockSpec((None, GATE_ROWS, tm), t_map),
        ],
        out_shape=[
            jax.ShapeDtypeStruct((m, SEG), BF16),
            jax.ShapeDtypeStruct((m, SEG), F32),
            jax.ShapeDtypeStruct((m // seq, 3 * SEG, seq), BF16),
            jax.ShapeDtypeStruct((m // seq, 3 * SEG, seq), F32),
            jax.ShapeDtypeStruct((m // seq, GATE_ROWS, seq), F32),
        ],
        compiler_params=_params(("arbitrary",)),
        name="inproj",
    )(x2, nw, w_t, w_gate)


def _split3(v):
    hi = v.astype(BF16)
    r1 = v - hi.astype(F32)
    mid = r1.astype(BF16)
    lo = (r1 - mid.astype(F32)).astype(BF16)
    return hi, mid, lo


def _gates_kernel(g_ref, b_ref, mcol_ref, mrow_ref, carry_ref, mprev_ref):
    width, chunk = GATE_R, MLSTM_L
    nh = FOX_HEADS
    lanes_of = lambda tile, n: jnp.concatenate([tile] * (n // LANES), axis=1)

    @pl.when(pl.program_id(1) == 0)
    def _():
        carry_ref[...] = jnp.zeros_like(carry_ref)
        mprev_ref[...] = jnp.zeros_like(mprev_ref)

    v = g_ref[...] + lanes_of(b_ref[...], width)
    vf = v[GW_FOX_F:GW_MLSTM_F + nh]
    ls = jnp.minimum(vf, 0.0) - jnp.log1p(jnp.exp(-jnp.abs(vf)))

    upper = (lax.broadcasted_iota(jnp.int32, (chunk, chunk), 0)
             <= lax.broadcasted_iota(jnp.int32, (chunk, chunk), 1)).astype(BF16)
    hi, mid, lo = _split3(ls)
    cuml = jnp.concatenate([
        jnp.dot(hi[:, c * chunk:(c + 1) * chunk], upper, preferred_element_type=F32)
        + jnp.dot(mid[:, c * chunk:(c + 1) * chunk], upper, preferred_element_type=F32)
        + jnp.dot(lo[:, c * chunk:(c + 1) * chunk], upper, preferred_element_type=F32)
        for c in range(width // chunk)], axis=1)
    b_all = cuml[GW_MLSTM_F:GW_MLSTM_F + nh]
    a = v[GW_MLSTM_I:GW_MLSTM_I + nh] - b_all

    t_in = lax.broadcasted_iota(jnp.int32, (nh, width), 1) % chunk
    cm = a
    s = 1
    while s < chunk:
        cm = jnp.where(t_in >= s, jnp.maximum(cm, pltpu.roll(cm, s, axis=1)), cm)
        s *= 2

    last = lambda x: jnp.broadcast_to(x[:, chunk - 1:chunk], (nh, LANES))
    carry = carry_ref[...]
    mprev = mprev_ref[...]
    for c in range(width // chunk):
        sl = slice(c * chunk, (c + 1) * chunk)
        cumf = cuml[GW_FOX_F:GW_FOX_F + nh, sl] + lanes_of(carry, chunk)
        carry = last(cumf)
        b = b_all[:, sl]
        mp = lanes_of(mprev, chunk)
        mt = b + jnp.maximum(mp, cm[:, sl])
        rows = {ROW_G: b - mt, ROW_U: (b + mp) - mt, ROW_A: a[:, sl], ROW_NEGM: -mt}
        mprev = last(mt)
        for r0, val in rows.items():
            mrow_ref[r0:r0 + nh, sl] = val
        tile_t = jnp.concatenate([cumf * LOG2E] + [rows[r0] for r0 in (ROW_G, ROW_U, ROW_A, ROW_NEGM)]
                                 + [jnp.zeros((LANES - 5 * nh, chunk), F32)], axis=0)
        mcol_ref[sl, :] = tile_t.T
    carry_ref[...] = carry
    mprev_ref[...] = mprev


def _gates(gt, bias_rows):
    b, _, s = gt.shape
    r = GATE_R
    return pl.pallas_call(
        _gates_kernel,
        grid=(b, s // r),
        in_specs=[
            pl.BlockSpec((None, GATE_ROWS, r), lambda bi, ri: (bi, 0, ri)),
            pl.BlockSpec((GATE_ROWS, LANES), lambda bi, ri: (0, 0)),
        ],
        out_specs=[
            pl.BlockSpec((None, r, LANES), lambda bi, ri: (bi, ri, 0)),
            pl.BlockSpec((None, GATE_ROWS, r), lambda bi, ri: (bi, 0, ri)),
        ],
        out_shape=[
            jax.ShapeDtypeStruct((b, s, LANES), F32),
            jax.ShapeDtypeStruct((b, GATE_ROWS, s), F32),
        ],
        scratch_shapes=[pltpu.VMEM((FOX_HEADS, LANES), F32), pltpu.VMEM((MLSTM_HEADS, LANES), F32)],
        compiler_params=_params(("arbitrary", "arbitrary")),
        name="gates",
    )(gt, bias_rows)


def _fox_kernel(it_ref, jt_ref, qt_ref, k_ref, vt_ref, gc_ref, zt_ref, nwb_ref, sel_ref, o_ref, m_ref,
                acc_ref, st_ref):
    step_id = pl.program_id(1)
    i = it_ref[step_id]
    j = jt_ref[step_id]
    t = ATT_T
    hd = FOX_HEAD_DIM

    @pl.when(j == 0)
    def _():
        m_ref[...] = jnp.full_like(m_ref, NEG_INF)
        acc_ref[...] = jnp.zeros_like(acc_ref)

    def step(modes):
        gc = gc_ref[...]
        lane = lax.broadcasted_iota(jnp.int32, gc.shape, 1)
        hi = gc.astype(BF16).astype(F32)
        r1 = gc - hi
        mid = r1.astype(BF16).astype(F32)
        lo = r1 - mid
        gk = jnp.where(lane < 8, hi,
             jnp.where(lane < 16, pltpu.roll(mid, 8, axis=1),
             jnp.where(lane < 24, pltpu.roll(lo, 16, axis=1), 0.0))).astype(BF16)
        ones = jnp.ones((ONES_ROWS, t), BF16)
        units = [(h, c) for h in range(FOX_HEADS) for c in range(ATT_QBLOCKS) if modes[c]]

        def scores(h, c, slot):
            hs = slice(h * hd, (h + 1) * hd)
            sel = jnp.concatenate([sel_ref[h]] * (t // LANES), axis=1)
            st = jnp.dot(jnp.concatenate([k_ref[:, hs], gk], axis=1),
                         jnp.concatenate([qt_ref[hs, c * t:(c + 1) * t], sel], axis=0),
                         preferred_element_type=F32)
            if modes[c] == 'diag':
                valid = (lax.broadcasted_iota(jnp.int32, (t, t), 0)
                         <= lax.broadcasted_iota(jnp.int32, (t, t), 1))
                st = jnp.where(valid, st, NEG_INF)
            st_ref[slot] = st
            return slot, jnp.max(st, axis=0, keepdims=True)

        def probs(h, c, parked):
            slot, st_max = parked
            st = st_ref[slot]
            qs = slice(c * t, (c + 1) * t)
            m_prev = m_ref[h:h + 1, qs]
            m_new = jnp.maximum(m_prev, st_max)
            m_ref[h:h + 1, qs] = m_new
            return jnp.exp2(m_prev - m_new), jnp.exp2(st - m_new).astype(BF16)

        def accumulate(h, c, alpha, pt):
            qs = slice(c * t, (c + 1) * t)
            vt_aug = jnp.concatenate([vt_ref[h * hd:(h + 1) * hd, :], ones], axis=0)
            acc_ref[h, :, qs] = alpha * acc_ref[h, :, qs] + jnp.dot(vt_aug, pt, preferred_element_type=F32)

        st, ap = {}, {}
        for u in range(-ATT_LOOKAHEAD, len(units) + 1):
            if 0 <= u + ATT_LOOKAHEAD < len(units):
                st[u + ATT_LOOKAHEAD] = scores(*units[u + ATT_LOOKAHEAD], (u + ATT_LOOKAHEAD) % 2)
            if 0 <= u < len(units):
                ap[u] = probs(*units[u], st.pop(u))
            if 0 <= u - 1 < len(units):
                accumulate(*units[u - 1], *ap.pop(u - 1))

    first = i * ATT_QBLOCKS

    @pl.when(j < first)
    def _():
        step(('full',) * ATT_QBLOCKS)

    for r in range(ATT_QBLOCKS):
        @pl.when(j == first + r)
        def _(r=r):
            step(tuple(None if c < r else 'diag' if c == r else 'full' for c in range(ATT_QBLOCKS)))
            if r == ATT_QBLOCKS - 1:
                for h in range(FOX_HEADS):
                    hs = slice(h * hd, (h + 1) * hd)
                    for c in range(ATT_QBLOCKS):
                        qs = slice(c * t, (c + 1) * t)
                        acc = acc_ref[h, :, qs]
                        a = acc[:hd]
                        inv = 1.0 / acc[hd:hd + 1]
                        factor = inv * lax.rsqrt(jnp.mean(a * a, axis=0, keepdims=True) * (inv * inv) + NORM_EPS)
                        o = a * factor * nwb_ref[hs, :]
                        o_ref[hs, qs] = (o * _silu_of_half(zt_ref[hs, qs])).astype(BF16)


def _fox(kr, t16, t32, mcol, nwb):
    b, s, _ = kr.shape
    t = ATT_T
    tq = ATT_T * ATT_QBLOCKS
    pairs = [(i, j) for i in range(s // tq) for j in range((i + 1) * ATT_QBLOCKS)]
    it = jnp.asarray([p[0] for p in pairs], jnp.int32)
    jt = jnp.asarray([p[1] for p in pairs], jnp.int32)
    row = lax.broadcasted_iota(jnp.int32, (FOX_HEADS, LANES, LANES), 1)
    head = lax.broadcasted_iota(jnp.int32, (FOX_HEADS, LANES, LANES), 0)
    sel = jnp.where((row < 24) & (row % 8 == head), -1.0, 0.0).astype(BF16)

    grid_spec = pltpu.PrefetchScalarGridSpec(
        num_scalar_prefetch=2,
        grid=(b, len(pairs)),
        in_specs=[
            pl.BlockSpec((None, FOX_WIDTH, tq), lambda bi, p, it, jt: (bi, 0, it[p])),
            pl.BlockSpec((None, t, FOX_WIDTH), lambda bi, p, it, jt: (bi, jt[p], 0)),
            pl.BlockSpec((None, FOX_WIDTH, t), lambda bi, p, it, jt: (bi, 1, jt[p])),
            pl.BlockSpec((None, t, LANES), lambda bi, p, it, jt: (bi, jt[p], 0)),
            pl.BlockSpec((None, FOX_WIDTH, tq), lambda bi, p, it, jt: (bi, 0, it[p])),
            pl.BlockSpec((FOX_WIDTH, t), lambda bi, p, it, jt: (0, 0)),
            pl.BlockSpec((FOX_HEADS, LANES, LANES), lambda bi, p, it, jt: (0, 0, 0)),
        ],
        out_specs=pl.BlockSpec((None, FOX_WIDTH, tq), lambda bi, p, it, jt: (bi, 0, it[p])),
        scratch_shapes=[pltpu.VMEM((FOX_HEADS, tq), F32),
                        pltpu.VMEM((FOX_HEADS, FOX_HEAD_DIM + ONES_ROWS, tq), F32),
                        pltpu.VMEM((2, t, t), F32)],
    )
    return pl.pallas_call(
        _fox_kernel,
        grid_spec=grid_spec,
        out_shape=jax.ShapeDtypeStruct((b, FOX_WIDTH, s), BF16),
        compiler_params=_params(("arbitrary", "arbitrary")),
        name="fox_attention",
    )(it, jt, t16, kr, t16, mcol, t32, nwb, sel)


def _mlstm_conv_tile(c, t, n_chunks, qk_ref, cw_ref, cb_ref, hist_ref):
    n = MLSTM_L
    ls = slice(t * LANES, (t + 1) * LANES)
    u = qk_ref[c * n:(c + 1) * n, ls]
    before = hist_ref[:, ls] if c == 0 else qk_ref[c * n - F32_SUBLANES:c * n, ls]
    ext = jnp.concatenate([before, u], axis=0)
    if c == n_chunks - 1:
        hist_ref[:, ls] = u[n - F32_SUBLANES:n, :]

    def tap(jj):
        back = CONV_WIDTH - 1 - jj
        return pltpu.roll(ext, back, axis=0)[F32_SUBLANES:, :] if back else u

    y = tap(0) * cw_ref[0:1, ls]
    for jj in range(1, CONV_WIDTH):
        y = y + tap(jj) * cw_ref[jj:jj + 1, ls]
    y = y + cb_ref[:, ls]
    return _silu(y)


def _mlstm_chunk(c, sy, vt_ref, ogt_ref, zt_ref, mcol_ref, mrow_ref, nwb_ref, write_out, st_ref,
                 after_head=lambda h: None):
    n = MLSTM_L
    dv = MLSTM_V_DIM
    ts = slice(c * n, (c + 1) * n)

    mc = mcol_ref[ts, :]
    mr = mrow_ref[:, ts]
    lane = lax.broadcasted_iota(jnp.int32, (n, LANES), 1)
    ones_rows = jnp.where(lax.broadcasted_iota(jnp.int32, (ONES_ROWS, n), 0) == 0, 1.0, 0.0).astype(BF16)
    half = n // 2
    valid = lax.broadcasted_iota(jnp.int32, (half, half), 0) <= lax.broadcasted_iota(jnp.int32, (half, half), 1)
    heads = range(MLSTM_HEADS)
    row = lambda base_row, h: mr[base_row + h:base_row + h + 1, :]

    q_pair = [sy[pr].astype(BF16) for pr in range(MLSTM_HEADS // 2)]
    k_pair = [sy[MLSTM_QK_WIDTH // LANES + pr] * (MLSTM_QK_DIM ** -0.5) for pr in range(MLSTM_HEADS // 2)]
    qp, kh, st_t, inter_t, vt_aug = [], [], [], [], []
    for h in heads:
        qp.append(q_pair[h // 2])
        kh.append(jnp.where((lane < MLSTM_QK_DIM) if h % 2 == 0 else (lane >= MLSTM_QK_DIM), k_pair[h // 2], 0.0))
        st_t.append(lax.dot_general(kh[h].astype(BF16), qp[h], NT_DIMS, preferred_element_type=F32))
        inter_t.append(lax.dot_general(st_ref[h].astype(BF16), qp[h], NT_DIMS, preferred_element_type=F32))
        vt_aug.append(jnp.concatenate([vt_ref[h * dv:(h + 1) * dv, ts], ones_rows], axis=0))

    for h in heads:
        hs = slice(h * dv, (h + 1) * dv)
        a_col = mc[:, LANE_A + h:LANE_A + h + 1]
        g_row = row(ROW_G, h)
        w_ul = jnp.exp(jnp.where(valid, a_col[:half] + g_row[:, :half], NEG_INF))
        w_ur = jnp.exp(a_col[:half] + g_row[:, half:])
        w_lr = jnp.exp(jnp.where(valid, a_col[half:] + g_row[:, half:], NEG_INF))
        sc_t = jnp.concatenate([
            st_t[h][:half] * jnp.concatenate([w_ul, w_ur], axis=1),
            jnp.concatenate([jnp.zeros((half, half), F32), st_t[h][half:, half:] * w_lr], axis=1)], axis=0)
        pv = jnp.dot(vt_aug[h], sc_t.astype(BF16), preferred_element_type=F32)
        after_head(h)
        res = pv + jnp.exp(row(ROW_U, h)) * inter_t[h]
        den = res[dv:dv + 1, :]
        half_inv = 0.5 / jnp.maximum(jnp.abs(den), jnp.exp(row(ROW_NEGM, h)))
        hb = (res[:dv] * half_inv) * (1.0 + jnp.tanh(ogt_ref[hs, ts]))
        hn = hb * lax.rsqrt(jnp.mean(hb * hb, axis=0, keepdims=True) + NORM_EPS) * nwb_ref[hs, :]
        write_out(hs, ts, (hn * _silu_of_half(zt_ref[hs, ts])).astype(BF16))

    for h in heads:
        a_col = mc[:, LANE_A + h:LANE_A + h + 1]
        g_last = mc[n - 1:n, LANE_G + h:LANE_G + h + 1]
        u_last = mc[n - 1:n, LANE_U + h:LANE_U + h + 1]
        kw = (kh[h] * jnp.exp(a_col + g_last)).astype(BF16)
        upd = jnp.dot(vt_aug[h], kw, preferred_element_type=F32)
        st_ref[h] = jnp.exp(u_last) * st_ref[h] + upd


def _mlstm_out_kernel(qk_ref, vt_ref, ogt_ref, zt_ref, mcol_ref, mrow_ref, cw_ref, cb_ref, nwb_ref,
                      yat_ref, wo_ref, x_ref, fw_ref, o_ref, hist_ref, st_ref, ybt_ref, lhs_ref,
                      *, per_seq, final):
    step = pl.program_id(0)
    cur = step % 2
    n = MLSTM_L

    @pl.when(step % per_seq == 0)
    def _():
        hist_ref[...] = jnp.zeros_like(hist_ref)
        st_ref[...] = jnp.zeros_like(st_ref)

    @pl.when(step == 0)
    def _():
        ybt_ref[1] = jnp.zeros_like(ybt_ref[1])

    def write_yb(hs, ts, val):
        ybt_ref[cur, hs, ts] = val

    prev = ybt_ref.at[1 - cur]
    n_chunks = FUSE_TM // n
    q_tiles = MLSTM_QK_WIDTH // LANES
    conv_tile = lambda c, t: _mlstm_conv_tile(c, t, n_chunks, qk_ref, cw_ref, cb_ref, hist_ref)
    every = MLSTM_HEADS // FUSE_SLICES
    cols = D_MODEL // FUSE_SLICES
    for c in range(n_chunks):
        rs = slice(c * n, (c + 1) * n)
        sy = [conv_tile(c, t) for t in range(2 * q_tiles)]
        proj = []
        lhs_ref[:, :FOX_WIDTH] = yat_ref[:, rs].T
        lhs_ref[:, FOX_WIDTH:] = prev[:, rs].T

        def woven(h, proj=proj):
            if h % every == 0:
                cs = slice(len(proj) * cols, (len(proj) + 1) * cols)
                proj.append(jnp.dot(lhs_ref[...], wo_ref[:, cs], preferred_element_type=F32))

        _mlstm_chunk(c, sy, vt_ref, ogt_ref, zt_ref, mcol_ref, mrow_ref, nwb_ref, write_yb, st_ref,
                     after_head=woven)
        acc = jnp.concatenate(proj, axis=1)
        r = x_ref[rs, :] + acc
        if final:
            r = r * lax.rsqrt(jnp.mean(r * r, axis=-1, keepdims=True) + NORM_EPS) * fw_ref[...]
        o_ref[rs, :] = r


def _mlstm_out(qkr, t16, t32, mcol, mrow, cw, cb, nwb, yat, wo, x2, fw, final):
    assert FOX_WIDTH == MLSTM_V_WIDTH and wo.shape == (FOX_WIDTH + MLSTM_V_WIDTH, D_MODEL)
    m = x2.shape[0]
    _, _, seq = yat.shape
    tm = FUSE_TM
    per_seq = seq // tm
    blocks = m // tm
    cur = lambda s: jnp.minimum(s, blocks - 1)
    prv = lambda s: jnp.maximum(s - 1, 0)
    t_cur = lambda which: (lambda s: (cur(s) // per_seq, which, cur(s) % per_seq))
    return pl.pallas_call(
        functools.partial(_mlstm_out_kernel, per_seq=per_seq, final=final),
        grid=(blocks + 1,),
        in_specs=[
            pl.BlockSpec((tm, SEG), lambda s: (cur(s), 0)),
            pl.BlockSpec((None, SEG, tm), t_cur(2)),
            pl.BlockSpec((None, SEG, tm), t_cur(1)),
            pl.BlockSpec((None, SEG, tm), t_cur(2)),
            pl.BlockSpec((tm, LANES), lambda s: (cur(s), 0)),
            pl.BlockSpec((None, GATE_ROWS, tm), lambda s: (cur(s) // per_seq, 0, cur(s) % per_seq)),
            pl.BlockSpec((CONV_WIDTH, SEG), lambda s: (0, 0)),
            pl.BlockSpec((1, SEG), lambda s: (0, 0)),
            pl.BlockSpec((SEG, MLSTM_L), lambda s: (0, 0)),
            pl.BlockSpec((None, FOX_WIDTH, tm), lambda s: (prv(s) // per_seq, 0, prv(s) % per_seq)),
            pl.BlockSpec(wo.shape, lambda s: (0, 0)),
            pl.BlockSpec((tm, D_MODEL), lambda s: (prv(s), 0)),
            pl.BlockSpec((1, D_MODEL), lambda s: (0, 0)),
        ],
        out_specs=pl.BlockSpec((tm, D_MODEL), lambda s: (prv(s), 0)),
        out_shape=jax.ShapeDtypeStruct((m, D_MODEL), F32),
        scratch_shapes=[pltpu.VMEM((F32_SUBLANES, SEG), F32),
                        pltpu.VMEM((MLSTM_HEADS, MLSTM_V_DIM + ONES_ROWS, LANES), F32),
                        pltpu.VMEM((2, MLSTM_V_WIDTH, tm), BF16),
                        pltpu.VMEM((MLSTM_L, FOX_WIDTH + MLSTM_V_WIDTH), BF16)],
        compiler_params=_params(("arbitrary",)),
        name="mlstm_outproj",
    )(qkr, t16, t32, t32, mcol, mrow, cw, cb, nwb, yat, wo, x2, fw)


SEG_STARTS = (IN_OFFS[0], IN_OFFS[1], IN_OFFS[5], IN_OFFS[2], IN_OFFS[7], IN_OFFS[3], IN_OFFS[8], IN_OFFS[9])
SEG_SHIFT = FOX_HEADS
assert all(st % SEG in (0, SEG_SHIFT) for st in SEG_STARTS) and IN_OFFS[6] == IN_OFFS[5] + MLSTM_QK_WIDTH
assert SEG_SHIFT == F32_SUBLANES and MLSTM_HEADS == F32_SUBLANES

J_FOX_GATE = 2
assert SEG_STARTS[J_FOX_GATE] - SEG_SHIFT == IN_OFFS[4]


def _wprep_kernel(main_blk_ref, extra_blk_ref, main_ref, extra_ref, mi_ref, mf_ref, wout_ref,
                  o_ref, gw_ref, wo_ref, gwf_ref):
    del main_blk_ref, extra_blk_ref
    j = pl.program_id(0)
    in_seg = j < len(SEG_STARTS)
    shifted = functools.reduce(jnp.logical_or, [j == k for k, st in enumerate(SEG_STARTS) if st % SEG])

    @pl.when(jnp.logical_not(in_seg))
    def _():
        wo_ref[...] = wout_ref[...].astype(BF16)

    @pl.when(j == 0)
    def _():
        gwf_ref[...] = jnp.zeros_like(gwf_ref)
        gwf_ref[GW_MLSTM_I:GW_MLSTM_I + MLSTM_HEADS, :] = mi_ref[...]
        gwf_ref[GW_MLSTM_F:GW_MLSTM_F + MLSTM_HEADS, :] = mf_ref[...]

    @pl.when(in_seg & jnp.logical_not(shifted))
    def _():
        o_ref[...] = main_ref[...].astype(BF16)

    @pl.when(shifted)
    def _():
        full = jnp.concatenate([main_ref[...], extra_ref[...]], axis=0)
        o_ref[...] = full[SEG_SHIFT:SEG_SHIFT + SEG, :].astype(BF16)

    @pl.when(j == J_FOX_GATE)
    def _():
        gwf_ref[GW_FOX_F:GW_FOX_F + FOX_HEADS, :] = main_ref[0:SEG_SHIFT, :]

    @pl.when(j == len(SEG_STARTS) - 1)
    def _():
        gw_ref[...] = gwf_ref[...].astype(BF16)


def _wprep(w_t, w_out):
    d = w_t.shape[1]
    nseg = len(SEG_STARTS)
    out_steps = w_out.shape[0] // SEG
    pad = [SEG_STARTS[-1]] * out_steps
    main_blk = jnp.asarray([st // SEG for st in SEG_STARTS + tuple(pad)], jnp.int32)
    extra_blk = jnp.asarray([(st // SEG + 1) * (SEG // F32_SUBLANES) for st in SEG_STARTS + tuple(pad)],
                            jnp.int32)
    out_blk = lambda j: jnp.clip(j - nseg, 0, out_steps - 1)
    grid_spec = pltpu.PrefetchScalarGridSpec(
        num_scalar_prefetch=2,
        grid=(nseg + out_steps,),
        in_specs=[
            pl.BlockSpec((SEG, d), lambda j, mb, eb: (mb[j], 0)),
            pl.BlockSpec((F32_SUBLANES, d), lambda j, mb, eb: (eb[j], 0)),
            pl.BlockSpec((MLSTM_HEADS, d), lambda j, mb, eb: (IN_OFFS[10] // F32_SUBLANES, 0)),
            pl.BlockSpec((MLSTM_HEADS, d), lambda j, mb, eb: (IN_OFFS[11] // F32_SUBLANES, 0)),
            pl.BlockSpec((SEG, w_out.shape[1]), lambda j, mb, eb: (out_blk(j), 0)),
        ],
        out_specs=[
            pl.BlockSpec((SEG, d), lambda j, mb, eb: (jnp.minimum(j, nseg - 1), 0)),
            pl.BlockSpec((GATE_ROWS, d), lambda j, mb, eb: (0, 0)),
            pl.BlockSpec((SEG, w_out.shape[1]), lambda j, mb, eb: (out_blk(j), 0)),
        ],
        scratch_shapes=[pltpu.VMEM((GATE_ROWS, d), F32)],
    )
    return pl.pallas_call(
        _wprep_kernel,
        grid_spec=grid_spec,
        out_shape=[jax.ShapeDtypeStruct((nseg * SEG, d), BF16),
                   jax.ShapeDtypeStruct((GATE_ROWS, d), BF16),
                   jax.ShapeDtypeStruct(w_out.shape, BF16)],
        compiler_params=_params(("arbitrary",)),
        name="weight_layout",
    )(main_blk, extra_blk, w_t, w_t, w_t, w_t, w_out)


def _lane_broadcast(v, width):
    return jnp.broadcast_to(v[:, None], (v.shape[0], width))


def kernel(x, norm_w, w_in, fox_f_bias, conv_w, conv_b, mlstm_i_bias, mlstm_f_bias,
           fox_out_norm_w, mlstm_out_norm_w, w_out, final_norm_w):
    b, s, d = x.shape
    depth = norm_w.shape[0]
    x2 = x.reshape(b * s, d)
    for l in range(depth):
        w_t, w_gate, wo = _wprep(w_in[l].T, w_out[l])
        bias_rows = _lane_broadcast(jnp.concatenate([
            fox_f_bias[l], mlstm_f_bias[l], mlstm_i_bias[l], jnp.zeros((GATE_ROWS - 24,), F32)]), LANES)

        kr, qkr, t16, t32, g = _inproj(x2, norm_w[l].reshape(1, d), w_t, w_gate, s)
        mcol, mrow = _gates(g, bias_rows)

        yat = _fox(kr.reshape(b, s, SEG), t16, t32, mcol, _lane_broadcast(fox_out_norm_w[l], ATT_T))
        x2 = _mlstm_out(qkr, t16, t32, mcol.reshape(b * s, LANES), mrow, conv_w[l], conv_b[l].reshape(1, SEG),
                        _lane_broadcast(mlstm_out_norm_w[l], MLSTM_L),
                        yat, wo, x2, final_norm_w.reshape(1, d),
                        final=(l == depth - 1))
    return x2.reshape(b, s, d)
```

```python
import functools

import numpy as np
import jax
import jax.numpy as jnp
from jax import lax
from jax.experimental import pallas as pl
from jax.experimental.pallas import tpu as pltpu

D_MODEL = 2048
NORM_EPS = 1e-6
NEG_INF = -1e30

FOX_HEADS = 8
FOX_HEAD_DIM = 128
FOX_WIDTH = FOX_HEADS * FOX_HEAD_DIM
MLSTM_HEADS = 8
MLSTM_V_DIM = 128
MLSTM_QK_DIM = 64
MLSTM_V_WIDTH = MLSTM_HEADS * MLSTM_V_DIM
MLSTM_QK_WIDTH = MLSTM_HEADS * MLSTM_QK_DIM
CONV_WIDTH = 4
IN_SIZES = (FOX_WIDTH, FOX_WIDTH, FOX_WIDTH, FOX_WIDTH, FOX_HEADS,
            MLSTM_QK_WIDTH, MLSTM_QK_WIDTH, MLSTM_V_WIDTH, MLSTM_V_WIDTH, MLSTM_V_WIDTH,
            MLSTM_HEADS, MLSTM_HEADS)
IN_OFFS = tuple(int(v) for v in np.cumsum((0,) + IN_SIZES))

LANES = 128
F32_SUBLANES = 8
SEG = 1024
VMEM_LIMIT = 56 * 1024 * 1024

PROJ_TM = 256
GATE_R = 2048
MLSTM_L = 256
ATT_T = 512
FUSE_TM = 512
FUSE_SLICES = 8

LOG2E = 1.4426950408889634
FOX_Q_SCALE = FOX_HEAD_DIM ** -0.5 * LOG2E
ONES_ROWS = 16
ATT_QBLOCKS = 2
ATT_LOOKAHEAD = 1

GW_FOX_F, GW_MLSTM_F, GW_MLSTM_I = 0, 8, 16
GATE_ROWS = 32
ROW_G, ROW_U, ROW_A, ROW_NEGM = 0, 8, 16, 24
LANE_FOX = 0
LANE_G, LANE_U, LANE_A, LANE_NEGM = (8 + r for r in (ROW_G, ROW_U, ROW_A, ROW_NEGM))

BF16 = jnp.bfloat16
F32 = jnp.float32
NT_DIMS = (((1,), (1,)), ((), ()))


def _sigmoid(x):
    return 1.0 / (1.0 + jnp.exp(-x))


def _silu(x):
    h = 0.5 * x
    return h * (1.0 + jnp.tanh(h))


def _silu_of_half(h):
    return h * (1.0 + jnp.tanh(h))


def _params(sem):
    return pltpu.CompilerParams(dimension_semantics=sem, vmem_limit_bytes=VMEM_LIMIT)


def _inproj_kernel(x_ref, nw_ref, wt_ref, wg_ref, kr_ref, qkr_ref, t16_ref, t32_ref, g_ref):
    xf = x_ref[...]
    ms = jnp.mean(xf * xf, axis=-1, keepdims=True)
    hb = (xf * lax.rsqrt(ms + NORM_EPS) * nw_ref[...]).astype(BF16)

    def seg_t(j):
        return lax.dot_general(wt_ref[j * SEG:(j + 1) * SEG, :], hb, NT_DIMS, preferred_element_type=F32)

    t16_ref[0:SEG, :] = (seg_t(0) * FOX_Q_SCALE).astype(BF16)
    kr_ref[...] = seg_t(1).T.astype(BF16)
    qkr_ref[...] = seg_t(2).T
    t16_ref[SEG:2 * SEG, :] = seg_t(3).astype(BF16)
    t16_ref[2 * SEG:3 * SEG, :] = seg_t(4).astype(BF16)
    for r in range(2):
        t32_ref[r * SEG:(r + 1) * SEG, :] = seg_t(5 + r) * 0.5
    last = jnp.concatenate([wt_ref[7 * SEG:8 * SEG, :], wg_ref[...]], axis=0)
    acc = lax.dot_general(last, hb, NT_DIMS, preferred_element_type=F32)
    t32_ref[2 * SEG:3 * SEG, :] = acc[:SEG] * 0.5
    g_ref[...] = acc[SEG:]


def _inproj(x2, nw, w_t, w_gate, seq):
    m = x2.shape[0]
    tm = PROJ_TM
    per_seq = seq // tm
    t_map = lambda i: (i // per_seq, 0, i % per_seq)
    return pl.pallas_call(
        _inproj_kernel,
        grid=(m // tm,),
        in_specs=[
            pl.BlockSpec((tm, D_MODEL), lambda i: (i, 0)),
            pl.BlockSpec((1, D_MODEL), lambda i: (0, 0)),
            pl.BlockSpec(w_t.shape, lambda i: (0, 0)),
            pl.BlockSpec((GATE_ROWS, D_MODEL), lambda i: (0, 0)),
        ],
        out_specs=[
            pl.BlockSpec((tm, SEG), lambda i: (i, 0)),
            pl.BlockSpec((tm, SEG), lambda i: (i, 0)),
            pl.BlockSpec((None, 3 * SEG, tm), t_map),
            pl.BlockSpec((None, 3 * SEG, tm), t_map),
            pl.BlockSpec((None, GATE_ROWS, tm), t_map),
        ],
        out_shape=[
            jax.ShapeDtypeStruct((m, SEG), BF16),
            jax.ShapeDtypeStruct((m, SEG), F32),
            jax.ShapeDtypeStruct((m // seq, 3 * SEG, seq), BF16),
            jax.ShapeDtypeStruct((m // seq, 3 * SEG, seq), F32),
            jax.ShapeDtypeStruct((m // seq, GATE_ROWS, seq), F32),
        ],
        compiler_params=_params(("arbitrary",)),
        name="inproj",
    )(x2, nw, w_t, w_gate)


def _split3(v):
    hi = v.astype(BF16)
    r1 = v - hi.astype(F32)
    mid = r1.astype(BF16)
    lo = (r1 - mid.astype(F32)).astype(BF16)
    return hi, mid, lo


def _gates_kernel(g_ref, b_ref, mcol_ref, mrow_ref, carry_ref, mprev_ref):
    width, chunk = GATE_R, MLSTM_L
    nh = FOX_HEADS
    lanes_of = lambda tile, n: jnp.concatenate([tile] * (n // LANES), axis=1)

    @pl.when(pl.program_id(1) == 0)
    def _():
        carry_ref[...] = jnp.zeros_like(carry_ref)
        mprev_ref[...] = jnp.zeros_like(mprev_ref)

    v = g_ref[...] + lanes_of(b_ref[...], width)
    vf = v[GW_FOX_F:GW_MLSTM_F + nh]
    ls = jnp.minimum(vf, 0.0) - jnp.log1p(jnp.exp(-jnp.abs(vf)))

    upper = (lax.broadcasted_iota(jnp.int32, (chunk, chunk), 0)
             <= lax.broadcasted_iota(jnp.int32, (chunk, chunk), 1)).astype(BF16)
    hi, mid, lo = _split3(ls)
    cuml = jnp.concatenate([
        jnp.dot(hi[:, c * chunk:(c + 1) * chunk], upper, preferred_element_type=F32)
        + jnp.dot(mid[:, c * chunk:(c + 1) * chunk], upper, preferred_element_type=F32)
        + jnp.dot(lo[:, c * chunk:(c + 1) * chunk], upper, preferred_element_type=F32)
        for c in range(width // chunk)], axis=1)
    b_all = cuml[GW_MLSTM_F:GW_MLSTM_F + nh]
    a = v[GW_MLSTM_I:GW_MLSTM_I + nh] - b_all

    t_in = lax.broadcasted_iota(jnp.int32, (nh, width), 1) % chunk
    cm = a
    s = 1
    while s < chunk:
        cm = jnp.where(t_in >= s, jnp.maximum(cm, pltpu.roll(cm, s, axis=1)), cm)
        s *= 2

    last = lambda x: jnp.broadcast_to(x[:, chunk - 1:chunk], (nh, LANES))
    carry = carry_ref[...]
    mprev = mprev_ref[...]
    for c in range(width // chunk):
        sl = slice(c * chunk, (c + 1) * chunk)
        cumf = cuml[GW_FOX_F:GW_FOX_F + nh, sl] + lanes_of(carry, chunk)
        carry = last(cumf)
        b = b_all[:, sl]
        mp = lanes_of(mprev, chunk)
        mt = b + jnp.maximum(mp, cm[:, sl])
        rows = {ROW_G: b - mt, ROW_U: (b + mp) - mt, ROW_A: a[:, sl], ROW_NEGM: -mt}
        mprev = last(mt)
        for r0, val in rows.items():
            mrow_ref[r0:r0 + nh, sl] = val
        tile_t = jnp.concatenate([cumf * LOG2E] + [rows[r0] for r0 in (ROW_G, ROW_U, ROW_A, ROW_NEGM)]
                                 + [jnp.zeros((LANES - 5 * nh, chunk), F32)], axis=0)
        mcol_ref[sl, :] = tile_t.T
    carry_ref[...] = carry
    mprev_ref[...] = mprev


def _gates(gt, bias_rows):
    b, _, s = gt.shape
    r = GATE_R
    return pl.pallas_call(
        _gates_kernel,
        grid=(b, s // r),
        in_specs=[
            pl.BlockSpec((None, GATE_ROWS, r), lambda bi, ri: (bi, 0, ri)),
            pl.BlockSpec((GATE_ROWS, LANES), lambda bi, ri: (0, 0)),
        ],
        out_specs=[
            pl.BlockSpec((None, r, LANES), lambda bi, ri: (bi, ri, 0)),
            pl.BlockSpec((None, GATE_ROWS, r), lambda bi, ri: (bi, 0, ri)),
        ],
        out_shape=[
            jax.ShapeDtypeStruct((b, s, LANES), F32),
            jax.ShapeDtypeStruct((b, GATE_ROWS, s), F32),
        ],
        scratch_shapes=[pltpu.VMEM((FOX_HEADS, LANES), F32), pltpu.VMEM((MLSTM_HEADS, LANES), F32)],
        compiler_params=_params(("arbitrary", "arbitrary")),
        name="gates",
    )(gt, bias_rows)


def _fox_kernel(it_ref, jt_ref, qt_ref, k_ref, vt_ref, gc_ref, zt_ref, nwb_ref, sel_ref, o_ref, m_ref,
                acc_ref, st_ref):
    step_id = pl.program_id(1)
    i = it_ref[step_id]
    j = jt_ref[step_id]
    t = ATT_T
    hd = FOX_HEAD_DIM

    @pl.when(j == 0)
    def _():
        m_ref[...] = jnp.full_like(m_ref, NEG_INF)
        acc_ref[...] = jnp.zeros_like(acc_ref)

    def step(modes):
        gc = gc_ref[...]
        lane = lax.broadcasted_iota(jnp.int32, gc.shape, 1)
        hi = gc.astype(BF16).astype(F32)
        r1 = gc - hi
        mid = r1.astype(BF16).astype(F32)
        lo = r1 - mid
        gk = jnp.where(lane < 8, hi,
             jnp.where(lane < 16, pltpu.roll(mid, 8, axis=1),
             jnp.where(lane < 24, pltpu.roll(lo, 16, axis=1), 0.0))).astype(BF16)
        ones = jnp.ones((ONES_ROWS, t), BF16)
        units = [(h, c) for h in range(FOX_HEADS) for c in range(ATT_QBLOCKS) if modes[c]]

        def scores(h, c, slot):
            hs = slice(h * hd, (h + 1) * hd)
            sel = jnp.concatenate([sel_ref[h]] * (t // LANES), axis=1)
            st = jnp.dot(jnp.concatenate([k_ref[:, hs], gk], axis=1),
                         jnp.concatenate([qt_ref[hs, c * t:(c + 1) * t], sel], axis=0),
                         preferred_element_type=F32)
            if modes[c] == 'diag':
                valid = (lax.broadcasted_iota(jnp.int32, (t, t), 0)
                         <= lax.broadcasted_iota(jnp.int32, (t, t), 1))
                st = jnp.where(valid, st, NEG_INF)
            st_ref[slot] = st
            return slot, jnp.max(st, axis=0, keepdims=True)

        def probs(h, c, parked):
            slot, st_max = parked
            st = st_ref[slot]
            qs = slice(c * t, (c + 1) * t)
            m_prev = m_ref[h:h + 1, qs]
            m_new = jnp.maximum(m_prev, st_max)
            m_ref[h:h + 1, qs] = m_new
            return jnp.exp2(m_prev - m_new), jnp.exp2(st - m_new).astype(BF16)

        def accumulate(h, c, alpha, pt):
            qs = slice(c * t, (c + 1) * t)
            vt_aug = jnp.concatenate([vt_ref[h * hd:(h + 1) * hd, :], ones], axis=0)
            acc_ref[h, :, qs] = alpha * acc_ref[h, :, qs] + jnp.dot(vt_aug, pt, preferred_element_type=F32)

        st, ap = {}, {}
        for u in range(-ATT_LOOKAHEAD, len(units) + 1):
            if 0 <= u + ATT_LOOKAHEAD < len(units):
                st[u + ATT_LOOKAHEAD] = scores(*units[u + ATT_LOOKAHEAD], (u + ATT_LOOKAHEAD) % 2)
            if 0 <= u < len(units):
                ap[u] = probs(*units[u], st.pop(u))
            if 0 <= u - 1 < len(units):
                accumulate(*units[u - 1], *ap.pop(u - 1))

    first = i * ATT_QBLOCKS

    @pl.when(j < first)
    def _():
        step(('full',) * ATT_QBLOCKS)

    for r in range(ATT_QBLOCKS):
        @pl.when(j == first + r)
        def _(r=r):
            step(tuple(None if c < r else 'diag' if c == r else 'full' for c in range(ATT_QBLOCKS)))
            if r == ATT_QBLOCKS - 1:
                for h in range(FOX_HEADS):
                    hs = slice(h * hd, (h + 1) * hd)
                    for c in range(ATT_QBLOCKS):
                        qs = slice(c * t, (c + 1) * t)
                        acc = acc_ref[h, :, qs]
                        a = acc[:hd]
                        inv = 1.0 / acc[hd:hd + 1]
                        factor = inv * lax.rsqrt(jnp.mean(a * a, axis=0, keepdims=True) * (inv * inv) + NORM_EPS)
                        o = a * factor * nwb_ref[hs, :]
                        o_ref[hs, qs] = (o * _silu_of_half(zt_ref[hs, qs])).astype(BF16)


def _fox(kr, t16, t32, mcol, nwb):
    b, s, _ = kr.shape
    t = ATT_T
    tq = ATT_T * ATT_QBLOCKS
    pairs = [(i, j) for i in range(s // tq) for j in range((i + 1) * ATT_QBLOCKS)]
    it = jnp.asarray([p[0] for p in pairs], jnp.int32)
    jt = jnp.asarray([p[1] for p in pairs], jnp.int32)
    row = lax.broadcasted_iota(jnp.int32, (FOX_HEADS, LANES, LANES), 1)
    head = lax.broadcasted_iota(jnp.int32, (FOX_HEADS, LANES, LANES), 0)
    sel = jnp.where((row < 24) & (row % 8 == head), -1.0, 0.0).astype(BF16)

    grid_spec = pltpu.PrefetchScalarGridSpec(
        num_scalar_prefetch=2,
        grid=(b, len(pairs)),
        in_specs=[
            pl.BlockSpec((None, FOX_WIDTH, tq), lambda bi, p, it, jt: (bi, 0, it[p])),
            pl.BlockSpec((None, t, FOX_WIDTH), lambda bi, p, it, jt: (bi, jt[p], 0)),
            pl.BlockSpec((None, FOX_WIDTH, t), lambda bi, p, it, jt: (bi, 1, jt[p])),
            pl.BlockSpec((None, t, LANES), lambda bi, p, it, jt: (bi, jt[p], 0)),
            pl.BlockSpec((None, FOX_WIDTH, tq), lambda bi, p, it, jt: (bi, 0, it[p])),
            pl.BlockSpec((FOX_WIDTH, t), lambda bi, p, it, jt: (0, 0)),
            pl.BlockSpec((FOX_HEADS, LANES, LANES), lambda bi, p, it, jt: (0, 0, 0)),
        ],
        out_specs=pl.BlockSpec((None, FOX_WIDTH, tq), lambda bi, p, it, jt: (bi, 0, it[p])),
        scratch_shapes=[pltpu.VMEM((FOX_HEADS, tq), F32),
                        pltpu.VMEM((FOX_HEADS, FOX_HEAD_DIM + ONES_ROWS, tq), F32),
                        pltpu.VMEM((2, t, t), F32)],
    )
    return pl.pallas_call(
        _fox_kernel,
        grid_spec=grid_spec,
        out_shape=jax.ShapeDtypeStruct((b, FOX_WIDTH, s), BF16),
        compiler_params=_params(("arbitrary", "arbitrary")),
        name="fox_attention",
    )(it, jt, t16, kr, t16, mcol, t32, nwb, sel)


def _mlstm_conv_tile(c, t, n_chunks, qk_ref, cw_ref, cb_ref, hist_ref):
    n = MLSTM_L
    ls = slice(t * LANES, (t + 1) * LANES)
    u = qk_ref[c * n:(c + 1) * n, ls]
    before = hist_ref[:, ls] if c == 0 else qk_ref[c * n - F32_SUBLANES:c * n, ls]
    ext = jnp.concatenate([before, u], axis=0)
    if c == n_chunks - 1:
        hist_ref[:, ls] = u[n - F32_SUBLANES:n, :]

    def tap(jj):
        back = CONV_WIDTH - 1 - jj
        return pltpu.roll(ext, back, axis=0)[F32_SUBLANES:, :] if back else u

    y = tap(0) * cw_ref[0:1, ls]
    for jj in range(1, CONV_WIDTH):
        y = y + tap(jj) * cw_ref[jj:jj + 1, ls]
    y = y + cb_ref[:, ls]
    return _silu(y)


def _mlstm_chunk(c, sy, vt_ref, ogt_ref, zt_ref, mcol_ref, mrow_ref, nwb_ref, write_out, st_ref,
                 after_head=lambda h: None):
    n = MLSTM_L
    dv = MLSTM_V_DIM
    ts = slice(c * n, (c + 1) * n)

    mc = mcol_ref[ts, :]
    mr = mrow_ref[:, ts]
    lane = lax.broadcasted_iota(jnp.int32, (n, LANES), 1)
    ones_rows = jnp.where(lax.broadcasted_iota(jnp.int32, (ONES_ROWS, n), 0) == 0, 1.0, 0.0).astype(BF16)
    half = n // 2
    valid = lax.broadcasted_iota(jnp.int32, (half, half), 0) <= lax.broadcasted_iota(jnp.int32, (half, half), 1)
    heads = range(MLSTM_HEADS)
    row = lambda base_row, h: mr[base_row + h:base_row + h + 1, :]

    q_pair = [sy[pr].astype(BF16) for pr in range(MLSTM_HEADS // 2)]
    k_pair = [sy[MLSTM_QK_WIDTH // LANES + pr] * (MLSTM_QK_DIM ** -0.5) for pr in range(MLSTM_HEADS // 2)]
    qp, kh, st_t, inter_t, vt_aug = [], [], [], [], []
    for h in heads:
        qp.append(q_pair[h // 2])
        kh.append(jnp.where((lane < MLSTM_QK_DIM) if h % 2 == 0 else (lane >= MLSTM_QK_DIM), k_pair[h // 2], 0.0))
        st_t.append(lax.dot_general(kh[h].astype(BF16), qp[h], NT_DIMS, preferred_element_type=F32))
        inter_t.append(lax.dot_general(st_ref[h].astype(BF16), qp[h], NT_DIMS, preferred_element_type=F32))
        vt_aug.append(jnp.concatenate([vt_ref[h * dv:(h + 1) * dv, ts], ones_rows], axis=0))

    for h in heads:
        hs = slice(h * dv, (h + 1) * dv)
        a_col = mc[:, LANE_A + h:LANE_A + h + 1]
        g_row = row(ROW_G, h)
        w_ul = jnp.exp(jnp.where(valid, a_col[:half] + g_row[:, :half], NEG_INF))
        w_ur = jnp.exp(a_col[:half] + g_row[:, half:])
        w_lr = jnp.exp(jnp.where(valid, a_col[half:] + g_row[:, half:], NEG_INF))
        sc_t = jnp.concatenate([
            st_t[h][:half] * jnp.concatenate([w_ul, w_ur], axis=1),
            jnp.concatenate([jnp.zeros((half, half), F32), st_t[h][half:, half:] * w_lr], axis=1)], axis=0)
        pv = jnp.dot(vt_aug[h], sc_t.astype(BF16), preferred_element_type=F32)
        after_head(h)
        res = pv + jnp.exp(row(ROW_U, h)) * inter_t[h]
        den = res[dv:dv + 1, :]
        half_inv = 0.5 / jnp.maximum(jnp.abs(den), jnp.exp(row(ROW_NEGM, h)))
        gated = res[:dv] * (1.0 + jnp.tanh(ogt_ref[hs, ts]))
        ms = jnp.mean(gated * gated, axis=0, keepdims=True)
        hn = gated * (half_inv * lax.rsqrt(ms * half_inv * half_inv + NORM_EPS)) * nwb_ref[hs, :]
        write_out(hs, ts, (hn * _silu_of_half(zt_ref[hs, ts])).astype(BF16))

    for h in heads:
        a_col = mc[:, LANE_A + h:LANE_A + h + 1]
        g_last = mc[n - 1:n, LANE_G + h:LANE_G + h + 1]
        u_last = mc[n - 1:n, LANE_U + h:LANE_U + h + 1]
        kw = (kh[h] * jnp.exp(a_col + g_last)).astype(BF16)
        upd = jnp.dot(vt_aug[h], kw, preferred_element_type=F32)
        st_ref[h] = jnp.exp(u_last) * st_ref[h] + upd


def _mlstm_out_kernel(qk_ref, vt_ref, ogt_ref, zt_ref, mcol_ref, mrow_ref, cw_ref, cb_ref, nwb_ref,
                      yat_ref, wo_ref, x_ref, fw_ref, o_ref, hist_ref, st_ref, ybt_ref, lhs_ref,
                      *, per_seq, final):
    step = pl.program_id(0)
    cur = step % 2
    n = MLSTM_L

    @pl.when(step % per_seq == 0)
    def _():
        hist_ref[...] = jnp.zeros_like(hist_ref)
        st_ref[...] = jnp.zeros_like(st_ref)

    @pl.when(step == 0)
    def _():
        ybt_ref[1] = jnp.zeros_like(ybt_ref[1])

    def write_yb(hs, ts, val):
        ybt_ref[cur, hs, ts] = val

    prev = ybt_ref.at[1 - cur]
    n_chunks = FUSE_TM // n
    q_tiles = MLSTM_QK_WIDTH // LANES
    conv_tile = lambda c, t: _mlstm_conv_tile(c, t, n_chunks, qk_ref, cw_ref, cb_ref, hist_ref)
    every = MLSTM_HEADS // FUSE_SLICES
    cols = D_MODEL // FUSE_SLICES
    for c in range(n_chunks):
        rs = slice(c * n, (c + 1) * n)
        sy = [conv_tile(c, t) for t in range(2 * q_tiles)]
        proj = []
        lhs_ref[:, :FOX_WIDTH] = yat_ref[:, rs].T
        lhs_ref[:, FOX_WIDTH:] = prev[:, rs].T

        def woven(h, proj=proj):
            if h % every == 0:
                cs = slice(len(proj) * cols, (len(proj) + 1) * cols)
                proj.append(jnp.dot(lhs_ref[...], wo_ref[:, cs], preferred_element_type=F32))

        _mlstm_chunk(c, sy, vt_ref, ogt_ref, zt_ref, mcol_ref, mrow_ref, nwb_ref, write_yb, st_ref,
                     after_head=woven)
        acc = jnp.concatenate(proj, axis=1)
        r = x_ref[rs, :] + acc
        if final:
            r = r * lax.rsqrt(jnp.mean(r * r, axis=-1, keepdims=True) + NORM_EPS) * fw_ref[...]
        o_ref[rs, :] = r


def _mlstm_out(qkr, t16, t32, mcol, mrow, cw, cb, nwb, yat, wo, x2, fw, final):
    assert FOX_WIDTH == MLSTM_V_WIDTH and wo.shape == (FOX_WIDTH + MLSTM_V_WIDTH, D_MODEL)
    m = x2.shape[0]
    _, _, seq = yat.shape
    tm = FUSE_TM
    per_seq = seq // tm
    blocks = m // tm
    cur = lambda s: jnp.minimum(s, blocks - 1)
    prv = lambda s: jnp.maximum(s - 1, 0)
    t_cur = lambda which: (lambda s: (cur(s) // per_seq, which, cur(s) % per_seq))
    return pl.pallas_call(
        functools.partial(_mlstm_out_kernel, per_seq=per_seq, final=final),
        grid=(blocks + 1,),
        in_specs=[
            pl.BlockSpec((tm, SEG), lambda s: (cur(s), 0)),
            pl.BlockSpec((None, SEG, tm), t_cur(2)),
            pl.BlockSpec((None, SEG, tm), t_cur(1)),
            pl.BlockSpec((None, SEG, tm), t_cur(2)),
            pl.BlockSpec((tm, LANES), lambda s: (cur(s), 0)),
            pl.BlockSpec((None, GATE_ROWS, tm), lambda s: (cur(s) // per_seq, 0, cur(s) % per_seq)),
            pl.BlockSpec((CONV_WIDTH, SEG), lambda s: (0, 0)),
            pl.BlockSpec((1, SEG), lambda s: (0, 0)),
            pl.BlockSpec((SEG, MLSTM_L), lambda s: (0, 0)),
            pl.BlockSpec((None, FOX_WIDTH, tm), lambda s: (prv(s) // per_seq, 0, prv(s) % per_seq)),
            pl.BlockSpec(wo.shape, lambda s: (0, 0)),
            pl.BlockSpec((tm, D_MODEL), lambda s: (prv(s), 0)),
            pl.BlockSpec((1, D_MODEL), lambda s: (0, 0)),
        ],
        out_specs=pl.BlockSpec((tm, D_MODEL), lambda s: (prv(s), 0)),
        out_shape=jax.ShapeDtypeStruct((m, D_MODEL), F32),
        scratch_shapes=[pltpu.VMEM((F32_SUBLANES, SEG), F32),
                        pltpu.VMEM((MLSTM_HEADS, MLSTM_V_DIM + ONES_ROWS, LANES), F32),
                        pltpu.VMEM((2, MLSTM_V_WIDTH, tm), BF16),
                        pltpu.VMEM((MLSTM_L, FOX_WIDTH + MLSTM_V_WIDTH), BF16)],
        compiler_params=_params(("arbitrary",)),
        name="mlstm_outproj",
    )(qkr, t16, t32, t32, mcol, mrow, cw, cb, nwb, yat, wo, x2, fw)


SEG_STARTS = (IN_OFFS[0], IN_OFFS[1], IN_OFFS[5], IN_OFFS[2], IN_OFFS[7], IN_OFFS[3], IN_OFFS[8], IN_OFFS[9])
SEG_SHIFT = FOX_HEADS
assert all(st % SEG in (0, SEG_SHIFT) for st in SEG_STARTS) and IN_OFFS[6] == IN_OFFS[5] + MLSTM_QK_WIDTH
assert SEG_SHIFT == F32_SUBLANES and MLSTM_HEADS == F32_SUBLANES

J_FOX_GATE = 2
assert SEG_STARTS[J_FOX_GATE] - SEG_SHIFT == IN_OFFS[4]


def _wprep_kernel(main_blk_ref, extra_blk_ref, main_ref, extra_ref, mi_ref, mf_ref, wout_ref,
                  o_ref, gw_ref, wo_ref, gwf_ref):
    del main_blk_ref, extra_blk_ref
    j = pl.program_id(0)
    in_seg = j < len(SEG_STARTS)
    shifted = functools.reduce(jnp.logical_or, [j == k for k, st in enumerate(SEG_STARTS) if st % SEG])

    @pl.when(jnp.logical_not(in_seg))
    def _():
        wo_ref[...] = wout_ref[...].astype(BF16)

    @pl.when(j == 0)
    def _():
        gwf_ref[...] = jnp.zeros_like(gwf_ref)
        gwf_ref[GW_MLSTM_I:GW_MLSTM_I + MLSTM_HEADS, :] = mi_ref[...]
        gwf_ref[GW_MLSTM_F:GW_MLSTM_F + MLSTM_HEADS, :] = mf_ref[...]

    @pl.when(in_seg & jnp.logical_not(shifted))
    def _():
        o_ref[...] = main_ref[...].astype(BF16)

    @pl.when(shifted)
    def _():
        full = jnp.concatenate([main_ref[...], extra_ref[...]], axis=0)
        o_ref[...] = full[SEG_SHIFT:SEG_SHIFT + SEG, :].astype(BF16)

    @pl.when(j == J_FOX_GATE)
    def _():
        gwf_ref[GW_FOX_F:GW_FOX_F + FOX_HEADS, :] = main_ref[0:SEG_SHIFT, :]

    @pl.when(j == len(SEG_STARTS) - 1)
    def _():
        gw_ref[...] = gwf_ref[...].astype(BF16)


def _wprep(w_t, w_out):
    d = w_t.shape[1]
    nseg = len(SEG_STARTS)
    out_steps = w_out.shape[0] // SEG
    pad = [SEG_STARTS[-1]] * out_steps
    main_blk = jnp.asarray([st // SEG for st in SEG_STARTS + tuple(pad)], jnp.int32)
    extra_blk = jnp.asarray([(st // SEG + 1) * (SEG // F32_SUBLANES) for st in SEG_STARTS + tuple(pad)],
                            jnp.int32)
    out_blk = lambda j: jnp.clip(j - nseg, 0, out_steps - 1)
    grid_spec = pltpu.PrefetchScalarGridSpec(
        num_scalar_prefetch=2,
        grid=(nseg + out_steps,),
        in_specs=[
            pl.BlockSpec((SEG, d), lambda j, mb, eb: (mb[j], 0)),
            pl.BlockSpec((F32_SUBLANES, d), lambda j, mb, eb: (eb[j], 0)),
            pl.BlockSpec((MLSTM_HEADS, d), lambda j, mb, eb: (IN_OFFS[10] // F32_SUBLANES, 0)),
            pl.BlockSpec((MLSTM_HEADS, d), lambda j, mb, eb: (IN_OFFS[11] // F32_SUBLANES, 0)),
            pl.BlockSpec((SEG, w_out.shape[1]), lambda j, mb, eb: (out_blk(j), 0)),
        ],
        out_specs=[
            pl.BlockSpec((SEG, d), lambda j, mb, eb: (jnp.minimum(j, nseg - 1), 0)),
            pl.BlockSpec((GATE_ROWS, d), lambda j, mb, eb: (0, 0)),
            pl.BlockSpec((SEG, w_out.shape[1]), lambda j, mb, eb: (out_blk(j), 0)),
        ],
        scratch_shapes=[pltpu.VMEM((GATE_ROWS, d), F32)],
    )
    return pl.pallas_call(
        _wprep_kernel,
        grid_spec=grid_spec,
        out_shape=[jax.ShapeDtypeStruct((nseg * SEG, d), BF16),
                   jax.ShapeDtypeStruct((GATE_ROWS, d), BF16),
                   jax.ShapeDtypeStruct(w_out.shape, BF16)],
        compiler_params=_params(("arbitrary",)),
        name="weight_layout",
    )(main_blk, extra_blk, w_t, w_t, w_t, w_t, w_out)


def _lane_broadcast(v, width):
    return jnp.broadcast_to(v[:, None], (v.shape[0], width))


def kernel(x, norm_w, w_in, fox_f_bias, conv_w, conv_b, mlstm_i_bias, mlstm_f_bias,
           fox_out_norm_w, mlstm_out_norm_w, w_out, final_norm_w):
    b, s, d = x.shape
    depth = norm_w.shape[0]
    x2 = x.reshape(b * s, d)
    for l in range(depth):
        w_t, w_gate, wo = _wprep(w_in[l].T, w_out[l])
        bias_rows = _lane_broadcast(jnp.concatenate([
            fox_f_bias[l], mlstm_f_bias[l], mlstm_i_bias[l], jnp.zeros((GATE_ROWS - 24,), F32)]), LANES)

        kr, qkr, t16, t32, g = _inproj(x2, norm_w[l].reshape(1, d), w_t, w_gate, s)
        mcol, mrow = _gates(g, bias_rows)

        yat = _fox(kr.reshape(b, s, SEG), t16, t32, mcol, _lane_broadcast(fox_out_norm_w[l], ATT_T))
        x2 = _mlstm_out(qkr, t16, t32, mcol.reshape(b * s, LANES), mrow, conv_w[l], conv_b[l].reshape(1, SEG),
                        _lane_broadcast(mlstm_out_norm_w[l], MLSTM_L),
                        yat, wo, x2, final_norm_w.reshape(1, d),
                        final=(l == depth - 1))
    return x2.reshape(b, s, d)
```
